```python
import math
import jax, jax.numpy as jnp
from jax import lax
import numpy as np

D_MODEL = 1024
BATCH = 2
SEQ = 8192
DEPTH = 1

D_MIX = D_MODEL
D_ATTN = D_MIX // 2
D_RNN = D_MIX - D_ATTN
HEAD_DIM = 64
N_HEADS = D_ATTN // HEAD_DIM
N_KV_HEADS = 2
GQA_REP = N_HEADS // N_KV_HEADS
D_KV = N_KV_HEADS * HEAD_DIM
CMP_LEN = 32
CMP_STRIDE = 16
SEL_BLOCK = 64
N_SEL = 16
WINDOW = 512
Q_BLOCK = 128
FORCE_BONUS = 1e4
N_BUCKETS = 32
MAX_DISTANCE = 128
RG_BLOCKS = 8
RG_BLOCK_DIM = D_RNN // RG_BLOCKS
CONV_WIDTH = 4
RG_C = 8.0
N_GROUPS = 4
EXPERTS_PER_GROUP = 4
N_EXPERTS = N_GROUPS * EXPERTS_PER_GROUP
TOP_K_INNER = 2
D_FF_EXPERT = 512
D_PLE = 256
ALPHA = (2.0 * DEPTH) ** 0.25
BETA = (8.0 * DEPTH) ** -0.25
LN_EPS = 1e-5
RMS_EPS = 1e-6
NEG = -1e30
N_GATE_COLS = 3 * N_HEADS
SPLIT_IDX = [D_ATTN, D_ATTN + 6 * D_KV, D_ATTN + 6 * D_KV + N_GATE_COLS,
             D_ATTN + 6 * D_KV + N_GATE_COLS + D_RNN]
D_IN_PROJ = D_ATTN + 6 * D_KV + N_GATE_COLS + 2 * D_RNN

kernel_name = 'hymba_nsa_rglru_hmoe_deepnorm'


def layer_norm(x, g, b):
    xf = x.astype(jnp.float32)
    mu = jnp.mean(xf, axis=-1, keepdims=True)
    var = jnp.mean(jnp.square(xf - mu), axis=-1, keepdims=True)
    y = (xf - mu) * lax.rsqrt(var + LN_EPS) * g.astype(jnp.float32) + b.astype(jnp.float32)
    return y.astype(x.dtype)


def rms_norm(x, g):
    xf = x.astype(jnp.float32)
    y = xf * lax.rsqrt(jnp.mean(jnp.square(xf), axis=-1, keepdims=True) + RMS_EPS)
    return y * g.astype(jnp.float32)


def t5_bucket(dist):
    max_exact = N_BUCKETS // 2
    d = jnp.maximum(dist, 0)
    df = jnp.maximum(d, 1).astype(jnp.float32)
    large = max_exact + (jnp.log(df / max_exact) / math.log(MAX_DISTANCE / max_exact)
                         * (N_BUCKETS - max_exact)).astype(jnp.int32)
    large = jnp.minimum(large, N_BUCKETS - 1)
    return jnp.where(d < max_exact, d, large)


def masked_softmax(logits, mask):
    lf = jnp.where(mask, logits.astype(jnp.float32), NEG)
    m = jnp.max(lf, axis=-1, keepdims=True)
    e = jnp.where(mask, jnp.exp(lf - m), 0.0)
    return e / jnp.maximum(jnp.sum(e, axis=-1, keepdims=True), 1e-30)


def nsa_attention(q, k_cmp_raw, v_cmp_raw, k_slc, v_slc, k_win, v_win, gates,
                  cmp_pe_k, cmp_pe_v, cmp_w_k, cmp_w_v, rel_bias):
    B, T = q.shape[0], q.shape[1]
    G, R, Dh = N_KV_HEADS, GQA_REP, HEAD_DIM
    n_cmp = (T - CMP_LEN) // CMP_STRIDE + 1
    n_slc = T // SEL_BLOCK
    n_sel = min(N_SEL, n_slc)
    n_qblk = T // Q_BLOCK
    scale = Dh ** -0.5

    tok = jnp.arange(n_cmp)[:, None] * CMP_STRIDE + jnp.arange(CMP_LEN)[None, :]

    def compress(raw, pe, w):
        blk = raw[:, tok] + pe[None, None, :, None, :]
        blk = jnp.moveaxis(blk, 3, 2).reshape(B, n_cmp, G, CMP_LEN * Dh)
        return blk @ w

    k_c = compress(k_cmp_raw, cmp_pe_k, cmp_w_k)
    v_c = compress(v_cmp_raw, cmp_pe_v, cmp_w_v)
    cmp_end = jnp.arange(n_cmp) * CMP_STRIDE + CMP_LEN - 1
    cs = jnp.arange(n_cmp) * CMP_STRIDE
    ss = jnp.arange(n_slc) * SEL_BLOCK
    overlap = ((cs[:, None] < ss[None, :] + SEL_BLOCK) &
               (cs[:, None] + CMP_LEN > ss[None, :])).astype(jnp.float32)

    k_s_t = jnp.moveaxis(k_slc, 1, 2)
    v_s_t = jnp.moveaxis(v_slc, 1, 2)
    k_w_p = jnp.pad(k_win, ((0, 0), (WINDOW, 0), (0, 0), (0, 0)))
    v_w_p = jnp.pad(v_win, ((0, 0), (WINDOW, 0), (0, 0), (0, 0)))

    tab = rel_bias.T.reshape(G, R, N_BUCKETS)
    g_ix = jnp.arange(G)[None, :, None, None, None]
    r_ix = jnp.arange(R)[None, None, :, None, None]
    sblk = jnp.arange(n_slc)

    def block(i):
        qs = i * Q_BLOCK
        qpos = qs + jnp.arange(Q_BLOCK)
        qb = lax.dynamic_slice_in_dim(q, qs, Q_BLOCK, axis=1).reshape(B, Q_BLOCK, G, R, Dh) * scale
        gb = lax.dynamic_slice_in_dim(gates, qs, Q_BLOCK, axis=1)

        dist_c = qpos[:, None] - cmp_end[None, :]
        s_c = jnp.einsum('bqgrd,bcgd->bgrqc', qb, k_c) + tab[:, :, t5_bucket(dist_c)]
        p_c = masked_softmax(s_c, dist_c >= 0)
        o_c = jnp.einsum('bgrqc,bcgd->bqgrd', p_c, v_c)

        imp = jnp.einsum('bgrqc,cs->bgqs', p_c, overlap)
        cur = (qpos // SEL_BLOCK)[:, None]
        valid = sblk[None, :] <= cur
        forced = (sblk[None, :] == 0) | (sblk[None, :] == cur) | (sblk[None, :] == cur - 1)
        score = jnp.where(valid, imp + jnp.where(forced, FORCE_BONUS, 0.0), NEG)
        _, sel = lax.top_k(score, n_sel)

        kidx = (sel[..., None] * SEL_BLOCK + jnp.arange(SEL_BLOCK)).reshape(B, G, Q_BLOCK * n_sel * SEL_BLOCK)
        ks = jnp.take_along_axis(k_s_t, kidx[..., None], axis=2).reshape(B, G, Q_BLOCK, n_sel * SEL_BLOCK, Dh)
        vs = jnp.take_along_axis(v_s_t, kidx[..., None], axis=2).reshape(B, G, Q_BLOCK, n_sel * SEL_BLOCK, Dh)
        kpos = kidx.reshape(B, G, Q_BLOCK, n_sel * SEL_BLOCK)
        dist_s = qpos[None, None, :, None] - kpos
        bias_s = tab[g_ix, r_ix, t5_bucket(dist_s)[:, :, None]]
        s_s = jnp.einsum('bqgrd,bgqkd->bgrqk', qb, ks) + bias_s
        p_s = masked_softmax(s_s, (dist_s >= 0)[:, :, None])
        o_s = jnp.einsum('bgrqk,bgqkd->bqgrd', p_s, vs)

        kw = lax.dynamic_slice_in_dim(k_w_p, qs, WINDOW + Q_BLOCK, axis=1)
        vw = lax.dynamic_slice_in_dim(v_w_p, qs, WINDOW + Q_BLOCK, axis=1)
        wpos = qs - WINDOW + jnp.arange(WINDOW + Q_BLOCK)
        dist_w = qpos[:, None] - wpos[None, :]
        mask_w = (dist_w >= 0) & (dist_w < WINDOW) & (wpos[None, :] >= 0)
        s_w = jnp.einsum('bqgrd,bkgd->bgrqk', qb, kw) + tab[:, :, t5_bucket(dist_w)]
        p_w = masked_softmax(s_w, mask_w)
        o_w = jnp.einsum('bgrqk,bkgd->bqgrd', p_w, vw)

        out = gb[..., 0:1] * o_c + gb[..., 1:2] * o_s + gb[..., 2:3] * o_w
        return out.reshape(B, Q_BLOCK, N_HEADS * Dh)

    out = lax.map(block, jnp.arange(n_qblk))
    return jnp.moveaxis(out, 0, 1).reshape(B, T, N_HEADS * Dh)


def rglru_branch(xr, yr, conv_w, conv_b, w_a, b_a, w_x, b_x, lam):
    B, T, C = xr.shape
    xf = xr.astype(jnp.float32)
    xc = lax.conv_general_dilated(xf, conv_w.astype(jnp.float32).reshape(CONV_WIDTH, 1, C),
                                  window_strides=(1,), padding=[(CONV_WIDTH - 1, 0)],
                                  dimension_numbers=('NWC', 'WIO', 'NWC'),
                                  feature_group_count=C) + conv_b.astype(jnp.float32)
    xb = xc.reshape(B, T, RG_BLOCKS, RG_BLOCK_DIM)
    r = jax.nn.sigmoid(jnp.einsum('btnk,nkj->btnj', xb, w_a.astype(jnp.float32)).reshape(B, T, C) + b_a)
    i_g = jax.nn.sigmoid(jnp.einsum('btnk,nkj->btnj', xb, w_x.astype(jnp.float32)).reshape(B, T, C) + b_x)
    log_a = -RG_C * r * jax.nn.softplus(-lam.astype(jnp.float32))
    a = jnp.exp(log_a)
    bterm = jnp.sqrt(-jnp.expm1(2.0 * log_a)) * (i_g * xc)

    def combine(c1, c2):
        a1, b1 = c1
        a2, b2 = c2
        return a1 * a2, a2 * b1 + b2

    _, h = lax.associative_scan(combine, (a, bterm), axis=1)
    return h * jax.nn.gelu(yr.astype(jnp.float32))


def hmoe(h, wg, bg, we, be, w_up, w_down):
    B, T, D = h.shape
    t = h.reshape(B * T, D)
    n = t.shape[0]
    pg = jax.nn.softmax((t @ wg + bg).astype(jnp.float32), axis=-1)
    pg_top, gi = lax.top_k(pg, 1)
    le = (t @ we + be).astype(jnp.float32).reshape(n, N_GROUPS, EXPERTS_PER_GROUP)
    le_g = jnp.take_along_axis(le, gi[:, :, None], axis=1)[:, 0]
    pe = jax.nn.softmax(le_g, axis=-1)
    pe_top, ei = lax.top_k(pe, TOP_K_INNER)
    w = pg_top * pe_top / jnp.sum(pe_top, axis=-1, keepdims=True)
    eid = gi * EXPERTS_PER_GROUP + ei
    comb = jnp.sum(jax.nn.one_hot(eid, N_EXPERTS, dtype=jnp.float32) * w[..., None], axis=1)
    out = jnp.zeros((n, D), jnp.float32)
    for e in range(N_EXPERTS):
        u = t @ w_up[e]
        ua, ub = jnp.split(u, 2, axis=-1)
        out = out + comb[:, e:e + 1] * ((jax.nn.silu(ua) * ub) @ w_down[e])
    return out.reshape(B, T, D).astype(h.dtype)


def setup_inputs(seed: int = 0) -> dict:
    key = jax.random.key(seed)
    ks = jax.random.split(key, 32)
    f32 = jnp.float32
    nrm = lambda k, shape, s: jax.random.normal(k, shape, f32) * s
    col_scale = jnp.concatenate([
        jnp.ones((D_ATTN,), f32),
        jnp.ones((D_KV,), f32), jnp.full((D_KV,), BETA, f32),
        jnp.ones((D_KV,), f32), jnp.full((D_KV,), BETA, f32),
        jnp.ones((D_KV,), f32), jnp.full((D_KV,), BETA, f32),
        jnp.ones((N_GATE_COLS,), f32),
        jnp.full((D_RNN,), BETA, f32), jnp.ones((D_RNN,), f32)])
    u = jax.random.uniform(ks[10], (DEPTH, D_RNN), f32, 0.9, 0.999)
    a0 = u ** (1.0 / RG_C)
    return {
        'x': nrm(ks[0], (BATCH, SEQ, D_MODEL), 1.0),
        'p': nrm(ks[1], (DEPTH, BATCH, SEQ, D_PLE), 1.0),
        'rel_bias': nrm(ks[2], (N_BUCKETS, N_HEADS), 0.2),
        'w_in': nrm(ks[3], (DEPTH, D_MODEL, D_IN_PROJ), D_MODEL ** -0.5) * col_scale,
        'cmp_pe_k': nrm(ks[4], (DEPTH, CMP_LEN, HEAD_DIM), 0.02),
        'cmp_pe_v': nrm(ks[5], (DEPTH, CMP_LEN, HEAD_DIM), 0.02),
        'cmp_w_k': nrm(ks[6], (DEPTH, CMP_LEN * HEAD_DIM, HEAD_DIM), (CMP_LEN * HEAD_DIM) ** -0.5),
        'cmp_w_v': nrm(ks[7], (DEPTH, CMP_LEN * HEAD_DIM, HEAD_DIM), (CMP_LEN * HEAD_DIM) ** -0.5),
        'conv_w': nrm(ks[8], (DEPTH, CONV_WIDTH, D_RNN), CONV_WIDTH ** -0.5),
        'conv_b': nrm(ks[9], (DEPTH, D_RNN), 0.02),
        'rg_w_a': nrm(ks[11], (DEPTH, RG_BLOCKS, RG_BLOCK_DIM, RG_BLOCK_DIM), RG_BLOCK_DIM ** -0.5),
        'rg_b_a': nrm(ks[12], (DEPTH, D_RNN), 0.02),
        'rg_w_x': nrm(ks[13], (DEPTH, RG_BLOCKS, RG_BLOCK_DIM, RG_BLOCK_DIM), RG_BLOCK_DIM ** -0.5),
        'rg_b_x': nrm(ks[14], (DEPTH, D_RNN), 0.02),
        'rg_lambda': jnp.log(a0 / (1.0 - a0)),
        'attn_out_gain': 1.0 + nrm(ks[15], (DEPTH, D_ATTN), 0.02),
        'rnn_out_gain': 1.0 + nrm(ks[16], (DEPTH, D_RNN), 0.02),
        'w_out': nrm(ks[17], (DEPTH, D_MIX, D_MODEL), D_MIX ** -0.5 * BETA),
        'ln1_g': 1.0 + nrm(ks[18], (DEPTH, D_MODEL), 0.02),
        'ln1_b': nrm(ks[19], (DEPTH, D_MODEL), 0.02),
        'router_group_w': nrm(ks[20], (DEPTH, D_MODEL, N_GROUPS), D_MODEL ** -0.5),
        'router_group_b': nrm(ks[21], (DEPTH, N_GROUPS), 0.01),
        'router_expert_w': nrm(ks[22], (DEPTH, D_MODEL, N_EXPERTS), D_MODEL ** -0.5),
        'router_expert_b': nrm(ks[23], (DEPTH, N_EXPERTS), 0.01),
        'expert_w_up': nrm(ks[24], (DEPTH, N_EXPERTS, D_MODEL, 2 * D_FF_EXPERT), D_MODEL ** -0.5),
        'expert_w_down': nrm(ks[25], (DEPTH, N_EXPERTS, D_FF_EXPERT, D_MODEL), D_FF_EXPERT ** -0.5 * BETA),
        'ple_w': nrm(ks[26], (DEPTH, D_PLE, D_MODEL), D_PLE ** -0.5 * BETA),
        'ple_gate_w': nrm(ks[27], (DEPTH, D_MODEL, D_MODEL), D_MODEL ** -0.5),
        'ln2_g': 1.0 + nrm(ks[28], (DEPTH, D_MODEL), 0.02),
        'ln2_b': nrm(ks[29], (DEPTH, D_MODEL), 0.02),
    }


def reference(x, p, rel_bias, w_in, cmp_pe_k, cmp_pe_v, cmp_w_k, cmp_w_v, conv_w, conv_b,
              rg_w_a, rg_b_a, rg_w_x, rg_b_x, rg_lambda, attn_out_gain, rnn_out_gain, w_out,
              ln1_g, ln1_b, router_group_w, router_group_b, router_expert_w, router_expert_b,
              expert_w_up, expert_w_down, ple_w, ple_gate_w, ln2_g, ln2_b):
    B, T, _ = x.shape
    for i in range(DEPTH):
        proj = x @ w_in[i]
        q, kv, g, rx, ry = jnp.split(proj, SPLIT_IDX, axis=-1)
        q = q.reshape(B, T, N_HEADS, HEAD_DIM)
        k_c, v_c, k_s, v_s, k_w, v_w = [c.reshape(B, T, N_KV_HEADS, HEAD_DIM)
                                        for c in jnp.split(kv, 6, axis=-1)]
        gates = jax.nn.sigmoid(g).reshape(B, T, N_KV_HEADS, GQA_REP, 3)
        attn = nsa_attention(q, k_c, v_c, k_s, v_s, k_w, v_w, gates,
                             cmp_pe_k[i], cmp_pe_v[i], cmp_w_k[i], cmp_w_v[i], rel_bias)
        rnn = rglru_branch(rx, ry, conv_w[i], conv_b[i], rg_w_a[i], rg_b_a[i],
                           rg_w_x[i], rg_b_x[i], rg_lambda[i])
        heads = jnp.concatenate([rms_norm(attn, attn_out_gain[i]),
                                 rms_norm(rnn, rnn_out_gain[i])], axis=-1).astype(x.dtype)
        mix = heads @ w_out[i]
        x = layer_norm(ALPHA * x + mix, ln1_g[i], ln1_b[i])
        moe = hmoe(x, router_group_w[i], router_group_b[i], router_expert_w[i],
                   router_expert_b[i], expert_w_up[i], expert_w_down[i])
        ple = jax.nn.sigmoid(x @ ple_gate_w[i]) * (p[i] @ ple_w[i])
        x = layer_norm(ALPHA * x + moe + ple, ln2_g[i], ln2_b[i])
    return x
```

```python
import functools
import math

import jax
import jax.numpy as jnp
from jax import lax
from jax.experimental import pallas as pl
from jax.experimental.pallas import tpu as pltpu

F32 = jnp.float32
BF16 = jnp.bfloat16
NEG = -1e30

D_MODEL = 1024
HEAD_DIM = 64
N_HEADS = 8
N_GROUPS_KV = 2
GQA_REP = 4
D_ATTN = 512
D_RNN = 512
D_KV = 128
CMP_LEN = 32
CMP_STRIDE = 16
SEL_BLOCK = 64
N_SEL = 16
WINDOW = 512
N_BUCKETS = 32
MAX_DISTANCE = 128
N_EXPERTS = 16
EXPERTS_PER_GROUP = 4
N_EXP_GROUPS = 4
D_FF = 512
D_PLE = 256
ALPHA = 2.0 ** 0.25
LN_EPS = 1e-5
RMS_EPS = 1e-6
FORCE_BONUS = 1e4
RG_C = 8.0

TQ = 128
N_SLC = 128
C_PAD = 640
C_OFF = 16
FAR = 512
TAIL = WINDOW + TQ
BAND = 24
VMEM_LIMIT = 56 * 1024 * 1024


def _cparams(sem, vmem=None):
    return pltpu.CompilerParams(dimension_semantics=sem, vmem_limit_bytes=vmem)


def _inproj_kernel(x_ref, wm_ref, wt_ref, q_ref, kc_ref, vc_ref, vs_ref, vw_ref, gt_ref,
                   rx_ref, ry_ref, t_ref):
    xb = x_ref[...].astype(BF16)

    def mm(lo, hi):
        return jnp.dot(xb, wm_ref[:, lo:hi], preferred_element_type=F32)

    q_ref[...] = mm(0, 512).astype(BF16)
    kc_ref[...] = mm(512, 640)
    vc_ref[...] = mm(640, 768)
    vs_ref[...] = mm(768, 896).astype(BF16)
    vw_ref[...] = mm(896, 1024).astype(BF16)
    gt_ref[...] = jax.nn.sigmoid(mm(1024, 1152))
    rx_ref[...] = mm(1152, 1664)
    ry_ref[...] = mm(1664, 2176)
    t_ref[...] = lax.dot_general(wt_ref[...], xb, (((1,), (1,)), ((), ())),
                                 preferred_element_type=F32).astype(BF16)


def _inproj(x2, w_main, w_t, B, T):
    N = B * T
    tm = 512
    nt = T // tm
    row = lambda w: pl.BlockSpec((tm, w), lambda i: (i, 0))
    full = lambda a: pl.BlockSpec(a.shape, lambda i: (0,) * a.ndim)
    return pl.pallas_call(
        _inproj_kernel,
        grid=(N // tm,),
        in_specs=[row(D_MODEL), full(w_main), full(w_t)],
        out_specs=[row(512), row(128), row(128), row(128), row(128), row(128), row(512), row(512),
                   pl.BlockSpec((None, 768, tm), lambda i: (i // nt, 0, i % nt))],
        out_shape=[jax.ShapeDtypeStruct((N, 512), BF16),
                   jax.ShapeDtypeStruct((N, 128), F32),
                   jax.ShapeDtypeStruct((N, 128), F32),
                   jax.ShapeDtypeStruct((N, 128), BF16),
                   jax.ShapeDtypeStruct((N, 128), BF16),
                   jax.ShapeDtypeStruct((N, 128), F32),
                   jax.ShapeDtypeStruct((N, 512), F32),
                   jax.ShapeDtypeStruct((N, 512), F32),
                   jax.ShapeDtypeStruct((B, 768, T), BF16)],
        compiler_params=_cparams(("parallel",), VMEM_LIMIT),
        name="inproj",
    )(x2, w_main, w_t)


def _compress_kernel(xk_ref, xv_ref, pek_ref, pev_ref, wk_ref, wvt_ref, kc_ref, vct_ref, *, n_cmp):
    xk = xk_ref[...]
    lo = jnp.dot((xk + pek_ref[0:1, :]).astype(BF16), wk_ref[0], preferred_element_type=F32)
    hi = jnp.dot((xk + pek_ref[1:2, :]).astype(BF16), wk_ref[1], preferred_element_type=F32)
    hi = pltpu.roll(hi, C_PAD - 1, axis=0)
    rid = lax.broadcasted_iota(jnp.int32, (C_PAD, HEAD_DIM), 0)
    ok = (rid >= C_OFF) & (rid < C_OFF + n_cmp)
    kc_ref[...] = jnp.where(ok, lo + hi, 0.0).astype(BF16)

    xv = xv_ref[...]
    dn = (((1,), (1,)), ((), ()))
    lo_t = lax.dot_general(wvt_ref[0], (xv + pev_ref[0:1, :]).astype(BF16), dn,
                           preferred_element_type=F32)
    hi_t = lax.dot_general(wvt_ref[1], (xv + pev_ref[1:2, :]).astype(BF16), dn,
                           preferred_element_type=F32)
    hi_t = pltpu.roll(hi_t, C_PAD - 1, axis=1)
    cid = lax.broadcasted_iota(jnp.int32, (HEAD_DIM, C_PAD), 1)
    ok_t = (cid >= C_OFF) & (cid < C_OFF + n_cmp)
    vct_ref[...] = jnp.where(ok_t, lo_t + hi_t, 0.0).astype(BF16)


def _compress(xk, xv, pek, pev, wk, wvt, B, n_cmp):
    blk = pl.BlockSpec((None, None, C_PAD, 1024), lambda b, g: (b, g, 0, 0))
    full = lambda a: pl.BlockSpec(a.shape, lambda b, g: (0,) * a.ndim)
    return pl.pallas_call(
        functools.partial(_compress_kernel, n_cmp=n_cmp),
        grid=(B, N_GROUPS_KV),
        in_specs=[blk, blk, full(pek), full(pev), full(wk), full(wvt)],
        out_specs=[pl.BlockSpec((None, None, C_PAD, HEAD_DIM), lambda b, g: (b, g, 0, 0)),
                   pl.BlockSpec((None, HEAD_DIM, C_PAD), lambda b, g: (b, g, 0))],
        out_shape=[jax.ShapeDtypeStruct((B, N_GROUPS_KV, C_PAD, HEAD_DIM), BF16),
                   jax.ShapeDtypeStruct((B, 2 * HEAD_DIM, C_PAD), BF16)],
        compiler_params=_cparams(("parallel", "parallel"), VMEM_LIMIT),
        name="compress",
    )(xk, xv, pek, pev, wk, wvt)


def _bias_kernel(tab_ref, dt_ref, bt_ref, dc_ref, bc_ref, os_ref, ow_ref, oc_ref):
    h = pl.program_id(0)

    def lookup(bk):
        out = jnp.zeros(bk.shape, F32)
        for b in range(N_BUCKETS):
            out = jnp.where(bk == b, tab_ref[h, b], out)
        return out - tab_ref[h, N_BUCKETS - 1]

    dt = dt_ref[...]
    v = lookup(bt_ref[...])
    os_ref[...] = jnp.where(dt >= 0, v, NEG)
    ow_ref[...] = jnp.where((dt >= 0) & (dt < WINDOW), v, NEG)
    oc_ref[...] = jnp.where(dc_ref[...] >= 0, lookup(bc_ref[...]), NEG)


def _t5_bucket(dist):
    max_exact = N_BUCKETS // 2
    d = jnp.maximum(dist, 0)
    df = jnp.maximum(d, 1).astype(F32)
    large = max_exact + (jnp.log(df / max_exact) / math.log(MAX_DISTANCE / max_exact)
                         * (N_BUCKETS - max_exact)).astype(jnp.int32)
    large = jnp.minimum(large, N_BUCKETS - 1)
    return jnp.where(d < max_exact, d, large)


def _bias_tiles(rel_bias):
    tq = jnp.arange(TQ, dtype=jnp.int32)
    dt = tq[:, None] + WINDOW - jnp.arange(TAIL, dtype=jnp.int32)[None, :]
    cc = jnp.arange(BAND, dtype=jnp.int32) - C_OFF
    dc = tq[None, :] - (cc[:, None] * CMP_STRIDE + CMP_LEN - 1)
    full = lambda a: pl.BlockSpec(a.shape, lambda h: (0,) * a.ndim)
    ds, dw, bc = pl.pallas_call(
        _bias_kernel,
        grid=(N_HEADS,),
        in_specs=[pl.BlockSpec(memory_space=pltpu.SMEM), full(dt), full(dt), full(dc), full(dc)],
        out_specs=[pl.BlockSpec((None, TQ, TAIL), lambda h: (h, 0, 0)),
                   pl.BlockSpec((None, TQ, TAIL), lambda h: (h, 0, 0)),
                   pl.BlockSpec((None, BAND, TQ), lambda h: (h, 0, 0))],
        out_shape=[jax.ShapeDtypeStruct((N_HEADS, TQ, TAIL), F32),
                   jax.ShapeDtypeStruct((N_HEADS, TQ, TAIL), F32),
                   jax.ShapeDtypeStruct((N_HEADS, BAND, TQ), F32)],
        compiler_params=_cparams(("arbitrary",)),
        name="biastile",
    )(rel_bias.T, dt, _t5_bucket(dt), dc, _t5_bucket(dc))
    ds = ds.reshape(N_GROUPS_KV, GQA_REP * TQ, TAIL)
    dw = dw.reshape(N_GROUPS_KV, GQA_REP * TQ, TAIL)
    bc = bc.reshape(N_GROUPS_KV, GQA_REP, BAND, TQ).transpose(0, 2, 1, 3).reshape(
        N_GROUPS_KV, BAND, GQA_REP * TQ)
    return ds, dw, bc


def _attn_kernel(q_ref, qt_ref, gt_ref, kc_ref, vct_ref, ovt_ref, kaug_ref, vs_ref, kwt_ref, vw_ref,
                 ds_ref, dw_ref, bc_ref, o_ref, sc_ref, m_ref, acc_ref):
    i = pl.program_id(1)
    qs = pl.multiple_of(i * TQ, TQ)
    rows4 = GQA_REP * TQ
    q = q_ref[...]
    gates = gt_ref[...]
    nfar = jnp.maximum(i - 1, 0) // 4
    ones_far = jnp.ones((FAR, 128), BF16)
    ones_tail = jnp.ones((TAIL, 128), BF16)
    kglob = qs - WINDOW + lax.broadcasted_iota(jnp.int32, (1, TAIL), 1)
    outs = []
    for g in range(N_GROUPS_KV):
        qgt = jnp.concatenate(
            [qt_ref[(GQA_REP * g + r) * HEAD_DIM:(GQA_REP * g + r + 1) * HEAD_DIM, :]
             for r in range(GQA_REP)], axis=1)
        sc_ref[...] = jnp.dot(kc_ref[g], qgt, preferred_element_type=F32)
        band0 = pl.multiple_of(i * 8, 8)
        sc_ref[pl.ds(band0, BAND), :] = sc_ref[pl.ds(band0, BAND), :] + bc_ref[g]
        s = sc_ref[...]
        rid = lax.broadcasted_iota(jnp.int32, s.shape, 0)
        s = jnp.where((rid < C_OFF) | (rid >= i * 8 + BAND), NEG, s)
        m = jnp.max(s, axis=0, keepdims=True)
        e = jnp.exp(s - m)
        l = jnp.sum(e, axis=0, keepdims=True)
        tq_col = lax.broadcasted_iota(jnp.int32, (1, rows4), 1) % TQ
        pinv = jnp.where(qs + tq_col >= CMP_LEN - 1, 1.0 / l, 0.0)
        p = e * pinv
        oct_ = jnp.dot(vct_ref[...], p.astype(BF16), preferred_element_type=F32)
        psum = p[:, 0:TQ] + p[:, TQ:2 * TQ] + p[:, 2 * TQ:3 * TQ] + p[:, 3 * TQ:4 * TQ]
        p_hi = psum.astype(BF16)
        rem = psum - p_hi.astype(F32)
        p_mid = rem.astype(BF16)
        p_lo = (rem - p_mid.astype(F32)).astype(BF16)
        ov = ovt_ref[...]
        imp = (jnp.dot(ov, p_hi, preferred_element_type=F32)
               + jnp.dot(ov, p_mid, preferred_element_type=F32)
               + jnp.dot(ov, p_lo, preferred_element_type=F32))

        sidx = lax.broadcasted_iota(jnp.int32, (N_SLC, TQ), 0)
        sidx_f = sidx.astype(F32)
        cur = (qs + lax.broadcasted_iota(jnp.int32, (N_SLC, TQ), 1)) // SEL_BLOCK
        forced = (sidx == 0) | (sidx == cur) | (sidx == cur - 1)
        score = jnp.where(sidx <= cur, imp + jnp.where(forced, FORCE_BONUS, 0.0), NEG)
        mbt = jnp.full((N_SLC, TQ), NEG, F32)
        for _ in range(N_SEL):
            cm = jnp.max(score, axis=0, keepdims=True)
            first = jnp.min(jnp.where(score == cm, sidx_f, float(N_SLC)), axis=0, keepdims=True)
            pick = sidx_f == first
            mbt = jnp.where(pick, 0.0, mbt)
            score = jnp.where(pick, -jnp.inf, score)
        mb = mbt.T.astype(BF16)

        qg = jnp.concatenate(
            [q[:, (GQA_REP * g + r) * HEAD_DIM:(GQA_REP * g + r + 1) * HEAD_DIM]
             for r in range(GQA_REP)], axis=0)
        qaug = jnp.concatenate([jnp.concatenate([mb] * GQA_REP, axis=0), qg], axis=1)

        m_ref[...] = jnp.full(m_ref.shape, NEG, F32)
        acc_ref[...] = jnp.zeros(acc_ref.shape, F32)

        def far_body(j, carry):
            c0 = pl.multiple_of(WINDOW + j * FAR, FAR)
            sf = jnp.dot(qaug, kaug_ref[g, :, pl.ds(c0, FAR)], preferred_element_type=F32)
            mprev = m_ref[...]
            mnew = jnp.maximum(mprev, jnp.max(sf, axis=1, keepdims=True))
            pf = jnp.exp(sf - mnew)
            vch = jnp.concatenate([vs_ref[pl.ds(c0, FAR), :], ones_far], axis=1)
            acc_ref[...] = jnp.exp(mprev - mnew) * acc_ref[...] + jnp.dot(
                pf.astype(BF16), vch, preferred_element_type=F32)
            m_ref[...] = mnew
            return carry

        lax.fori_loop(0, nfar, far_body, 0)

        st = jnp.dot(qaug, kaug_ref[g, :, pl.ds(qs, TAIL)], preferred_element_type=F32)
        st = st + ds_ref[g] + jnp.where(kglob >= nfar * FAR, 0.0, NEG)
        mprev = m_ref[...]
        mnew = jnp.maximum(mprev, jnp.max(st, axis=1, keepdims=True))
        pt = jnp.exp(st - mnew)
        vt = jnp.concatenate([vs_ref[pl.ds(qs, TAIL), :], ones_tail], axis=1)
        acc_s = jnp.exp(mprev - mnew) * acc_ref[...] + jnp.dot(
            pt.astype(BF16), vt, preferred_element_type=F32)
        o_s = acc_s[:, g * HEAD_DIM:(g + 1) * HEAD_DIM] * (1.0 / acc_s[:, 128:129])

        sw = jnp.dot(qg, kwt_ref[g * HEAD_DIM:(g + 1) * HEAD_DIM, pl.ds(qs, TAIL)],
                     preferred_element_type=F32)
        sw = sw + dw_ref[g] + jnp.where(kglob >= 0, 0.0, NEG)
        pw = jnp.exp(sw - jnp.max(sw, axis=1, keepdims=True))
        vwt = jnp.concatenate([vw_ref[pl.ds(qs, TAIL), :], ones_tail], axis=1)
        acc_w = jnp.dot(pw.astype(BF16), vwt, preferred_element_type=F32)
        o_w = acc_w[:, g * HEAD_DIM:(g + 1) * HEAD_DIM] * (1.0 / acc_w[:, 128:129])

        for r in range(GQA_REP):
            rows = slice(r * TQ, (r + 1) * TQ)
            o_c = oct_[:, rows].T[:, g * HEAD_DIM:(g + 1) * HEAD_DIM]
            gc = 3 * (GQA_REP * g + r)
            outs.append(gates[:, gc:gc + 1] * o_c + gates[:, gc + 1:gc + 2] * o_s[rows]
                        + gates[:, gc + 2:gc + 3] * o_w[rows])
    o_ref[...] = jnp.concatenate(outs, axis=1)


def _attention(q, t_all, gates, kc, vct, ovt, kaug, vs_pad, kwt_pad, vw_pad, ds, dw, bc, B, T):
    tp = T + WINDOW
    per_b = lambda shape: pl.BlockSpec((None,) + shape, lambda b, i: (b,) + (0,) * len(shape))
    const = lambda a: pl.BlockSpec(a.shape, lambda b, i: (0,) * a.ndim)
    return pl.pallas_call(
        _attn_kernel,
        grid=(B, T // TQ),
        in_specs=[pl.BlockSpec((None, TQ, 512), lambda b, i: (b, i, 0)),
                  pl.BlockSpec((None, 512, TQ), lambda b, i: (b, 0, i)),
                  pl.BlockSpec((None, TQ, 128), lambda b, i: (b, i, 0)),
                  per_b((N_GROUPS_KV, C_PAD, HEAD_DIM)),
                  per_b((2 * HEAD_DIM, C_PAD)),
                  const(ovt),
                  per_b((N_GROUPS_KV, 192, tp)),
                  per_b((tp, 128)),
                  per_b((128, tp)),
                  per_b((tp, 128)),
                  const(ds), const(dw), const(bc)],
        out_specs=pl.BlockSpec((None, TQ, 512), lambda b, i: (b, i, 0)),
        out_shape=jax.ShapeDtypeStruct((B, T, 512), F32),
        scratch_shapes=[pltpu.VMEM((C_PAD, GQA_REP * TQ), F32),
                        pltpu.VMEM((GQA_REP * TQ, 1), F32),
                        pltpu.VMEM((GQA_REP * TQ, 256), F32)],
        compiler_params=_cparams(("parallel", "arbitrary"), VMEM_LIMIT),
        name="attn",
    )(q, t_all, gates, kc, vct, ovt, kaug, vs_pad, kwt_pad, vw_pad, ds, dw, bc)


def _rglru_kernel(rx_ref, ry_ref, cw_ref, cb_ref, wa_ref, ba_ref, wx_ref, bx_ref, sp_ref, o_ref,
                  xprev_ref, h_ref, a_sc, b_sc, *, L):
    @pl.when(pl.program_id(1) == 0)
    def _():
        xprev_ref[...] = jnp.zeros(xprev_ref.shape, F32)
        h_ref[...] = jnp.zeros(h_ref.shape, F32)

    x = rx_ref[...]
    xe = jnp.concatenate([xprev_ref[...], x], axis=0)
    xc = cb_ref[...] + cw_ref[0:1, :] * xe[5:5 + L]
    for j in range(1, 4):
        xc = xc + cw_ref[j:j + 1, :] * xe[5 + j:5 + j + L]
    xprev_ref[...] = x[L - 8:L]
    xcb = xc.astype(BF16)
    r = jax.nn.sigmoid(jnp.dot(xcb, wa_ref[...], preferred_element_type=F32) + ba_ref[...])
    ig = jax.nn.sigmoid(jnp.dot(xcb, wx_ref[...], preferred_element_type=F32) + bx_ref[...])
    log_a = -RG_C * r * sp_ref[...]
    a = jnp.exp(log_a)
    bt = jnp.sqrt(1.0 - jnp.exp(2.0 * log_a)) * (ig * xc)

    rid = lax.broadcasted_iota(jnp.int32, (L, D_RNN), 0) % 8
    for sh in (1, 2, 4):
        a_s = pltpu.roll(a, sh, axis=0)
        b_s = pltpu.roll(bt, sh, axis=0)
        msk = rid >= sh
        bt = jnp.where(msk, a * b_s + bt, bt)
        a = jnp.where(msk, a * a_s, a)
    a_sc[...] = a
    b_sc[...] = bt

    def body(gi, hprev):
        r0 = pl.multiple_of(gi * 8, 8)
        h = a_sc[pl.ds(r0, 8), :] * hprev + b_sc[pl.ds(r0, 8), :]
        b_sc[pl.ds(r0, 8), :] = h
        return jnp.broadcast_to(h[7:8, :], (8, D_RNN))

    h_ref[...] = lax.fori_loop(0, L // 8, body, h_ref[...])
    y = ry_ref[...]
    cdf = 0.5 * (1.0 + jnp.tanh(math.sqrt(2.0 / math.pi) * (y + 0.044715 * (y * y * y))))
    o_ref[...] = b_sc[...] * (y * cdf)


def _rglru(rx, ry, conv_w, conv_b, wa_bd, b_a, wx_bd, b_x, sp, B, T):
    L = 512
    blk = pl.BlockSpec((None, L, D_RNN), lambda b, t: (b, t, 0))
    full = lambda a: pl.BlockSpec(a.shape, lambda b, t: (0,) * a.ndim)
    return pl.pallas_call(
        functools.partial(_rglru_kernel, L=L),
        grid=(B, T // L),
        in_specs=[blk, blk, full(conv_w), full(conv_b), full(wa_bd), full(b_a), full(wx_bd),
                  full(b_x), full(sp)],
        out_specs=blk,
        out_shape=jax.ShapeDtypeStruct((B, T, D_RNN), F32),
        scratch_shapes=[pltpu.VMEM((8, D_RNN), F32), pltpu.VMEM((8, D_RNN), F32),
                        pltpu.VMEM((L, D_RNN), F32), pltpu.VMEM((L, D_RNN), F32)],
        compiler_params=_cparams(("parallel", "arbitrary"), VMEM_LIMIT),
        name="rglru",
    )(rx, ry, conv_w, conv_b, wa_bd, b_a, wx_bd, b_x, sp)


def _layer_norm(y, g, b):
    mu = jnp.mean(y, axis=-1, keepdims=True)
    d = y - mu
    var = jnp.mean(d * d, axis=-1, keepdims=True)
    return d * lax.rsqrt(var + LN_EPS) * g + b


def _outproj_kernel(at_ref, rn_ref, x_ref, ga_ref, gr_ref, wo_ref, g1_ref, b1_ref, wrh_ref, wrl_ref,
                    br_ref, x1_ref, comb_ref):
    a = at_ref[...]
    rn = rn_ref[...]
    ha = a * lax.rsqrt(jnp.mean(a * a, axis=-1, keepdims=True) + RMS_EPS) * ga_ref[...]
    hr = rn * lax.rsqrt(jnp.mean(rn * rn, axis=-1, keepdims=True) + RMS_EPS) * gr_ref[...]
    heads = jnp.concatenate([ha, hr], axis=1).astype(BF16)
    mix = jnp.dot(heads, wo_ref[...], preferred_element_type=F32)
    x1 = _layer_norm(ALPHA * x_ref[...] + mix, g1_ref[...], b1_ref[...])
    x1_ref[...] = x1

    xh = x1.astype(BF16)
    xl = (x1 - xh.astype(F32)).astype(BF16)
    lg = (jnp.dot(xh, wrh_ref[...], preferred_element_type=F32)
          + jnp.dot(xl, wrh_ref[...], preferred_element_type=F32)
          + jnp.dot(xh, wrl_ref[...], preferred_element_type=F32)) + br_ref[...]
    lane = lax.broadcasted_iota(jnp.int32, lg.shape, 1)
    lane_f = lane.astype(F32)
    big = 1e9
    isg = lane < N_EXP_GROUPS
    gmax = jnp.max(jnp.where(isg, lg, -jnp.inf), axis=-1, keepdims=True)
    pg_top = 1.0 / jnp.sum(jnp.where(isg, jnp.exp(lg - gmax), 0.0), axis=-1, keepdims=True)
    gi = jnp.min(jnp.where(isg & (lg == gmax), lane_f, big), axis=-1, keepdims=True)
    egrp = ((lane - N_EXP_GROUPS) // EXPERTS_PER_GROUP).astype(F32)
    ise = (lane >= N_EXP_GROUPS) & (lane < N_EXP_GROUPS + N_EXPERTS) & (egrp == gi)
    emax = jnp.max(jnp.where(ise, lg, -jnp.inf), axis=-1, keepdims=True)
    i1 = jnp.min(jnp.where(ise & (lg == emax), lane_f, big), axis=-1, keepdims=True)
    rest = ise & (lane_f != i1)
    m2 = jnp.max(jnp.where(rest, lg, -jnp.inf), axis=-1, keepdims=True)
    i2 = jnp.min(jnp.where(rest & (lg == m2), lane_f, big), axis=-1, keepdims=True)
    e2 = jnp.exp(m2 - emax)
    inv = pg_top / (1.0 + e2)
    comb_ref[...] = jnp.where(lane_f == i1, inv, 0.0) + jnp.where(lane_f == i2, inv * e2, 0.0)


def _outproj(attn, rnn, x2, ga, gr, wo, g1, b1, wrh, wrl, br, N):
    tm = 512
    row = lambda w: pl.BlockSpec((tm, w), lambda i: (i, 0))
    full = lambda a: pl.BlockSpec(a.shape, lambda i: (0,) * a.ndim)
    return pl.pallas_call(
        _outproj_kernel,
        grid=(N // tm,),
        in_specs=[row(512), row(512), row(D_MODEL), full(ga), full(gr), full(wo), full(g1), full(b1),
                  full(wrh), full(wrl), full(br)],
        out_specs=[row(D_MODEL), row(128)],
        out_shape=[jax.ShapeDtypeStruct((N, D_MODEL), F32), jax.ShapeDtypeStruct((N, 128), F32)],
        compiler_params=_cparams(("parallel",), VMEM_LIMIT),
        name="outproj",
    )(attn, rnn, x2, ga, gr, wo, g1, b1, wrh, wrl, br)


def _moe_kernel(x1_ref, comb_ref, p_ref, wup_ref, wdn_ref, wg_ref, wp_ref, g2_ref, b2_ref, o_ref,
                acc_ref):
    e = pl.program_id(1)
    x1 = x1_ref[...]
    xb = x1.astype(BF16)

    @pl.when(e == 0)
    def _():
        gate = jax.nn.sigmoid(jnp.dot(xb, wg_ref[...], preferred_element_type=F32))
        ple = gate * jnp.dot(p_ref[...].astype(BF16), wp_ref[...], preferred_element_type=F32)
        acc_ref[...] = ALPHA * x1 + ple

    u = jnp.dot(xb, wup_ref[...], preferred_element_type=F32)
    ua = u[:, :D_FF]
    hsw = (ua * jax.nn.sigmoid(ua) * u[:, D_FF:]).astype(BF16)
    y = jnp.dot(hsw, wdn_ref[...], preferred_element_type=F32)
    comb = comb_ref[...]
    lane = lax.broadcasted_iota(jnp.int32, comb.shape, 1)
    ce = jnp.sum(jnp.where(lane == e + N_EXP_GROUPS, comb, 0.0), axis=-1, keepdims=True)
    acc_ref[...] = acc_ref[...] + ce * y

    @pl.when(e == N_EXPERTS - 1)
    def _():
        o_ref[...] = _layer_norm(acc_ref[...], g2_ref[...], b2_ref[...])


def _moe(x1, comb, p2, wup, wdn, wg, wp, g2, b2, N):
    tm = 1024
    row = lambda w: pl.BlockSpec((tm, w), lambda i, e: (i, 0))
    full = lambda a: pl.BlockSpec(a.shape, lambda i, e: (0,) * a.ndim)
    return pl.pallas_call(
        _moe_kernel,
        grid=(N // tm, N_EXPERTS),
        in_specs=[row(D_MODEL), row(128), row(D_PLE),
                  pl.BlockSpec((None, D_MODEL, 2 * D_FF), lambda i, e: (e, 0, 0)),
                  pl.BlockSpec((None, D_FF, D_MODEL), lambda i, e: (e, 0, 0)),
                  full(wg), full(wp), full(g2), full(b2)],
        out_specs=row(D_MODEL),
        out_shape=jax.ShapeDtypeStruct((N, D_MODEL), F32),
        scratch_shapes=[pltpu.VMEM((tm, D_MODEL), F32)],
        compiler_params=_cparams(("parallel", "arbitrary"), VMEM_LIMIT),
        name="moe",
    )(x1, comb, p2, wup, wdn, wg, wp, g2, b2)


def _block_diag(w):
    n, k, j = w.shape
    eye = jnp.eye(n, dtype=w.dtype)
    return (w[:, :, None, :] * eye[:, None, :, None]).reshape(n * k, n * j)


def kernel(x, p, rel_bias, w_in, cmp_pe_k, cmp_pe_v, cmp_w_k, cmp_w_v, conv_w, conv_b, rg_w_a, rg_b_a,
           rg_w_x, rg_b_x, rg_lambda, attn_out_gain, rnn_out_gain, w_out, ln1_g, ln1_b, router_group_w,
           router_group_b, router_expert_w, router_expert_b, expert_w_up, expert_w_down, ple_w,
           ple_gate_w, ln2_g, ln2_b):
    B, T, _ = x.shape
    N = B * T
    assert T % FAR == 0 and T // SEL_BLOCK <= N_SLC and w_in.shape[0] == 1
    n_cmp = (T - CMP_LEN) // CMP_STRIDE + 1
    n_chunk = T // CMP_STRIDE
    assert C_OFF + n_chunk <= C_PAD
    row1 = lambda v: v.reshape(1, -1)

    w = w_in[0]
    c0 = D_ATTN
    kv = lambda k: w[:, c0 + k * D_KV:c0 + (k + 1) * D_KV]
    g0 = c0 + 6 * D_KV
    wq = w[:, :D_ATTN] * (HEAD_DIM ** -0.5)
    w_gate = jnp.pad(w[:, g0:g0 + 24], ((0, 0), (0, 104)))
    w_main = jnp.concatenate([wq, kv(0), kv(1), kv(3), kv(5), w_gate, w[:, g0 + 24:]], axis=1).astype(BF16)
    w_t = jnp.concatenate([wq, kv(2), kv(4)], axis=1).T.astype(BF16)

    x2 = x.reshape(N, D_MODEL)
    q, kc_raw, vc_raw, vs, vw, gates, rx, ry, t_all = _inproj(x2, w_main, w_t, B, T)

    def chunked(raw):
        r = raw.reshape(B, n_chunk, CMP_STRIDE, N_GROUPS_KV, HEAD_DIM).transpose(0, 3, 1, 2, 4)
        r = r.reshape(B, N_GROUPS_KV, n_chunk, CMP_STRIDE * HEAD_DIM)
        return jnp.pad(r, ((0, 0), (0, 0), (C_OFF, C_PAD - C_OFF - n_chunk), (0, 0)))

    half = CMP_STRIDE * HEAD_DIM
    pek = cmp_pe_k[0].reshape(2, half)
    pev = cmp_pe_v[0].reshape(2, half)
    wk = cmp_w_k[0].reshape(2, half, HEAD_DIM).astype(BF16)
    wvt = cmp_w_v[0].reshape(2, half, HEAD_DIM).transpose(0, 2, 1).astype(BF16)
    kc, vct = _compress(chunked(kc_raw), chunked(vc_raw), pek, pev, wk, wvt, B, n_cmp)

    cidx = jnp.arange(C_PAD) - C_OFF
    sidx = jnp.arange(N_SLC)
    cs = cidx * CMP_STRIDE
    ss = sidx * SEL_BLOCK
    ovt = ((cs[None, :] < ss[:, None] + SEL_BLOCK) & (cs[None, :] + CMP_LEN > ss[:, None])
           & (cidx[None, :] >= 0) & (cidx[None, :] < n_cmp)).astype(BF16)
    onehot_t = (jnp.arange(T)[None, :] // SEL_BLOCK == sidx[:, None]).astype(BF16)

    ks_t = t_all[:, 512:640].reshape(B, N_GROUPS_KV, HEAD_DIM, T)
    kaug = jnp.concatenate([jnp.broadcast_to(onehot_t, (B, N_GROUPS_KV, N_SLC, T)), ks_t], axis=2)
    kaug = jnp.pad(kaug, ((0, 0), (0, 0), (0, 0), (WINDOW, 0)))
    kwt_pad = jnp.pad(t_all[:, 640:768], ((0, 0), (0, 0), (WINDOW, 0)))
    vs_pad = jnp.pad(vs.reshape(B, T, 128), ((0, 0), (WINDOW, 0), (0, 0)))
    vw_pad = jnp.pad(vw.reshape(B, T, 128), ((0, 0), (WINDOW, 0), (0, 0)))
    ds, dw, bc = _bias_tiles(rel_bias)

    attn = _attention(q.reshape(B, T, 512), t_all, gates.reshape(B, T, 128), kc, vct, ovt, kaug,
                      vs_pad, kwt_pad, vw_pad, ds, dw, bc, B, T)

    sp = jax.nn.softplus(-rg_lambda[0].astype(F32))
    rnn = _rglru(rx.reshape(B, T, D_RNN), ry.reshape(B, T, D_RNN), conv_w[0], row1(conv_b[0]),
                 _block_diag(rg_w_a[0]).astype(BF16), row1(rg_b_a[0]),
                 _block_diag(rg_w_x[0]).astype(BF16), row1(rg_b_x[0]), row1(sp), B, T)

    wr = jnp.pad(jnp.concatenate([router_group_w[0], router_expert_w[0]], axis=1), ((0, 0), (0, 108)))
    wrh = wr.astype(BF16)
    wrl = (wr - wrh.astype(F32)).astype(BF16)
    br = jnp.pad(jnp.concatenate([router_group_b[0], router_expert_b[0]]), (0, 108)).reshape(1, 128)
    x1, comb = _outproj(attn.reshape(N, 512), rnn.reshape(N, D_RNN), x2, row1(attn_out_gain[0]),
                        row1(rnn_out_gain[0]), w_out[0].astype(BF16), row1(ln1_g[0]), row1(ln1_b[0]),
                        wrh, wrl, br, N)

    out = _moe(x1, comb, p[0].reshape(N, D_PLE), expert_w_up[0].astype(BF16),
               expert_w_down[0].astype(BF16), ple_gate_w[0].astype(BF16), ple_w[0].astype(BF16),
               row1(ln2_g[0]), row1(ln2_b[0]), N)
    return out.reshape(B, T, D_MODEL)
```

```python
import functools
import math

import jax
import jax.numpy as jnp
from jax import lax
from jax.experimental import pallas as pl
from jax.experimental.pallas import tpu as pltpu

F32 = jnp.float32
BF16 = jnp.bfloat16
NEG = -1e30

D_MODEL = 1024
HEAD_DIM = 64
N_HEADS = 8
N_GROUPS_KV = 2
GQA_REP = 4
D_ATTN = 512
D_RNN = 512
D_KV = 128
CMP_LEN = 32
CMP_STRIDE = 16
SEL_BLOCK = 64
N_SEL = 16
WINDOW = 512
N_BUCKETS = 32
MAX_DISTANCE = 128
N_EXPERTS = 16
EXPERTS_PER_GROUP = 4
N_EXP_GROUPS = 4
D_FF = 512
D_PLE = 256
ALPHA = 2.0 ** 0.25
LN_EPS = 1e-5
RMS_EPS = 1e-6
FORCE_BONUS = 1e4
RG_C = 8.0

TQ = 128
N_SLC = 128
C_PAD = 640
C_OFF = 16
FAR = 512
TAIL = WINDOW + TQ
BAND = 24
GATE_ROWS = 32
V_ROWS = 144
VMEM_LIMIT = 56 * 1024 * 1024


def _cparams(sem, vmem=None):
    return pltpu.CompilerParams(dimension_semantics=sem, vmem_limit_bytes=vmem)


def _inproj_kernel(x_ref, wm_ref, wt_ref, wg_ref, ks_ref, kw_ref, kc_ref, vc_ref, rx_ref, ry_ref,
                   t_ref, gt_ref):
    xb = x_ref[...].astype(BF16)

    def mm(lo, hi):
        return jnp.dot(xb, wm_ref[:, lo:hi], preferred_element_type=F32)

    ks_ref[...] = mm(0, 128).astype(BF16)
    kw_ref[...] = mm(128, 256).astype(BF16)
    kc_ref[...] = mm(256, 384)
    vc_ref[...] = mm(384, 512)
    rx_ref[...] = mm(512, 1024)
    ry_ref[...] = mm(1024, 1536)
    dn = (((1,), (1,)), ((), ()))
    t_ref[...] = lax.dot_general(wt_ref[...], xb, dn, preferred_element_type=F32).astype(BF16)
    gt_ref[...] = jax.nn.sigmoid(lax.dot_general(wg_ref[...], xb, dn, preferred_element_type=F32))


def _inproj(x2, w_main, w_t, w_g, B, T):
    N = B * T
    tm = 512
    nt = T // tm
    row = lambda w: pl.BlockSpec((tm, w), lambda i: (i, 0))
    full = lambda a: pl.BlockSpec(a.shape, lambda i: (0,) * a.ndim)
    tr = lambda r: pl.BlockSpec((None, r, tm), lambda i: (i // nt, 0, i % nt))
    return pl.pallas_call(
        _inproj_kernel,
        grid=(N // tm,),
        in_specs=[row(D_MODEL), full(w_main), full(w_t), full(w_g)],
        out_specs=[row(128), row(128), row(128), row(128), row(512), row(512), tr(768), tr(GATE_ROWS)],
        out_shape=[jax.ShapeDtypeStruct((N, 128), BF16),
                   jax.ShapeDtypeStruct((N, 128), BF16),
                   jax.ShapeDtypeStruct((N, 128), F32),
                   jax.ShapeDtypeStruct((N, 128), F32),
                   jax.ShapeDtypeStruct((N, 512), F32),
                   jax.ShapeDtypeStruct((N, 512), F32),
                   jax.ShapeDtypeStruct((B, 768, T), BF16),
                   jax.ShapeDtypeStruct((B, GATE_ROWS, T), F32)],
        compiler_params=_cparams(("parallel",), VMEM_LIMIT),
        name="inproj",
    )(x2, w_main, w_t, w_g)


def _compress_kernel(xk_ref, xv_ref, pek_ref, pev_ref, wk_ref, wvt_ref, kc_ref, vct_ref, *, n_cmp):
    xk = xk_ref[...]
    lo = jnp.dot((xk + pek_ref[0:1, :]).astype(BF16), wk_ref[0], preferred_element_type=F32)
    hi = jnp.dot((xk + pek_ref[1:2, :]).astype(BF16), wk_ref[1], preferred_element_type=F32)
    hi = pltpu.roll(hi, C_PAD - 1, axis=0)
    rid = lax.broadcasted_iota(jnp.int32, (C_PAD, HEAD_DIM), 0)
    ok = (rid >= C_OFF) & (rid < C_OFF + n_cmp)
    kc_ref[...] = jnp.where(ok, lo + hi, 0.0).astype(BF16)

    xv = xv_ref[...]
    dn = (((1,), (1,)), ((), ()))
    lo_t = lax.dot_general(wvt_ref[0], (xv + pev_ref[0:1, :]).astype(BF16), dn,
                           preferred_element_type=F32)
    hi_t = lax.dot_general(wvt_ref[1], (xv + pev_ref[1:2, :]).astype(BF16), dn,
                           preferred_element_type=F32)
    hi_t = pltpu.roll(hi_t, C_PAD - 1, axis=1)
    cid = lax.broadcasted_iota(jnp.int32, (HEAD_DIM, C_PAD), 1)
    ok_t = (cid >= C_OFF) & (cid < C_OFF + n_cmp)
    vct_ref[...] = jnp.where(ok_t, lo_t + hi_t, 0.0).astype(BF16)


def _compress(xk, xv, pek, pev, wk, wvt, B, n_cmp):
    blk = pl.BlockSpec((None, None, C_PAD, 1024), lambda b, g: (b, g, 0, 0))
    full = lambda a: pl.BlockSpec(a.shape, lambda b, g: (0,) * a.ndim)
    return pl.pallas_call(
        functools.partial(_compress_kernel, n_cmp=n_cmp),
        grid=(B, N_GROUPS_KV),
        in_specs=[blk, blk, full(pek), full(pev), full(wk), full(wvt)],
        out_specs=[pl.BlockSpec((None, None, C_PAD, HEAD_DIM), lambda b, g: (b, g, 0, 0)),
                   pl.BlockSpec((None, HEAD_DIM, C_PAD), lambda b, g: (b, g, 0))],
        out_shape=[jax.ShapeDtypeStruct((B, N_GROUPS_KV, C_PAD, HEAD_DIM), BF16),
                   jax.ShapeDtypeStruct((B, 2 * HEAD_DIM, C_PAD), BF16)],
        compiler_params=_cparams(("parallel", "parallel"), VMEM_LIMIT),
        name="compress",
    )(xk, xv, pek, pev, wk, wvt)


def _bias_kernel(tab_ref, dt_ref, bt_ref, dc_ref, bc_ref, os_ref, ow_ref, oc_ref):
    h = pl.program_id(0) * GQA_REP + pl.program_id(1)

    def lookup(bk):
        out = jnp.zeros(bk.shape, F32)
        for b in range(N_BUCKETS):
            out = jnp.where(bk == b, tab_ref[h, b], out)
        return out - tab_ref[h, N_BUCKETS - 1]

    dt = dt_ref[...]
    v = lookup(bt_ref[...])
    os_ref[...] = jnp.where(dt >= 0, v, NEG)
    ow_ref[...] = jnp.where((dt >= 0) & (dt < WINDOW), v, NEG)
    oc_ref[...] = jnp.where(dc_ref[...] >= 0, lookup(bc_ref[...]), NEG)


def _t5_bucket(dist):
    max_exact = N_BUCKETS // 2
    d = jnp.maximum(dist, 0)
    df = jnp.maximum(d, 1).astype(F32)
    large = max_exact + (jnp.log(df / max_exact) / math.log(MAX_DISTANCE / max_exact)
                         * (N_BUCKETS - max_exact)).astype(jnp.int32)
    large = jnp.minimum(large, N_BUCKETS - 1)
    return jnp.where(d < max_exact, d, large)


def _bias_tiles(rel_bias):
    tq = jnp.arange(TQ, dtype=jnp.int32)
    dt = tq[None, :] + WINDOW - jnp.arange(TAIL, dtype=jnp.int32)[:, None]
    cc = jnp.arange(BAND, dtype=jnp.int32) - C_OFF
    dc = tq[None, :] - (cc[:, None] * CMP_STRIDE + CMP_LEN - 1)
    full = lambda a: pl.BlockSpec(a.shape, lambda g, r: (0,) * a.ndim)
    out = lambda rows: pl.BlockSpec((None, rows, TQ), lambda g, r: (g, 0, r))
    return pl.pallas_call(
        _bias_kernel,
        grid=(N_GROUPS_KV, GQA_REP),
        in_specs=[pl.BlockSpec(memory_space=pltpu.SMEM), full(dt), full(dt), full(dc), full(dc)],
        out_specs=[out(TAIL), out(TAIL), out(BAND)],
        out_shape=[jax.ShapeDtypeStruct((N_GROUPS_KV, TAIL, GQA_REP * TQ), F32),
                   jax.ShapeDtypeStruct((N_GROUPS_KV, TAIL, GQA_REP * TQ), F32),
                   jax.ShapeDtypeStruct((N_GROUPS_KV, BAND, GQA_REP * TQ), F32)],
        compiler_params=_cparams(("arbitrary", "arbitrary")),
        name="biastile",
    )(rel_bias.T, dt, _t5_bucket(dt), dc, _t5_bucket(dc))


def _attn_kernel(qt_ref, gt_ref, kc_ref, vct_ref, kaug_ref, vst_ref, kw_ref, vwt_ref, ds_ref, dw_ref,
                 bc_ref, o_ref, sc_ref, ps_ref, qa_ref, m_ref, acc_ref, sb_ref, *, n_far_max):
    i = pl.program_id(1)
    qs = pl.multiple_of(i * TQ, TQ)
    cols4 = GQA_REP * TQ
    groups = range(N_GROUPS_KV)
    nfar = jnp.maximum(i - 1, 0) // 4
    zeros_q = jnp.zeros((HEAD_DIM, cols4), BF16)
    qpad = []
    for g in groups:
        qgt = jnp.concatenate(
            [qt_ref[(GQA_REP * g + r) * HEAD_DIM:(GQA_REP * g + r + 1) * HEAD_DIM, :]
             for r in range(GQA_REP)], axis=1)
        qpad.append(jnp.concatenate([qgt, zeros_q] if g == 0 else [zeros_q, qgt], axis=0))

    grp = lax.broadcasted_iota(jnp.int32, (128, cols4), 0)
    gmask = jnp.where((grp < C_OFF // 8) | (grp >= i + BAND // 8), NEG, 0.0).astype(BF16)
    band0 = pl.multiple_of(i * 8, 8)
    tq_col = lax.broadcasted_iota(jnp.int32, (1, cols4), 1) % TQ
    has_cmp = qs + tq_col >= CMP_LEN - 1
    o_c = []
    for g in groups:
        sc_ref[g] = jnp.dot(kc_ref[...], jnp.concatenate([qpad[g], gmask], axis=0),
                            preferred_element_type=F32)
        sc_ref[g, pl.ds(band0, BAND), :] = sc_ref[g, pl.ds(band0, BAND), :] + bc_ref[g]
    for g in groups:
        s = sc_ref[g]
        e = jnp.exp(s - jnp.max(s, axis=0, keepdims=True))
        pinv = jnp.where(has_cmp, 1.0 / jnp.sum(e, axis=0, keepdims=True), 0.0)
        p = e * pinv
        o_c.append(jnp.dot(vct_ref[...], p.astype(BF16), preferred_element_type=F32))
        ps_ref[g] = p[:, 0:TQ] + p[:, TQ:2 * TQ] + p[:, 2 * TQ:3 * TQ] + p[:, 3 * TQ:4 * TQ]

    sidx = lax.broadcasted_iota(jnp.int32, (N_SLC, TQ), 0)
    sidx_f = sidx.astype(F32)
    cur = (qs + lax.broadcasted_iota(jnp.int32, (N_SLC, TQ), 1)) // SEL_BLOCK
    forced = (sidx == 0) | (sidx == cur) | (sidx == cur - 1)
    for g in groups:
        imp = ps_ref[g, pl.ds(C_OFF - 1, N_SLC, stride=4), :]
        for d in range(1, 5):
            imp = imp + ps_ref[g, pl.ds(C_OFF - 1 + d, N_SLC, stride=4), :]
        score = jnp.where(forced, -jnp.inf, jnp.where(sidx <= cur, imp, NEG))
        mbt = jnp.where(forced, 0.0, NEG)
        for _ in range(N_SEL - 3):
            cm = jnp.max(score, axis=0, keepdims=True)
            first = jnp.min(jnp.where(score == cm, sidx_f, float(N_SLC)), axis=0, keepdims=True)
            pick = sidx_f == first
            mbt = jnp.where(pick, 0.0, mbt)
            score = jnp.where(pick, -jnp.inf, score)
        mbt = mbt.astype(BF16)
        qa_ref[g] = jnp.concatenate([jnp.concatenate([mbt] * GQA_REP, axis=1), qpad[g]], axis=0)
        m_ref[g] = jnp.full((1, cols4), NEG, F32)
        acc_ref[g] = jnp.zeros((V_ROWS, cols4), F32)

    def chunk_start(c):
        return pl.multiple_of(WINDOW + jnp.minimum(c, n_far_max - 1) * FAR, FAR)

    def scores(c, par):
        kch = kaug_ref[pl.ds(chunk_start(c), FAR), :]
        for g in groups:
            sb_ref[2 * g + par] = jnp.dot(kch, qa_ref[g], preferred_element_type=F32)

    def consume(c, par):
        vch = vst_ref[:, pl.ds(chunk_start(c), FAR)]
        live = c < nfar
        for g in groups:
            sf = sb_ref[2 * g + par]
            mprev = m_ref[g]
            mcand = jnp.maximum(mprev, jnp.max(sf, axis=0, keepdims=True))
            mnew = jnp.where(live, mcand, mprev)
            pf = jnp.exp(sf - jnp.where(live, mcand, -NEG)).astype(BF16)
            acc_ref[g] = jnp.exp(mprev - mnew) * acc_ref[g] + jnp.dot(
                vch, pf, preferred_element_type=F32)
            m_ref[g] = mnew

    def far_pair(kk, carry):
        c = 2 * kk
        scores(c + 1, 1)
        consume(c, 0)
        scores(c + 2, 0)
        consume(c + 1, 1)
        return carry

    scores(0, 0)
    lax.fori_loop(0, (nfar + 1) // 2, far_pair, 0)

    jrow = lax.broadcasted_iota(jnp.int32, (TAIL, cols4), 0)
    kt = kaug_ref[pl.ds(qs, TAIL), :]
    vt = vst_ref[:, pl.ds(qs, TAIL)]
    kwc = kw_ref[pl.ds(qs, TAIL), :]
    vwc = vwt_ref[:, pl.ds(qs, TAIL)]
    in_tail = jrow >= nfar * FAR + WINDOW - qs
    in_seq = jrow >= WINDOW - qs
    out_rows = []
    for g in groups:
        st = jnp.dot(kt, qa_ref[g], preferred_element_type=F32) + ds_ref[g]
        st = jnp.where(in_tail, st, NEG)
        mprev = m_ref[g]
        mnew = jnp.maximum(mprev, jnp.max(st, axis=0, keepdims=True))
        pt = jnp.exp(st - mnew).astype(BF16)
        acc_s = jnp.exp(mprev - mnew) * acc_ref[g] + jnp.dot(vt, pt, preferred_element_type=F32)
        sw = jnp.dot(kwc, qpad[g], preferred_element_type=F32) + dw_ref[g]
        sw = jnp.where(in_seq, sw, NEG)
        pw = jnp.exp(sw - jnp.max(sw, axis=0, keepdims=True)).astype(BF16)
        acc_w = jnp.dot(vwc, pw, preferred_element_type=F32)
        dsl = slice(g * HEAD_DIM, (g + 1) * HEAD_DIM)
        o_s = acc_s[dsl, :] * (1.0 / acc_s[2 * HEAD_DIM:2 * HEAD_DIM + 1, :])
        o_w = acc_w[dsl, :] * (1.0 / acc_w[2 * HEAD_DIM:2 * HEAD_DIM + 1, :])
        for r in range(GQA_REP):
            cs = slice(r * TQ, (r + 1) * TQ)
            gc = 3 * (GQA_REP * g + r)
            out_rows.append(gt_ref[gc:gc + 1, :] * o_c[g][dsl, cs] + gt_ref[gc + 1:gc + 2, :] * o_s[:, cs]
                            + gt_ref[gc + 2:gc + 3, :] * o_w[:, cs])
    o_ref[...] = jnp.concatenate(out_rows, axis=0).T


def _attention(t_all, gates_t, kc2, vct, kaug, vst, kw_pad, vwt, ds, dw, bc, B, T):
    tp = T + WINDOW
    per_b = lambda shape: pl.BlockSpec((None,) + shape, lambda b, i: (b,) + (0,) * len(shape))
    const = lambda a: pl.BlockSpec(a.shape, lambda b, i: (0,) * a.ndim)
    cols4 = GQA_REP * TQ
    return pl.pallas_call(
        functools.partial(_attn_kernel, n_far_max=T // FAR),
        grid=(B, T // TQ),
        in_specs=[pl.BlockSpec((None, 512, TQ), lambda b, i: (b, 0, i)),
                  pl.BlockSpec((None, GATE_ROWS, TQ), lambda b, i: (b, 0, i)),
                  per_b((C_PAD, 256)),
                  per_b((128, C_PAD)),
                  per_b((tp, 256)),
                  per_b((V_ROWS, tp)),
                  per_b((tp, 128)),
                  per_b((V_ROWS, tp)),
                  const(ds), const(dw), const(bc)],
        out_specs=pl.BlockSpec((None, TQ, 512), lambda b, i: (b, i, 0)),
        out_shape=jax.ShapeDtypeStruct((B, T, 512), F32),
        scratch_shapes=[pltpu.VMEM((N_GROUPS_KV, C_PAD, cols4), F32),
                        pltpu.VMEM((N_GROUPS_KV, C_PAD, TQ), F32),
                        pltpu.VMEM((N_GROUPS_KV, 256, cols4), BF16),
                        pltpu.VMEM((N_GROUPS_KV, 1, cols4), F32),
                        pltpu.VMEM((N_GROUPS_KV, V_ROWS, cols4), F32),
                        pltpu.VMEM((2 * N_GROUPS_KV, FAR, cols4), F32)],
        compiler_params=_cparams(("parallel", "arbitrary"), VMEM_LIMIT),
        name="attn",
    )(t_all, gates_t, kc2, vct, kaug, vst, kw_pad, vwt, ds, dw, bc)


def _rglru_kernel(rx_ref, ry_ref, cw_ref, cb_ref, wa_ref, ba_ref, wx_ref, bx_ref, sp_ref, o_ref,
                  xprev_ref, h_ref, a_sc, b_sc, *, L):
    @pl.when(pl.program_id(1) == 0)
    def _():
        xprev_ref[...] = jnp.zeros(xprev_ref.shape, F32)
        h_ref[...] = jnp.zeros(h_ref.shape, F32)

    x = rx_ref[...]
    xe = jnp.concatenate([xprev_ref[...], x], axis=0)
    xc = cb_ref[...] + cw_ref[0:1, :] * xe[5:5 + L]
    for j in range(1, 4):
        xc = xc + cw_ref[j:j + 1, :] * xe[5 + j:5 + j + L]
    xprev_ref[...] = x[L - 8:L]
    xcb = xc.astype(BF16)
    r = jax.nn.sigmoid(jnp.dot(xcb, wa_ref[...], preferred_element_type=F32) + ba_ref[...])
    ig = jax.nn.sigmoid(jnp.dot(xcb, wx_ref[...], preferred_element_type=F32) + bx_ref[...])
    log_a = -RG_C * r * sp_ref[...]
    a = jnp.exp(log_a)
    bt = jnp.sqrt(1.0 - jnp.exp(2.0 * log_a)) * (ig * xc)

    rid = lax.broadcasted_iota(jnp.int32, (L, D_RNN), 0) % 8
    for sh in (1, 2, 4):
        a_s = pltpu.roll(a, sh, axis=0)
        b_s = pltpu.roll(bt, sh, axis=0)
        msk = rid >= sh
        bt = jnp.where(msk, a * b_s + bt, bt)
        a = jnp.where(msk, a * a_s, a)
    a_sc[...] = a
    b_sc[...] = bt

    def body(gi, hprev):
        r0 = pl.multiple_of(gi * 8, 8)
        h = a_sc[pl.ds(r0, 8), :] * hprev + b_sc[pl.ds(r0, 8), :]
        b_sc[pl.ds(r0, 8), :] = h
        return jnp.broadcast_to(h[7:8, :], (8, D_RNN))

    h_ref[...] = lax.fori_loop(0, L // 8, body, h_ref[...])
    y = ry_ref[...]
    cdf = 0.5 * (1.0 + jnp.tanh(math.sqrt(2.0 / math.pi) * (y + 0.044715 * (y * y * y))))
    o_ref[...] = b_sc[...] * (y * cdf)


def _rglru(rx, ry, conv_w, conv_b, wa_bd, b_a, wx_bd, b_x, sp, B, T):
    L = 512
    blk = pl.BlockSpec((None, L, D_RNN), lambda b, t: (b, t, 0))
    full = lambda a: pl.BlockSpec(a.shape, lambda b, t: (0,) * a.ndim)
    return pl.pallas_call(
        functools.partial(_rglru_kernel, L=L),
        grid=(B, T // L),
        in_specs=[blk, blk, full(conv_w), full(conv_b), full(wa_bd), full(b_a), full(wx_bd),
                  full(b_x), full(sp)],
        out_specs=blk,
        out_shape=jax.ShapeDtypeStruct((B, T, D_RNN), F32),
        scratch_shapes=[pltpu.VMEM((8, D_RNN), F32), pltpu.VMEM((8, D_RNN), F32),
                        pltpu.VMEM((L, D_RNN), F32), pltpu.VMEM((L, D_RNN), F32)],
        compiler_params=_cparams(("parallel", "arbitrary"), VMEM_LIMIT),
        name="rglru",
    )(rx, ry, conv_w, conv_b, wa_bd, b_a, wx_bd, b_x, sp)


def _layer_norm(y, g, b):
    mu = jnp.mean(y, axis=-1, keepdims=True)
    d = y - mu
    var = jnp.mean(d * d, axis=-1, keepdims=True)
    return d * lax.rsqrt(var + LN_EPS) * g + b


def _outproj_kernel(at_ref, rn_ref, x_ref, ga_ref, gr_ref, wo_ref, g1_ref, b1_ref, wrh_ref, wrl_ref,
                    br_ref, x1_ref, comb_ref):
    a = at_ref[...]
    rn = rn_ref[...]
    ha = a * lax.rsqrt(jnp.mean(a * a, axis=-1, keepdims=True) + RMS_EPS) * ga_ref[...]
    hr = rn * lax.rsqrt(jnp.mean(rn * rn, axis=-1, keepdims=True) + RMS_EPS) * gr_ref[...]
    heads = jnp.concatenate([ha, hr], axis=1).astype(BF16)
    mix = jnp.dot(heads, wo_ref[...], preferred_element_type=F32)
    x1 = _layer_norm(ALPHA * x_ref[...] + mix, g1_ref[...], b1_ref[...])
    x1_ref[...] = x1

    xh = x1.astype(BF16)
    xl = (x1 - xh.astype(F32)).astype(BF16)
    lg = (jnp.dot(xh, wrh_ref[...], preferred_element_type=F32)
          + jnp.dot(xl, wrh_ref[...], preferred_element_type=F32)
          + jnp.dot(xh, wrl_ref[...], preferred_element_type=F32)) + br_ref[...]
    lane = lax.broadcasted_iota(jnp.int32, lg.shape, 1)
    lane_f = lane.astype(F32)
    big = 1e9
    isg = lane < N_EXP_GROUPS
    gmax = jnp.max(jnp.where(isg, lg, -jnp.inf), axis=-1, keepdims=True)
    pg_top = 1.0 / jnp.sum(jnp.where(isg, jnp.exp(lg - gmax), 0.0), axis=-1, keepdims=True)
    gi = jnp.min(jnp.where(isg & (lg == gmax), lane_f, big), axis=-1, keepdims=True)
    egrp = ((lane - N_EXP_GROUPS) // EXPERTS_PER_GROUP).astype(F32)
    ise = (lane >= N_EXP_GROUPS) & (lane < N_EXP_GROUPS + N_EXPERTS) & (egrp == gi)
    emax = jnp.max(jnp.where(ise, lg, -jnp.inf), axis=-1, keepdims=True)
    i1 = jnp.min(jnp.where(ise & (lg == emax), lane_f, big), axis=-1, keepdims=True)
    rest = ise & (lane_f != i1)
    m2 = jnp.max(jnp.where(rest, lg, -jnp.inf), axis=-1, keepdims=True)
    i2 = jnp.min(jnp.where(rest & (lg == m2), lane_f, big), axis=-1, keepdims=True)
    e2 = jnp.exp(m2 - emax)
    inv = pg_top / (1.0 + e2)
    comb_ref[...] = jnp.where(lane_f == i1, inv, 0.0) + jnp.where(lane_f == i2, inv * e2, 0.0)


def _outproj(attn, rnn, x2, ga, gr, wo, g1, b1, wrh, wrl, br, N):
    tm = 512
    row = lambda w: pl.BlockSpec((tm, w), lambda i: (i, 0))
    full = lambda a: pl.BlockSpec(a.shape, lambda i: (0,) * a.ndim)
    return pl.pallas_call(
        _outproj_kernel,
        grid=(N // tm,),
        in_specs=[row(512), row(512), row(D_MODEL), full(ga), full(gr), full(wo), full(g1), full(b1),
                  full(wrh), full(wrl), full(br)],
        out_specs=[row(D_MODEL), row(128)],
        out_shape=[jax.ShapeDtypeStruct((N, D_MODEL), F32), jax.ShapeDtypeStruct((N, 128), F32)],
        compiler_params=_cparams(("parallel",), VMEM_LIMIT),
        name="outproj",
    )(attn, rnn, x2, ga, gr, wo, g1, b1, wrh, wrl, br)


def _moe_kernel(x1_ref, comb_ref, p_ref, wup_ref, wdn_ref, wg_ref, wp_ref, g2_ref, b2_ref, o_ref,
                acc_ref):
    e = pl.program_id(1)
    x1 = x1_ref[...]
    xb = x1.astype(BF16)

    @pl.when(e == 0)
    def _():
        gate = jax.nn.sigmoid(jnp.dot(xb, wg_ref[...], preferred_element_type=F32))
        ple = gate * jnp.dot(p_ref[...].astype(BF16), wp_ref[...], preferred_element_type=F32)
        acc_ref[...] = ALPHA * x1 + ple

    u = jnp.dot(xb, wup_ref[...], preferred_element_type=F32)
    ua = u[:, :D_FF]
    hsw = (ua * jax.nn.sigmoid(ua) * u[:, D_FF:]).astype(BF16)
    y = jnp.dot(hsw, wdn_ref[...], preferred_element_type=F32)
    comb = comb_ref[...]
    lane = lax.broadcasted_iota(jnp.int32, comb.shape, 1)
    ce = jnp.sum(jnp.where(lane == e + N_EXP_GROUPS, comb, 0.0), axis=-1, keepdims=True)
    acc_ref[...] = acc_ref[...] + ce * y

    @pl.when(e == N_EXPERTS - 1)
    def _():
        o_ref[...] = _layer_norm(acc_ref[...], g2_ref[...], b2_ref[...])


def _moe(x1, comb, p2, wup, wdn, wg, wp, g2, b2, N):
    tm = 1024
    row = lambda w: pl.BlockSpec((tm, w), lambda i, e: (i, 0))
    full = lambda a: pl.BlockSpec(a.shape, lambda i, e: (0,) * a.ndim)
    return pl.pallas_call(
        _moe_kernel,
        grid=(N // tm, N_EXPERTS),
        in_specs=[row(D_MODEL), row(128), row(D_PLE),
                  pl.BlockSpec((None, D_MODEL, 2 * D_FF), lambda i, e: (e, 0, 0)),
                  pl.BlockSpec((None, D_FF, D_MODEL), lambda i, e: (e, 0, 0)),
                  full(wg), full(wp), full(g2), full(b2)],
        out_specs=row(D_MODEL),
        out_shape=jax.ShapeDtypeStruct((N, D_MODEL), F32),
        scratch_shapes=[pltpu.VMEM((tm, D_MODEL), F32)],
        compiler_params=_cparams(("parallel", "arbitrary"), VMEM_LIMIT),
        name="moe",
    )(x1, comb, p2, wup, wdn, wg, wp, g2, b2)


def _block_diag(w):
    n, k, j = w.shape
    eye = jnp.eye(n, dtype=w.dtype)
    return (w[:, :, None, :] * eye[:, None, :, None]).reshape(n * k, n * j)


def kernel(x, p, rel_bias, w_in, cmp_pe_k, cmp_pe_v, cmp_w_k, cmp_w_v, conv_w, conv_b, rg_w_a, rg_b_a,
           rg_w_x, rg_b_x, rg_lambda, attn_out_gain, rnn_out_gain, w_out, ln1_g, ln1_b, router_group_w,
           router_group_b, router_expert_w, router_expert_b, expert_w_up, expert_w_down, ple_w,
           ple_gate_w, ln2_g, ln2_b):
    B, T, _ = x.shape
    N = B * T
    assert T % FAR == 0 and T // SEL_BLOCK <= N_SLC and w_in.shape[0] == 1
    n_cmp = (T - CMP_LEN) // CMP_STRIDE + 1
    n_chunk = T // CMP_STRIDE
    assert C_OFF + n_chunk <= C_PAD
    row1 = lambda v: v.reshape(1, -1)

    w = w_in[0]
    c0 = D_ATTN
    kv = lambda k: w[:, c0 + k * D_KV:c0 + (k + 1) * D_KV]
    g0 = c0 + 6 * D_KV
    wq = w[:, :D_ATTN] * (HEAD_DIM ** -0.5)
    w_main = jnp.concatenate([kv(2), kv(4), kv(0), kv(1), w[:, g0 + 24:]], axis=1).astype(BF16)
    w_t = jnp.concatenate([wq, kv(3), kv(5)], axis=1).T.astype(BF16)
    w_g = jnp.pad(w[:, g0:g0 + 24], ((0, 0), (0, GATE_ROWS - 24))).T.astype(BF16)

    x2 = x.reshape(N, D_MODEL)
    ks, kw, kc_raw, vc_raw, rx, ry, t_all, gates_t = _inproj(x2, w_main, w_t, w_g, B, T)

    def chunked(raw):
        r = raw.reshape(B, n_chunk, CMP_STRIDE, N_GROUPS_KV, HEAD_DIM).transpose(0, 3, 1, 2, 4)
        r = r.reshape(B, N_GROUPS_KV, n_chunk, CMP_STRIDE * HEAD_DIM)
        return jnp.pad(r, ((0, 0), (0, 0), (C_OFF, C_PAD - C_OFF - n_chunk), (0, 0)))

    hw = CMP_STRIDE * HEAD_DIM
    pek = cmp_pe_k[0].reshape(2, hw)
    pev = cmp_pe_v[0].reshape(2, hw)
    wk = cmp_w_k[0].reshape(2, hw, HEAD_DIM).astype(BF16)
    wvt = cmp_w_v[0].reshape(2, hw, HEAD_DIM).transpose(0, 2, 1).astype(BF16)
    kc, vct = _compress(chunked(kc_raw), chunked(vc_raw), pek, pev, wk, wvt, B, n_cmp)
    grp_onehot = (jnp.arange(C_PAD)[:, None] // 8 == jnp.arange(128)[None, :]).astype(BF16)
    kc2 = jnp.concatenate([kc.transpose(0, 2, 1, 3).reshape(B, C_PAD, 2 * HEAD_DIM),
                           jnp.broadcast_to(grp_onehot, (B, C_PAD, 128))], axis=2)

    onehot = (jnp.arange(T)[:, None] // SEL_BLOCK == jnp.arange(N_SLC)[None, :]).astype(BF16)
    kaug = jnp.concatenate([jnp.broadcast_to(onehot, (B, T, N_SLC)), ks.reshape(B, T, 128)], axis=2)
    kaug = jnp.pad(kaug, ((0, 0), (WINDOW, 0), (0, 0)))
    kw_pad = jnp.pad(kw.reshape(B, T, 128), ((0, 0), (WINDOW, 0), (0, 0)))
    ones_rows = jnp.concatenate([jnp.ones((B, 1, T), BF16), jnp.zeros((B, V_ROWS - 129, T), BF16)], axis=1)
    vst = jnp.pad(jnp.concatenate([t_all[:, 512:640], ones_rows], axis=1), ((0, 0), (0, 0), (WINDOW, 0)))
    vwt = jnp.pad(jnp.concatenate([t_all[:, 640:768], ones_rows], axis=1), ((0, 0), (0, 0), (WINDOW, 0)))
    ds, dw, bc = _bias_tiles(rel_bias)

    attn = _attention(t_all, gates_t, kc2, vct, kaug, vst, kw_pad, vwt, ds, dw, bc, B, T)

    sp = jax.nn.softplus(-rg_lambda[0].astype(F32))
    rnn = _rglru(rx.reshape(B, T, D_RNN), ry.reshape(B, T, D_RNN), conv_w[0], row1(conv_b[0]),
                 _block_diag(rg_w_a[0]).astype(BF16), row1(rg_b_a[0]),
                 _block_diag(rg_w_x[0]).astype(BF16), row1(rg_b_x[0]), row1(sp), B, T)

    wr = jnp.pad(jnp.concatenate([router_group_w[0], router_expert_w[0]], axis=1), ((0, 0), (0, 108)))
    wrh = wr.astype(BF16)
    wrl = (wr - wrh.astype(F32)).astype(BF16)
    br = jnp.pad(jnp.concatenate([router_group_b[0], router_expert_b[0]]), (0, 108)).reshape(1, 128)
    x1, comb = _outproj(attn.reshape(N, 512), rnn.reshape(N, D_RNN), x2, row1(attn_out_gain[0]),
                        row1(rnn_out_gain[0]), w_out[0].astype(BF16), row1(ln1_g[0]), row1(ln1_b[0]),
                        wrh, wrl, br, N)

    out = _moe(x1, comb, p[0].reshape(N, D_PLE), expert_w_up[0].astype(BF16),
               expert_w_down[0].astype(BF16), ple_gate_w[0].astype(BF16), ple_w[0].astype(BF16),
               row1(ln2_g[0]), row1(ln2_b[0]), N)
    return out.reshape(B, T, D_MODEL)
```

```python
import functools
import math

import jax
import jax.numpy as jnp
from jax import lax
from jax.experimental import pallas as pl
from jax.experimental.pallas import tpu as pltpu

F32 = jnp.float32
BF16 = jnp.bfloat16
NEG = -1e30

D_MODEL = 1024
HEAD_DIM = 64
N_HEADS = 8
N_GROUPS_KV = 2
GQA_REP = 4
D_ATTN = 512
D_RNN = 512
D_KV = 128
CMP_LEN = 32
CMP_STRIDE = 16
SEL_BLOCK = 64
N_SEL = 16
WINDOW = 512
N_BUCKETS = 32
MAX_DISTANCE = 128
N_EXPERTS = 16
EXPERTS_PER_GROUP = 4
N_EXP_GROUPS = 4
D_FF = 512
D_PLE = 256
ALPHA = 2.0 ** 0.25
LN_EPS = 1e-5
RMS_EPS = 1e-6
FORCE_BONUS = 1e4
RG_C = 8.0

TQ = 128
N_SLC = 128
C_PAD = 640
C_OFF = 16
FAR = 512
TAIL = WINDOW + TQ
BAND = 24
GATE_ROWS = 32
V_ROWS = 144
MOE_TILE = 1024
MOE_SUB = 512
MOE_SLOTS = 128
VMEM_LIMIT = 56 * 1024 * 1024


def _cparams(sem, vmem=None):
    return pltpu.CompilerParams(dimension_semantics=sem, vmem_limit_bytes=vmem)


def _inproj_kernel(x_ref, wm_ref, wt_ref, wg_ref, ks_ref, kw_ref, kc_ref, vc_ref, rx_ref, ry_ref,
                   t_ref, gt_ref):
    xb = x_ref[...].astype(BF16)

    def mm(lo, hi):
        return jnp.dot(xb, wm_ref[:, lo:hi], preferred_element_type=F32)

    ks_ref[...] = mm(0, 128).astype(BF16)
    kw_ref[...] = mm(128, 256).astype(BF16)
    kc_ref[...] = mm(256, 384)
    vc_ref[...] = mm(384, 512)
    rx_ref[...] = mm(512, 1024)
    ry_ref[...] = mm(1024, 1536)
    dn = (((1,), (1,)), ((), ()))
    t_ref[...] = lax.dot_general(wt_ref[...], xb, dn, preferred_element_type=F32).astype(BF16)
    gt_ref[...] = jax.nn.sigmoid(lax.dot_general(wg_ref[...], xb, dn, preferred_element_type=F32))


def _inproj(x2, w_main, w_t, w_g, B, T):
    N = B * T
    tm = 512
    nt = T // tm
    row = lambda w: pl.BlockSpec((tm, w), lambda i: (i, 0))
    full = lambda a: pl.BlockSpec(a.shape, lambda i: (0,) * a.ndim)
    tr = lambda r: pl.BlockSpec((None, r, tm), lambda i: (i // nt, 0, i % nt))
    return pl.pallas_call(
        _inproj_kernel,
        grid=(N // tm,),
        in_specs=[row(D_MODEL), full(w_main), full(w_t), full(w_g)],
        out_specs=[row(128), row(128), row(128), row(128), row(512), row(512), tr(768), tr(GATE_ROWS)],
        out_shape=[jax.ShapeDtypeStruct((N, 128), BF16),
                   jax.ShapeDtypeStruct((N, 128), BF16),
                   jax.ShapeDtypeStruct((N, 128), F32),
                   jax.ShapeDtypeStruct((N, 128), F32),
                   jax.ShapeDtypeStruct((N, 512), F32),
                   jax.ShapeDtypeStruct((N, 512), F32),
                   jax.ShapeDtypeStruct((B, 768, T), BF16),
                   jax.ShapeDtypeStruct((B, GATE_ROWS, T), F32)],
        compiler_params=_cparams(("parallel",), VMEM_LIMIT),
        name="inproj",
    )(x2, w_main, w_t, w_g)


def _compress_kernel(xk_ref, xv_ref, pek_ref, pev_ref, wk_ref, wvt_ref, kc_ref, vct_ref, *, n_cmp):
    xk = xk_ref[...]
    lo = jnp.dot((xk + pek_ref[0:1, :]).astype(BF16), wk_ref[0], preferred_element_type=F32)
    hi = jnp.dot((xk + pek_ref[1:2, :]).astype(BF16), wk_ref[1], preferred_element_type=F32)
    hi = pltpu.roll(hi, C_PAD - 1, axis=0)
    rid = lax.broadcasted_iota(jnp.int32, (C_PAD, HEAD_DIM), 0)
    ok = (rid >= C_OFF) & (rid < C_OFF + n_cmp)
    kc_ref[...] = jnp.where(ok, lo + hi, 0.0).astype(BF16)

    xv = xv_ref[...]
    dn = (((1,), (1,)), ((), ()))
    lo_t = lax.dot_general(wvt_ref[0], (xv + pev_ref[0:1, :]).astype(BF16), dn,
                           preferred_element_type=F32)
    hi_t = lax.dot_general(wvt_ref[1], (xv + pev_ref[1:2, :]).astype(BF16), dn,
                           preferred_element_type=F32)
    hi_t = pltpu.roll(hi_t, C_PAD - 1, axis=1)
    cid = lax.broadcasted_iota(jnp.int32, (HEAD_DIM, C_PAD), 1)
    ok_t = (cid >= C_OFF) & (cid < C_OFF + n_cmp)
    vct_ref[...] = jnp.where(ok_t, lo_t + hi_t, 0.0).astype(BF16)


def _compress(xk, xv, pek, pev, wk, wvt, B, n_cmp):
    blk = pl.BlockSpec((None, None, C_PAD, 1024), lambda b, g: (b, g, 0, 0))
    full = lambda a: pl.BlockSpec(a.shape, lambda b, g: (0,) * a.ndim)
    return pl.pallas_call(
        functools.partial(_compress_kernel, n_cmp=n_cmp),
        grid=(B, N_GROUPS_KV),
        in_specs=[blk, blk, full(pek), full(pev), full(wk), full(wvt)],
        out_specs=[pl.BlockSpec((None, None, C_PAD, HEAD_DIM), lambda b, g: (b, g, 0, 0)),
                   pl.BlockSpec((None, HEAD_DIM, C_PAD), lambda b, g: (b, g, 0))],
        out_shape=[jax.ShapeDtypeStruct((B, N_GROUPS_KV, C_PAD, HEAD_DIM), BF16),
                   jax.ShapeDtypeStruct((B, 2 * HEAD_DIM, C_PAD), BF16)],
        compiler_params=_cparams(("parallel", "parallel"), VMEM_LIMIT),
        name="compress",
    )(xk, xv, pek, pev, wk, wvt)


def _bias_kernel(tab_ref, dt_ref, bt_ref, dc_ref, bc_ref, os_ref, ow_ref, oc_ref):
    h = pl.program_id(0) * GQA_REP + pl.program_id(1)

    def lookup(bk):
        out = jnp.zeros(bk.shape, F32)
        for b in range(N_BUCKETS):
            out = jnp.where(bk == b, tab_ref[h, b], out)
        return out - tab_ref[h, N_BUCKETS - 1]

    dt = dt_ref[...]
    v = lookup(bt_ref[...])
    os_ref[...] = jnp.where(dt >= 0, v, NEG)
    ow_ref[...] = jnp.where((dt >= 0) & (dt < WINDOW), v, NEG)
    oc_ref[...] = jnp.where(dc_ref[...] >= 0, lookup(bc_ref[...]), NEG)


def _t5_bucket(dist):
    max_exact = N_BUCKETS // 2
    d = jnp.maximum(dist, 0)
    df = jnp.maximum(d, 1).astype(F32)
    large = max_exact + (jnp.log(df / max_exact) / math.log(MAX_DISTANCE / max_exact)
                         * (N_BUCKETS - max_exact)).astype(jnp.int32)
    large = jnp.minimum(large, N_BUCKETS - 1)
    return jnp.where(d < max_exact, d, large)


def _bias_tiles(rel_bias):
    tq = jnp.arange(TQ, dtype=jnp.int32)
    dt = tq[None, :] + WINDOW - jnp.arange(TAIL, dtype=jnp.int32)[:, None]
    cc = jnp.arange(BAND, dtype=jnp.int32) - C_OFF
    dc = tq[None, :] - (cc[:, None] * CMP_STRIDE + CMP_LEN - 1)
    full = lambda a: pl.BlockSpec(a.shape, lambda g, r: (0,) * a.ndim)
    out = lambda rows: pl.BlockSpec((None, rows, TQ), lambda g, r: (g, 0, r))
    return pl.pallas_call(
        _bias_kernel,
        grid=(N_GROUPS_KV, GQA_REP),
        in_specs=[pl.BlockSpec(memory_space=pltpu.SMEM), full(dt), full(dt), full(dc), full(dc)],
        out_specs=[out(TAIL), out(TAIL), out(BAND)],
        out_shape=[jax.ShapeDtypeStruct((N_GROUPS_KV, TAIL, GQA_REP * TQ), F32),
                   jax.ShapeDtypeStruct((N_GROUPS_KV, TAIL, GQA_REP * TQ), F32),
                   jax.ShapeDtypeStruct((N_GROUPS_KV, BAND, GQA_REP * TQ), F32)],
        compiler_params=_cparams(("arbitrary", "arbitrary")),
        name="biastile",
    )(rel_bias.T, dt, _t5_bucket(dt), dc, _t5_bucket(dc))


def _attn_kernel(qt_ref, gt_ref, kc_ref, vct_ref, kaug_ref, vst_ref, kw_ref, vwt_ref, ds_ref, dw_ref,
                 bc_ref, o_ref, sc_ref, ps_ref, qa_ref, m_ref, acc_ref, sb_ref, *, n_far_max):
    i = pl.program_id(1)
    qs = pl.multiple_of(i * TQ, TQ)
    cols4 = GQA_REP * TQ
    groups = range(N_GROUPS_KV)
    nfar = jnp.maximum(i - 1, 0) // 4
    zeros_q = jnp.zeros((HEAD_DIM, cols4), BF16)
    qpad = []
    for g in groups:
        qgt = jnp.concatenate(
            [qt_ref[(GQA_REP * g + r) * HEAD_DIM:(GQA_REP * g + r + 1) * HEAD_DIM, :]
             for r in range(GQA_REP)], axis=1)
        qpad.append(jnp.concatenate([qgt, zeros_q] if g == 0 else [zeros_q, qgt], axis=0))

    grp = lax.broadcasted_iota(jnp.int32, (128, cols4), 0)
    gmask = jnp.where((grp < C_OFF // 8) | (grp >= i + BAND // 8), NEG, 0.0).astype(BF16)
    band0 = pl.multiple_of(i * 8, 8)
    tq_col = lax.broadcasted_iota(jnp.int32, (1, cols4), 1) % TQ
    has_cmp = qs + tq_col >= CMP_LEN - 1
    o_c = []
    for g in groups:
        sc_ref[g] = jnp.dot(kc_ref[...], jnp.concatenate([qpad[g], gmask], axis=0),
                            preferred_element_type=F32)
        sc_ref[g, pl.ds(band0, BAND), :] = sc_ref[g, pl.ds(band0, BAND), :] + bc_ref[g]
    for g in groups:
        s = sc_ref[g]
        e = jnp.exp(s - jnp.max(s, axis=0, keepdims=True))
        pinv = jnp.where(has_cmp, 1.0 / jnp.sum(e, axis=0, keepdims=True), 0.0)
        p = e * pinv
        o_c.append(jnp.dot(vct_ref[...], p.astype(BF16), preferred_element_type=F32))
        ps_ref[g] = p[:, 0:TQ] + p[:, TQ:2 * TQ] + p[:, 2 * TQ:3 * TQ] + p[:, 3 * TQ:4 * TQ]

    sidx = lax.broadcasted_iota(jnp.int32, (N_SLC, TQ), 0)
    sidx_f = sidx.astype(F32)
    cur = (qs + lax.broadcasted_iota(jnp.int32, (N_SLC, TQ), 1)) // SEL_BLOCK
    forced = (sidx == 0) | (sidx == cur) | (sidx == cur - 1)
    for g in groups:
        imp = ps_ref[g, pl.ds(C_OFF - 1, N_SLC, stride=4), :]
        for d in range(1, 5):
            imp = imp + ps_ref[g, pl.ds(C_OFF - 1 + d, N_SLC, stride=4), :]
        score = jnp.where(forced, -jnp.inf, jnp.where(sidx <= cur, imp, NEG))
        mbt = jnp.where(forced, 0.0, NEG)
        for _ in range(N_SEL - 3):
            cm = jnp.max(score, axis=0, keepdims=True)
            first = jnp.min(jnp.where(score == cm, sidx_f, float(N_SLC)), axis=0, keepdims=True)
            pick = sidx_f == first
            mbt = jnp.where(pick, 0.0, mbt)
            score = jnp.where(pick, -jnp.inf, score)
        mbt = mbt.astype(BF16)
        qa_ref[g] = jnp.concatenate([jnp.concatenate([mbt] * GQA_REP, axis=1), qpad[g]], axis=0)
        m_ref[g] = jnp.full((1, cols4), NEG, F32)
        acc_ref[g] = jnp.zeros((V_ROWS, cols4), F32)

    def chunk_start(c):
        return pl.multiple_of(WINDOW + jnp.minimum(c, n_far_max - 1) * FAR, FAR)

    def scores(c, par):
        kch = kaug_ref[pl.ds(chunk_start(c), FAR), :]
        for g in groups:
            sb_ref[2 * g + par] = jnp.dot(kch, qa_ref[g], preferred_element_type=F32)

    def consume(c, par):
        vch = vst_ref[:, pl.ds(chunk_start(c), FAR)]
        live = c < nfar
        for g in groups:
            sf = sb_ref[2 * g + par]
            mprev = m_ref[g]
            mcand = jnp.maximum(mprev, jnp.max(sf, axis=0, keepdims=True))
            mnew = jnp.where(live, mcand, mprev)
            pf = jnp.exp(sf - jnp.where(live, mcand, -NEG)).astype(BF16)
            acc_ref[g] = jnp.exp(mprev - mnew) * acc_ref[g] + jnp.dot(
                vch, pf, preferred_element_type=F32)
            m_ref[g] = mnew

    def far_pair(kk, carry):
        c = 2 * kk
        scores(c + 1, 1)
        consume(c, 0)
        scores(c + 2, 0)
        consume(c + 1, 1)
        return carry

    scores(0, 0)
    lax.fori_loop(0, (nfar + 1) // 2, far_pair, 0)

    jrow = lax.broadcasted_iota(jnp.int32, (TAIL, cols4), 0)
    kt = kaug_ref[pl.ds(qs, TAIL), :]
    vt = vst_ref[:, pl.ds(qs, TAIL)]
    kwc = kw_ref[pl.ds(qs, TAIL), :]
    vwc = vwt_ref[:, pl.ds(qs, TAIL)]
    in_tail = jrow >= nfar * FAR + WINDOW - qs
    in_seq = jrow >= WINDOW - qs
    out_rows = []
    for g in groups:
        st = jnp.dot(kt, qa_ref[g], preferred_element_type=F32) + ds_ref[g]
        st = jnp.where(in_tail, st, NEG)
        mprev = m_ref[g]
        mnew = jnp.maximum(mprev, jnp.max(st, axis=0, keepdims=True))
        pt = jnp.exp(st - mnew).astype(BF16)
        acc_s = jnp.exp(mprev - mnew) * acc_ref[g] + jnp.dot(vt, pt, preferred_element_type=F32)
        sw = jnp.dot(kwc, qpad[g], preferred_element_type=F32) + dw_ref[g]
        sw = jnp.where(in_seq, sw, NEG)
        pw = jnp.exp(sw - jnp.max(sw, axis=0, keepdims=True)).astype(BF16)
        acc_w = jnp.dot(vwc, pw, preferred_element_type=F32)
        dsl = slice(g * HEAD_DIM, (g + 1) * HEAD_DIM)
        o_s = acc_s[dsl, :] * (1.0 / acc_s[2 * HEAD_DIM:2 * HEAD_DIM + 1, :])
        o_w = acc_w[dsl, :] * (1.0 / acc_w[2 * HEAD_DIM:2 * HEAD_DIM + 1, :])
        for r in range(GQA_REP):
            cs = slice(r * TQ, (r + 1) * TQ)
            gc = 3 * (GQA_REP * g + r)
            out_rows.append(gt_ref[gc:gc + 1, :] * o_c[g][dsl, cs] + gt_ref[gc + 1:gc + 2, :] * o_s[:, cs]
                            + gt_ref[gc + 2:gc + 3, :] * o_w[:, cs])
    o_ref[...] = jnp.concatenate(out_rows, axis=0).T


def _attention(t_all, gates_t, kc2, vct, kaug, vst, kw_pad, vwt, ds, dw, bc, B, T):
    tp = T + WINDOW
    per_b = lambda shape: pl.BlockSpec((None,) + shape, lambda b, i: (b,) + (0,) * len(shape))
    const = lambda a: pl.BlockSpec(a.shape, lambda b, i: (0,) * a.ndim)
    cols4 = GQA_REP * TQ
    return pl.pallas_call(
        functools.partial(_attn_kernel, n_far_max=T // FAR),
        grid=(B, T // TQ),
        in_specs=[pl.BlockSpec((None, 512, TQ), lambda b, i: (b, 0, i)),
                  pl.BlockSpec((None, GATE_ROWS, TQ), lambda b, i: (b, 0, i)),
                  per_b((C_PAD, 256)),
                  per_b((128, C_PAD)),
                  per_b((tp, 256)),
                  per_b((V_ROWS, tp)),
                  per_b((tp, 128)),
                  per_b((V_ROWS, tp)),
                  const(ds), const(dw), const(bc)],
        out_specs=pl.BlockSpec((None, TQ, 512), lambda b, i: (b, i, 0)),
        out_shape=jax.ShapeDtypeStruct((B, T, 512), F32),
        scratch_shapes=[pltpu.VMEM((N_GROUPS_KV, C_PAD, cols4), F32),
                        pltpu.VMEM((N_GROUPS_KV, C_PAD, TQ), F32),
                        pltpu.VMEM((N_GROUPS_KV, 256, cols4), BF16),
                        pltpu.VMEM((N_GROUPS_KV, 1, cols4), F32),
                        pltpu.VMEM((N_GROUPS_KV, V_ROWS, cols4), F32),
                        pltpu.VMEM((2 * N_GROUPS_KV, FAR, cols4), F32)],
        compiler_params=_cparams(("parallel", "arbitrary"), VMEM_LIMIT),
        name="attn",
    )(t_all, gates_t, kc2, vct, kaug, vst, kw_pad, vwt, ds, dw, bc)


def _rglru_kernel(rx_ref, ry_ref, cw_ref, cb_ref, wa_ref, ba_ref, wx_ref, bx_ref, sp_ref, o_ref,
                  xprev_ref, h_ref, a_sc, b_sc, *, L):
    @pl.when(pl.program_id(1) == 0)
    def _():
        xprev_ref[...] = jnp.zeros(xprev_ref.shape, F32)
        h_ref[...] = jnp.zeros(h_ref.shape, F32)

    x = rx_ref[...]
    xe = jnp.concatenate([xprev_ref[...], x], axis=0)
    xc = cb_ref[...] + cw_ref[0:1, :] * xe[5:5 + L]
    for j in range(1, 4):
        xc = xc + cw_ref[j:j + 1, :] * xe[5 + j:5 + j + L]
    xprev_ref[...] = x[L - 8:L]
    xcb = xc.astype(BF16)
    r = jax.nn.sigmoid(jnp.dot(xcb, wa_ref[...], preferred_element_type=F32) + ba_ref[...])
    ig = jax.nn.sigmoid(jnp.dot(xcb, wx_ref[...], preferred_element_type=F32) + bx_ref[...])
    log_a = -RG_C * r * sp_ref[...]
    a = jnp.exp(log_a)
    bt = jnp.sqrt(1.0 - jnp.exp(2.0 * log_a)) * (ig * xc)

    rid = lax.broadcasted_iota(jnp.int32, (L, D_RNN), 0) % 8
    for sh in (1, 2, 4):
        a_s = pltpu.roll(a, sh, axis=0)
        b_s = pltpu.roll(bt, sh, axis=0)
        msk = rid >= sh
        bt = jnp.where(msk, a * b_s + bt, bt)
        a = jnp.where(msk, a * a_s, a)
    a_sc[...] = a
    b_sc[...] = bt

    def body(gi, hprev):
        r0 = pl.multiple_of(gi * 8, 8)
        h = a_sc[pl.ds(r0, 8), :] * hprev + b_sc[pl.ds(r0, 8), :]
        b_sc[pl.ds(r0, 8), :] = h
        return jnp.broadcast_to(h[7:8, :], (8, D_RNN))

    h_ref[...] = lax.fori_loop(0, L // 8, body, h_ref[...])
    y = ry_ref[...]
    cdf = 0.5 * (1.0 + jnp.tanh(math.sqrt(2.0 / math.pi) * (y + 0.044715 * (y * y * y))))
    o_ref[...] = b_sc[...] * (y * cdf)


def _rglru(rx, ry, conv_w, conv_b, wa_bd, b_a, wx_bd, b_x, sp, B, T):
    L = 512
    blk = pl.BlockSpec((None, L, D_RNN), lambda b, t: (b, t, 0))
    full = lambda a: pl.BlockSpec(a.shape, lambda b, t: (0,) * a.ndim)
    return pl.pallas_call(
        functools.partial(_rglru_kernel, L=L),
        grid=(B, T // L),
        in_specs=[blk, blk, full(conv_w), full(conv_b), full(wa_bd), full(b_a), full(wx_bd),
                  full(b_x), full(sp)],
        out_specs=blk,
        out_shape=jax.ShapeDtypeStruct((B, T, D_RNN), F32),
        scratch_shapes=[pltpu.VMEM((8, D_RNN), F32), pltpu.VMEM((8, D_RNN), F32),
                        pltpu.VMEM((L, D_RNN), F32), pltpu.VMEM((L, D_RNN), F32)],
        compiler_params=_cparams(("parallel", "arbitrary"), VMEM_LIMIT),
        name="rglru",
    )(rx, ry, conv_w, conv_b, wa_bd, b_a, wx_bd, b_x, sp)


def _layer_norm(y, g, b):
    mu = jnp.mean(y, axis=-1, keepdims=True)
    d = y - mu
    var = jnp.mean(d * d, axis=-1, keepdims=True)
    return d * lax.rsqrt(var + LN_EPS) * g + b


def _outproj_kernel(at_ref, rn_ref, x_ref, ga_ref, gr_ref, wo_ref, g1_ref, b1_ref, wrh_ref, wrl_ref,
                    br_ref, x1_ref, comb_ref):
    a = at_ref[...]
    rn = rn_ref[...]
    ha = a * lax.rsqrt(jnp.mean(a * a, axis=-1, keepdims=True) + RMS_EPS) * ga_ref[...]
    hr = rn * lax.rsqrt(jnp.mean(rn * rn, axis=-1, keepdims=True) + RMS_EPS) * gr_ref[...]
    heads = jnp.concatenate([ha, hr], axis=1).astype(BF16)
    mix = jnp.dot(heads, wo_ref[...], preferred_element_type=F32)
    x1 = _layer_norm(ALPHA * x_ref[...] + mix, g1_ref[...], b1_ref[...])
    x1_ref[...] = x1

    xh = x1.astype(BF16)
    xl = (x1 - xh.astype(F32)).astype(BF16)
    lg = (jnp.dot(xh, wrh_ref[...], preferred_element_type=F32)
          + jnp.dot(xl, wrh_ref[...], preferred_element_type=F32)
          + jnp.dot(xh, wrl_ref[...], preferred_element_type=F32)) + br_ref[...]
    lane = lax.broadcasted_iota(jnp.int32, lg.shape, 1)
    lane_f = lane.astype(F32)
    big = 1e9
    isg = lane < N_EXP_GROUPS
    gmax = jnp.max(jnp.where(isg, lg, -jnp.inf), axis=-1, keepdims=True)
    pg_top = 1.0 / jnp.sum(jnp.where(isg, jnp.exp(lg - gmax), 0.0), axis=-1, keepdims=True)
    gi = jnp.min(jnp.where(isg & (lg == gmax), lane_f, big), axis=-1, keepdims=True)
    egrp = ((lane - N_EXP_GROUPS) // EXPERTS_PER_GROUP).astype(F32)
    ise = (lane >= N_EXP_GROUPS) & (lane < N_EXP_GROUPS + N_EXPERTS) & (egrp == gi)
    emax = jnp.max(jnp.where(ise, lg, -jnp.inf), axis=-1, keepdims=True)
    i1 = jnp.min(jnp.where(ise & (lg == emax), lane_f, big), axis=-1, keepdims=True)
    rest = ise & (lane_f != i1)
    m2 = jnp.max(jnp.where(rest, lg, -jnp.inf), axis=-1, keepdims=True)
    i2 = jnp.min(jnp.where(rest & (lg == m2), lane_f, big), axis=-1, keepdims=True)
    e2 = jnp.exp(m2 - emax)
    inv = pg_top / (1.0 + e2)
    comb_ref[...] = jnp.where(lane_f == i1, inv, 0.0) + jnp.where(lane_f == i2, inv * e2, 0.0)


def _outproj(attn, rnn, x2, ga, gr, wo, g1, b1, wrh, wrl, br, N):
    tm = 512
    row = lambda w: pl.BlockSpec((tm, w), lambda i: (i, 0))
    full = lambda a: pl.BlockSpec(a.shape, lambda i: (0,) * a.ndim)
    return pl.pallas_call(
        _outproj_kernel,
        grid=(N // tm,),
        in_specs=[row(512), row(512), row(D_MODEL), full(ga), full(gr), full(wo), full(g1), full(b1),
                  full(wrh), full(wrl), full(br)],
        out_specs=[row(D_MODEL), row(128)],
        out_shape=[jax.ShapeDtypeStruct((N, D_MODEL), F32), jax.ShapeDtypeStruct((N, 128), F32)],
        compiler_params=_cparams(("parallel",), VMEM_LIMIT),
        name="outproj",
    )(attn, rnn, x2, ga, gr, wo, g1, b1, wrh, wrl, br)


def _moe_kernel(x1_ref, comb_ref, p_ref, tri_ref, wup_ref, wdn_ref, wg_ref, wp_ref, g2_ref, b2_ref, o_ref,
                acc_ref, xb_ref, rank_ref, rankt_ref, combt_ref):
    pi = pl.program_id(1)
    n_sub = x1_ref.shape[0] // MOE_SUB
    sub_rows = [slice(h * MOE_SUB, (h + 1) * MOE_SUB) for h in range(n_sub)]
    lane_col = N_EXP_GROUPS

    @pl.when(pi == 0)
    def _():
        x1 = x1_ref[...]
        xb = x1.astype(BF16)
        xb_ref[...] = xb
        gate = jax.nn.sigmoid(jnp.dot(xb, wg_ref[...], preferred_element_type=F32))
        ple = gate * jnp.dot(p_ref[...].astype(BF16), wp_ref[...], preferred_element_type=F32)
        acc_ref[...] = ALPHA * x1 + ple
        for rows in sub_rows:
            comb = comb_ref[rows, :]
            chosen = comb > 0.0
            rank = jnp.dot(tri_ref[...], jnp.where(chosen, 1.0, 0.0).astype(BF16),
                           preferred_element_type=F32)
            rank = jnp.where(chosen, rank, -1.0)
            rank_ref[rows, :] = rank
            rankt_ref[:, rows] = rank.T
            combt_ref[:, rows] = comb.T

    lane = lax.broadcasted_iota(jnp.int32, (MOE_SUB, 128), 1)
    slot_r = lax.broadcasted_iota(jnp.int32, (MOE_SLOTS, MOE_SUB), 0).astype(F32)
    slot_c = lax.broadcasted_iota(jnp.int32, (MOE_SUB, MOE_SLOTS), 1).astype(F32)

    def expert_block(rows, k, base):
        e_lane = 2 * pi + k + lane_col
        rk_row = rankt_ref[pl.ds(e_lane, 1), rows] - base
        w_row = combt_ref[pl.ds(e_lane, 1), rows]
        cmask = rk_row == slot_r
        xc = jnp.dot(jnp.where(cmask, 1.0, 0.0).astype(BF16), xb_ref[rows, :],
                     preferred_element_type=F32).astype(BF16)
        u = jnp.dot(xc, wup_ref[k], preferred_element_type=F32)
        ua = u[:, :D_FF]
        hsw = (ua * jax.nn.sigmoid(ua) * u[:, D_FF:]).astype(BF16)
        y = jnp.dot(hsw, wdn_ref[k], preferred_element_type=F32)
        w_slot = jnp.sum(jnp.where(cmask, w_row, 0.0), axis=1, keepdims=True)
        ys = w_slot * y
        hi = ys.astype(BF16)
        lo = (ys - hi.astype(F32)).astype(BF16)
        rk_col = jnp.sum(jnp.where(lane == e_lane, rank_ref[rows, :], 0.0), axis=1, keepdims=True) - base
        ct = jnp.where(rk_col == slot_c, 1.0, 0.0).astype(BF16)
        return ct, hi, lo

    def expand(cts, his, los):
        ct = jnp.concatenate(cts, axis=1)
        return (jnp.dot(ct, jnp.concatenate(his, axis=0), preferred_element_type=F32)
                + jnp.dot(ct, jnp.concatenate(los, axis=0), preferred_element_type=F32))

    for rows in sub_rows:
        blocks = [expert_block(rows, k, 0.0) for k in range(2)]
        acc_ref[rows, :] = acc_ref[rows, :] + expand(*zip(*blocks))

    top_rank = jnp.max(jnp.maximum(rankt_ref[pl.ds(2 * pi + lane_col, 1), :],
                                   rankt_ref[pl.ds(2 * pi + 1 + lane_col, 1), :]))

    @pl.when(top_rank >= MOE_SLOTS)
    def _():
        def extra(sb, carry):
            base = (sb * MOE_SLOTS).astype(F32)
            for rows in sub_rows:
                blocks = [expert_block(rows, k, base) for k in range(2)]
                acc_ref[rows, :] = acc_ref[rows, :] + expand(*zip(*blocks))
            return carry

        lax.fori_loop(1, top_rank.astype(jnp.int32) // MOE_SLOTS + 1, extra, 0)

    @pl.when(pi == N_EXPERTS // 2 - 1)
    def _():
        o_ref[...] = _layer_norm(acc_ref[...], g2_ref[...], b2_ref[...])


def _moe(x1, comb, p2, wup, wdn, wg, wp, g2, b2, N):
    tm = MOE_TILE
    tri = (jnp.arange(MOE_SUB)[None, :] < jnp.arange(MOE_SUB)[:, None]).astype(BF16)
    row = lambda w: pl.BlockSpec((tm, w), lambda i, e: (i, 0))
    full = lambda a: pl.BlockSpec(a.shape, lambda i, e: (0,) * a.ndim)
    return pl.pallas_call(
        _moe_kernel,
        grid=(N // tm, N_EXPERTS // 2),
        in_specs=[row(D_MODEL), row(128), row(D_PLE), full(tri),
                  pl.BlockSpec((2, D_MODEL, 2 * D_FF), lambda i, e: (e, 0, 0)),
                  pl.BlockSpec((2, D_FF, D_MODEL), lambda i, e: (e, 0, 0)),
                  full(wg), full(wp), full(g2), full(b2)],
        out_specs=row(D_MODEL),
        out_shape=jax.ShapeDtypeStruct((N, D_MODEL), F32),
        scratch_shapes=[pltpu.VMEM((tm, D_MODEL), F32),
                        pltpu.VMEM((tm, D_MODEL), BF16),
                        pltpu.VMEM((tm, 128), F32),
                        pltpu.VMEM((128, tm), F32),
                        pltpu.VMEM((128, tm), F32)],
        compiler_params=_cparams(("parallel", "arbitrary"), VMEM_LIMIT),
        name="moe",
    )(x1, comb, p2, tri, wup, wdn, wg, wp, g2, b2)


def _block_diag(w):
    n, k, j = w.shape
    eye = jnp.eye(n, dtype=w.dtype)
    return (w[:, :, None, :] * eye[:, None, :, None]).reshape(n * k, n * j)


def kernel(x, p, rel_bias, w_in, cmp_pe_k, cmp_pe_v, cmp_w_k, cmp_w_v, conv_w, conv_b, rg_w_a, rg_b_a,
           rg_w_x, rg_b_x, rg_lambda, attn_out_gain, rnn_out_gain, w_out, ln1_g, ln1_b, router_group_w,
           router_group_b, router_expert_w, router_expert_b, expert_w_up, expert_w_down, ple_w,
           ple_gate_w, ln2_g, ln2_b):
    B, T, _ = x.shape
    N = B * T
    assert T % FAR == 0 and T // SEL_BLOCK <= N_SLC and w_in.shape[0] == 1
    n_cmp = (T - CMP_LEN) // CMP_STRIDE + 1
    n_chunk = T // CMP_STRIDE
    assert C_OFF + n_chunk <= C_PAD
    row1 = lambda v: v.reshape(1, -1)

    w = w_in[0]
    c0 = D_ATTN
    kv = lambda k: w[:, c0 + k * D_KV:c0 + (k + 1) * D_KV]
    g0 = c0 + 6 * D_KV
    wq = w[:, :D_ATTN] * (HEAD_DIM ** -0.5)
    w_main = jnp.concatenate([kv(2), kv(4), kv(0), kv(1), w[:, g0 + 24:]], axis=1).astype(BF16)
    w_t = jnp.concatenate([wq, kv(3), kv(5)], axis=1).T.astype(BF16)
    w_g = jnp.pad(w[:, g0:g0 + 24], ((0, 0), (0, GATE_ROWS - 24))).T.astype(BF16)

    x2 = x.reshape(N, D_MODEL)
    ks, kw, kc_raw, vc_raw, rx, ry, t_all, gates_t = _inproj(x2, w_main, w_t, w_g, B, T)

    def chunked(raw):
        r = raw.reshape(B, n_chunk, CMP_STRIDE, N_GROUPS_KV, HEAD_DIM).transpose(0, 3, 1, 2, 4)
        r = r.reshape(B, N_GROUPS_KV, n_chunk, CMP_STRIDE * HEAD_DIM)
        return jnp.pad(r, ((0, 0), (0, 0), (C_OFF, C_PAD - C_OFF - n_chunk), (0, 0)))

    hw = CMP_STRIDE * HEAD_DIM
    pek = cmp_pe_k[0].reshape(2, hw)
    pev = cmp_pe_v[0].reshape(2, hw)
    wk = cmp_w_k[0].reshape(2, hw, HEAD_DIM).astype(BF16)
    wvt = cmp_w_v[0].reshape(2, hw, HEAD_DIM).transpose(0, 2, 1).astype(BF16)
    kc, vct = _compress(chunked(kc_raw), chunked(vc_raw), pek, pev, wk, wvt, B, n_cmp)
    grp_onehot = (jnp.arange(C_PAD)[:, None] // 8 == jnp.arange(128)[None, :]).astype(BF16)
    kc2 = jnp.concatenate([kc.transpose(0, 2, 1, 3).reshape(B, C_PAD, 2 * HEAD_DIM),
                           jnp.broadcast_to(grp_onehot, (B, C_PAD, 128))], axis=2)

    onehot = (jnp.arange(T)[:, None] // SEL_BLOCK == jnp.arange(N_SLC)[None, :]).astype(BF16)
    kaug = jnp.concatenate([jnp.broadcast_to(onehot, (B, T, N_SLC)), ks.reshape(B, T, 128)], axis=2)
    kaug = jnp.pad(kaug, ((0, 0), (WINDOW, 0), (0, 0)))
    kw_pad = jnp.pad(kw.reshape(B, T, 128), ((0, 0), (WINDOW, 0), (0, 0)))
    ones_rows = jnp.concatenate([jnp.ones((B, 1, T), BF16), jnp.zeros((B, V_ROWS - 129, T), BF16)], axis=1)
    vst = jnp.pad(jnp.concatenate([t_all[:, 512:640], ones_rows], axis=1), ((0, 0), (0, 0), (WINDOW, 0)))
    vwt = jnp.pad(jnp.concatenate([t_all[:, 640:768], ones_rows], axis=1), ((0, 0), (0, 0), (WINDOW, 0)))
    ds, dw, bc = _bias_tiles(rel_bias)

    attn = _attention(t_all, gates_t, kc2, vct, kaug, vst, kw_pad, vwt, ds, dw, bc, B, T)

    sp = jax.nn.softplus(-rg_lambda[0].astype(F32))
    rnn = _rglru(rx.reshape(B, T, D_RNN), ry.reshape(B, T, D_RNN), conv_w[0], row1(conv_b[0]),
                 _block_diag(rg_w_a[0]).astype(BF16), row1(rg_b_a[0]),
                 _block_diag(rg_w_x[0]).astype(BF16), row1(rg_b_x[0]), row1(sp), B, T)

    wr = jnp.pad(jnp.concatenate([router_group_w[0], router_expert_w[0]], axis=1), ((0, 0), (0, 108)))
    wrh = wr.astype(BF16)
    wrl = (wr - wrh.astype(F32)).astype(BF16)
    br = jnp.pad(jnp.concatenate([router_group_b[0], router_expert_b[0]]), (0, 108)).reshape(1, 128)
    x1, comb = _outproj(attn.reshape(N, 512), rnn.reshape(N, D_RNN), x2, row1(attn_out_gain[0]),
                        row1(rnn_out_gain[0]), w_out[0].astype(BF16), row1(ln1_g[0]), row1(ln1_b[0]),
                        wrh, wrl, br, N)

    out = _moe(x1, comb, p[0].reshape(N, D_PLE), expert_w_up[0].astype(BF16),
               expert_w_down[0].astype(BF16), ple_gate_w[0].astype(BF16), ple_w[0].astype(BF16),
               row1(ln2_g[0]), row1(ln2_b[0]), N)
    return out.reshape(B, T, D_MODEL)
```

```python
import functools
import math

import jax
import jax.numpy as jnp
from jax import lax
from jax.experimental import pallas as pl
from jax.experimental.pallas import tpu as pltpu

F32 = jnp.float32
BF16 = jnp.bfloat16
NEG = -1e30

D_MODEL = 1024
HEAD_DIM = 64
N_HEADS = 8
N_GROUPS_KV = 2
GQA_REP = 4
D_ATTN = 512
D_RNN = 512
D_KV = 128
CMP_LEN = 32
CMP_STRIDE = 16
SEL_BLOCK = 64
N_SEL = 16
WINDOW = 512
N_BUCKETS = 32
MAX_DISTANCE = 128
N_EXPERTS = 16
EXPERTS_PER_GROUP = 4
N_EXP_GROUPS = 4
D_FF = 512
D_PLE = 256
ALPHA = 2.0 ** 0.25
LN_EPS = 1e-5
RMS_EPS = 1e-6
FORCE_BONUS = 1e4
RG_C = 8.0

TQ = 128
N_SLC = 128
C_PAD = 640
C_OFF = 16
FAR = 512
TAIL = WINDOW + TQ
BAND = 24
GATE_ROWS = 32
V_ROWS = 144
MOE_TILE = 1024
MOE_SUB = 512
MOE_SLOTS = 128
VMEM_LIMIT = 56 * 1024 * 1024


def _cparams(sem, vmem=None):
    return pltpu.CompilerParams(dimension_semantics=sem, vmem_limit_bytes=vmem)


def _inproj_kernel(x_ref, wm_ref, wt_ref, wg_ref, ks_ref, kw_ref, kc_ref, vc_ref, rx_ref, ry_ref,
                   t_ref, gt_ref):
    xb = x_ref[...].astype(BF16)

    def mm(lo, hi):
        return jnp.dot(xb, wm_ref[:, lo:hi], preferred_element_type=F32)

    ks_ref[...] = mm(0, 128).astype(BF16)
    kw_ref[...] = mm(128, 256).astype(BF16)
    kc_ref[...] = mm(256, 384)
    vc_ref[...] = mm(384, 512)
    rx_ref[...] = mm(512, 1024)
    ry_ref[...] = mm(1024, 1536)
    dn = (((1,), (1,)), ((), ()))
    t_ref[...] = lax.dot_general(wt_ref[...], xb, dn, preferred_element_type=F32).astype(BF16)
    gt_ref[...] = jax.nn.sigmoid(lax.dot_general(wg_ref[...], xb, dn, preferred_element_type=F32))


def _inproj(x2, w_main, w_t, w_g, B, T):
    N = B * T
    tm = 512
    nt = T // tm
    row = lambda w: pl.BlockSpec((tm, w), lambda i: (i, 0))
    full = lambda a: pl.BlockSpec(a.shape, lambda i: (0,) * a.ndim)
    tr = lambda r: pl.BlockSpec((None, r, tm), lambda i: (i // nt, 0, i % nt))
    return pl.pallas_call(
        _inproj_kernel,
        grid=(N // tm,),
        in_specs=[row(D_MODEL), full(w_main), full(w_t), full(w_g)],
        out_specs=[row(128), row(128), row(128), row(128), row(512), row(512), tr(768), tr(GATE_ROWS)],
        out_shape=[jax.ShapeDtypeStruct((N, 128), BF16),
                   jax.ShapeDtypeStruct((N, 128), BF16),
                   jax.ShapeDtypeStruct((N, 128), F32),
                   jax.ShapeDtypeStruct((N, 128), F32),
                   jax.ShapeDtypeStruct((N, 512), F32),
                   jax.ShapeDtypeStruct((N, 512), F32),
                   jax.ShapeDtypeStruct((B, 768, T), BF16),
                   jax.ShapeDtypeStruct((B, GATE_ROWS, T), F32)],
        compiler_params=_cparams(("parallel",), VMEM_LIMIT),
        name="inproj",
    )(x2, w_main, w_t, w_g)


def _compress_kernel(kr_ref, vr_ref, pek_ref, pev_ref, wk_ref, wv_ref, oh_ref, kc_ref, vct_ref, *,
                     n_cmp, n_chunk):
    def compress(raw_ref, pe_ref, w_ref):
        lo = jnp.zeros((n_chunk, 2 * HEAD_DIM), F32)
        hi = jnp.zeros((n_chunk, 2 * HEAD_DIM), F32)
        for j in range(CMP_STRIDE):
            a = raw_ref[pl.ds(j, n_chunk, stride=CMP_STRIDE), :]
            lo = lo + jnp.dot((a + pe_ref[j:j + 1, :]).astype(BF16), w_ref[j],
                              preferred_element_type=F32)
            hi = hi + jnp.dot((a + pe_ref[CMP_STRIDE + j:CMP_STRIDE + j + 1, :]).astype(BF16),
                              w_ref[CMP_STRIDE + j], preferred_element_type=F32)
        hi = pltpu.roll(hi, n_chunk - 1, axis=0)
        rid = lax.broadcasted_iota(jnp.int32, (n_chunk, 2 * HEAD_DIM), 0)
        out = jnp.where(rid < n_cmp, lo + hi, 0.0)
        return jnp.concatenate([jnp.zeros((C_OFF, 2 * HEAD_DIM), F32), out,
                                jnp.zeros((C_PAD - C_OFF - n_chunk, 2 * HEAD_DIM), F32)], axis=0)

    kc_ref[:, 0:2 * HEAD_DIM] = compress(kr_ref, pek_ref, wk_ref).astype(BF16)
    kc_ref[:, 2 * HEAD_DIM:] = oh_ref[...]
    vct_ref[...] = compress(vr_ref, pev_ref, wv_ref).T.astype(BF16)


def _compress(kc_raw, vc_raw, pek, pev, wk, wv, grp_onehot, B, T, n_cmp):
    n_chunk = T // CMP_STRIDE
    blk = pl.BlockSpec((None, T, 2 * HEAD_DIM), lambda b: (b, 0, 0))
    full = lambda a: pl.BlockSpec(a.shape, lambda b: (0,) * a.ndim)
    return pl.pallas_call(
        functools.partial(_compress_kernel, n_cmp=n_cmp, n_chunk=n_chunk),
        grid=(B,),
        in_specs=[blk, blk, full(pek), full(pev), full(wk), full(wv), full(grp_onehot)],
        out_specs=[pl.BlockSpec((None, C_PAD, 256), lambda b: (b, 0, 0)),
                   pl.BlockSpec((None, 2 * HEAD_DIM, C_PAD), lambda b: (b, 0, 0))],
        out_shape=[jax.ShapeDtypeStruct((B, C_PAD, 256), BF16),
                   jax.ShapeDtypeStruct((B, 2 * HEAD_DIM, C_PAD), BF16)],
        compiler_params=_cparams(("parallel",), VMEM_LIMIT),
        name="compress",
    )(kc_raw, vc_raw, pek, pev, wk, wv, grp_onehot)


def _bias_kernel(tab_ref, dt_ref, bt_ref, dc_ref, bc_ref, os_ref, ow_ref, oc_ref):
    h = pl.program_id(0) * GQA_REP + pl.program_id(1)

    def lookup(bk):
        out = jnp.zeros(bk.shape, F32)
        for b in range(N_BUCKETS):
            out = jnp.where(bk == b, tab_ref[h, b], out)
        return out - tab_ref[h, N_BUCKETS - 1]

    dt = dt_ref[...]
    v = lookup(bt_ref[...])
    os_ref[...] = jnp.where(dt >= 0, v, NEG)
    ow_ref[...] = jnp.where((dt >= 0) & (dt < WINDOW), v, NEG)
    oc_ref[...] = jnp.where(dc_ref[...] >= 0, lookup(bc_ref[...]), NEG)


def _t5_bucket(dist):
    max_exact = N_BUCKETS // 2
    d = jnp.maximum(dist, 0)
    df = jnp.maximum(d, 1).astype(F32)
    large = max_exact + (jnp.log(df / max_exact) / math.log(MAX_DISTANCE / max_exact)
                         * (N_BUCKETS - max_exact)).astype(jnp.int32)
    large = jnp.minimum(large, N_BUCKETS - 1)
    return jnp.where(d < max_exact, d, large)


def _bias_tiles(rel_bias):
    tq = jnp.arange(TQ, dtype=jnp.int32)
    dt = tq[None, :] + WINDOW - jnp.arange(TAIL, dtype=jnp.int32)[:, None]
    cc = jnp.arange(BAND, dtype=jnp.int32) - C_OFF
    dc = tq[None, :] - (cc[:, None] * CMP_STRIDE + CMP_LEN - 1)
    full = lambda a: pl.BlockSpec(a.shape, lambda g, r: (0,) * a.ndim)
    out = lambda rows: pl.BlockSpec((None, rows, TQ), lambda g, r: (g, 0, r))
    return pl.pallas_call(
        _bias_kernel,
        grid=(N_GROUPS_KV, GQA_REP),
        in_specs=[pl.BlockSpec(memory_space=pltpu.SMEM), full(dt), full(dt), full(dc), full(dc)],
        out_specs=[out(TAIL), out(TAIL), out(BAND)],
        out_shape=[jax.ShapeDtypeStruct((N_GROUPS_KV, TAIL, GQA_REP * TQ), F32),
                   jax.ShapeDtypeStruct((N_GROUPS_KV, TAIL, GQA_REP * TQ), F32),
                   jax.ShapeDtypeStruct((N_GROUPS_KV, BAND, GQA_REP * TQ), F32)],
        compiler_params=_cparams(("arbitrary", "arbitrary")),
        name="biastile",
    )(rel_bias.T, dt, _t5_bucket(dt), dc, _t5_bucket(dc))


def _attn_kernel(qt_ref, gt_ref, kc_ref, vct_ref, kaug_ref, vst_ref, kw_ref, vwt_ref, ds_ref, dw_ref,
                 bc_ref, o_ref, *scratch, n_far_max):
    sc_ref, sw_ref, ps_ref = (scratch[2 * k:2 * k + 2] for k in range(3))
    qa_ref, m_ref, acc_ref, sb_ref = scratch[6:]
    i = pl.program_id(1)
    qs = pl.multiple_of(i * TQ, TQ)
    cols4 = GQA_REP * TQ
    groups = range(N_GROUPS_KV)
    nfar = jnp.maximum(i - 1, 0) // 4
    zeros_q = jnp.zeros((HEAD_DIM, cols4), BF16)
    qpad = []
    for g in groups:
        qgt = jnp.concatenate(
            [qt_ref[(GQA_REP * g + r) * HEAD_DIM:(GQA_REP * g + r + 1) * HEAD_DIM, :]
             for r in range(GQA_REP)], axis=1)
        qpad.append(jnp.concatenate([qgt, zeros_q] if g == 0 else [zeros_q, qgt], axis=0))

    def chunk_start(c):
        return pl.multiple_of(WINDOW + jnp.minimum(c, n_far_max - 1) * FAR, FAR)

    def scores(c, par):
        kch = kaug_ref[pl.ds(chunk_start(c), FAR), :]
        for g in groups:
            sb_ref[2 * g + par] = jnp.dot(kch, qa_ref[g], preferred_element_type=F32)

    def consume(c, par):
        vch = vst_ref[:, pl.ds(chunk_start(c), FAR)]
        live = c < nfar
        for g in groups:
            sf = sb_ref[2 * g + par]
            mprev = m_ref[g]
            mcand = jnp.maximum(mprev, jnp.max(sf, axis=0, keepdims=True))
            mnew = jnp.where(live, mcand, mprev)
            pf = jnp.exp(sf - jnp.where(live, mcand, -NEG)).astype(BF16)
            acc_ref[g] = jnp.exp(mprev - mnew) * acc_ref[g] + jnp.dot(
                vch, pf, preferred_element_type=F32)
            m_ref[g] = mnew

    def far_pair(kk, carry):
        c = 2 * kk
        scores(c + 1, 1)
        consume(c, 0)
        scores(c + 2, 0)
        consume(c + 1, 1)
        return carry

    grp = lax.broadcasted_iota(jnp.int32, (128, cols4), 0)
    gmask = jnp.where((grp < C_OFF // 8) | (grp >= i + BAND // 8), NEG, 0.0).astype(BF16)
    band0 = pl.multiple_of(i * 8, 8)
    tq_col = lax.broadcasted_iota(jnp.int32, (1, cols4), 1) % TQ
    has_cmp = qs + tq_col >= CMP_LEN - 1
    row_w = lax.broadcasted_iota(jnp.int32, (128, cols4), 0)
    neg_row = jnp.where(row_w == 0, NEG, 0.0).astype(BF16)
    kwc = kw_ref[pl.ds(qs, TAIL), :]
    vwc = vwt_ref[:, pl.ds(qs, TAIL)]
    for g in groups:
        sc_ref[g][...] = jnp.dot(kc_ref[...], jnp.concatenate([qpad[g], gmask], axis=0),
                                 preferred_element_type=F32)
        sc_ref[g][pl.ds(band0, BAND), :] = sc_ref[g][pl.ds(band0, BAND), :] + bc_ref[g]
    for g in groups:
        sw_ref[g][...] = jnp.dot(kwc, jnp.concatenate([qpad[g], neg_row], axis=0),
                                 preferred_element_type=F32)
    o_c = []
    for g in groups:
        s = sc_ref[g][...]
        e = jnp.exp(s - jnp.max(s, axis=0, keepdims=True))
        pinv = jnp.where(has_cmp, 1.0 / jnp.sum(e, axis=0, keepdims=True), 0.0)
        p = e * pinv
        o_c.append(jnp.dot(vct_ref[...], p.astype(BF16), preferred_element_type=F32))
        ps_ref[g][...] = p[:, 0:TQ] + p[:, TQ:2 * TQ] + p[:, 2 * TQ:3 * TQ] + p[:, 3 * TQ:4 * TQ]

    o_w = []
    for g in groups:
        sw = sw_ref[g][...] + dw_ref[g]
        pw = jnp.exp(sw - jnp.max(sw, axis=0, keepdims=True)).astype(BF16)
        acc_w = jnp.dot(vwc, pw, preferred_element_type=F32)
        o_w.append(acc_w[g * HEAD_DIM:(g + 1) * HEAD_DIM, :]
                   * (1.0 / acc_w[2 * HEAD_DIM:2 * HEAD_DIM + 1, :]))

    sidx = lax.broadcasted_iota(jnp.int32, (N_SLC, TQ), 0)
    sidx_f = sidx.astype(F32)
    cur = (qs + lax.broadcasted_iota(jnp.int32, (N_SLC, TQ), 1)) // SEL_BLOCK
    forced = (sidx == 0) | (sidx == cur) | (sidx == cur - 1)
    kt = kaug_ref[pl.ds(qs, TAIL), :]
    vt = vst_ref[:, pl.ds(qs, TAIL)]
    for g in groups:
        imp = ps_ref[g][pl.ds(C_OFF - 1, N_SLC, stride=4), :]
        for d in range(1, 5):
            imp = imp + ps_ref[g][pl.ds(C_OFF - 1 + d, N_SLC, stride=4), :]
        score = jnp.where(forced, -jnp.inf, jnp.where(sidx <= cur, imp, NEG))
        mbt = jnp.where(forced, 0.0, NEG)
        for _ in range(N_SEL - 3):
            cm = jnp.max(score, axis=0, keepdims=True)
            first = jnp.min(jnp.where(score == cm, sidx_f, float(N_SLC)), axis=0, keepdims=True)
            pick = sidx_f == first
            mbt = jnp.where(pick, 0.0, mbt)
            score = jnp.where(pick, -jnp.inf, score)
        qa_ref[g] = jnp.concatenate([jnp.concatenate([mbt.astype(BF16)] * GQA_REP, axis=1), qpad[g]],
                                    axis=0)
        mbt_tail = jnp.where(sidx < nfar * (FAR // SEL_BLOCK), NEG, mbt).astype(BF16)
        qa_tail = jnp.concatenate([jnp.concatenate([mbt_tail] * GQA_REP, axis=1), qpad[g]], axis=0)
        sc_ref[g][...] = jnp.dot(kt, qa_tail, preferred_element_type=F32)
    scores(0, 0)
    for g in groups:
        st = sc_ref[g][...] + ds_ref[g]
        mt = jnp.max(st, axis=0, keepdims=True)
        m_ref[g] = mt
        acc_ref[g] = jnp.dot(vt, jnp.exp(st - mt).astype(BF16), preferred_element_type=F32)

    lax.fori_loop(0, (nfar + 1) // 2, far_pair, 0)

    out_rows = []
    for g in groups:
        dsl = slice(g * HEAD_DIM, (g + 1) * HEAD_DIM)
        o_s = acc_ref[g, dsl, :] * (1.0 / acc_ref[g, 2 * HEAD_DIM:2 * HEAD_DIM + 1, :])
        for r in range(GQA_REP):
            cs = slice(r * TQ, (r + 1) * TQ)
            gc = 3 * (GQA_REP * g + r)
            out_rows.append(gt_ref[gc:gc + 1, :] * o_c[g][dsl, cs] + gt_ref[gc + 1:gc + 2, :] * o_s[:, cs]
                            + gt_ref[gc + 2:gc + 3, :] * o_w[g][:, cs])
    o_ref[...] = jnp.concatenate(out_rows, axis=0).T


def _attention(t_all, gates_t, kc2, vct, kaug, vst, kw_pad, vwt, ds, dw, bc, B, T):
    tp = T + WINDOW
    per_b = lambda shape: pl.BlockSpec((None,) + shape, lambda b, i: (b,) + (0,) * len(shape))
    const = lambda a: pl.BlockSpec(a.shape, lambda b, i: (0,) * a.ndim)
    cols4 = GQA_REP * TQ
    per_group = lambda shape: [shape] * N_GROUPS_KV
    return pl.pallas_call(
        functools.partial(_attn_kernel, n_far_max=T // FAR),
        grid=(B, T // TQ),
        in_specs=[pl.BlockSpec((None, 512, TQ), lambda b, i: (b, 0, i)),
                  pl.BlockSpec((None, GATE_ROWS, TQ), lambda b, i: (b, 0, i)),
                  per_b((C_PAD, 256)),
                  per_b((128, C_PAD)),
                  per_b((tp, 256)),
                  per_b((V_ROWS, tp)),
                  per_b((tp, 256)),
                  per_b((V_ROWS, tp)),
                  const(ds), const(dw), const(bc)],
        out_specs=pl.BlockSpec((None, TQ, 512), lambda b, i: (b, i, 0)),
        out_shape=jax.ShapeDtypeStruct((B, T, 512), F32),
        scratch_shapes=(per_group(pltpu.VMEM((C_PAD, cols4), F32))
                        + per_group(pltpu.VMEM((TAIL, cols4), F32))
                        + per_group(pltpu.VMEM((C_PAD, TQ), F32))
                        + [pltpu.VMEM((N_GROUPS_KV, 256, cols4), BF16),
                           pltpu.VMEM((N_GROUPS_KV, 1, cols4), F32),
                           pltpu.VMEM((N_GROUPS_KV, V_ROWS, cols4), F32),
                           pltpu.VMEM((2 * N_GROUPS_KV, FAR, cols4), F32)]),
        compiler_params=_cparams(("parallel", "arbitrary"), VMEM_LIMIT),
        name="attn",
    )(t_all, gates_t, kc2, vct, kaug, vst, kw_pad, vwt, ds, dw, bc)


def _rglru_kernel(rx_ref, ry_ref, cw_ref, cb_ref, wa_ref, ba_ref, wx_ref, bx_ref, sp_ref, o_ref,
                  xprev_ref, h_ref, a_sc, b_sc, *, L):
    @pl.when(pl.program_id(1) == 0)
    def _():
        xprev_ref[...] = jnp.zeros(xprev_ref.shape, F32)
        h_ref[...] = jnp.zeros(h_ref.shape, F32)

    x = rx_ref[...]
    xe = jnp.concatenate([xprev_ref[...], x], axis=0)
    xc = cb_ref[...] + cw_ref[0:1, :] * xe[5:5 + L]
    for j in range(1, 4):
        xc = xc + cw_ref[j:j + 1, :] * xe[5 + j:5 + j + L]
    xprev_ref[...] = x[L - 8:L]
    xcb = xc.astype(BF16)
    r = jax.nn.sigmoid(jnp.dot(xcb, wa_ref[...], preferred_element_type=F32) + ba_ref[...])
    ig = jax.nn.sigmoid(jnp.dot(xcb, wx_ref[...], preferred_element_type=F32) + bx_ref[...])
    log_a = -RG_C * r * sp_ref[...]
    a = jnp.exp(log_a)
    bt = jnp.sqrt(1.0 - jnp.exp(2.0 * log_a)) * (ig * xc)

    rid = lax.broadcasted_iota(jnp.int32, (L, D_RNN), 0) % 8
    for sh in (1, 2, 4):
        a_s = pltpu.roll(a, sh, axis=0)
        b_s = pltpu.roll(bt, sh, axis=0)
        msk = rid >= sh
        bt = jnp.where(msk, a * b_s + bt, bt)
        a = jnp.where(msk, a * a_s, a)
    a_sc[...] = a
    b_sc[...] = bt

    def body(gi, hprev):
        r0 = pl.multiple_of(gi * 8, 8)
        h = a_sc[pl.ds(r0, 8), :] * hprev + b_sc[pl.ds(r0, 8), :]
        b_sc[pl.ds(r0, 8), :] = h
        return jnp.broadcast_to(h[7:8, :], (8, D_RNN))

    h_ref[...] = lax.fori_loop(0, L // 8, body, h_ref[...])
    y = ry_ref[...]
    cdf = 0.5 * (1.0 + jnp.tanh(math.sqrt(2.0 / math.pi) * (y + 0.044715 * (y * y * y))))
    o_ref[...] = b_sc[...] * (y * cdf)


def _rglru(rx, ry, conv_w, conv_b, wa_bd, b_a, wx_bd, b_x, sp, B, T):
    L = 512
    blk = pl.BlockSpec((None, L, D_RNN), lambda b, t: (b, t, 0))
    full = lambda a: pl.BlockSpec(a.shape, lambda b, t: (0,) * a.ndim)
    return pl.pallas_call(
        functools.partial(_rglru_kernel, L=L),
        grid=(B, T // L),
        in_specs=[blk, blk, full(conv_w), full(conv_b), full(wa_bd), full(b_a), full(wx_bd),
                  full(b_x), full(sp)],
        out_specs=blk,
        out_shape=jax.ShapeDtypeStruct((B, T, D_RNN), F32),
        scratch_shapes=[pltpu.VMEM((8, D_RNN), F32), pltpu.VMEM((8, D_RNN), F32),
                        pltpu.VMEM((L, D_RNN), F32), pltpu.VMEM((L, D_RNN), F32)],
        compiler_params=_cparams(("parallel", "arbitrary"), VMEM_LIMIT),
        name="rglru",
    )(rx, ry, conv_w, conv_b, wa_bd, b_a, wx_bd, b_x, sp)


def _layer_norm(y, g, b):
    mu = jnp.mean(y, axis=-1, keepdims=True)
    d = y - mu
    var = jnp.mean(d * d, axis=-1, keepdims=True)
    return d * lax.rsqrt(var + LN_EPS) * g + b


def _outproj_kernel(at_ref, rn_ref, x_ref, ga_ref, gr_ref, wo_ref, g1_ref, b1_ref, wrh_ref, wrl_ref,
                    br_ref, x1_ref, comb_ref):
    a = at_ref[...]
    rn = rn_ref[...]
    ha = a * lax.rsqrt(jnp.mean(a * a, axis=-1, keepdims=True) + RMS_EPS) * ga_ref[...]
    hr = rn * lax.rsqrt(jnp.mean(rn * rn, axis=-1, keepdims=True) + RMS_EPS) * gr_ref[...]
    heads = jnp.concatenate([ha, hr], axis=1).astype(BF16)
    mix = jnp.dot(heads, wo_ref[...], preferred_element_type=F32)
    x1 = _layer_norm(ALPHA * x_ref[...] + mix, g1_ref[...], b1_ref[...])
    x1_ref[...] = x1

    xh = x1.astype(BF16)
    xl = (x1 - xh.astype(F32)).astype(BF16)
    lg = (jnp.dot(xh, wrh_ref[...], preferred_element_type=F32)
          + jnp.dot(xl, wrh_ref[...], preferred_element_type=F32)
          + jnp.dot(xh, wrl_ref[...], preferred_element_type=F32)) + br_ref[...]
    lane = lax.broadcasted_iota(jnp.int32, lg.shape, 1)
    lane_f = lane.astype(F32)
    big = 1e9
    isg = lane < N_EXP_GROUPS
    gmax = jnp.max(jnp.where(isg, lg, -jnp.inf), axis=-1, keepdims=True)
    pg_top = 1.0 / jnp.sum(jnp.where(isg, jnp.exp(lg - gmax), 0.0), axis=-1, keepdims=True)
    gi = jnp.min(jnp.where(isg & (lg == gmax), lane_f, big), axis=-1, keepdims=True)
    egrp = ((lane - N_EXP_GROUPS) // EXPERTS_PER_GROUP).astype(F32)
    ise = (lane >= N_EXP_GROUPS) & (lane < N_EXP_GROUPS + N_EXPERTS) & (egrp == gi)
    emax = jnp.max(jnp.where(ise, lg, -jnp.inf), axis=-1, keepdims=True)
    i1 = jnp.min(jnp.where(ise & (lg == emax), lane_f, big), axis=-1, keepdims=True)
    rest = ise & (lane_f != i1)
    m2 = jnp.max(jnp.where(rest, lg, -jnp.inf), axis=-1, keepdims=True)
    i2 = jnp.min(jnp.where(rest & (lg == m2), lane_f, big), axis=-1, keepdims=True)
    e2 = jnp.exp(m2 - emax)
    inv = pg_top / (1.0 + e2)
    comb_ref[...] = jnp.where(lane_f == i1, inv, 0.0) + jnp.where(lane_f == i2, inv * e2, 0.0)


def _outproj(attn, rnn, x2, ga, gr, wo, g1, b1, wrh, wrl, br, N):
    tm = 512
    row = lambda w: pl.BlockSpec((tm, w), lambda i: (i, 0))
    full = lambda a: pl.BlockSpec(a.shape, lambda i: (0,) * a.ndim)
    return pl.pallas_call(
        _outproj_kernel,
        grid=(N // tm,),
        in_specs=[row(512), row(512), row(D_MODEL), full(ga), full(gr), full(wo), full(g1), full(b1),
                  full(wrh), full(wrl), full(br)],
        out_specs=[row(D_MODEL), row(128)],
        out_shape=[jax.ShapeDtypeStruct((N, D_MODEL), F32), jax.ShapeDtypeStruct((N, 128), F32)],
        compiler_params=_cparams(("parallel",), VMEM_LIMIT),
        name="outproj",
    )(attn, rnn, x2, ga, gr, wo, g1, b1, wrh, wrl, br)


def _moe_kernel(x1_ref, comb_ref, p_ref, tri_ref, wup_ref, wdn_ref, wg_ref, wp_ref, g2_ref, b2_ref, o_ref,
                acc_ref, xb_ref, rank_ref, rankt_ref, combt_ref):
    pi = pl.program_id(1)
    n_sub = x1_ref.shape[0] // MOE_SUB
    sub_rows = [slice(h * MOE_SUB, (h + 1) * MOE_SUB) for h in range(n_sub)]
    lane_col = N_EXP_GROUPS

    @pl.when(pi == 0)
    def _():
        x1 = x1_ref[...]
        xb = x1.astype(BF16)
        xb_ref[...] = xb
        gate = jax.nn.sigmoid(jnp.dot(xb, wg_ref[...], preferred_element_type=F32))
        ple = gate * jnp.dot(p_ref[...].astype(BF16), wp_ref[...], preferred_element_type=F32)
        acc_ref[...] = ALPHA * x1 + ple
        for rows in sub_rows:
            comb = comb_ref[rows, :]
            chosen = comb > 0.0
            rank = jnp.dot(tri_ref[...], jnp.where(chosen, 1.0, 0.0).astype(BF16),
                           preferred_element_type=F32)
            rank = jnp.where(chosen, rank, -1.0)
            rank_ref[rows, :] = rank
            rankt_ref[:, rows] = rank.T
            combt_ref[:, rows] = comb.T

    lane = lax.broadcasted_iota(jnp.int32, (MOE_SUB, 128), 1)
    slot_r = lax.broadcasted_iota(jnp.int32, (MOE_SLOTS, MOE_SUB), 0).astype(F32)
    slot_c = lax.broadcasted_iota(jnp.int32, (MOE_SUB, MOE_SLOTS), 1).astype(F32)

    def expert_block(rows, k, base):
        e_lane = 2 * pi + k + lane_col
        rk_row = rankt_ref[pl.ds(e_lane, 1), rows] - base
        w_row = combt_ref[pl.ds(e_lane, 1), rows]
        cmask = rk_row == slot_r
        xc = jnp.dot(jnp.where(cmask, 1.0, 0.0).astype(BF16), xb_ref[rows, :],
                     preferred_element_type=F32).astype(BF16)
        u = jnp.dot(xc, wup_ref[k], preferred_element_type=F32)
        ua = u[:, :D_FF]
        hsw = (ua * jax.nn.sigmoid(ua) * u[:, D_FF:]).astype(BF16)
        y = jnp.dot(hsw, wdn_ref[k], preferred_element_type=F32)
        w_slot = jnp.sum(jnp.where(cmask, w_row, 0.0), axis=1, keepdims=True)
        ys = w_slot * y
        hi = ys.astype(BF16)
        lo = (ys - hi.astype(F32)).astype(BF16)
        rk_col = jnp.sum(jnp.where(lane == e_lane, rank_ref[rows, :], 0.0), axis=1, keepdims=True) - base
        ct = jnp.where(rk_col == slot_c, 1.0, 0.0).astype(BF16)
        return ct, hi, lo

    def expand(cts, his, los):
        ct = jnp.concatenate(cts, axis=1)
        return (jnp.dot(ct, jnp.concatenate(his, axis=0), preferred_element_type=F32)
                + jnp.dot(ct, jnp.concatenate(los, axis=0), preferred_element_type=F32))

    for rows in sub_rows:
        blocks = [expert_block(rows, k, 0.0) for k in range(2)]
        acc_ref[rows, :] = acc_ref[rows, :] + expand(*zip(*blocks))

    top_rank = jnp.max(jnp.maximum(rankt_ref[pl.ds(2 * pi + lane_col, 1), :],
                                   rankt_ref[pl.ds(2 * pi + 1 + lane_col, 1), :]))

    @pl.when(top_rank >= MOE_SLOTS)
    def _():
        def extra(sb, carry):
            base = (sb * MOE_SLOTS).astype(F32)
            for rows in sub_rows:
                blocks = [expert_block(rows, k, base) for k in range(2)]
                acc_ref[rows, :] = acc_ref[rows, :] + expand(*zip(*blocks))
            return carry

        lax.fori_loop(1, top_rank.astype(jnp.int32) // MOE_SLOTS + 1, extra, 0)

    @pl.when(pi == N_EXPERTS // 2 - 1)
    def _():
        o_ref[...] = _layer_norm(acc_ref[...], g2_ref[...], b2_ref[...])


def _moe(x1, comb, p2, wup, wdn, wg, wp, g2, b2, N):
    tm = MOE_TILE
    tri = (jnp.arange(MOE_SUB)[None, :] < jnp.arange(MOE_SUB)[:, None]).astype(BF16)
    row = lambda w: pl.BlockSpec((tm, w), lambda i, e: (i, 0))
    full = lambda a: pl.BlockSpec(a.shape, lambda i, e: (0,) * a.ndim)
    return pl.pallas_call(
        _moe_kernel,
        grid=(N // tm, N_EXPERTS // 2),
        in_specs=[row(D_MODEL), row(128), row(D_PLE), full(tri),
                  pl.BlockSpec((2, D_MODEL, 2 * D_FF), lambda i, e: (e, 0, 0)),
                  pl.BlockSpec((2, D_FF, D_MODEL), lambda i, e: (e, 0, 0)),
                  full(wg), full(wp), full(g2), full(b2)],
        out_specs=row(D_MODEL),
        out_shape=jax.ShapeDtypeStruct((N, D_MODEL), F32),
        scratch_shapes=[pltpu.VMEM((tm, D_MODEL), F32),
                        pltpu.VMEM((tm, D_MODEL), BF16),
                        pltpu.VMEM((tm, 128), F32),
                        pltpu.VMEM((128, tm), F32),
                        pltpu.VMEM((128, tm), F32)],
        compiler_params=_cparams(("parallel", "arbitrary"), VMEM_LIMIT),
        name="moe",
    )(x1, comb, p2, tri, wup, wdn, wg, wp, g2, b2)


def _block_diag(w):
    n, k, j = w.shape
    eye = jnp.eye(n, dtype=w.dtype)
    return (w[:, :, None, :] * eye[:, None, :, None]).reshape(n * k, n * j)


def kernel(x, p, rel_bias, w_in, cmp_pe_k, cmp_pe_v, cmp_w_k, cmp_w_v, conv_w, conv_b, rg_w_a, rg_b_a,
           rg_w_x, rg_b_x, rg_lambda, attn_out_gain, rnn_out_gain, w_out, ln1_g, ln1_b, router_group_w,
           router_group_b, router_expert_w, router_expert_b, expert_w_up, expert_w_down, ple_w,
           ple_gate_w, ln2_g, ln2_b):
    B, T, _ = x.shape
    N = B * T
    assert T % FAR == 0 and T // SEL_BLOCK <= N_SLC and w_in.shape[0] == 1
    n_cmp = (T - CMP_LEN) // CMP_STRIDE + 1
    n_chunk = T // CMP_STRIDE
    assert C_OFF + n_chunk <= C_PAD
    row1 = lambda v: v.reshape(1, -1)

    w = w_in[0]
    c0 = D_ATTN
    kv = lambda k: w[:, c0 + k * D_KV:c0 + (k + 1) * D_KV]
    g0 = c0 + 6 * D_KV
    wq = w[:, :D_ATTN] * (HEAD_DIM ** -0.5)
    w_main = jnp.concatenate([kv(2), kv(4), kv(0), kv(1), w[:, g0 + 24:]], axis=1).astype(BF16)
    w_t = jnp.concatenate([wq, kv(3), kv(5)], axis=1).T.astype(BF16)
    w_g = jnp.pad(w[:, g0:g0 + 24], ((0, 0), (0, GATE_ROWS - 24))).T.astype(BF16)

    x2 = x.reshape(N, D_MODEL)
    ks, kw, kc_raw, vc_raw, rx, ry, t_all, gates_t = _inproj(x2, w_main, w_t, w_g, B, T)

    def per_pos(wc):
        wl = wc.reshape(CMP_LEN, HEAD_DIM, HEAD_DIM)
        z = jnp.zeros_like(wl)
        return jnp.concatenate([jnp.concatenate([wl, z], axis=2), jnp.concatenate([z, wl], axis=2)],
                               axis=1).astype(BF16)

    pek = jnp.tile(cmp_pe_k[0], (1, N_GROUPS_KV))
    pev = jnp.tile(cmp_pe_v[0], (1, N_GROUPS_KV))
    grp_onehot = (jnp.arange(C_PAD)[:, None] // 8 == jnp.arange(128)[None, :]).astype(BF16)
    kc2, vct = _compress(kc_raw.reshape(B, T, 128), vc_raw.reshape(B, T, 128), pek, pev,
                         per_pos(cmp_w_k[0]), per_pos(cmp_w_v[0]), grp_onehot, B, T, n_cmp)

    onehot = (jnp.arange(T)[:, None] // SEL_BLOCK == jnp.arange(N_SLC)[None, :]).astype(BF16)
    kaug = jnp.concatenate([jnp.broadcast_to(onehot, (B, T, N_SLC)), ks.reshape(B, T, 128)], axis=2)
    lane256 = jnp.arange(256)
    pad_sel = jnp.broadcast_to((lane256 == N_SLC - 1).astype(BF16), (B, WINDOW, 256))
    pad_win = jnp.broadcast_to((lane256 == 128).astype(BF16), (B, WINDOW, 256))
    kaug = jnp.concatenate([pad_sel, kaug], axis=1)
    kw_pad = jnp.concatenate([pad_win, jnp.pad(kw.reshape(B, T, 128), ((0, 0), (0, 0), (0, 128)))], axis=1)
    ones_rows = jnp.concatenate([jnp.ones((B, 1, T), BF16), jnp.zeros((B, V_ROWS - 129, T), BF16)], axis=1)
    vst = jnp.pad(jnp.concatenate([t_all[:, 512:640], ones_rows], axis=1), ((0, 0), (0, 0), (WINDOW, 0)))
    vwt = jnp.pad(jnp.concatenate([t_all[:, 640:768], ones_rows], axis=1), ((0, 0), (0, 0), (WINDOW, 0)))
    ds, dw, bc = _bias_tiles(rel_bias)

    attn = _attention(t_all, gates_t, kc2, vct, kaug, vst, kw_pad, vwt, ds, dw, bc, B, T)

    sp = jax.nn.softplus(-rg_lambda[0].astype(F32))
    rnn = _rglru(rx.reshape(B, T, D_RNN), ry.reshape(B, T, D_RNN), conv_w[0], row1(conv_b[0]),
                 _block_diag(rg_w_a[0]).astype(BF16), row1(rg_b_a[0]),
                 _block_diag(rg_w_x[0]).astype(BF16), row1(rg_b_x[0]), row1(sp), B, T)

    wr = jnp.pad(jnp.concatenate([router_group_w[0], router_expert_w[0]], axis=1), ((0, 0), (0, 108)))
    wrh = wr.astype(BF16)
    wrl = (wr - wrh.astype(F32)).astype(BF16)
    br = jnp.pad(jnp.concatenate([router_group_b[0], router_expert_b[0]]), (0, 108)).reshape(1, 128)
    x1, comb = _outproj(attn.reshape(N, 512), rnn.reshape(N, D_RNN), x2, row1(attn_out_gain[0]),
                        row1(rnn_out_gain[0]), w_out[0].astype(BF16), row1(ln1_g[0]), row1(ln1_b[0]),
                        wrh, wrl, br, N)

    out = _moe(x1, comb, p[0].reshape(N, D_PLE), expert_w_up[0].astype(BF16),
               expert_w_down[0].astype(BF16), ple_gate_w[0].astype(BF16), ple_w[0].astype(BF16),
               row1(ln2_g[0]), row1(ln2_b[0]), N)
    return out.reshape(B, T, D_MODEL)
```

```python
import functools
import math

import jax
import jax.numpy as jnp
from jax import lax
from jax.experimental import pallas as pl
from jax.experimental.pallas import tpu as pltpu

F32 = jnp.float32
BF16 = jnp.bfloat16
NEG = -1e30
LOG2E = 1.4426950408889634

D_MODEL = 1024
HEAD_DIM = 64
N_HEADS = 8
N_GROUPS_KV = 2
GQA_REP = 4
D_ATTN = 512
D_RNN = 512
D_KV = 128
CMP_LEN = 32
CMP_STRIDE = 16
SEL_BLOCK = 64
N_SEL = 16
WINDOW = 512
N_BUCKETS = 32
MAX_DISTANCE = 128
N_EXPERTS = 16
EXPERTS_PER_GROUP = 4
N_EXP_GROUPS = 4
D_FF = 512
D_PLE = 256
ALPHA = 2.0 ** 0.25
LN_EPS = 1e-5
RMS_EPS = 1e-6
FORCE_BONUS = 1e4
RG_C = 8.0

TQ = 128
N_SLC = 128
C_PAD = 640
C_OFF = 16
FAR = 512
TAIL = WINDOW + TQ
BAND = 24
GATE_ROWS = 32
V_ROWS = 144
MOE_TILE = 1024
MOE_SUB = 512
MOE_SLOTS = 128
VMEM_LIMIT = 56 * 1024 * 1024


def _cparams(sem, vmem=None):
    return pltpu.CompilerParams(dimension_semantics=sem, vmem_limit_bytes=vmem)


def _inproj_kernel(x_ref, wm_ref, wt_ref, wg_ref, ks_ref, kw_ref, kc_ref, vc_ref, rx_ref, ry_ref,
                   t_ref, gt_ref):
    xb = x_ref[...].astype(BF16)

    def mm(lo, hi):
        return jnp.dot(xb, wm_ref[:, lo:hi], preferred_element_type=F32)

    ks_ref[...] = mm(0, 128).astype(BF16)
    kw_ref[...] = mm(128, 256).astype(BF16)
    kc_ref[...] = mm(256, 384)
    vc_ref[...] = mm(384, 512)
    rx_ref[...] = mm(512, 1024)
    ry_ref[...] = mm(1024, 1536)
    dn = (((1,), (1,)), ((), ()))
    t_ref[...] = lax.dot_general(wt_ref[...], xb, dn, preferred_element_type=F32).astype(BF16)
    gt_ref[...] = jax.nn.sigmoid(lax.dot_general(wg_ref[...], xb, dn, preferred_element_type=F32))


def _inproj(x2, w_main, w_t, w_g, B, T):
    N = B * T
    tm = 512
    nt = T // tm
    row = lambda w: pl.BlockSpec((tm, w), lambda i: (i, 0))
    full = lambda a: pl.BlockSpec(a.shape, lambda i: (0,) * a.ndim)
    tr = lambda r: pl.BlockSpec((None, r, tm), lambda i: (i // nt, 0, i % nt))
    return pl.pallas_call(
        _inproj_kernel,
        grid=(N // tm,),
        in_specs=[row(D_MODEL), full(w_main), full(w_t), full(w_g)],
        out_specs=[row(128), row(128), row(128), row(128), row(512), row(512), tr(768), tr(GATE_ROWS)],
        out_shape=[jax.ShapeDtypeStruct((N, 128), BF16),
                   jax.ShapeDtypeStruct((N, 128), BF16),
                   jax.ShapeDtypeStruct((N, 128), F32),
                   jax.ShapeDtypeStruct((N, 128), F32),
                   jax.ShapeDtypeStruct((N, 512), F32),
                   jax.ShapeDtypeStruct((N, 512), F32),
                   jax.ShapeDtypeStruct((B, 768, T), BF16),
                   jax.ShapeDtypeStruct((B, GATE_ROWS, T), F32)],
        compiler_params=_cparams(("parallel",), VMEM_LIMIT),
        name="inproj",
    )(x2, w_main, w_t, w_g)


def _compress_kernel(kr_ref, vr_ref, pek_ref, pev_ref, wk_ref, wv_ref, oh_ref, kc_ref, vct_ref, *,
                     n_cmp, n_chunk):
    def compress(raw_ref, pe_ref, w_ref):
        lo = jnp.zeros((n_chunk, 2 * HEAD_DIM), F32)
        hi = jnp.zeros((n_chunk, 2 * HEAD_DIM), F32)
        for j in range(CMP_STRIDE):
            a = raw_ref[pl.ds(j, n_chunk, stride=CMP_STRIDE), :]
            lo = lo + jnp.dot((a + pe_ref[j:j + 1, :]).astype(BF16), w_ref[j],
                              preferred_element_type=F32)
            hi = hi + jnp.dot((a + pe_ref[CMP_STRIDE + j:CMP_STRIDE + j + 1, :]).astype(BF16),
                              w_ref[CMP_STRIDE + j], preferred_element_type=F32)
        hi = pltpu.roll(hi, n_chunk - 1, axis=0)
        rid = lax.broadcasted_iota(jnp.int32, (n_chunk, 2 * HEAD_DIM), 0)
        out = jnp.where(rid < n_cmp, lo + hi, 0.0)
        return jnp.concatenate([jnp.zeros((C_OFF, 2 * HEAD_DIM), F32), out,
                                jnp.zeros((C_PAD - C_OFF - n_chunk, 2 * HEAD_DIM), F32)], axis=0)

    kc_ref[:, 0:2 * HEAD_DIM] = compress(kr_ref, pek_ref, wk_ref).astype(BF16)
    kc_ref[:, 2 * HEAD_DIM:] = oh_ref[...]
    vct_ref[...] = compress(vr_ref, pev_ref, wv_ref).T.astype(BF16)


def _compress(kc_raw, vc_raw, pek, pev, wk, wv, grp_onehot, B, T, n_cmp):
    n_chunk = T // CMP_STRIDE
    blk = pl.BlockSpec((None, T, 2 * HEAD_DIM), lambda b: (b, 0, 0))
    full = lambda a: pl.BlockSpec(a.shape, lambda b: (0,) * a.ndim)
    return pl.pallas_call(
        functools.partial(_compress_kernel, n_cmp=n_cmp, n_chunk=n_chunk),
        grid=(B,),
        in_specs=[blk, blk, full(pek), full(pev), full(wk), full(wv), full(grp_onehot)],
        out_specs=[pl.BlockSpec((None, C_PAD, 256), lambda b: (b, 0, 0)),
                   pl.BlockSpec((None, 2 * HEAD_DIM, C_PAD), lambda b: (b, 0, 0))],
        out_shape=[jax.ShapeDtypeStruct((B, C_PAD, 256), BF16),
                   jax.ShapeDtypeStruct((B, 2 * HEAD_DIM, C_PAD), BF16)],
        compiler_params=_cparams(("parallel",), VMEM_LIMIT),
        name="compress",
    )(kc_raw, vc_raw, pek, pev, wk, wv, grp_onehot)


def _bias_kernel(tab_ref, dt_ref, bt_ref, dc_ref, bc_ref, os_ref, ow_ref, oc_ref):
    h = pl.program_id(0) * GQA_REP + pl.program_id(1)

    def lookup(bk):
        out = jnp.zeros(bk.shape, F32)
        for b in range(N_BUCKETS):
            out = jnp.where(bk == b, tab_ref[h, b], out)
        return (out - tab_ref[h, N_BUCKETS - 1]) * LOG2E

    dt = dt_ref[...]
    v = lookup(bt_ref[...])
    os_ref[...] = jnp.where(dt >= 0, v, NEG)
    ow_ref[...] = jnp.where((dt >= 0) & (dt < WINDOW), v, NEG)
    oc_ref[...] = jnp.where(dc_ref[...] >= 0, lookup(bc_ref[...]), NEG)


def _t5_bucket(dist):
    max_exact = N_BUCKETS // 2
    d = jnp.maximum(dist, 0)
    df = jnp.maximum(d, 1).astype(F32)
    large = max_exact + (jnp.log(df / max_exact) / math.log(MAX_DISTANCE / max_exact)
                         * (N_BUCKETS - max_exact)).astype(jnp.int32)
    large = jnp.minimum(large, N_BUCKETS - 1)
    return jnp.where(d < max_exact, d, large)


def _bias_tiles(rel_bias):
    tq = jnp.arange(TQ, dtype=jnp.int32)
    dt = tq[None, :] + WINDOW - jnp.arange(TAIL, dtype=jnp.int32)[:, None]
    cc = jnp.arange(BAND, dtype=jnp.int32) - C_OFF
    dc = tq[None, :] - (cc[:, None] * CMP_STRIDE + CMP_LEN - 1)
    full = lambda a: pl.BlockSpec(a.shape, lambda g, r: (0,) * a.ndim)
    out = lambda rows: pl.BlockSpec((None, rows, TQ), lambda g, r: (g, 0, r))
    return pl.pallas_call(
        _bias_kernel,
        grid=(N_GROUPS_KV, GQA_REP),
        in_specs=[pl.BlockSpec(memory_space=pltpu.SMEM), full(dt), full(dt), full(dc), full(dc)],
        out_specs=[out(TAIL), out(TAIL), out(BAND)],
        out_shape=[jax.ShapeDtypeStruct((N_GROUPS_KV, TAIL, GQA_REP * TQ), F32),
                   jax.ShapeDtypeStruct((N_GROUPS_KV, TAIL, GQA_REP * TQ), F32),
                   jax.ShapeDtypeStruct((N_GROUPS_KV, BAND, GQA_REP * TQ), F32)],
        compiler_params=_cparams(("arbitrary", "arbitrary")),
        name="biastile",
    )(rel_bias.T, dt, _t5_bucket(dt), dc, _t5_bucket(dc))


def _attn_kernel(qt_ref, gt_ref, kc_ref, vct_ref, kaug_ref, vst_ref, kw_ref, vwt_ref, ds_ref, dw_ref,
                 bc_ref, o_ref, *scratch, n_far_max):
    sc_ref, sw_ref, ps_ref = (scratch[2 * k:2 * k + 2] for k in range(3))
    qa_ref, m_ref, acc_ref, sb_ref = scratch[6:]
    i = pl.program_id(1)
    qs = pl.multiple_of(i * TQ, TQ)
    cols4 = GQA_REP * TQ
    groups = range(N_GROUPS_KV)
    nfar = jnp.maximum(i - 1, 0) // 4
    zeros_q = jnp.zeros((HEAD_DIM, cols4), BF16)
    qpad = []
    for g in groups:
        qgt = jnp.concatenate(
            [qt_ref[(GQA_REP * g + r) * HEAD_DIM:(GQA_REP * g + r + 1) * HEAD_DIM, :]
             for r in range(GQA_REP)], axis=1)
        qpad.append(jnp.concatenate([qgt, zeros_q] if g == 0 else [zeros_q, qgt], axis=0))

    def chunk_start(c):
        return pl.multiple_of(WINDOW + jnp.minimum(c, n_far_max - 1) * FAR, FAR)

    def scores(c, par):
        kch = kaug_ref[pl.ds(chunk_start(c), FAR), :]
        for g in groups:
            sb_ref[2 * g + par] = jnp.dot(kch, qa_ref[g], preferred_element_type=F32)

    def consume(c, par):
        vch = vst_ref[:, pl.ds(chunk_start(c), FAR)]
        live = c < nfar
        for g in groups:
            sf = sb_ref[2 * g + par]
            mprev = m_ref[g]
            mcand = jnp.maximum(mprev, jnp.max(sf, axis=0, keepdims=True))
            mnew = jnp.where(live, mcand, mprev)
            pf = jnp.exp2(sf - jnp.where(live, mcand, -NEG)).astype(BF16)
            acc_ref[g] = jnp.exp2(mprev - mnew) * acc_ref[g] + jnp.dot(
                vch, pf, preferred_element_type=F32)
            m_ref[g] = mnew

    def far_pair(kk, carry):
        c = 2 * kk
        scores(c + 1, 1)
        consume(c, 0)
        scores(c + 2, 0)
        consume(c + 1, 1)
        return carry

    grp = lax.broadcasted_iota(jnp.int32, (128, cols4), 0)
    gmask = jnp.where((grp < C_OFF // 8) | (grp >= i + BAND // 8), NEG, 0.0).astype(BF16)
    band0 = pl.multiple_of(i * 8, 8)
    tq_col = lax.broadcasted_iota(jnp.int32, (1, cols4), 1) % TQ
    has_cmp = qs + tq_col >= CMP_LEN - 1
    row_w = lax.broadcasted_iota(jnp.int32, (128, cols4), 0)
    neg_row = jnp.where(row_w == 0, NEG, 0.0).astype(BF16)
    kwc = kw_ref[pl.ds(qs, TAIL), :]
    vwc = vwt_ref[:, pl.ds(qs, TAIL)]
    for g in groups:
        sc_ref[g][...] = jnp.dot(kc_ref[...], jnp.concatenate([qpad[g], gmask], axis=0),
                                 preferred_element_type=F32)
        sc_ref[g][pl.ds(band0, BAND), :] = sc_ref[g][pl.ds(band0, BAND), :] + bc_ref[g]
    for g in groups:
        sw_ref[g][...] = jnp.dot(kwc, jnp.concatenate([qpad[g], neg_row], axis=0),
                                 preferred_element_type=F32)
    o_c = []
    for g in groups:
        s = sc_ref[g][...]
        e = jnp.exp2(s - jnp.max(s, axis=0, keepdims=True))
        pinv = jnp.where(has_cmp, 1.0 / jnp.sum(e, axis=0, keepdims=True), 0.0)
        p = e * pinv
        o_c.append(jnp.dot(vct_ref[...], p.astype(BF16), preferred_element_type=F32))
        ps_ref[g][...] = p[:, 0:TQ] + p[:, TQ:2 * TQ] + p[:, 2 * TQ:3 * TQ] + p[:, 3 * TQ:4 * TQ]

    o_w = []
    for g in groups:
        sw = sw_ref[g][...] + dw_ref[g]
        pw = jnp.exp2(sw - jnp.max(sw, axis=0, keepdims=True)).astype(BF16)
        acc_w = jnp.dot(vwc, pw, preferred_element_type=F32)
        o_w.append(acc_w[g * HEAD_DIM:(g + 1) * HEAD_DIM, :]
                   * (1.0 / acc_w[2 * HEAD_DIM:2 * HEAD_DIM + 1, :]))

    sidx = lax.broadcasted_iota(jnp.int32, (N_SLC, TQ), 0)
    sidx_f = sidx.astype(F32)
    cur = (qs + lax.broadcasted_iota(jnp.int32, (N_SLC, TQ), 1)) // SEL_BLOCK
    forced = (sidx == 0) | (sidx == cur) | (sidx == cur - 1)
    kt = kaug_ref[pl.ds(qs, TAIL), :]
    vt = vst_ref[:, pl.ds(qs, TAIL)]
    for g in groups:
        imp = ps_ref[g][pl.ds(C_OFF - 1, N_SLC, stride=4), :]
        for d in range(1, 5):
            imp = imp + ps_ref[g][pl.ds(C_OFF - 1 + d, N_SLC, stride=4), :]
        score = jnp.where(forced, -jnp.inf, jnp.where(sidx <= cur, imp, NEG))
        mbt = jnp.where(forced, 0.0, NEG)
        for _ in range(N_SEL - 3):
            cm = jnp.max(score, axis=0, keepdims=True)
            first = jnp.min(jnp.where(score == cm, sidx_f, float(N_SLC)), axis=0, keepdims=True)
            pick = sidx_f == first
            mbt = jnp.where(pick, 0.0, mbt)
            score = jnp.where(pick, -jnp.inf, score)
        qa_ref[g] = jnp.concatenate([jnp.concatenate([mbt.astype(BF16)] * GQA_REP, axis=1), qpad[g]],
                                    axis=0)
        mbt_tail = jnp.where(sidx < nfar * (FAR // SEL_BLOCK), NEG, mbt).astype(BF16)
        qa_tail = jnp.concatenate([jnp.concatenate([mbt_tail] * GQA_REP, axis=1), qpad[g]], axis=0)
        sc_ref[g][...] = jnp.dot(kt, qa_tail, preferred_element_type=F32)
    scores(0, 0)
    for g in groups:
        st = sc_ref[g][...] + ds_ref[g]
        mt = jnp.max(st, axis=0, keepdims=True)
        m_ref[g] = mt
        acc_ref[g] = jnp.dot(vt, jnp.exp2(st - mt).astype(BF16), preferred_element_type=F32)

    lax.fori_loop(0, (nfar + 1) // 2, far_pair, 0)

    out_rows = []
    for g in groups:
        dsl = slice(g * HEAD_DIM, (g + 1) * HEAD_DIM)
        o_s = acc_ref[g, dsl, :] * (1.0 / acc_ref[g, 2 * HEAD_DIM:2 * HEAD_DIM + 1, :])
        for r in range(GQA_REP):
            cs = slice(r * TQ, (r + 1) * TQ)
            gc = 3 * (GQA_REP * g + r)
            out_rows.append(gt_ref[gc:gc + 1, :] * o_c[g][dsl, cs] + gt_ref[gc + 1:gc + 2, :] * o_s[:, cs]
                            + gt_ref[gc + 2:gc + 3, :] * o_w[g][:, cs])
    o_ref[...] = jnp.concatenate(out_rows, axis=0).T


def _attention(t_all, gates_t, kc2, vct, kaug, vst, kw_pad, vwt, ds, dw, bc, B, T):
    tp = T + WINDOW
    per_b = lambda shape: pl.BlockSpec((None,) + shape, lambda b, i: (b,) + (0,) * len(shape))
    const = lambda a: pl.BlockSpec(a.shape, lambda b, i: (0,) * a.ndim)
    cols4 = GQA_REP * TQ
    per_group = lambda shape: [shape] * N_GROUPS_KV
    return pl.pallas_call(
        functools.partial(_attn_kernel, n_far_max=T // FAR),
        grid=(B, T // TQ),
        in_specs=[pl.BlockSpec((None, 512, TQ), lambda b, i: (b, 0, i)),
                  pl.BlockSpec((None, GATE_ROWS, TQ), lambda b, i: (b, 0, i)),
                  per_b((C_PAD, 256)),
                  per_b((128, C_PAD)),
                  per_b((tp, 256)),
                  per_b((V_ROWS, tp)),
                  per_b((tp, 256)),
                  per_b((V_ROWS, tp)),
                  const(ds), const(dw), const(bc)],
        out_specs=pl.BlockSpec((None, TQ, 512), lambda b, i: (b, i, 0)),
        out_shape=jax.ShapeDtypeStruct((B, T, 512), F32),
        scratch_shapes=(per_group(pltpu.VMEM((C_PAD, cols4), F32))
                        + per_group(pltpu.VMEM((TAIL, cols4), F32))
                        + per_group(pltpu.VMEM((C_PAD, TQ), F32))
                        + [pltpu.VMEM((N_GROUPS_KV, 256, cols4), BF16),
                           pltpu.VMEM((N_GROUPS_KV, 1, cols4), F32),
                           pltpu.VMEM((N_GROUPS_KV, V_ROWS, cols4), F32),
                           pltpu.VMEM((2 * N_GROUPS_KV, FAR, cols4), F32)]),
        compiler_params=_cparams(("parallel", "arbitrary"), VMEM_LIMIT),
        name="attn",
    )(t_all, gates_t, kc2, vct, kaug, vst, kw_pad, vwt, ds, dw, bc)


def _rglru_kernel(rx_ref, ry_ref, cw_ref, cb_ref, wa_ref, ba_ref, wx_ref, bx_ref, sp_ref, o_ref,
                  xprev_ref, h_ref, a_sc, b_sc, *, L):
    @pl.when(pl.program_id(1) == 0)
    def _():
        xprev_ref[...] = jnp.zeros(xprev_ref.shape, F32)
        h_ref[...] = jnp.zeros(h_ref.shape, F32)

    x = rx_ref[...]
    xe = jnp.concatenate([xprev_ref[...], x], axis=0)
    xc = cb_ref[...] + cw_ref[0:1, :] * xe[5:5 + L]
    for j in range(1, 4):
        xc = xc + cw_ref[j:j + 1, :] * xe[5 + j:5 + j + L]
    xprev_ref[...] = x[L - 8:L]
    xcb = xc.astype(BF16)
    r = jax.nn.sigmoid(jnp.dot(xcb, wa_ref[...], preferred_element_type=F32) + ba_ref[...])
    ig = jax.nn.sigmoid(jnp.dot(xcb, wx_ref[...], preferred_element_type=F32) + bx_ref[...])
    log_a = -RG_C * r * sp_ref[...]
    a = jnp.exp(log_a)
    bt = jnp.sqrt(1.0 - a * a) * (ig * xc)

    rid = lax.broadcasted_iota(jnp.int32, (L, D_RNN), 0) % 8
    for sh in (1, 2, 4):
        a_s = pltpu.roll(a, sh, axis=0)
        b_s = pltpu.roll(bt, sh, axis=0)
        msk = rid >= sh
        bt = jnp.where(msk, a * b_s + bt, bt)
        a = jnp.where(msk, a * a_s, a)
    a_sc[...] = a
    b_sc[...] = bt

    def body(gi, hprev):
        r0 = pl.multiple_of(gi * 8, 8)
        h = a_sc[pl.ds(r0, 8), :] * hprev + b_sc[pl.ds(r0, 8), :]
        b_sc[pl.ds(r0, 8), :] = h
        return jnp.broadcast_to(h[7:8, :], (8, D_RNN))

    h_ref[...] = lax.fori_loop(0, L // 8, body, h_ref[...])
    y = ry_ref[...]
    cdf = 0.5 * (1.0 + jnp.tanh(math.sqrt(2.0 / math.pi) * (y + 0.044715 * (y * y * y))))
    o_ref[...] = b_sc[...] * (y * cdf)


def _rglru(rx, ry, conv_w, conv_b, wa_bd, b_a, wx_bd, b_x, sp, B, T):
    L = 512
    blk = pl.BlockSpec((None, L, D_RNN), lambda b, t: (b, t, 0))
    full = lambda a: pl.BlockSpec(a.shape, lambda b, t: (0,) * a.ndim)
    return pl.pallas_call(
        functools.partial(_rglru_kernel, L=L),
        grid=(B, T // L),
        in_specs=[blk, blk, full(conv_w), full(conv_b), full(wa_bd), full(b_a), full(wx_bd),
                  full(b_x), full(sp)],
        out_specs=blk,
        out_shape=jax.ShapeDtypeStruct((B, T, D_RNN), F32),
        scratch_shapes=[pltpu.VMEM((8, D_RNN), F32), pltpu.VMEM((8, D_RNN), F32),
                        pltpu.VMEM((L, D_RNN), F32), pltpu.VMEM((L, D_RNN), F32)],
        compiler_params=_cparams(("parallel", "arbitrary"), VMEM_LIMIT),
        name="rglru",
    )(rx, ry, conv_w, conv_b, wa_bd, b_a, wx_bd, b_x, sp)


def _layer_norm(y, g, b):
    mu = jnp.mean(y, axis=-1, keepdims=True)
    d = y - mu
    var = jnp.mean(d * d, axis=-1, keepdims=True)
    return d * lax.rsqrt(var + LN_EPS) * g + b


def _outproj_kernel(at_ref, rn_ref, x_ref, ga_ref, gr_ref, wo_ref, g1_ref, b1_ref, wrh_ref, wrl_ref,
                    br_ref, x1_ref, comb_ref):
    a = at_ref[...]
    rn = rn_ref[...]
    ha = a * lax.rsqrt(jnp.mean(a * a, axis=-1, keepdims=True) + RMS_EPS) * ga_ref[...]
    hr = rn * lax.rsqrt(jnp.mean(rn * rn, axis=-1, keepdims=True) + RMS_EPS) * gr_ref[...]
    heads = jnp.concatenate([ha, hr], axis=1).astype(BF16)
    mix = jnp.dot(heads, wo_ref[...], preferred_element_type=F32)
    x1 = _layer_norm(ALPHA * x_ref[...] + mix, g1_ref[...], b1_ref[...])
    x1_ref[...] = x1

    xh = x1.astype(BF16)
    xl = (x1 - xh.astype(F32)).astype(BF16)
    lg = (jnp.dot(xh, wrh_ref[...], preferred_element_type=F32)
          + jnp.dot(xl, wrh_ref[...], preferred_element_type=F32)
          + jnp.dot(xh, wrl_ref[...], preferred_element_type=F32)) + br_ref[...]
    lane = lax.broadcasted_iota(jnp.int32, lg.shape, 1)
    lane_f = lane.astype(F32)
    big = 1e9
    isg = lane < N_EXP_GROUPS
    gmax = jnp.max(jnp.where(isg, lg, -jnp.inf), axis=-1, keepdims=True)
    pg_top = 1.0 / jnp.sum(jnp.where(isg, jnp.exp(lg - gmax), 0.0), axis=-1, keepdims=True)
    gi = jnp.min(jnp.where(isg & (lg == gmax), lane_f, big), axis=-1, keepdims=True)
    egrp = ((lane - N_EXP_GROUPS) // EXPERTS_PER_GROUP).astype(F32)
    ise = (lane >= N_EXP_GROUPS) & (lane < N_EXP_GROUPS + N_EXPERTS) & (egrp == gi)
    emax = jnp.max(jnp.where(ise, lg, -jnp.inf), axis=-1, keepdims=True)
    i1 = jnp.min(jnp.where(ise & (lg == emax), lane_f, big), axis=-1, keepdims=True)
    rest = ise & (lane_f != i1)
    m2 = jnp.max(jnp.where(rest, lg, -jnp.inf), axis=-1, keepdims=True)
    i2 = jnp.min(jnp.where(rest & (lg == m2), lane_f, big), axis=-1, keepdims=True)
    e2 = jnp.exp(m2 - emax)
    inv = pg_top / (1.0 + e2)
    comb_ref[...] = jnp.where(lane_f == i1, inv, 0.0) + jnp.where(lane_f == i2, inv * e2, 0.0)


def _outproj(attn, rnn, x2, ga, gr, wo, g1, b1, wrh, wrl, br, N):
    tm = 512
    row = lambda w: pl.BlockSpec((tm, w), lambda i: (i, 0))
    full = lambda a: pl.BlockSpec(a.shape, lambda i: (0,) * a.ndim)
    return pl.pallas_call(
        _outproj_kernel,
        grid=(N // tm,),
        in_specs=[row(512), row(512), row(D_MODEL), full(ga), full(gr), full(wo), full(g1), full(b1),
                  full(wrh), full(wrl), full(br)],
        out_specs=[row(D_MODEL), row(128)],
        out_shape=[jax.ShapeDtypeStruct((N, D_MODEL), F32), jax.ShapeDtypeStruct((N, 128), F32)],
        compiler_params=_cparams(("parallel",), VMEM_LIMIT),
        name="outproj",
    )(attn, rnn, x2, ga, gr, wo, g1, b1, wrh, wrl, br)


def _moe_kernel(x1_ref, comb_ref, p_ref, tri_ref, wup_ref, wdn_ref, wg_ref, wp_ref, g2_ref, b2_ref, o_ref,
                acc_ref, xb_ref, rank_ref, rankt_ref, combt_ref):
    pi = pl.program_id(1)
    n_sub = x1_ref.shape[0] // MOE_SUB
    sub_rows = [slice(h * MOE_SUB, (h + 1) * MOE_SUB) for h in range(n_sub)]
    lane_col = N_EXP_GROUPS

    @pl.when(pi == 0)
    def _():
        x1 = x1_ref[...]
        xb = x1.astype(BF16)
        xb_ref[...] = xb
        gate = jax.nn.sigmoid(jnp.dot(xb, wg_ref[...], preferred_element_type=F32))
        ple = gate * jnp.dot(p_ref[...].astype(BF16), wp_ref[...], preferred_element_type=F32)
        acc_ref[...] = ALPHA * x1 + ple
        for rows in sub_rows:
            comb = comb_ref[rows, :]
            chosen = comb > 0.0
            rank = jnp.dot(tri_ref[...], jnp.where(chosen, 1.0, 0.0).astype(BF16),
                           preferred_element_type=F32)
            rank = jnp.where(chosen, rank, -1.0)
            rank_ref[rows, :] = rank
            rankt_ref[:, rows] = rank.T
            combt_ref[:, rows] = comb.T

    lane = lax.broadcasted_iota(jnp.int32, (MOE_SUB, 128), 1)
    slot_r = lax.broadcasted_iota(jnp.int32, (MOE_SLOTS, MOE_SUB), 0).astype(F32)
    slot_c = lax.broadcasted_iota(jnp.int32, (MOE_SUB, MOE_SLOTS), 1).astype(F32)

    def route_pair(base):
        chains = [(rows, k) for rows in sub_rows for k in range(2)]
        cmask, xc, u, hsw, y, ys, ct = {}, {}, {}, {}, {}, {}, {}
        for ch in chains:
            rows, k = ch
            rk_row = rankt_ref[pl.ds(2 * pi + k + lane_col, 1), rows] - base
            cmask[ch] = rk_row == slot_r
            xc[ch] = jnp.dot(jnp.where(cmask[ch], 1.0, 0.0).astype(BF16), xb_ref[rows, :],
                             preferred_element_type=F32).astype(BF16)
        for ch in chains:
            u[ch] = jnp.dot(xc[ch], wup_ref[ch[1]], preferred_element_type=F32)
        for ch in chains:
            ua = u[ch][:, :D_FF]
            hsw[ch] = (ua * jax.nn.sigmoid(ua) * u[ch][:, D_FF:]).astype(BF16)
        for ch in chains:
            y[ch] = jnp.dot(hsw[ch], wdn_ref[ch[1]], preferred_element_type=F32)
        for ch in chains:
            rows, k = ch
            e_lane = 2 * pi + k + lane_col
            w_row = combt_ref[pl.ds(e_lane, 1), rows]
            w_slot = jnp.sum(jnp.where(cmask[ch], w_row, 0.0), axis=1, keepdims=True)
            ys[ch] = (w_slot * y[ch]).astype(BF16)
            rk_col = jnp.sum(jnp.where(lane == e_lane, rank_ref[rows, :], 0.0), axis=1,
                             keepdims=True) - base
            ct[ch] = jnp.where(rk_col == slot_c, 1.0, 0.0).astype(BF16)
        for rows in sub_rows:
            pair = [(rows, k) for k in range(2)]
            ctp = jnp.concatenate([ct[ch] for ch in pair], axis=1)
            acc_ref[rows, :] = acc_ref[rows, :] + jnp.dot(
                ctp, jnp.concatenate([ys[ch] for ch in pair], axis=0), preferred_element_type=F32)

    route_pair(0.0)

    top_rank = jnp.max(jnp.maximum(rankt_ref[pl.ds(2 * pi + lane_col, 1), :],
                                   rankt_ref[pl.ds(2 * pi + 1 + lane_col, 1), :]))

    @pl.when(top_rank >= MOE_SLOTS)
    def _():
        def extra(sb, carry):
            route_pair((sb * MOE_SLOTS).astype(F32))
            return carry

        lax.fori_loop(1, top_rank.astype(jnp.int32) // MOE_SLOTS + 1, extra, 0)

    @pl.when(pi == N_EXPERTS // 2 - 1)
    def _():
        o_ref[...] = _layer_norm(acc_ref[...], g2_ref[...], b2_ref[...])


def _moe(x1, comb, p2, wup, wdn, wg, wp, g2, b2, N):
    tm = MOE_TILE
    tri = (jnp.arange(MOE_SUB)[None, :] < jnp.arange(MOE_SUB)[:, None]).astype(BF16)
    row = lambda w: pl.BlockSpec((tm, w), lambda i, e: (i, 0))
    full = lambda a: pl.BlockSpec(a.shape, lambda i, e: (0,) * a.ndim)
    return pl.pallas_call(
        _moe_kernel,
        grid=(N // tm, N_EXPERTS // 2),
        in_specs=[row(D_MODEL), row(128), row(D_PLE), full(tri),
                  pl.BlockSpec((2, D_MODEL, 2 * D_FF), lambda i, e: (e, 0, 0)),
                  pl.BlockSpec((2, D_FF, D_MODEL), lambda i, e: (e, 0, 0)),
                  full(wg), full(wp), full(g2), full(b2)],
        out_specs=row(D_MODEL),
        out_shape=jax.ShapeDtypeStruct((N, D_MODEL), F32),
        scratch_shapes=[pltpu.VMEM((tm, D_MODEL), F32),
                        pltpu.VMEM((tm, D_MODEL), BF16),
                        pltpu.VMEM((tm, 128), F32),
                        pltpu.VMEM((128, tm), F32),
                        pltpu.VMEM((128, tm), F32)],
        compiler_params=_cparams(("parallel", "arbitrary"), VMEM_LIMIT),
        name="moe",
    )(x1, comb, p2, tri, wup, wdn, wg, wp, g2, b2)


def _block_diag(w):
    n, k, j = w.shape
    eye = jnp.eye(n, dtype=w.dtype)
    return (w[:, :, None, :] * eye[:, None, :, None]).reshape(n * k, n * j)


def kernel(x, p, rel_bias, w_in, cmp_pe_k, cmp_pe_v, cmp_w_k, cmp_w_v, conv_w, conv_b, rg_w_a, rg_b_a,
           rg_w_x, rg_b_x, rg_lambda, attn_out_gain, rnn_out_gain, w_out, ln1_g, ln1_b, router_group_w,
           router_group_b, router_expert_w, router_expert_b, expert_w_up, expert_w_down, ple_w,
           ple_gate_w, ln2_g, ln2_b):
    B, T, _ = x.shape
    N = B * T
    assert T % FAR == 0 and T // SEL_BLOCK <= N_SLC and w_in.shape[0] == 1
    n_cmp = (T - CMP_LEN) // CMP_STRIDE + 1
    n_chunk = T // CMP_STRIDE
    assert C_OFF + n_chunk <= C_PAD
    row1 = lambda v: v.reshape(1, -1)

    w = w_in[0]
    c0 = D_ATTN
    kv = lambda k: w[:, c0 + k * D_KV:c0 + (k + 1) * D_KV]
    g0 = c0 + 6 * D_KV
    wq = w[:, :D_ATTN] * (HEAD_DIM ** -0.5 * LOG2E)
    w_main = jnp.concatenate([kv(2), kv(4), kv(0), kv(1), w[:, g0 + 24:]], axis=1).astype(BF16)
    w_t = jnp.concatenate([wq, kv(3), kv(5)], axis=1).T.astype(BF16)
    w_g = jnp.pad(w[:, g0:g0 + 24], ((0, 0), (0, GATE_ROWS - 24))).T.astype(BF16)

    x2 = x.reshape(N, D_MODEL)
    ks, kw, kc_raw, vc_raw, rx, ry, t_all, gates_t = _inproj(x2, w_main, w_t, w_g, B, T)

    def per_pos(wc):
        wl = wc.reshape(CMP_LEN, HEAD_DIM, HEAD_DIM)
        z = jnp.zeros_like(wl)
        return jnp.concatenate([jnp.concatenate([wl, z], axis=2), jnp.concatenate([z, wl], axis=2)],
                               axis=1).astype(BF16)

    pek = jnp.tile(cmp_pe_k[0], (1, N_GROUPS_KV))
    pev = jnp.tile(cmp_pe_v[0], (1, N_GROUPS_KV))
    grp_onehot = (jnp.arange(C_PAD)[:, None] // 8 == jnp.arange(128)[None, :]).astype(BF16)
    kc2, vct = _compress(kc_raw.reshape(B, T, 128), vc_raw.reshape(B, T, 128), pek, pev,
                         per_pos(cmp_w_k[0]), per_pos(cmp_w_v[0]), grp_onehot, B, T, n_cmp)

    onehot = (jnp.arange(T)[:, None] // SEL_BLOCK == jnp.arange(N_SLC)[None, :]).astype(BF16)
    kaug = jnp.concatenate([jnp.broadcast_to(onehot, (B, T, N_SLC)), ks.reshape(B, T, 128)], axis=2)
    lane256 = jnp.arange(256)
    pad_sel = jnp.broadcast_to((lane256 == N_SLC - 1).astype(BF16), (B, WINDOW, 256))
    pad_win = jnp.broadcast_to((lane256 == 128).astype(BF16), (B, WINDOW, 256))
    kaug = jnp.concatenate([pad_sel, kaug], axis=1)
    kw_pad = jnp.concatenate([pad_win, jnp.pad(kw.reshape(B, T, 128), ((0, 0), (0, 0), (0, 128)))], axis=1)
    ones_rows = jnp.concatenate([jnp.ones((B, 1, T), BF16), jnp.zeros((B, V_ROWS - 129, T), BF16)], axis=1)
    vst = jnp.pad(jnp.concatenate([t_all[:, 512:640], ones_rows], axis=1), ((0, 0), (0, 0), (WINDOW, 0)))
    vwt = jnp.pad(jnp.concatenate([t_all[:, 640:768], ones_rows], axis=1), ((0, 0), (0, 0), (WINDOW, 0)))
    ds, dw, bc = _bias_tiles(rel_bias)

    attn = _attention(t_all, gates_t, kc2, vct, kaug, vst, kw_pad, vwt, ds, dw, bc, B, T)

    sp = jax.nn.softplus(-rg_lambda[0].astype(F32))
    rnn = _rglru(rx.reshape(B, T, D_RNN), ry.reshape(B, T, D_RNN), conv_w[0], row1(conv_b[0]),
                 _block_diag(rg_w_a[0]).astype(BF16), row1(rg_b_a[0]),
                 _block_diag(rg_w_x[0]).astype(BF16), row1(rg_b_x[0]), row1(sp), B, T)

    wr = jnp.pad(jnp.concatenate([router_group_w[0], router_expert_w[0]], axis=1), ((0, 0), (0, 108)))
    wrh = wr.astype(BF16)
    wrl = (wr - wrh.astype(F32)).astype(BF16)
    br = jnp.pad(jnp.concatenate([router_group_b[0], router_expert_b[0]]), (0, 108)).reshape(1, 128)
    x1, comb = _outproj(attn.reshape(N, 512), rnn.reshape(N, D_RNN), x2, row1(attn_out_gain[0]),
                        row1(rnn_out_gain[0]), w_out[0].astype(BF16), row1(ln1_g[0]), row1(ln1_b[0]),
                        wrh, wrl, br, N)

    out = _moe(x1, comb, p[0].reshape(N, D_PLE), expert_w_up[0].astype(BF16),
               expert_w_down[0].astype(BF16), ple_gate_w[0].astype(BF16), ple_w[0].astype(BF16),
               row1(ln2_g[0]), row1(ln2_b[0]), N)
    return out.reshape(B, T, D_MODEL)
```

```python
import functools
import math

import jax
import jax.numpy as jnp
from jax import lax
from jax.experimental import pallas as pl
from jax.experimental.pallas import tpu as pltpu

F32 = jnp.float32
BF16 = jnp.bfloat16
NEG = -1e30
LOG2E = 1.4426950408889634

D_MODEL = 1024
HEAD_DIM = 64
N_HEADS = 8
N_GROUPS_KV = 2
GQA_REP = 4
D_ATTN = 512
D_RNN = 512
D_KV = 128
CMP_LEN = 32
CMP_STRIDE = 16
SEL_BLOCK = 64
N_SEL = 16
WINDOW = 512
N_BUCKETS = 32
MAX_DISTANCE = 128
N_EXPERTS = 16
EXPERTS_PER_GROUP = 4
N_EXP_GROUPS = 4
D_FF = 512
D_PLE = 256
ALPHA = 2.0 ** 0.25
LN_EPS = 1e-5
RMS_EPS = 1e-6
FORCE_BONUS = 1e4
RG_C = 8.0

TQ = 128
N_SLC = 128
C_PAD = 640
C_OFF = 16
FAR = 512
TAIL = WINDOW + TQ
BAND = 24
GATE_ROWS = 32
V_ROWS = 144
MOE_TILE = 1024
MOE_SUB = 512
MOE_SLOTS = 128
VMEM_LIMIT = 56 * 1024 * 1024


def _cparams(sem, vmem=None):
    return pltpu.CompilerParams(dimension_semantics=sem, vmem_limit_bytes=vmem)


def _inproj_kernel(x_ref, wm_ref, wt_ref, wg_ref, ks_ref, kw_ref, kc_ref, vc_ref, rx_ref, ry_ref,
                   t_ref, gt_ref):
    xb = x_ref[...].astype(BF16)

    def mm(lo, hi):
        return jnp.dot(xb, wm_ref[:, lo:hi], preferred_element_type=F32)

    ks_ref[...] = mm(0, 128).astype(BF16)
    kw_ref[...] = mm(128, 256).astype(BF16)
    kc_ref[...] = mm(256, 384)
    vc_ref[...] = mm(384, 512)
    rx_ref[...] = mm(512, 1024)
    ry_ref[...] = mm(1024, 1536)
    dn = (((1,), (1,)), ((), ()))
    t_ref[...] = lax.dot_general(wt_ref[...], xb, dn, preferred_element_type=F32).astype(BF16)
    gt_ref[...] = jax.nn.sigmoid(lax.dot_general(wg_ref[...], xb, dn, preferred_element_type=F32))


def _inproj(x2, w_main, w_t, w_g, B, T):
    N = B * T
    tm = 512
    nt = T // tm
    row = lambda w: pl.BlockSpec((tm, w), lambda i: (i, 0))
    full = lambda a: pl.BlockSpec(a.shape, lambda i: (0,) * a.ndim)
    tr = lambda r: pl.BlockSpec((None, r, tm), lambda i: (i // nt, 0, i % nt))
    return pl.pallas_call(
        _inproj_kernel,
        grid=(N // tm,),
        in_specs=[row(D_MODEL), full(w_main), full(w_t), full(w_g)],
        out_specs=[row(128), row(128), row(128), row(128), row(512), row(512), tr(768), tr(GATE_ROWS)],
        out_shape=[jax.ShapeDtypeStruct((N, 128), BF16),
                   jax.ShapeDtypeStruct((N, 128), BF16),
                   jax.ShapeDtypeStruct((N, 128), F32),
                   jax.ShapeDtypeStruct((N, 128), F32),
                   jax.ShapeDtypeStruct((N, 512), F32),
                   jax.ShapeDtypeStruct((N, 512), F32),
                   jax.ShapeDtypeStruct((B, 768, T), BF16),
                   jax.ShapeDtypeStruct((B, GATE_ROWS, T), F32)],
        compiler_params=_cparams(("parallel",), VMEM_LIMIT),
        name="inproj",
    )(x2, w_main, w_t, w_g)


def _compress_kernel(kr_ref, vr_ref, pek_ref, pev_ref, wk_ref, wv_ref, oh_ref, kc_ref, vct_ref, *,
                     n_cmp, n_chunk):
    def compress(raw_ref, pe_ref, w_ref):
        lo = jnp.zeros((n_chunk, 2 * HEAD_DIM), F32)
        hi = jnp.zeros((n_chunk, 2 * HEAD_DIM), F32)
        for j in range(CMP_STRIDE):
            a = raw_ref[pl.ds(j, n_chunk, stride=CMP_STRIDE), :]
            lo = lo + jnp.dot((a + pe_ref[j:j + 1, :]).astype(BF16), w_ref[j],
                              preferred_element_type=F32)
            hi = hi + jnp.dot((a + pe_ref[CMP_STRIDE + j:CMP_STRIDE + j + 1, :]).astype(BF16),
                              w_ref[CMP_STRIDE + j], preferred_element_type=F32)
        hi = pltpu.roll(hi, n_chunk - 1, axis=0)
        rid = lax.broadcasted_iota(jnp.int32, (n_chunk, 2 * HEAD_DIM), 0)
        out = jnp.where(rid < n_cmp, lo + hi, 0.0)
        return jnp.concatenate([jnp.zeros((C_OFF, 2 * HEAD_DIM), F32), out,
                                jnp.zeros((C_PAD - C_OFF - n_chunk, 2 * HEAD_DIM), F32)], axis=0)

    kc_ref[:, 0:2 * HEAD_DIM] = compress(kr_ref, pek_ref, wk_ref).astype(BF16)
    kc_ref[:, 2 * HEAD_DIM:] = oh_ref[...]
    vct_ref[...] = compress(vr_ref, pev_ref, wv_ref).T.astype(BF16)


def _compress(kc_raw, vc_raw, pek, pev, wk, wv, grp_onehot, B, T, n_cmp):
    n_chunk = T // CMP_STRIDE
    blk = pl.BlockSpec((None, T, 2 * HEAD_DIM), lambda b: (b, 0, 0))
    full = lambda a: pl.BlockSpec(a.shape, lambda b: (0,) * a.ndim)
    return pl.pallas_call(
        functools.partial(_compress_kernel, n_cmp=n_cmp, n_chunk=n_chunk),
        grid=(B,),
        in_specs=[blk, blk, full(pek), full(pev), full(wk), full(wv), full(grp_onehot)],
        out_specs=[pl.BlockSpec((None, C_PAD, 256), lambda b: (b, 0, 0)),
                   pl.BlockSpec((None, 2 * HEAD_DIM, C_PAD), lambda b: (b, 0, 0))],
        out_shape=[jax.ShapeDtypeStruct((B, C_PAD, 256), BF16),
                   jax.ShapeDtypeStruct((B, 2 * HEAD_DIM, C_PAD), BF16)],
        compiler_params=_cparams(("parallel",), VMEM_LIMIT),
        name="compress",
    )(kc_raw, vc_raw, pek, pev, wk, wv, grp_onehot)


def _bias_kernel(tab_ref, dt_ref, bt_ref, dc_ref, bc_ref, os_ref, ow_ref, oc_ref):
    h = pl.program_id(0) * GQA_REP + pl.program_id(1)

    def lookup(bk):
        out = jnp.zeros(bk.shape, F32)
        for b in range(N_BUCKETS):
            out = jnp.where(bk == b, tab_ref[h, b], out)
        return (out - tab_ref[h, N_BUCKETS - 1]) * LOG2E

    dt = dt_ref[...]
    v = lookup(bt_ref[...])
    os_ref[...] = jnp.where(dt >= 0, v, NEG)
    ow_ref[...] = jnp.where((dt >= 0) & (dt < WINDOW), v, NEG)
    oc_ref[...] = jnp.where(dc_ref[...] >= 0, lookup(bc_ref[...]), NEG)


def _t5_bucket(dist):
    max_exact = N_BUCKETS // 2
    d = jnp.maximum(dist, 0)
    df = jnp.maximum(d, 1).astype(F32)
    large = max_exact + (jnp.log(df / max_exact) / math.log(MAX_DISTANCE / max_exact)
                         * (N_BUCKETS - max_exact)).astype(jnp.int32)
    large = jnp.minimum(large, N_BUCKETS - 1)
    return jnp.where(d < max_exact, d, large)


def _bias_tiles(rel_bias):
    tq = jnp.arange(TQ, dtype=jnp.int32)
    dt = tq[None, :] + WINDOW - jnp.arange(TAIL, dtype=jnp.int32)[:, None]
    cc = jnp.arange(BAND, dtype=jnp.int32) - C_OFF
    dc = tq[None, :] - (cc[:, None] * CMP_STRIDE + CMP_LEN - 1)
    full = lambda a: pl.BlockSpec(a.shape, lambda g, r: (0,) * a.ndim)
    out = lambda rows: pl.BlockSpec((None, rows, TQ), lambda g, r: (g, 0, r))
    return pl.pallas_call(
        _bias_kernel,
        grid=(N_GROUPS_KV, GQA_REP),
        in_specs=[pl.BlockSpec(memory_space=pltpu.SMEM), full(dt), full(dt), full(dc), full(dc)],
        out_specs=[out(TAIL), out(TAIL), out(BAND)],
        out_shape=[jax.ShapeDtypeStruct((N_GROUPS_KV, TAIL, GQA_REP * TQ), F32),
                   jax.ShapeDtypeStruct((N_GROUPS_KV, TAIL, GQA_REP * TQ), F32),
                   jax.ShapeDtypeStruct((N_GROUPS_KV, BAND, GQA_REP * TQ), F32)],
        compiler_params=_cparams(("arbitrary", "arbitrary")),
        name="biastile",
    )(rel_bias.T, dt, _t5_bucket(dt), dc, _t5_bucket(dc))


def _attn_kernel(qt_ref, gt_ref, kc_ref, vct_ref, kaug_ref, vst_ref, kw_ref, vwt_ref, ds_ref, dw_ref,
                 bc_ref, o_ref, *scratch, n_far_max):
    sc_ref, sw_ref, ps_ref = (scratch[2 * k:2 * k + 2] for k in range(3))
    qa_ref, m_ref, acc_ref, sb_ref = scratch[6:]
    i = pl.program_id(1)
    qs = pl.multiple_of(i * TQ, TQ)
    cols4 = GQA_REP * TQ
    groups = range(N_GROUPS_KV)
    nfar = jnp.maximum(i - 1, 0) // 4
    zeros_q = jnp.zeros((HEAD_DIM, cols4), BF16)
    qpad = []
    for g in groups:
        qgt = jnp.concatenate(
            [qt_ref[(GQA_REP * g + r) * HEAD_DIM:(GQA_REP * g + r + 1) * HEAD_DIM, :]
             for r in range(GQA_REP)], axis=1)
        qpad.append(jnp.concatenate([qgt, zeros_q] if g == 0 else [zeros_q, qgt], axis=0))

    def chunk_start(c):
        return pl.multiple_of(WINDOW + jnp.minimum(c, n_far_max - 1) * FAR, FAR)

    def scores(c, par):
        kch = kaug_ref[pl.ds(chunk_start(c), FAR), :]
        for g in groups:
            sb_ref[2 * g + par] = jnp.dot(kch, qa_ref[g], preferred_element_type=F32)

    def consume(c, par):
        vch = vst_ref[:, pl.ds(chunk_start(c), FAR)]
        live = c < nfar
        for g in groups:
            sf = sb_ref[2 * g + par]
            mprev = m_ref[g]
            mcand = jnp.maximum(mprev, jnp.max(sf, axis=0, keepdims=True))
            mnew = jnp.where(live, mcand, mprev)
            pf = jnp.exp2(sf - jnp.where(live, mcand, -NEG)).astype(BF16)
            acc_ref[g] = jnp.exp2(mprev - mnew) * acc_ref[g] + jnp.dot(
                vch, pf, preferred_element_type=F32)
            m_ref[g] = mnew

    def far_pair(kk, carry):
        c = 2 * kk
        scores(c + 1, 1)
        consume(c, 0)
        scores(c + 2, 0)
        consume(c + 1, 1)
        return carry

    grp = lax.broadcasted_iota(jnp.int32, (128, cols4), 0)
    gmask = jnp.where((grp < C_OFF // 8) | (grp >= i + BAND // 8), NEG, 0.0).astype(BF16)
    band0 = pl.multiple_of(i * 8, 8)
    tq_col = lax.broadcasted_iota(jnp.int32, (1, cols4), 1) % TQ
    has_cmp = qs + tq_col >= CMP_LEN - 1
    row_w = lax.broadcasted_iota(jnp.int32, (128, cols4), 0)
    neg_row = jnp.where(row_w == 0, NEG, 0.0).astype(BF16)
    kwc = kw_ref[pl.ds(qs, TAIL), :]
    vwc = vwt_ref[:, pl.ds(qs, TAIL)]
    for g in groups:
        sc_ref[g][...] = jnp.dot(kc_ref[...], jnp.concatenate([qpad[g], gmask], axis=0),
                                 preferred_element_type=F32)
        sc_ref[g][pl.ds(band0, BAND), :] = sc_ref[g][pl.ds(band0, BAND), :] + bc_ref[g]
    for g in groups:
        sw_ref[g][...] = jnp.dot(kwc, jnp.concatenate([qpad[g], neg_row], axis=0),
                                 preferred_element_type=F32)
    o_c = []
    for g in groups:
        s = sc_ref[g][...]
        e = jnp.exp2(s - jnp.max(s, axis=0, keepdims=True))
        pinv = jnp.where(has_cmp, 1.0 / jnp.sum(e, axis=0, keepdims=True), 0.0)
        p = e * pinv
        o_c.append(jnp.dot(vct_ref[...], p.astype(BF16), preferred_element_type=F32))
        ps_ref[g][...] = p[:, 0:TQ] + p[:, TQ:2 * TQ] + p[:, 2 * TQ:3 * TQ] + p[:, 3 * TQ:4 * TQ]

    o_w = []
    for g in groups:
        sw = sw_ref[g][...] + dw_ref[g]
        pw = jnp.exp2(sw - jnp.max(sw, axis=0, keepdims=True)).astype(BF16)
        acc_w = jnp.dot(vwc, pw, preferred_element_type=F32)
        o_w.append(acc_w[g * HEAD_DIM:(g + 1) * HEAD_DIM, :]
                   * (1.0 / acc_w[2 * HEAD_DIM:2 * HEAD_DIM + 1, :]))

    sidx = lax.broadcasted_iota(jnp.int32, (N_SLC, TQ), 0)
    sidx_f = sidx.astype(F32)
    cur = (qs + lax.broadcasted_iota(jnp.int32, (N_SLC, TQ), 1)) // SEL_BLOCK
    forced = (sidx == 0) | (sidx == cur) | (sidx == cur - 1)
    kt = kaug_ref[pl.ds(qs, TAIL), :]
    vt = vst_ref[:, pl.ds(qs, TAIL)]
    for g in groups:
        imp = ps_ref[g][pl.ds(C_OFF - 1, N_SLC, stride=4), :]
        for d in range(1, 5):
            imp = imp + ps_ref[g][pl.ds(C_OFF - 1 + d, N_SLC, stride=4), :]
        score = jnp.where(forced, -jnp.inf, jnp.where(sidx <= cur, imp, NEG))
        mbt = jnp.where(forced, 0.0, NEG)
        for _ in range(N_SEL - 3):
            cm = jnp.max(score, axis=0, keepdims=True)
            first = jnp.min(jnp.where(score == cm, sidx_f, float(N_SLC)), axis=0, keepdims=True)
            pick = sidx_f == first
            mbt = jnp.where(pick, 0.0, mbt)
            score = jnp.where(pick, -jnp.inf, score)
        qa_ref[g] = jnp.concatenate([jnp.concatenate([mbt.astype(BF16)] * GQA_REP, axis=1), qpad[g]],
                                    axis=0)
        mbt_tail = jnp.where(sidx < nfar * (FAR // SEL_BLOCK), NEG, mbt).astype(BF16)
        qa_tail = jnp.concatenate([jnp.concatenate([mbt_tail] * GQA_REP, axis=1), qpad[g]], axis=0)
        sc_ref[g][...] = jnp.dot(kt, qa_tail, preferred_element_type=F32)
    scores(0, 0)
    for g in groups:
        st = sc_ref[g][...] + ds_ref[g]
        mt = jnp.max(st, axis=0, keepdims=True)
        m_ref[g] = mt
        acc_ref[g] = jnp.dot(vt, jnp.exp2(st - mt).astype(BF16), preferred_element_type=F32)

    lax.fori_loop(0, (nfar + 1) // 2, far_pair, 0)

    out_rows = []
    for g in groups:
        dsl = slice(g * HEAD_DIM, (g + 1) * HEAD_DIM)
        o_s = acc_ref[g, dsl, :] * (1.0 / acc_ref[g, 2 * HEAD_DIM:2 * HEAD_DIM + 1, :])
        for r in range(GQA_REP):
            cs = slice(r * TQ, (r + 1) * TQ)
            gc = 3 * (GQA_REP * g + r)
            out_rows.append(gt_ref[gc:gc + 1, :] * o_c[g][dsl, cs] + gt_ref[gc + 1:gc + 2, :] * o_s[:, cs]
                            + gt_ref[gc + 2:gc + 3, :] * o_w[g][:, cs])
    o_ref[...] = jnp.concatenate(out_rows, axis=0).T


def _attention(t_all, gates_t, kc2, vct, kaug, vst, kw_pad, vwt, ds, dw, bc, B, T):
    tp = T + WINDOW
    per_b = lambda shape: pl.BlockSpec((None,) + shape, lambda b, i: (b,) + (0,) * len(shape))
    const = lambda a: pl.BlockSpec(a.shape, lambda b, i: (0,) * a.ndim)
    cols4 = GQA_REP * TQ
    per_group = lambda shape: [shape] * N_GROUPS_KV
    return pl.pallas_call(
        functools.partial(_attn_kernel, n_far_max=T // FAR),
        grid=(B, T // TQ),
        in_specs=[pl.BlockSpec((None, 512, TQ), lambda b, i: (b, 0, i)),
                  pl.BlockSpec((None, GATE_ROWS, TQ), lambda b, i: (b, 0, i)),
                  per_b((C_PAD, 256)),
                  per_b((128, C_PAD)),
                  per_b((tp, 256)),
                  per_b((V_ROWS, tp)),
                  per_b((tp, 256)),
                  per_b((V_ROWS, tp)),
                  const(ds), const(dw), const(bc)],
        out_specs=pl.BlockSpec((None, TQ, 512), lambda b, i: (b, i, 0)),
        out_shape=jax.ShapeDtypeStruct((B, T, 512), F32),
        scratch_shapes=(per_group(pltpu.VMEM((C_PAD, cols4), F32))
                        + per_group(pltpu.VMEM((TAIL, cols4), F32))
                        + per_group(pltpu.VMEM((C_PAD, TQ), F32))
                        + [pltpu.VMEM((N_GROUPS_KV, 256, cols4), BF16),
                           pltpu.VMEM((N_GROUPS_KV, 1, cols4), F32),
                           pltpu.VMEM((N_GROUPS_KV, V_ROWS, cols4), F32),
                           pltpu.VMEM((2 * N_GROUPS_KV, FAR, cols4), F32)]),
        compiler_params=_cparams(("parallel", "arbitrary"), VMEM_LIMIT),
        name="attn",
    )(t_all, gates_t, kc2, vct, kaug, vst, kw_pad, vwt, ds, dw, bc)


def _rglru_kernel(*refs, L):
    J = L // 8
    n_col = D_RNN // 128
    rx_refs, ry_refs = refs[:n_col], refs[n_col:2 * n_col]
    cw_ref, cb_ref, wa_ref, ba_ref, wx_ref, bx_ref, sp_ref, o_ref, xprev_ref, h_ref, os_ref = refs[2 * n_col:]

    @pl.when(pl.program_id(1) == 0)
    def _():
        xprev_ref[...] = jnp.zeros(xprev_ref.shape, F32)
        h_ref[...] = jnp.zeros(h_ref.shape, F32)

    def strided(col_refs):
        return jnp.concatenate(
            [jnp.concatenate([ref[pl.ds(j, 8, stride=J), :] for j in range(J)], axis=0)
             for ref in col_refs], axis=1)

    xp = strided(rx_refs)
    yp = strided(ry_refs)
    sub = lax.broadcasted_iota(jnp.int32, (8, D_RNN), 0)

    def delayed(d):
        heads = []
        for j in range(d):
            src = pltpu.roll(xp[(J + j - d) * 8:(J + j - d + 1) * 8], 1, axis=0)
            heads.append(jnp.where(sub == 0, xprev_ref[8 + j - d:9 + j - d, :], src))
        return jnp.concatenate(heads + [xp[:L - 8 * d]], axis=0)

    xc = (cb_ref[...] + cw_ref[0:1, :] * delayed(3) + cw_ref[1:2, :] * delayed(2)
          + cw_ref[2:3, :] * delayed(1) + cw_ref[3:4, :] * xp)
    xprev_ref[...] = jnp.concatenate([ref[L - 8:L, :] for ref in rx_refs], axis=1)
    xcb = xc.astype(BF16)
    r = jax.nn.sigmoid(jnp.dot(xcb, wa_ref[...], preferred_element_type=F32) + ba_ref[...])
    ig = jax.nn.sigmoid(jnp.dot(xcb, wx_ref[...], preferred_element_type=F32) + bx_ref[...])
    a = jnp.exp(-RG_C * r * sp_ref[...])
    bt = jnp.sqrt(1.0 - a * a) * (ig * xc)

    h = jnp.zeros((8, D_RNN), F32)
    dec = jnp.ones((8, D_RNN), F32)
    hs, decs = [], []
    for j in range(J):
        aj = a[j * 8:(j + 1) * 8]
        h = aj * h + bt[j * 8:(j + 1) * 8]
        dec = aj * dec
        hs.append(h)
        decs.append(dec)
    carry = h_ref[0:1, :]
    carries = []
    for s in range(8):
        carries.append(carry)
        carry = dec[s:s + 1, :] * carry + h[s:s + 1, :]
    h_ref[...] = jnp.broadcast_to(carry, h_ref.shape)
    h_in = jnp.concatenate(carries, axis=0)
    cdf_c = math.sqrt(2.0 / math.pi)
    for j in range(J):
        y = yp[j * 8:(j + 1) * 8]
        cdf = 0.5 * (1.0 + jnp.tanh(cdf_c * (y + 0.044715 * (y * y * y))))
        res = (hs[j] + decs[j] * h_in) * (y * cdf)
        for c in range(n_col):
            os_ref[c, pl.ds(j, 8, stride=J), :] = res[:, c * 128:(c + 1) * 128]
    for c in range(n_col):
        o_ref[:, c * 128:(c + 1) * 128] = os_ref[c]


def _rglru(rx, ry, conv_w, conv_b, wa_bd, b_a, wx_bd, b_x, sp, B, T):
    L = 512
    blk = pl.BlockSpec((None, L, D_RNN), lambda b, t: (b, t, 0))
    n_col = D_RNN // 128
    cols = [pl.BlockSpec((None, L, 128), functools.partial(lambda b, t, c: (b, t, c), c=c))
            for c in range(n_col)]
    full = lambda a: pl.BlockSpec(a.shape, lambda b, t: (0,) * a.ndim)
    return pl.pallas_call(
        functools.partial(_rglru_kernel, L=L),
        grid=(B, T // L),
        in_specs=cols + cols + [full(conv_w), full(conv_b), full(wa_bd), full(b_a), full(wx_bd),
                                full(b_x), full(sp)],
        out_specs=blk,
        out_shape=jax.ShapeDtypeStruct((B, T, D_RNN), F32),
        scratch_shapes=[pltpu.VMEM((8, D_RNN), F32), pltpu.VMEM((8, D_RNN), F32),
                        pltpu.VMEM((n_col, L, 128), F32)],
        compiler_params=_cparams(("parallel", "arbitrary"), VMEM_LIMIT),
        name="rglru",
    )(*([rx] * n_col + [ry] * n_col), conv_w, conv_b, wa_bd, b_a, wx_bd, b_x, sp)


def _layer_norm(y, g, b):
    mu = jnp.mean(y, axis=-1, keepdims=True)
    d = y - mu
    var = jnp.mean(d * d, axis=-1, keepdims=True)
    return d * lax.rsqrt(var + LN_EPS) * g + b


def _outproj_kernel(at_ref, rn_ref, x_ref, ga_ref, gr_ref, wo_ref, g1_ref, b1_ref, wrh_ref, wrl_ref,
                    br_ref, x1_ref, comb_ref):
    a = at_ref[...]
    rn = rn_ref[...]
    ha = a * lax.rsqrt(jnp.mean(a * a, axis=-1, keepdims=True) + RMS_EPS) * ga_ref[...]
    hr = rn * lax.rsqrt(jnp.mean(rn * rn, axis=-1, keepdims=True) + RMS_EPS) * gr_ref[...]
    heads = jnp.concatenate([ha, hr], axis=1).astype(BF16)
    mix = jnp.dot(heads, wo_ref[...], preferred_element_type=F32)
    x1 = _layer_norm(ALPHA * x_ref[...] + mix, g1_ref[...], b1_ref[...])
    x1_ref[...] = x1

    xh = x1.astype(BF16)
    xl = (x1 - xh.astype(F32)).astype(BF16)
    lg = (jnp.dot(xh, wrh_ref[...], preferred_element_type=F32)
          + jnp.dot(xl, wrh_ref[...], preferred_element_type=F32)
          + jnp.dot(xh, wrl_ref[...], preferred_element_type=F32)) + br_ref[...]
    lane = lax.broadcasted_iota(jnp.int32, lg.shape, 1)
    lane_f = lane.astype(F32)
    big = 1e9
    isg = lane < N_EXP_GROUPS
    gmax = jnp.max(jnp.where(isg, lg, -jnp.inf), axis=-1, keepdims=True)
    pg_top = 1.0 / jnp.sum(jnp.where(isg, jnp.exp(lg - gmax), 0.0), axis=-1, keepdims=True)
    gi = jnp.min(jnp.where(isg & (lg == gmax), lane_f, big), axis=-1, keepdims=True)
    egrp = ((lane - N_EXP_GROUPS) // EXPERTS_PER_GROUP).astype(F32)
    ise = (lane >= N_EXP_GROUPS) & (lane < N_EXP_GROUPS + N_EXPERTS) & (egrp == gi)
    emax = jnp.max(jnp.where(ise, lg, -jnp.inf), axis=-1, keepdims=True)
    i1 = jnp.min(jnp.where(ise & (lg == emax), lane_f, big), axis=-1, keepdims=True)
    rest = ise & (lane_f != i1)
    m2 = jnp.max(jnp.where(rest, lg, -jnp.inf), axis=-1, keepdims=True)
    i2 = jnp.min(jnp.where(rest & (lg == m2), lane_f, big), axis=-1, keepdims=True)
    e2 = jnp.exp(m2 - emax)
    inv = pg_top / (1.0 + e2)
    comb_ref[...] = jnp.where(lane_f == i1, inv, 0.0) + jnp.where(lane_f == i2, inv * e2, 0.0)


def _outproj(attn, rnn, x2, ga, gr, wo, g1, b1, wrh, wrl, br, N):
    tm = 512
    row = lambda w: pl.BlockSpec((tm, w), lambda i: (i, 0))
    full = lambda a: pl.BlockSpec(a.shape, lambda i: (0,) * a.ndim)
    return pl.pallas_call(
        _outproj_kernel,
        grid=(N // tm,),
        in_specs=[row(512), row(512), row(D_MODEL), full(ga), full(gr), full(wo), full(g1), full(b1),
                  full(wrh), full(wrl), full(br)],
        out_specs=[row(D_MODEL), row(128)],
        out_shape=[jax.ShapeDtypeStruct((N, D_MODEL), F32), jax.ShapeDtypeStruct((N, 128), F32)],
        compiler_params=_cparams(("parallel",), VMEM_LIMIT),
        name="outproj",
    )(attn, rnn, x2, ga, gr, wo, g1, b1, wrh, wrl, br)


def _moe_kernel(x1_ref, comb_ref, p_ref, tri_ref, wup_ref, wdn_ref, wg_ref, wp_ref, g2_ref, b2_ref, o_ref,
                acc_ref, xb_ref, rank_ref, rankt_ref, combt_ref):
    pi = pl.program_id(1)
    n_sub = x1_ref.shape[0] // MOE_SUB
    sub_rows = [slice(h * MOE_SUB, (h + 1) * MOE_SUB) for h in range(n_sub)]
    lane_col = N_EXP_GROUPS

    @pl.when(pi == 0)
    def _():
        x1 = x1_ref[...]
        xb = x1.astype(BF16)
        xb_ref[...] = xb
        gate = jax.nn.sigmoid(jnp.dot(xb, wg_ref[...], preferred_element_type=F32))
        ple = gate * jnp.dot(p_ref[...].astype(BF16), wp_ref[...], preferred_element_type=F32)
        acc_ref[...] = ALPHA * x1 + ple
        for rows in sub_rows:
            comb = comb_ref[rows, :]
            chosen = comb > 0.0
            rank = jnp.dot(tri_ref[...], jnp.where(chosen, 1.0, 0.0).astype(BF16),
                           preferred_element_type=F32)
            rank = jnp.where(chosen, rank, -1.0)
            rank_ref[rows, :] = rank
            rankt_ref[:, rows] = rank.T
            combt_ref[:, rows] = comb.T

    lane = lax.broadcasted_iota(jnp.int32, (MOE_SUB, 128), 1)
    slot_r = lax.broadcasted_iota(jnp.int32, (MOE_SLOTS, MOE_SUB), 0).astype(F32)
    slot_c = lax.broadcasted_iota(jnp.int32, (MOE_SUB, MOE_SLOTS), 1).astype(F32)

    def route_pair(base):
        chains = [(rows, k) for rows in sub_rows for k in range(2)]
        cmask, xc, u, hsw, y, ys, ct = {}, {}, {}, {}, {}, {}, {}
        for ch in chains:
            rows, k = ch
            rk_row = rankt_ref[pl.ds(2 * pi + k + lane_col, 1), rows] - base
            cmask[ch] = rk_row == slot_r
            xc[ch] = jnp.dot(jnp.where(cmask[ch], 1.0, 0.0).astype(BF16), xb_ref[rows, :],
                             preferred_element_type=F32).astype(BF16)
        for ch in chains:
            u[ch] = jnp.dot(xc[ch], wup_ref[ch[1]], preferred_element_type=F32)
        for ch in chains:
            ua = u[ch][:, :D_FF]
            hsw[ch] = (ua * jax.nn.sigmoid(ua) * u[ch][:, D_FF:]).astype(BF16)
        for ch in chains:
            y[ch] = jnp.dot(hsw[ch], wdn_ref[ch[1]], preferred_element_type=F32)
        for ch in chains:
            rows, k = ch
            e_lane = 2 * pi + k + lane_col
            w_row = combt_ref[pl.ds(e_lane, 1), rows]
            w_slot = jnp.sum(jnp.where(cmask[ch], w_row, 0.0), axis=1, keepdims=True)
            ys[ch] = (w_slot * y[ch]).astype(BF16)
            rk_col = jnp.sum(jnp.where(lane == e_lane, rank_ref[rows, :], 0.0), axis=1,
                             keepdims=True) - base
            ct[ch] = jnp.where(rk_col == slot_c, 1.0, 0.0).astype(BF16)
        for rows in sub_rows:
            pair = [(rows, k) for k in range(2)]
            ctp = jnp.concatenate([ct[ch] for ch in pair], axis=1)
            acc_ref[rows, :] = acc_ref[rows, :] + jnp.dot(
                ctp, jnp.concatenate([ys[ch] for ch in pair], axis=0), preferred_element_type=F32)

    route_pair(0.0)

    top_rank = jnp.max(jnp.maximum(rankt_ref[pl.ds(2 * pi + lane_col, 1), :],
                                   rankt_ref[pl.ds(2 * pi + 1 + lane_col, 1), :]))

    @pl.when(top_rank >= MOE_SLOTS)
    def _():
        def extra(sb, carry):
            route_pair((sb * MOE_SLOTS).astype(F32))
            return carry

        lax.fori_loop(1, top_rank.astype(jnp.int32) // MOE_SLOTS + 1, extra, 0)

    @pl.when(pi == N_EXPERTS // 2 - 1)
    def _():
        o_ref[...] = _layer_norm(acc_ref[...], g2_ref[...], b2_ref[...])


def _moe(x1, comb, p2, wup, wdn, wg, wp, g2, b2, N):
    tm = MOE_TILE
    tri = (jnp.arange(MOE_SUB)[None, :] < jnp.arange(MOE_SUB)[:, None]).astype(BF16)
    row = lambda w: pl.BlockSpec((tm, w), lambda i, e: (i, 0))
    full = lambda a: pl.BlockSpec(a.shape, lambda i, e: (0,) * a.ndim)
    return pl.pallas_call(
        _moe_kernel,
        grid=(N // tm, N_EXPERTS // 2),
        in_specs=[row(D_MODEL), row(128), row(D_PLE), full(tri),
                  pl.BlockSpec((2, D_MODEL, 2 * D_FF), lambda i, e: (e, 0, 0)),
                  pl.BlockSpec((2, D_FF, D_MODEL), lambda i, e: (e, 0, 0)),
                  full(wg), full(wp), full(g2), full(b2)],
        out_specs=row(D_MODEL),
        out_shape=jax.ShapeDtypeStruct((N, D_MODEL), F32),
        scratch_shapes=[pltpu.VMEM((tm, D_MODEL), F32),
                        pltpu.VMEM((tm, D_MODEL), BF16),
                        pltpu.VMEM((tm, 128), F32),
                        pltpu.VMEM((128, tm), F32),
                        pltpu.VMEM((128, tm), F32)],
        compiler_params=_cparams(("parallel", "arbitrary"), VMEM_LIMIT),
        name="moe",
    )(x1, comb, p2, tri, wup, wdn, wg, wp, g2, b2)


def _block_diag(w):
    n, k, j = w.shape
    eye = jnp.eye(n, dtype=w.dtype)
    return (w[:, :, None, :] * eye[:, None, :, None]).reshape(n * k, n * j)


def kernel(x, p, rel_bias, w_in, cmp_pe_k, cmp_pe_v, cmp_w_k, cmp_w_v, conv_w, conv_b, rg_w_a, rg_b_a,
           rg_w_x, rg_b_x, rg_lambda, attn_out_gain, rnn_out_gain, w_out, ln1_g, ln1_b, router_group_w,
           router_group_b, router_expert_w, router_expert_b, expert_w_up, expert_w_down, ple_w,
           ple_gate_w, ln2_g, ln2_b):
    B, T, _ = x.shape
    N = B * T
    assert T % FAR == 0 and T // SEL_BLOCK <= N_SLC and w_in.shape[0] == 1
    n_cmp = (T - CMP_LEN) // CMP_STRIDE + 1
    n_chunk = T // CMP_STRIDE
    assert C_OFF + n_chunk <= C_PAD
    row1 = lambda v: v.reshape(1, -1)

    w = w_in[0]
    c0 = D_ATTN
    kv = lambda k: w[:, c0 + k * D_KV:c0 + (k + 1) * D_KV]
    g0 = c0 + 6 * D_KV
    wq = w[:, :D_ATTN] * (HEAD_DIM ** -0.5 * LOG2E)
    w_main = jnp.concatenate([kv(2), kv(4), kv(0), kv(1), w[:, g0 + 24:]], axis=1).astype(BF16)
    w_t = jnp.concatenate([wq, kv(3), kv(5)], axis=1).T.astype(BF16)
    w_g = jnp.pad(w[:, g0:g0 + 24], ((0, 0), (0, GATE_ROWS - 24))).T.astype(BF16)

    x2 = x.reshape(N, D_MODEL)
    ks, kw, kc_raw, vc_raw, rx, ry, t_all, gates_t = _inproj(x2, w_main, w_t, w_g, B, T)

    def per_pos(wc):
        wl = wc.reshape(CMP_LEN, HEAD_DIM, HEAD_DIM)
        z = jnp.zeros_like(wl)
        return jnp.concatenate([jnp.concatenate([wl, z], axis=2), jnp.concatenate([z, wl], axis=2)],
                               axis=1).astype(BF16)

    pek = jnp.tile(cmp_pe_k[0], (1, N_GROUPS_KV))
    pev = jnp.tile(cmp_pe_v[0], (1, N_GROUPS_KV))
    grp_onehot = (jnp.arange(C_PAD)[:, None] // 8 == jnp.arange(128)[None, :]).astype(BF16)
    kc2, vct = _compress(kc_raw.reshape(B, T, 128), vc_raw.reshape(B, T, 128), pek, pev,
                         per_pos(cmp_w_k[0]), per_pos(cmp_w_v[0]), grp_onehot, B, T, n_cmp)

    onehot = (jnp.arange(T)[:, None] // SEL_BLOCK == jnp.arange(N_SLC)[None, :]).astype(BF16)
    kaug = jnp.concatenate([jnp.broadcast_to(onehot, (B, T, N_SLC)), ks.reshape(B, T, 128)], axis=2)
    lane256 = jnp.arange(256)
    pad_sel = jnp.broadcast_to((lane256 == N_SLC - 1).astype(BF16), (B, WINDOW, 256))
    pad_win = jnp.broadcast_to((lane256 == 128).astype(BF16), (B, WINDOW, 256))
    kaug = jnp.concatenate([pad_sel, kaug], axis=1)
    kw_pad = jnp.concatenate([pad_win, jnp.pad(kw.reshape(B, T, 128), ((0, 0), (0, 0), (0, 128)))], axis=1)
    ones_rows = jnp.concatenate([jnp.ones((B, 1, T), BF16), jnp.zeros((B, V_ROWS - 129, T), BF16)], axis=1)
    vst = jnp.pad(jnp.concatenate([t_all[:, 512:640], ones_rows], axis=1), ((0, 0), (0, 0), (WINDOW, 0)))
    vwt = jnp.pad(jnp.concatenate([t_all[:, 640:768], ones_rows], axis=1), ((0, 0), (0, 0), (WINDOW, 0)))
    ds, dw, bc = _bias_tiles(rel_bias)

    attn = _attention(t_all, gates_t, kc2, vct, kaug, vst, kw_pad, vwt, ds, dw, bc, B, T)

    sp = jax.nn.softplus(-rg_lambda[0].astype(F32))
    rnn = _rglru(rx.reshape(B, T, D_RNN), ry.reshape(B, T, D_RNN), conv_w[0], row1(conv_b[0]),
                 _block_diag(rg_w_a[0]).astype(BF16), row1(rg_b_a[0]),
                 _block_diag(rg_w_x[0]).astype(BF16), row1(rg_b_x[0]), row1(sp), B, T)

    wr = jnp.pad(jnp.concatenate([router_group_w[0], router_expert_w[0]], axis=1), ((0, 0), (0, 108)))
    wrh = wr.astype(BF16)
    wrl = (wr - wrh.astype(F32)).astype(BF16)
    br = jnp.pad(jnp.concatenate([router_group_b[0], router_expert_b[0]]), (0, 108)).reshape(1, 128)
    x1, comb = _outproj(attn.reshape(N, 512), rnn.reshape(N, D_RNN), x2, row1(attn_out_gain[0]),
                        row1(rnn_out_gain[0]), w_out[0].astype(BF16), row1(ln1_g[0]), row1(ln1_b[0]),
                        wrh, wrl, br, N)

    out = _moe(x1, comb, p[0].reshape(N, D_PLE), expert_w_up[0].astype(BF16),
               expert_w_down[0].astype(BF16), ple_gate_w[0].astype(BF16), ple_w[0].astype(BF16),
               row1(ln2_g[0]), row1(ln2_b[0]), N)
    return out.reshape(B, T, D_MODEL)
```

```python
import functools
import math

import jax
import jax.numpy as jnp
from jax import lax
from jax.experimental import pallas as pl
from jax.experimental.pallas import tpu as pltpu

F32 = jnp.float32
BF16 = jnp.bfloat16
NEG = -1e30
LOG2E = 1.4426950408889634

D_MODEL = 1024
HEAD_DIM = 64
N_HEADS = 8
N_GROUPS_KV = 2
GQA_REP = 4
D_ATTN = 512
D_RNN = 512
D_KV = 128
CMP_LEN = 32
CMP_STRIDE = 16
SEL_BLOCK = 64
N_SEL = 16
WINDOW = 512
N_BUCKETS = 32
MAX_DISTANCE = 128
N_EXPERTS = 16
EXPERTS_PER_GROUP = 4
N_EXP_GROUPS = 4
D_FF = 512
D_PLE = 256
ALPHA = 2.0 ** 0.25
LN_EPS = 1e-5
RMS_EPS = 1e-6
FORCE_BONUS = 1e4
RG_C = 8.0

TQ = 128
TILES = 2
N_SLC = 128
C_PAD = 640
C_OFF = 16
FAR = 512
TAIL = WINDOW + TQ
BAND = 24
GATE_ROWS = 32
V_ROWS = 80
MOE_TILE = 1024
MOE_SUB = 512
MOE_SLOTS = 128
VMEM_LIMIT = 56 * 1024 * 1024


def _cparams(sem, vmem=None):
    return pltpu.CompilerParams(dimension_semantics=sem, vmem_limit_bytes=vmem)


def _inproj_kernel(x_ref, wm_ref, wt_ref, wg_ref, ks_ref, kw_ref, kc_ref, vc_ref, rx_ref, ry_ref,
                   t_ref, gt_ref):
    xb = x_ref[...].astype(BF16)

    def mm(lo, hi):
        return jnp.dot(xb, wm_ref[:, lo:hi], preferred_element_type=F32)

    ks_ref[...] = mm(0, 128).astype(BF16)
    kw_ref[...] = mm(128, 256).astype(BF16)
    kc_ref[...] = mm(256, 384)
    vc_ref[...] = mm(384, 512)
    rx_ref[...] = mm(512, 1024)
    ry_ref[...] = mm(1024, 1536)
    dn = (((1,), (1,)), ((), ()))
    t_ref[...] = lax.dot_general(wt_ref[...], xb, dn, preferred_element_type=F32).astype(BF16)
    gt_ref[...] = jax.nn.sigmoid(lax.dot_general(wg_ref[...], xb, dn, preferred_element_type=F32))


def _inproj(x2, w_main, w_t, w_g, B, T):
    N = B * T
    tm = 512
    nt = T // tm
    row = lambda w: pl.BlockSpec((tm, w), lambda i: (i, 0))
    full = lambda a: pl.BlockSpec(a.shape, lambda i: (0,) * a.ndim)
    tr = lambda r: pl.BlockSpec((None, r, tm), lambda i: (i // nt, 0, i % nt))
    return pl.pallas_call(
        _inproj_kernel,
        grid=(N // tm,),
        in_specs=[row(D_MODEL), full(w_main), full(w_t), full(w_g)],
        out_specs=[row(128), row(128), row(128), row(128), row(512), row(512), tr(768), tr(GATE_ROWS)],
        out_shape=[jax.ShapeDtypeStruct((N, 128), BF16),
                   jax.ShapeDtypeStruct((N, 128), BF16),
                   jax.ShapeDtypeStruct((N, 128), F32),
                   jax.ShapeDtypeStruct((N, 128), F32),
                   jax.ShapeDtypeStruct((N, 512), F32),
                   jax.ShapeDtypeStruct((N, 512), F32),
                   jax.ShapeDtypeStruct((B, 768, T), BF16),
                   jax.ShapeDtypeStruct((B, GATE_ROWS, T), F32)],
        compiler_params=_cparams(("parallel",), VMEM_LIMIT),
        name="inproj",
    )(x2, w_main, w_t, w_g)


def _compress_kernel(kr_ref, vr_ref, pek_ref, pev_ref, wk_ref, wv_ref, oh_ref, kc_ref, vct_ref, *,
                     n_cmp, n_chunk):
    def compress(raw_ref, pe_ref, w_ref):
        lo = jnp.zeros((n_chunk, 2 * HEAD_DIM), F32)
        hi = jnp.zeros((n_chunk, 2 * HEAD_DIM), F32)
        for j in range(CMP_STRIDE):
            a = raw_ref[pl.ds(j, n_chunk, stride=CMP_STRIDE), :]
            lo = lo + jnp.dot((a + pe_ref[j:j + 1, :]).astype(BF16), w_ref[j],
                              preferred_element_type=F32)
            hi = hi + jnp.dot((a + pe_ref[CMP_STRIDE + j:CMP_STRIDE + j + 1, :]).astype(BF16),
                              w_ref[CMP_STRIDE + j], preferred_element_type=F32)
        hi = pltpu.roll(hi, n_chunk - 1, axis=0)
        rid = lax.broadcasted_iota(jnp.int32, (n_chunk, 2 * HEAD_DIM), 0)
        out = jnp.where(rid < n_cmp, lo + hi, 0.0)
        return jnp.concatenate([jnp.zeros((C_OFF, 2 * HEAD_DIM), F32), out,
                                jnp.zeros((C_PAD - C_OFF - n_chunk, 2 * HEAD_DIM), F32)], axis=0)

    kc_ref[:, 0:2 * HEAD_DIM] = compress(kr_ref, pek_ref, wk_ref).astype(BF16)
    kc_ref[:, 2 * HEAD_DIM:] = oh_ref[...]
    vct_ref[...] = compress(vr_ref, pev_ref, wv_ref).T.astype(BF16)


def _compress(kc_raw, vc_raw, pek, pev, wk, wv, grp_onehot, B, T, n_cmp):
    n_chunk = T // CMP_STRIDE
    blk = pl.BlockSpec((None, T, 2 * HEAD_DIM), lambda b: (b, 0, 0))
    full = lambda a: pl.BlockSpec(a.shape, lambda b: (0,) * a.ndim)
    return pl.pallas_call(
        functools.partial(_compress_kernel, n_cmp=n_cmp, n_chunk=n_chunk),
        grid=(B,),
        in_specs=[blk, blk, full(pek), full(pev), full(wk), full(wv), full(grp_onehot)],
        out_specs=[pl.BlockSpec((None, C_PAD, 256), lambda b: (b, 0, 0)),
                   pl.BlockSpec((None, 2 * HEAD_DIM, C_PAD), lambda b: (b, 0, 0))],
        out_shape=[jax.ShapeDtypeStruct((B, C_PAD, 256), BF16),
                   jax.ShapeDtypeStruct((B, 2 * HEAD_DIM, C_PAD), BF16)],
        compiler_params=_cparams(("parallel",), VMEM_LIMIT),
        name="compress",
    )(kc_raw, vc_raw, pek, pev, wk, wv, grp_onehot)


def _bias_kernel(tab_ref, dt_ref, bt_ref, dc_ref, bc_ref, os_ref, ow_ref, oc_ref):
    h = pl.program_id(0) * GQA_REP + pl.program_id(1)

    def lookup(bk):
        out = jnp.zeros(bk.shape, F32)
        for b in range(N_BUCKETS):
            out = jnp.where(bk == b, tab_ref[h, b], out)
        return (out - tab_ref[h, N_BUCKETS - 1]) * LOG2E

    dt = dt_ref[...]
    v = lookup(bt_ref[...])
    os_ref[...] = jnp.where(dt >= 0, v, NEG)
    ow_ref[...] = jnp.where((dt >= 0) & (dt < WINDOW), v, NEG)
    oc_ref[...] = jnp.where(dc_ref[...] >= 0, lookup(bc_ref[...]), NEG)


def _t5_bucket(dist):
    max_exact = N_BUCKETS // 2
    d = jnp.maximum(dist, 0)
    df = jnp.maximum(d, 1).astype(F32)
    large = max_exact + (jnp.log(df / max_exact) / math.log(MAX_DISTANCE / max_exact)
                         * (N_BUCKETS - max_exact)).astype(jnp.int32)
    large = jnp.minimum(large, N_BUCKETS - 1)
    return jnp.where(d < max_exact, d, large)


def _bias_tiles(rel_bias):
    tq = jnp.arange(TQ, dtype=jnp.int32)
    dt = tq[None, :] + WINDOW - jnp.arange(TAIL, dtype=jnp.int32)[:, None]
    cc = jnp.arange(BAND, dtype=jnp.int32) - C_OFF
    dc = tq[None, :] - (cc[:, None] * CMP_STRIDE + CMP_LEN - 1)
    full = lambda a: pl.BlockSpec(a.shape, lambda g, r: (0,) * a.ndim)
    out = lambda rows: pl.BlockSpec((None, rows, TQ), lambda g, r: (g, 0, r))
    return pl.pallas_call(
        _bias_kernel,
        grid=(N_GROUPS_KV, GQA_REP),
        in_specs=[pl.BlockSpec(memory_space=pltpu.SMEM), full(dt), full(dt), full(dc), full(dc)],
        out_specs=[out(TAIL), out(TAIL), out(BAND)],
        out_shape=[jax.ShapeDtypeStruct((N_GROUPS_KV, TAIL, GQA_REP * TQ), F32),
                   jax.ShapeDtypeStruct((N_GROUPS_KV, TAIL, GQA_REP * TQ), F32),
                   jax.ShapeDtypeStruct((N_GROUPS_KV, BAND, GQA_REP * TQ), F32)],
        compiler_params=_cparams(("arbitrary", "arbitrary")),
        name="biastile",
    )(rel_bias.T, dt, _t5_bucket(dt), dc, _t5_bucket(dc))


def _attn_kernel(qt_ref, gt_ref, kc_ref, vct_ref, kaug_ref, vst_ref, kw_ref, vwt_ref, ds_ref, dw_ref,
                 bc_ref, o_ref, *scratch, n_far_max):
    n_hg = TILES * N_GROUPS_KV
    sc_ref, sw_ref, ps_ref = (scratch[n_hg * k:n_hg * (k + 1)] for k in range(3))
    qa_ref, m_ref, acc_ref, sb_ref = scratch[3 * n_hg:]
    cols4 = GQA_REP * TQ
    groups = range(N_GROUPS_KV)
    zeros_q = jnp.zeros((HEAD_DIM, cols4), BF16)
    grp = lax.broadcasted_iota(jnp.int32, (128, cols4), 0)
    neg_row = jnp.where(grp == 0, NEG, 0.0).astype(BF16)
    tq_col = lax.broadcasted_iota(jnp.int32, (1, cols4), 1) % TQ
    sidx = lax.broadcasted_iota(jnp.int32, (N_SLC, TQ), 0)
    sidx_f = sidx.astype(F32)
    tq_lane = lax.broadcasted_iota(jnp.int32, (N_SLC, TQ), 1)

    def chunk_start(c):
        return pl.multiple_of(WINDOW + jnp.minimum(c, n_far_max - 1) * FAR, FAR)

    def scores(h, c, par):
        kch = kaug_ref[pl.ds(chunk_start(c), FAR), :]
        for g in groups:
            sb_ref[4 * h + 2 * g + par] = jnp.dot(kch, qa_ref[2 * h + g], preferred_element_type=F32)

    def consume(h, nfar, c, par):
        c0 = chunk_start(c)
        live = c < nfar
        for g in groups:
            hg = 2 * h + g
            sf = sb_ref[4 * h + 2 * g + par]
            mprev = m_ref[hg]
            mcand = jnp.maximum(mprev, jnp.max(sf, axis=0, keepdims=True))
            mnew = jnp.where(live, mcand, mprev)
            pf = jnp.exp2(sf - jnp.where(live, mcand, -NEG)).astype(BF16)
            acc_ref[hg] = jnp.exp2(mprev - mnew) * acc_ref[hg] + jnp.dot(
                vst_ref[g, :, pl.ds(c0, FAR)], pf, preferred_element_type=F32)
            m_ref[hg] = mnew

    def before_loop(h):
        i = pl.program_id(1) * TILES + h
        qs = pl.multiple_of(i * TQ, TQ)
        lanes = slice(h * TQ, (h + 1) * TQ)
        nfar = jnp.maximum(i - 1, 0) // 4
        qpad = []
        for g in groups:
            qgt = jnp.concatenate(
                [qt_ref[(GQA_REP * g + r) * HEAD_DIM:(GQA_REP * g + r + 1) * HEAD_DIM, lanes]
                 for r in range(GQA_REP)], axis=1)
            qpad.append(jnp.concatenate([qgt, zeros_q] if g == 0 else [zeros_q, qgt], axis=0))

        gmask = jnp.where((grp < C_OFF // 8) | (grp >= i + BAND // 8), NEG, 0.0).astype(BF16)
        band0 = pl.multiple_of(i * 8, 8)
        has_cmp = qs + tq_col >= CMP_LEN - 1
        kwc = kw_ref[pl.ds(qs, TAIL), :]
        for g in groups:
            hg = 2 * h + g
            sc_ref[hg][...] = jnp.dot(kc_ref[...], jnp.concatenate([qpad[g], gmask], axis=0),
                                      preferred_element_type=F32)
            sc_ref[hg][pl.ds(band0, BAND), :] = sc_ref[hg][pl.ds(band0, BAND), :] + bc_ref[g]
        for g in groups:
            sw_ref[2 * h + g][...] = jnp.dot(kwc, jnp.concatenate([qpad[g], neg_row], axis=0),
                                             preferred_element_type=F32)
        o_c = []
        for g in groups:
            hg = 2 * h + g
            s = sc_ref[hg][...]
            e = jnp.exp2(s - jnp.max(s, axis=0, keepdims=True))
            pinv = jnp.where(has_cmp, 1.0 / jnp.sum(e, axis=0, keepdims=True), 0.0)
            o_c.append(jnp.dot(vct_ref[...], e.astype(BF16), preferred_element_type=F32) * pinv)
            ps_ref[hg][...] = sum(e[:, r * TQ:(r + 1) * TQ] * pinv[:, r * TQ:(r + 1) * TQ]
                                  for r in range(GQA_REP))

        o_w = []
        for g in groups:
            sw = sw_ref[2 * h + g][...] + dw_ref[g]
            pw = jnp.exp2(sw - jnp.max(sw, axis=0, keepdims=True)).astype(BF16)
            acc_w = jnp.dot(vwt_ref[g, :, pl.ds(qs, TAIL)], pw, preferred_element_type=F32)
            o_w.append(acc_w[0:HEAD_DIM, :] * (1.0 / acc_w[HEAD_DIM:HEAD_DIM + 1, :]))

        cur = (qs + tq_lane) // SEL_BLOCK
        forced = (sidx == 0) | (sidx == cur) | (sidx == cur - 1)
        kt = kaug_ref[pl.ds(qs, TAIL), :]
        for g in groups:
            hg = 2 * h + g
            imp = ps_ref[hg][pl.ds(C_OFF - 1, N_SLC, stride=4), :]
            for d in range(1, 5):
                imp = imp + ps_ref[hg][pl.ds(C_OFF - 1 + d, N_SLC, stride=4), :]
            score = jnp.where(forced, -jnp.inf, jnp.where(sidx <= cur, imp, NEG))
            mbt = jnp.where(forced, 0.0, NEG)
            for _ in range(N_SEL - 3):
                cm = jnp.max(score, axis=0, keepdims=True)
                first = jnp.min(jnp.where(score == cm, sidx_f, float(N_SLC)), axis=0, keepdims=True)
                pick = sidx_f == first
                mbt = jnp.where(pick, 0.0, mbt)
                score = jnp.where(pick, -jnp.inf, score)
            qa_ref[hg] = jnp.concatenate([jnp.concatenate([mbt.astype(BF16)] * GQA_REP, axis=1), qpad[g]],
                                         axis=0)
            mbt_tail = jnp.where(sidx < nfar * (FAR // SEL_BLOCK), NEG, mbt).astype(BF16)
            qa_tail = jnp.concatenate([jnp.concatenate([mbt_tail] * GQA_REP, axis=1), qpad[g]], axis=0)
            sc_ref[hg][...] = jnp.dot(kt, qa_tail, preferred_element_type=F32)
        scores(h, 0, 0)
        for g in groups:
            hg = 2 * h + g
            st = sc_ref[hg][...] + ds_ref[g]
            mt = jnp.max(st, axis=0, keepdims=True)
            m_ref[hg] = mt
            acc_ref[hg] = jnp.dot(vst_ref[g, :, pl.ds(qs, TAIL)], jnp.exp2(st - mt).astype(BF16),
                                  preferred_element_type=F32)
        return nfar, o_c, o_w

    state = [before_loop(h) for h in range(TILES)]

    for h in range(TILES):
        nfar = state[h][0]

        def far_pair(kk, carry, h=h, nfar=nfar):
            c = 2 * kk
            scores(h, c + 1, 1)
            consume(h, nfar, c, 0)
            scores(h, c + 2, 0)
            consume(h, nfar, c + 1, 1)
            return carry

        lax.fori_loop(0, (nfar + 1) // 2, far_pair, 0)

    for h in range(TILES):
        _, o_c, o_w = state[h]
        lanes = slice(h * TQ, (h + 1) * TQ)
        out_rows = []
        for g in groups:
            dsl = slice(g * HEAD_DIM, (g + 1) * HEAD_DIM)
            hg = 2 * h + g
            o_s = acc_ref[hg, 0:HEAD_DIM, :] * (1.0 / acc_ref[hg, HEAD_DIM:HEAD_DIM + 1, :])
            for r in range(GQA_REP):
                cs = slice(r * TQ, (r + 1) * TQ)
                gc = 3 * (GQA_REP * g + r)
                out_rows.append(gt_ref[gc:gc + 1, lanes] * o_c[g][dsl, cs]
                                + gt_ref[gc + 1:gc + 2, lanes] * o_s[:, cs]
                                + gt_ref[gc + 2:gc + 3, lanes] * o_w[g][:, cs])
        o_ref[h * TQ:(h + 1) * TQ, :] = jnp.concatenate(out_rows, axis=0).T


def _attention(t_all, gates_t, kc2, vct, kaug, vst, kw_pad, vwt, ds, dw, bc, B, T):
    tp = T + WINDOW
    tqs = TQ * TILES
    per_b = lambda shape: pl.BlockSpec((None,) + shape, lambda b, i: (b,) + (0,) * len(shape),
                                      pipeline_mode=pl.Buffered(1))
    const = lambda a: pl.BlockSpec(a.shape, lambda b, i: (0,) * a.ndim, pipeline_mode=pl.Buffered(1))
    cols4 = GQA_REP * TQ
    n_hg = TILES * N_GROUPS_KV
    return pl.pallas_call(
        functools.partial(_attn_kernel, n_far_max=T // FAR),
        grid=(B, T // tqs),
        in_specs=[pl.BlockSpec((None, 512, tqs), lambda b, i: (b, 0, i)),
                  pl.BlockSpec((None, GATE_ROWS, tqs), lambda b, i: (b, 0, i)),
                  per_b((C_PAD, 256)),
                  per_b((128, C_PAD)),
                  per_b((tp, 256)),
                  per_b((N_GROUPS_KV, V_ROWS, tp)),
                  per_b((tp, 256)),
                  per_b((N_GROUPS_KV, V_ROWS, tp)),
                  const(ds), const(dw), const(bc)],
        out_specs=pl.BlockSpec((None, tqs, 512), lambda b, i: (b, i, 0)),
        out_shape=jax.ShapeDtypeStruct((B, T, 512), F32),
        scratch_shapes=([pltpu.VMEM((C_PAD, cols4), F32)] * n_hg
                        + [pltpu.VMEM((TAIL, cols4), F32)] * n_hg
                        + [pltpu.VMEM((C_PAD, TQ), F32)] * n_hg
                        + [pltpu.VMEM((n_hg, 256, cols4), BF16),
                           pltpu.VMEM((n_hg, 1, cols4), F32),
                           pltpu.VMEM((n_hg, V_ROWS, cols4), F32),
                           pltpu.VMEM((2 * n_hg, FAR, cols4), F32)]),
        compiler_params=_cparams(("parallel", "arbitrary"), VMEM_LIMIT),
        name="attn",
    )(t_all, gates_t, kc2, vct, kaug, vst, kw_pad, vwt, ds, dw, bc)


def _rglru_kernel(*refs, L):
    J = L // 8
    n_col = D_RNN // 128
    rx_refs, ry_refs = refs[:n_col], refs[n_col:2 * n_col]
    cw_ref, cb_ref, wa_ref, ba_ref, wx_ref, bx_ref, sp_ref, o_ref, xprev_ref, h_ref, os_ref = refs[2 * n_col:]

    @pl.when(pl.program_id(1) == 0)
    def _():
        xprev_ref[...] = jnp.zeros(xprev_ref.shape, F32)
        h_ref[...] = jnp.zeros(h_ref.shape, F32)

    def strided(col_refs):
        return jnp.concatenate(
            [jnp.concatenate([ref[pl.ds(j, 8, stride=J), :] for j in range(J)], axis=0)
             for ref in col_refs], axis=1)

    xp = strided(rx_refs)
    yp = strided(ry_refs)
    sub = lax.broadcasted_iota(jnp.int32, (8, D_RNN), 0)

    def delayed(d):
        heads = []
        for j in range(d):
            src = pltpu.roll(xp[(J + j - d) * 8:(J + j - d + 1) * 8], 1, axis=0)
            heads.append(jnp.where(sub == 0, xprev_ref[8 + j - d:9 + j - d, :], src))
        return jnp.concatenate(heads + [xp[:L - 8 * d]], axis=0)

    xc = (cb_ref[...] + cw_ref[0:1, :] * delayed(3) + cw_ref[1:2, :] * delayed(2)
          + cw_ref[2:3, :] * delayed(1) + cw_ref[3:4, :] * xp)
    xprev_ref[...] = jnp.concatenate([ref[L - 8:L, :] for ref in rx_refs], axis=1)
    xcb = xc.astype(BF16)
    r = jax.nn.sigmoid(jnp.dot(xcb, wa_ref[...], preferred_element_type=F32) + ba_ref[...])
    ig = jax.nn.sigmoid(jnp.dot(xcb, wx_ref[...], preferred_element_type=F32) + bx_ref[...])
    a = jnp.exp(-RG_C * r * sp_ref[...])
    bt = jnp.sqrt(1.0 - a * a) * (ig * xc)

    h = jnp.zeros((8, D_RNN), F32)
    dec = jnp.ones((8, D_RNN), F32)
    hs, decs = [], []
    for j in range(J):
        aj = a[j * 8:(j + 1) * 8]
        h = aj * h + bt[j * 8:(j + 1) * 8]
        dec = aj * dec
        hs.append(h)
        decs.append(dec)
    carry = h_ref[0:1, :]
    carries = []
    for s in range(8):
        carries.append(carry)
        carry = dec[s:s + 1, :] * carry + h[s:s + 1, :]
    h_ref[...] = jnp.broadcast_to(carry, h_ref.shape)
    h_in = jnp.concatenate(carries, axis=0)
    cdf_c = math.sqrt(2.0 / math.pi)
    for j in range(J):
        y = yp[j * 8:(j + 1) * 8]
        cdf = 0.5 * (1.0 + jnp.tanh(cdf_c * (y + 0.044715 * (y * y * y))))
        res = (hs[j] + decs[j] * h_in) * (y * cdf)
        for c in range(n_col):
            os_ref[c, pl.ds(j, 8, stride=J), :] = res[:, c * 128:(c + 1) * 128]
    for c in range(n_col):
        o_ref[:, c * 128:(c + 1) * 128] = os_ref[c]


def _rglru(rx, ry, conv_w, conv_b, wa_bd, b_a, wx_bd, b_x, sp, B, T):
    L = 512
    blk = pl.BlockSpec((None, L, D_RNN), lambda b, t: (b, t, 0))
    n_col = D_RNN // 128
    cols = [pl.BlockSpec((None, L, 128), functools.partial(lambda b, t, c: (b, t, c), c=c))
            for c in range(n_col)]
    full = lambda a: pl.BlockSpec(a.shape, lambda b, t: (0,) * a.ndim)
    return pl.pallas_call(
        functools.partial(_rglru_kernel, L=L),
        grid=(B, T // L),
        in_specs=cols + cols + [full(conv_w), full(conv_b), full(wa_bd), full(b_a), full(wx_bd),
                                full(b_x), full(sp)],
        out_specs=blk,
        out_shape=jax.ShapeDtypeStruct((B, T, D_RNN), F32),
        scratch_shapes=[pltpu.VMEM((8, D_RNN), F32), pltpu.VMEM((8, D_RNN), F32),
                        pltpu.VMEM((n_col, L, 128), F32)],
        compiler_params=_cparams(("parallel", "arbitrary"), VMEM_LIMIT),
        name="rglru",
    )(*([rx] * n_col + [ry] * n_col), conv_w, conv_b, wa_bd, b_a, wx_bd, b_x, sp)


def _layer_norm(y, g, b):
    mu = jnp.mean(y, axis=-1, keepdims=True)
    d = y - mu
    var = jnp.mean(d * d, axis=-1, keepdims=True)
    return d * lax.rsqrt(var + LN_EPS) * g + b


def _outproj_kernel(at_ref, rn_ref, x_ref, ga_ref, gr_ref, wo_ref, g1_ref, b1_ref, wrh_ref, wrl_ref,
                    br_ref, x1_ref, comb_ref):
    a = at_ref[...]
    rn = rn_ref[...]
    ha = a * lax.rsqrt(jnp.mean(a * a, axis=-1, keepdims=True) + RMS_EPS) * ga_ref[...]
    hr = rn * lax.rsqrt(jnp.mean(rn * rn, axis=-1, keepdims=True) + RMS_EPS) * gr_ref[...]
    heads = jnp.concatenate([ha, hr], axis=1).astype(BF16)
    mix = jnp.dot(heads, wo_ref[...], preferred_element_type=F32)
    x1 = _layer_norm(ALPHA * x_ref[...] + mix, g1_ref[...], b1_ref[...])
    x1_ref[...] = x1

    xh = x1.astype(BF16)
    xl = (x1 - xh.astype(F32)).astype(BF16)
    lg = (jnp.dot(xh, wrh_ref[...], preferred_element_type=F32)
          + jnp.dot(xl, wrh_ref[...], preferred_element_type=F32)
          + jnp.dot(xh, wrl_ref[...], preferred_element_type=F32)) + br_ref[...]
    lane = lax.broadcasted_iota(jnp.int32, lg.shape, 1)
    lane_f = lane.astype(F32)
    big = 1e9
    isg = lane < N_EXP_GROUPS
    gmax = jnp.max(jnp.where(isg, lg, -jnp.inf), axis=-1, keepdims=True)
    pg_top = 1.0 / jnp.sum(jnp.where(isg, jnp.exp(lg - gmax), 0.0), axis=-1, keepdims=True)
    gi = jnp.min(jnp.where(isg & (lg == gmax), lane_f, big), axis=-1, keepdims=True)
    egrp = ((lane - N_EXP_GROUPS) // EXPERTS_PER_GROUP).astype(F32)
    ise = (lane >= N_EXP_GROUPS) & (lane < N_EXP_GROUPS + N_EXPERTS) & (egrp == gi)
    emax = jnp.max(jnp.where(ise, lg, -jnp.inf), axis=-1, keepdims=True)
    i1 = jnp.min(jnp.where(ise & (lg == emax), lane_f, big), axis=-1, keepdims=True)
    rest = ise & (lane_f != i1)
    m2 = jnp.max(jnp.where(rest, lg, -jnp.inf), axis=-1, keepdims=True)
    i2 = jnp.min(jnp.where(rest & (lg == m2), lane_f, big), axis=-1, keepdims=True)
    e2 = jnp.exp(m2 - emax)
    inv = pg_top / (1.0 + e2)
    comb_ref[...] = jnp.where(lane_f == i1, inv, 0.0) + jnp.where(lane_f == i2, inv * e2, 0.0)


def _outproj(attn, rnn, x2, ga, gr, wo, g1, b1, wrh, wrl, br, N):
    tm = 512
    row = lambda w: pl.BlockSpec((tm, w), lambda i: (i, 0))
    full = lambda a: pl.BlockSpec(a.shape, lambda i: (0,) * a.ndim)
    return pl.pallas_call(
        _outproj_kernel,
        grid=(N // tm,),
        in_specs=[row(512), row(512), row(D_MODEL), full(ga), full(gr), full(wo), full(g1), full(b1),
                  full(wrh), full(wrl), full(br)],
        out_specs=[row(D_MODEL), row(128)],
        out_shape=[jax.ShapeDtypeStruct((N, D_MODEL), F32), jax.ShapeDtypeStruct((N, 128), F32)],
        compiler_params=_cparams(("parallel",), VMEM_LIMIT),
        name="outproj",
    )(attn, rnn, x2, ga, gr, wo, g1, b1, wrh, wrl, br)


def _moe_kernel(x1_ref, comb_ref, p_ref, tri_ref, wup_ref, wdn_ref, wg_ref, wp_ref, g2_ref, b2_ref, o_ref,
                acc_ref, xb_ref, rank_ref, rankt_ref, combt_ref):
    pi = pl.program_id(1)
    n_sub = x1_ref.shape[0] // MOE_SUB
    sub_rows = [slice(h * MOE_SUB, (h + 1) * MOE_SUB) for h in range(n_sub)]
    lane_col = N_EXP_GROUPS

    @pl.when(pi == 0)
    def _():
        x1 = x1_ref[...]
        xb = x1.astype(BF16)
        xb_ref[...] = xb
        gate = jax.nn.sigmoid(jnp.dot(xb, wg_ref[...], preferred_element_type=F32))
        ple = gate * jnp.dot(p_ref[...].astype(BF16), wp_ref[...], preferred_element_type=F32)
        acc_ref[...] = ALPHA * x1 + ple
        for rows in sub_rows:
            comb = comb_ref[rows, :]
            chosen = comb > 0.0
            rank = jnp.dot(tri_ref[...], jnp.where(chosen, 1.0, 0.0).astype(BF16),
                           preferred_element_type=F32)
            rank = jnp.where(chosen, rank, -1.0)
            rank_ref[rows, :] = rank
            rankt_ref[:, rows] = rank.T
            combt_ref[:, rows] = comb.T

    lane = lax.broadcasted_iota(jnp.int32, (MOE_SUB, 128), 1)
    slot_r = lax.broadcasted_iota(jnp.int32, (MOE_SLOTS, MOE_SUB), 0).astype(F32)
    slot_c = lax.broadcasted_iota(jnp.int32, (MOE_SUB, MOE_SLOTS), 1).astype(F32)

    def route_pair(base):
        chains = [(rows, k) for rows in sub_rows for k in range(2)]
        cmask, xc, u, hsw, y, ys, ct = {}, {}, {}, {}, {}, {}, {}
        for ch in chains:
            rows, k = ch
            rk_row = rankt_ref[pl.ds(2 * pi + k + lane_col, 1), rows] - base
            cmask[ch] = rk_row == slot_r
            xc[ch] = jnp.dot(jnp.where(cmask[ch], 1.0, 0.0).astype(BF16), xb_ref[rows, :],
                             preferred_element_type=F32).astype(BF16)
        for ch in chains:
            u[ch] = jnp.dot(xc[ch], wup_ref[ch[1]], preferred_element_type=F32)
        for ch in chains:
            ua = u[ch][:, :D_FF]
            hsw[ch] = (ua * jax.nn.sigmoid(ua) * u[ch][:, D_FF:]).astype(BF16)
        for ch in chains:
            y[ch] = jnp.dot(hsw[ch], wdn_ref[ch[1]], preferred_element_type=F32)
        for ch in chains:
            rows, k = ch
            e_lane = 2 * pi + k + lane_col
            w_row = combt_ref[pl.ds(e_lane, 1), rows]
            w_slot = jnp.sum(jnp.where(cmask[ch], w_row, 0.0), axis=1, keepdims=True)
            ys[ch] = (w_slot * y[ch]).astype(BF16)
            rk_col = jnp.sum(jnp.where(lane == e_lane, rank_ref[rows, :], 0.0), axis=1,
                             keepdims=True) - base
            ct[ch] = jnp.where(rk_col == slot_c, 1.0, 0.0).astype(BF16)
        for rows in sub_rows:
            pair = [(rows, k) for k in range(2)]
            ctp = jnp.concatenate([ct[ch] for ch in pair], axis=1)
            acc_ref[rows, :] = acc_ref[rows, :] + jnp.dot(
                ctp, jnp.concatenate([ys[ch] for ch in pair], axis=0), preferred_element_type=F32)

    route_pair(0.0)

    top_rank = jnp.max(jnp.maximum(rankt_ref[pl.ds(2 * pi + lane_col, 1), :],
                                   rankt_ref[pl.ds(2 * pi + 1 + lane_col, 1), :]))

    @pl.when(top_rank >= MOE_SLOTS)
    def _():
        def extra(sb, carry):
            route_pair((sb * MOE_SLOTS).astype(F32))
            return carry

        lax.fori_loop(1, top_rank.astype(jnp.int32) // MOE_SLOTS + 1, extra, 0)

    @pl.when(pi == N_EXPERTS // 2 - 1)
    def _():
        o_ref[...] = _layer_norm(acc_ref[...], g2_ref[...], b2_ref[...])


def _moe(x1, comb, p2, wup, wdn, wg, wp, g2, b2, N):
    tm = MOE_TILE
    tri = (jnp.arange(MOE_SUB)[None, :] < jnp.arange(MOE_SUB)[:, None]).astype(BF16)
    row = lambda w: pl.BlockSpec((tm, w), lambda i, e: (i, 0))
    full = lambda a: pl.BlockSpec(a.shape, lambda i, e: (0,) * a.ndim)
    return pl.pallas_call(
        _moe_kernel,
        grid=(N // tm, N_EXPERTS // 2),
        in_specs=[row(D_MODEL), row(128), row(D_PLE), full(tri),
                  pl.BlockSpec((2, D_MODEL, 2 * D_FF), lambda i, e: (e, 0, 0)),
                  pl.BlockSpec((2, D_FF, D_MODEL), lambda i, e: (e, 0, 0)),
                  full(wg), full(wp), full(g2), full(b2)],
        out_specs=row(D_MODEL),
        out_shape=jax.ShapeDtypeStruct((N, D_MODEL), F32),
        scratch_shapes=[pltpu.VMEM((tm, D_MODEL), F32),
                        pltpu.VMEM((tm, D_MODEL), BF16),
                        pltpu.VMEM((tm, 128), F32),
                        pltpu.VMEM((128, tm), F32),
                        pltpu.VMEM((128, tm), F32)],
        compiler_params=_cparams(("parallel", "arbitrary"), VMEM_LIMIT),
        name="moe",
    )(x1, comb, p2, tri, wup, wdn, wg, wp, g2, b2)


def _block_diag(w):
    n, k, j = w.shape
    eye = jnp.eye(n, dtype=w.dtype)
    return (w[:, :, None, :] * eye[:, None, :, None]).reshape(n * k, n * j)


def kernel(x, p, rel_bias, w_in, cmp_pe_k, cmp_pe_v, cmp_w_k, cmp_w_v, conv_w, conv_b, rg_w_a, rg_b_a,
           rg_w_x, rg_b_x, rg_lambda, attn_out_gain, rnn_out_gain, w_out, ln1_g, ln1_b, router_group_w,
           router_group_b, router_expert_w, router_expert_b, expert_w_up, expert_w_down, ple_w,
           ple_gate_w, ln2_g, ln2_b):
    B, T, _ = x.shape
    N = B * T
    assert T % FAR == 0 and T % (TQ * TILES) == 0 and T // SEL_BLOCK <= N_SLC and w_in.shape[0] == 1
    n_cmp = (T - CMP_LEN) // CMP_STRIDE + 1
    n_chunk = T // CMP_STRIDE
    assert C_OFF + n_chunk <= C_PAD
    row1 = lambda v: v.reshape(1, -1)

    w = w_in[0]
    c0 = D_ATTN
    kv = lambda k: w[:, c0 + k * D_KV:c0 + (k + 1) * D_KV]
    g0 = c0 + 6 * D_KV
    wq = w[:, :D_ATTN] * (HEAD_DIM ** -0.5 * LOG2E)
    w_main = jnp.concatenate([kv(2), kv(4), kv(0), kv(1), w[:, g0 + 24:]], axis=1).astype(BF16)
    w_t = jnp.concatenate([wq, kv(3), kv(5)], axis=1).T.astype(BF16)
    w_g = jnp.pad(w[:, g0:g0 + 24], ((0, 0), (0, GATE_ROWS - 24))).T.astype(BF16)

    x2 = x.reshape(N, D_MODEL)
    ks, kw, kc_raw, vc_raw, rx, ry, t_all, gates_t = _inproj(x2, w_main, w_t, w_g, B, T)

    def per_pos(wc):
        wl = wc.reshape(CMP_LEN, HEAD_DIM, HEAD_DIM)
        z = jnp.zeros_like(wl)
        return jnp.concatenate([jnp.concatenate([wl, z], axis=2), jnp.concatenate([z, wl], axis=2)],
                               axis=1).astype(BF16)

    pek = jnp.tile(cmp_pe_k[0], (1, N_GROUPS_KV))
    pev = jnp.tile(cmp_pe_v[0], (1, N_GROUPS_KV))
    grp_onehot = (jnp.arange(C_PAD)[:, None] // 8 == jnp.arange(128)[None, :]).astype(BF16)
    kc2, vct = _compress(kc_raw.reshape(B, T, 128), vc_raw.reshape(B, T, 128), pek, pev,
                         per_pos(cmp_w_k[0]), per_pos(cmp_w_v[0]), grp_onehot, B, T, n_cmp)

    onehot = (jnp.arange(T)[:, None] // SEL_BLOCK == jnp.arange(N_SLC)[None, :]).astype(BF16)
    kaug = jnp.concatenate([jnp.broadcast_to(onehot, (B, T, N_SLC)), ks.reshape(B, T, 128)], axis=2)
    lane256 = jnp.arange(256)
    pad_sel = jnp.broadcast_to((lane256 == N_SLC - 1).astype(BF16), (B, WINDOW, 256))
    pad_win = jnp.broadcast_to((lane256 == 128).astype(BF16), (B, WINDOW, 256))
    kaug = jnp.concatenate([pad_sel, kaug], axis=1)
    kw_pad = jnp.concatenate([pad_win, jnp.pad(kw.reshape(B, T, 128), ((0, 0), (0, 0), (0, 128)))], axis=1)
    ones_rows = jnp.concatenate([jnp.ones((B, N_GROUPS_KV, 1, T), BF16),
                                 jnp.zeros((B, N_GROUPS_KV, V_ROWS - HEAD_DIM - 1, T), BF16)], axis=2)

    def v_rows(vt):
        vt = jnp.concatenate([vt.reshape(B, N_GROUPS_KV, HEAD_DIM, T), ones_rows], axis=2)
        return jnp.pad(vt, ((0, 0), (0, 0), (0, 0), (WINDOW, 0)))

    vst = v_rows(t_all[:, 512:640])
    vwt = v_rows(t_all[:, 640:768])
    ds, dw, bc = _bias_tiles(rel_bias)

    attn = _attention(t_all, gates_t, kc2, vct, kaug, vst, kw_pad, vwt, ds, dw, bc, B, T)

    sp = jax.nn.softplus(-rg_lambda[0].astype(F32))
    rnn = _rglru(rx.reshape(B, T, D_RNN), ry.reshape(B, T, D_RNN), conv_w[0], row1(conv_b[0]),
                 _block_diag(rg_w_a[0]).astype(BF16), row1(rg_b_a[0]),
                 _block_diag(rg_w_x[0]).astype(BF16), row1(rg_b_x[0]), row1(sp), B, T)

    wr = jnp.pad(jnp.concatenate([router_group_w[0], router_expert_w[0]], axis=1), ((0, 0), (0, 108)))
    wrh = wr.astype(BF16)
    wrl = (wr - wrh.astype(F32)).astype(BF16)
    br = jnp.pad(jnp.concatenate([router_group_b[0], router_expert_b[0]]), (0, 108)).reshape(1, 128)
    x1, comb = _outproj(attn.reshape(N, 512), rnn.reshape(N, D_RNN), x2, row1(attn_out_gain[0]),
                        row1(rnn_out_gain[0]), w_out[0].astype(BF16), row1(ln1_g[0]), row1(ln1_b[0]),
                        wrh, wrl, br, N)

    out = _moe(x1, comb, p[0].reshape(N, D_PLE), expert_w_up[0].astype(BF16),
               expert_w_down[0].astype(BF16), ple_gate_w[0].astype(BF16), ple_w[0].astype(BF16),
               row1(ln2_g[0]), row1(ln2_b[0]), N)
    return out.reshape(B, T, D_MODEL)
```

```python
import functools
import math

import jax
import jax.numpy as jnp
from jax import lax
from jax.experimental import pallas as pl
from jax.experimental.pallas import tpu as pltpu

F32 = jnp.float32
BF16 = jnp.bfloat16
NEG = -1e30
LOG2E = 1.4426950408889634

D_MODEL = 1024
HEAD_DIM = 64
N_HEADS = 8
N_GROUPS_KV = 2
GQA_REP = 4
D_ATTN = 512
D_RNN = 512
D_KV = 128
CMP_LEN = 32
CMP_STRIDE = 16
SEL_BLOCK = 64
N_SEL = 16
WINDOW = 512
N_BUCKETS = 32
MAX_DISTANCE = 128
N_EXPERTS = 16
EXPERTS_PER_GROUP = 4
N_EXP_GROUPS = 4
D_FF = 512
D_PLE = 256
ALPHA = 2.0 ** 0.25
LN_EPS = 1e-5
RMS_EPS = 1e-6
FORCE_BONUS = 1e4
RG_C = 8.0

TQ = 128
TILES = 2
N_SLC = 128
C_PAD = 640
C_OFF = 16
FAR = 512
TAIL = WINDOW + TQ
BAND = 24
GATE_ROWS = 32
V_ROWS = 80
MOE_TILE = 1024
MOE_SUB = 512
MOE_SLOTS = 128
VMEM_LIMIT = 56 * 1024 * 1024


def _cparams(sem, vmem=None):
    return pltpu.CompilerParams(dimension_semantics=sem, vmem_limit_bytes=vmem)


def _inproj_kernel(x_ref, wm_ref, wt_ref, wg_ref, ks_ref, kw_ref, kc_ref, vc_ref, rx_ref, ry_ref,
                   t_ref, gt_ref):
    xb = x_ref[...].astype(BF16)

    def mm(lo, hi):
        return jnp.dot(xb, wm_ref[:, lo:hi], preferred_element_type=F32)

    ks_ref[...] = mm(0, 128).astype(BF16)
    kw_ref[...] = mm(128, 256).astype(BF16)
    kc_ref[...] = mm(256, 384)
    vc_ref[...] = mm(384, 512)
    rx_ref[...] = mm(512, 1024)
    ry_ref[...] = mm(1024, 1536)
    dn = (((1,), (1,)), ((), ()))
    t_ref[...] = lax.dot_general(wt_ref[...], xb, dn, preferred_element_type=F32).astype(BF16)
    gt_ref[...] = jax.nn.sigmoid(lax.dot_general(wg_ref[...], xb, dn, preferred_element_type=F32))


def _inproj(x2, w_main, w_t, w_g, B, T):
    N = B * T
    tm = 512
    nt = T // tm
    row = lambda w: pl.BlockSpec((tm, w), lambda i: (i, 0))
    full = lambda a: pl.BlockSpec(a.shape, lambda i: (0,) * a.ndim)
    tr = lambda r: pl.BlockSpec((None, r, tm), lambda i: (i // nt, 0, i % nt))
    return pl.pallas_call(
        _inproj_kernel,
        grid=(N // tm,),
        in_specs=[row(D_MODEL), full(w_main), full(w_t), full(w_g)],
        out_specs=[row(128), row(128), row(128), row(128), row(512), row(512), tr(768), tr(GATE_ROWS)],
        out_shape=[jax.ShapeDtypeStruct((N, 128), BF16),
                   jax.ShapeDtypeStruct((N, 128), BF16),
                   jax.ShapeDtypeStruct((N, 128), F32),
                   jax.ShapeDtypeStruct((N, 128), F32),
                   jax.ShapeDtypeStruct((N, 512), F32),
                   jax.ShapeDtypeStruct((N, 512), F32),
                   jax.ShapeDtypeStruct((B, 768, T), BF16),
                   jax.ShapeDtypeStruct((B, GATE_ROWS, T), F32)],
        compiler_params=_cparams(("parallel",), VMEM_LIMIT),
        name="inproj",
    )(x2, w_main, w_t, w_g)


def _compress_kernel(kr_ref, vr_ref, pek_ref, pev_ref, wk_ref, wv_ref, oh_ref, kc_ref, vct_ref, *,
                     n_cmp, n_chunk):
    def compress(raw_ref, pe_ref, w_ref):
        lo = jnp.zeros((n_chunk, 2 * HEAD_DIM), F32)
        hi = jnp.zeros((n_chunk, 2 * HEAD_DIM), F32)
        for j in range(CMP_STRIDE):
            a = raw_ref[pl.ds(j, n_chunk, stride=CMP_STRIDE), :]
            lo = lo + jnp.dot((a + pe_ref[j:j + 1, :]).astype(BF16), w_ref[j],
                              preferred_element_type=F32)
            hi = hi + jnp.dot((a + pe_ref[CMP_STRIDE + j:CMP_STRIDE + j + 1, :]).astype(BF16),
                              w_ref[CMP_STRIDE + j], preferred_element_type=F32)
        hi = pltpu.roll(hi, n_chunk - 1, axis=0)
        rid = lax.broadcasted_iota(jnp.int32, (n_chunk, 2 * HEAD_DIM), 0)
        out = jnp.where(rid < n_cmp, lo + hi, 0.0)
        return jnp.concatenate([jnp.zeros((C_OFF, 2 * HEAD_DIM), F32), out,
                                jnp.zeros((C_PAD - C_OFF - n_chunk, 2 * HEAD_DIM), F32)], axis=0)

    kc_ref[:, 0:2 * HEAD_DIM] = compress(kr_ref, pek_ref, wk_ref).astype(BF16)
    kc_ref[:, 2 * HEAD_DIM:] = oh_ref[...]
    vct_ref[...] = compress(vr_ref, pev_ref, wv_ref).T.astype(BF16)


def _compress(kc_raw, vc_raw, pek, pev, wk, wv, grp_onehot, B, T, n_cmp):
    n_chunk = T // CMP_STRIDE
    blk = pl.BlockSpec((None, T, 2 * HEAD_DIM), lambda b: (b, 0, 0))
    full = lambda a: pl.BlockSpec(a.shape, lambda b: (0,) * a.ndim)
    return pl.pallas_call(
        functools.partial(_compress_kernel, n_cmp=n_cmp, n_chunk=n_chunk),
        grid=(B,),
        in_specs=[blk, blk, full(pek), full(pev), full(wk), full(wv), full(grp_onehot)],
        out_specs=[pl.BlockSpec((None, C_PAD, 256), lambda b: (b, 0, 0)),
                   pl.BlockSpec((None, 2 * HEAD_DIM, C_PAD), lambda b: (b, 0, 0))],
        out_shape=[jax.ShapeDtypeStruct((B, C_PAD, 256), BF16),
                   jax.ShapeDtypeStruct((B, 2 * HEAD_DIM, C_PAD), BF16)],
        compiler_params=_cparams(("parallel",), VMEM_LIMIT),
        name="compress",
    )(kc_raw, vc_raw, pek, pev, wk, wv, grp_onehot)


def _bias_kernel(tab_ref, dt_ref, bt_ref, dc_ref, bc_ref, os_ref, ow_ref, oc_ref):
    h = pl.program_id(0) * GQA_REP + pl.program_id(1)

    def lookup(bk):
        out = jnp.zeros(bk.shape, F32)
        for b in range(N_BUCKETS):
            out = jnp.where(bk == b, tab_ref[h, b], out)
        return (out - tab_ref[h, N_BUCKETS - 1]) * LOG2E

    dt = dt_ref[...]
    v = lookup(bt_ref[...])
    os_ref[...] = jnp.where(dt >= 0, v, NEG)
    ow_ref[...] = jnp.where((dt >= 0) & (dt < WINDOW), v, NEG)
    oc_ref[...] = jnp.where(dc_ref[...] >= 0, lookup(bc_ref[...]), NEG)


def _t5_bucket(dist):
    max_exact = N_BUCKETS // 2
    d = jnp.maximum(dist, 0)
    df = jnp.maximum(d, 1).astype(F32)
    large = max_exact + (jnp.log(df / max_exact) / math.log(MAX_DISTANCE / max_exact)
                         * (N_BUCKETS - max_exact)).astype(jnp.int32)
    large = jnp.minimum(large, N_BUCKETS - 1)
    return jnp.where(d < max_exact, d, large)


def _bias_tiles(rel_bias):
    tq = jnp.arange(TQ, dtype=jnp.int32)
    dt = tq[None, :] + WINDOW - jnp.arange(TAIL, dtype=jnp.int32)[:, None]
    cc = jnp.arange(BAND, dtype=jnp.int32) - C_OFF
    dc = tq[None, :] - (cc[:, None] * CMP_STRIDE + CMP_LEN - 1)
    full = lambda a: pl.BlockSpec(a.shape, lambda g, r: (0,) * a.ndim)
    out = lambda rows: pl.BlockSpec((None, rows, TQ), lambda g, r: (g, 0, r))
    return pl.pallas_call(
        _bias_kernel,
        grid=(N_GROUPS_KV, GQA_REP),
        in_specs=[pl.BlockSpec(memory_space=pltpu.SMEM), full(dt), full(dt), full(dc), full(dc)],
        out_specs=[out(TAIL), out(TAIL), out(BAND)],
        out_shape=[jax.ShapeDtypeStruct((N_GROUPS_KV, TAIL, GQA_REP * TQ), F32),
                   jax.ShapeDtypeStruct((N_GROUPS_KV, TAIL, GQA_REP * TQ), F32),
                   jax.ShapeDtypeStruct((N_GROUPS_KV, BAND, GQA_REP * TQ), F32)],
        compiler_params=_cparams(("arbitrary", "arbitrary")),
        name="biastile",
    )(rel_bias.T, dt, _t5_bucket(dt), dc, _t5_bucket(dc))


def _attn_kernel(qt_ref, gt_ref, kc_ref, vct_ref, kaug_ref, vst_ref, kw_ref, vwt_ref, ds_ref, dw_ref,
                 bc_ref, o_ref, *scratch, n_far_max):
    n_hg = TILES * N_GROUPS_KV
    sc_ref, sw_ref, ps_ref = (scratch[n_hg * k:n_hg * (k + 1)] for k in range(3))
    qa_ref, m_ref, acc_ref, sb_ref = scratch[3 * n_hg:]
    cols4 = GQA_REP * TQ
    groups = range(N_GROUPS_KV)
    zeros_q = jnp.zeros((HEAD_DIM, cols4), BF16)
    grp = lax.broadcasted_iota(jnp.int32, (128, cols4), 0)
    neg_row = jnp.where(grp == 0, NEG, 0.0).astype(BF16)
    tq_col = lax.broadcasted_iota(jnp.int32, (1, cols4), 1) % TQ
    sidx = lax.broadcasted_iota(jnp.int32, (N_SLC, TQ), 0)
    sidx_f = sidx.astype(F32)
    tq_lane = lax.broadcasted_iota(jnp.int32, (N_SLC, TQ), 1)

    def chunk_start(c):
        return pl.multiple_of(WINDOW + jnp.minimum(c, n_far_max - 1) * FAR, FAR)

    def scores(h, c, par):
        kch = kaug_ref[pl.ds(chunk_start(c), FAR), :]
        for g in groups:
            sb_ref[4 * h + 2 * g + par] = jnp.dot(kch, qa_ref[2 * h + g], preferred_element_type=F32)

    def consume(h, nfar, c, par):
        c0 = chunk_start(c)
        live = c < nfar
        for g in groups:
            hg = 2 * h + g
            sf = sb_ref[4 * h + 2 * g + par]
            mprev = m_ref[hg]
            mcand = jnp.maximum(mprev, jnp.max(sf, axis=0, keepdims=True))
            mnew = jnp.where(live, mcand, mprev)
            pf = jnp.exp2(sf - jnp.where(live, mcand, -NEG)).astype(BF16)
            acc_ref[hg] = jnp.exp2(mprev - mnew) * acc_ref[hg] + jnp.dot(
                vst_ref[g, :, pl.ds(c0, FAR)], pf, preferred_element_type=F32)
            m_ref[hg] = mnew

    tiles = range(TILES)
    chains = [(h, g) for h in tiles for g in groups]
    t_i = [pl.program_id(1) * TILES + h for h in tiles]
    t_qs = [pl.multiple_of(i * TQ, TQ) for i in t_i]
    t_nfar = [jnp.maximum(i - 1, 0) // 4 for i in t_i]
    qpad = {}
    for h, g in chains:
        qgt = jnp.concatenate(
            [qt_ref[(GQA_REP * g + r) * HEAD_DIM:(GQA_REP * g + r + 1) * HEAD_DIM, h * TQ:(h + 1) * TQ]
             for r in range(GQA_REP)], axis=1)
        qpad[h, g] = jnp.concatenate([qgt, zeros_q] if g == 0 else [zeros_q, qgt], axis=0)

    for h, g in chains:
        i = t_i[h]
        gmask = jnp.where((grp < C_OFF // 8) | (grp >= i + BAND // 8), NEG, 0.0).astype(BF16)
        band0 = pl.multiple_of(i * 8, 8)
        sc_ref[2 * h + g][...] = jnp.dot(kc_ref[...], jnp.concatenate([qpad[h, g], gmask], axis=0),
                                         preferred_element_type=F32)
        sc_ref[2 * h + g][pl.ds(band0, BAND), :] = sc_ref[2 * h + g][pl.ds(band0, BAND), :] + bc_ref[g]
    for h, g in chains:
        sw_ref[2 * h + g][...] = jnp.dot(kw_ref[pl.ds(t_qs[h], TAIL), :],
                                         jnp.concatenate([qpad[h, g], neg_row], axis=0),
                                         preferred_element_type=F32) + dw_ref[g]
    o_c, o_w = {}, {}
    for h, g in chains:
        s = sc_ref[2 * h + g][...]
        e = jnp.exp2(s - jnp.max(s, axis=0, keepdims=True))
        has_cmp = t_qs[h] + tq_col >= CMP_LEN - 1
        pinv = jnp.where(has_cmp, 1.0 / jnp.sum(e, axis=0, keepdims=True), 0.0)
        o_c[h, g] = jnp.dot(vct_ref[...], e.astype(BF16), preferred_element_type=F32) * pinv
        ps_ref[2 * h + g][...] = sum(e[:, r * TQ:(r + 1) * TQ] * pinv[:, r * TQ:(r + 1) * TQ]
                                     for r in range(GQA_REP))
    for h, g in chains:
        sw = sw_ref[2 * h + g][...]
        pw = jnp.exp2(sw - jnp.max(sw, axis=0, keepdims=True)).astype(BF16)
        acc_w = jnp.dot(vwt_ref[g, :, pl.ds(t_qs[h], TAIL)], pw, preferred_element_type=F32)
        o_w[h, g] = acc_w[0:HEAD_DIM, :] * (1.0 / acc_w[HEAD_DIM:HEAD_DIM + 1, :])

    score, mbt = {}, {}
    for h, g in chains:
        cur = (t_qs[h] + tq_lane) // SEL_BLOCK
        forced = (sidx == 0) | (sidx == cur) | (sidx == cur - 1)
        imp = ps_ref[2 * h + g][pl.ds(C_OFF - 1, N_SLC, stride=4), :]
        for d in range(1, 5):
            imp = imp + ps_ref[2 * h + g][pl.ds(C_OFF - 1 + d, N_SLC, stride=4), :]
        score[h, g] = jnp.where(forced, -jnp.inf, jnp.where(sidx <= cur, imp, NEG))
        mbt[h, g] = jnp.where(forced, 0.0, NEG)
    for _ in range(N_SEL - 3):
        for ch in chains:
            cm = jnp.max(score[ch], axis=0, keepdims=True)
            first = jnp.min(jnp.where(score[ch] == cm, sidx_f, float(N_SLC)), axis=0, keepdims=True)
            pick = sidx_f == first
            mbt[ch] = jnp.where(pick, 0.0, mbt[ch])
            score[ch] = jnp.where(pick, -jnp.inf, score[ch])

    for h, g in chains:
        qa_ref[2 * h + g] = jnp.concatenate(
            [jnp.concatenate([mbt[h, g].astype(BF16)] * GQA_REP, axis=1), qpad[h, g]], axis=0)
        mbt_tail = jnp.where(sidx < t_nfar[h] * (FAR // SEL_BLOCK), NEG, mbt[h, g]).astype(BF16)
        qa_tail = jnp.concatenate([jnp.concatenate([mbt_tail] * GQA_REP, axis=1), qpad[h, g]], axis=0)
        sc_ref[2 * h + g][...] = jnp.dot(kaug_ref[pl.ds(t_qs[h], TAIL), :], qa_tail,
                                         preferred_element_type=F32) + ds_ref[g]
    for h in tiles:
        scores(h, 0, 0)
    for h, g in chains:
        st = sc_ref[2 * h + g][...]
        mt = jnp.max(st, axis=0, keepdims=True)
        m_ref[2 * h + g] = mt
        acc_ref[2 * h + g] = jnp.dot(vst_ref[g, :, pl.ds(t_qs[h], TAIL)], jnp.exp2(st - mt).astype(BF16),
                                     preferred_element_type=F32)

    for h in tiles:
        nfar = t_nfar[h]

        def far_pair(kk, carry, h=h, nfar=nfar):
            c = 2 * kk
            scores(h, c + 1, 1)
            consume(h, nfar, c, 0)
            scores(h, c + 2, 0)
            consume(h, nfar, c + 1, 1)
            return carry

        lax.fori_loop(0, (nfar + 1) // 2, far_pair, 0)

    for h in tiles:
        lanes = slice(h * TQ, (h + 1) * TQ)
        out_rows = []
        for g in groups:
            dsl = slice(g * HEAD_DIM, (g + 1) * HEAD_DIM)
            hg = 2 * h + g
            o_s = acc_ref[hg, 0:HEAD_DIM, :] * (1.0 / acc_ref[hg, HEAD_DIM:HEAD_DIM + 1, :])
            for r in range(GQA_REP):
                cs = slice(r * TQ, (r + 1) * TQ)
                gc = 3 * (GQA_REP * g + r)
                out_rows.append(gt_ref[gc:gc + 1, lanes] * o_c[h, g][dsl, cs]
                                + gt_ref[gc + 1:gc + 2, lanes] * o_s[:, cs]
                                + gt_ref[gc + 2:gc + 3, lanes] * o_w[h, g][:, cs])
        o_ref[h * TQ:(h + 1) * TQ, :] = jnp.concatenate(out_rows, axis=0).T


def _attention(t_all, gates_t, kc2, vct, kaug, vst, kw_pad, vwt, ds, dw, bc, B, T):
    tp = T + WINDOW
    tqs = TQ * TILES
    per_b = lambda shape: pl.BlockSpec((None,) + shape, lambda b, i: (b,) + (0,) * len(shape),
                                      pipeline_mode=pl.Buffered(1))
    const = lambda a: pl.BlockSpec(a.shape, lambda b, i: (0,) * a.ndim, pipeline_mode=pl.Buffered(1))
    cols4 = GQA_REP * TQ
    n_hg = TILES * N_GROUPS_KV
    return pl.pallas_call(
        functools.partial(_attn_kernel, n_far_max=T // FAR),
        grid=(B, T // tqs),
        in_specs=[pl.BlockSpec((None, 512, tqs), lambda b, i: (b, 0, i)),
                  pl.BlockSpec((None, GATE_ROWS, tqs), lambda b, i: (b, 0, i)),
                  per_b((C_PAD, 256)),
                  per_b((128, C_PAD)),
                  per_b((tp, 256)),
                  per_b((N_GROUPS_KV, V_ROWS, tp)),
                  per_b((tp, 256)),
                  per_b((N_GROUPS_KV, V_ROWS, tp)),
                  const(ds), const(dw), const(bc)],
        out_specs=pl.BlockSpec((None, tqs, 512), lambda b, i: (b, i, 0)),
        out_shape=jax.ShapeDtypeStruct((B, T, 512), F32),
        scratch_shapes=([pltpu.VMEM((C_PAD, cols4), F32)] * n_hg
                        + [pltpu.VMEM((TAIL, cols4), F32)] * n_hg
                        + [pltpu.VMEM((C_PAD, TQ), F32)] * n_hg
                        + [pltpu.VMEM((n_hg, 256, cols4), BF16),
                           pltpu.VMEM((n_hg, 1, cols4), F32),
                           pltpu.VMEM((n_hg, V_ROWS, cols4), F32),
                           pltpu.VMEM((2 * n_hg, FAR, cols4), F32)]),
        compiler_params=_cparams(("parallel", "arbitrary"), VMEM_LIMIT),
        name="attn",
    )(t_all, gates_t, kc2, vct, kaug, vst, kw_pad, vwt, ds, dw, bc)


def _rglru_kernel(*refs, L):
    J = L // 8
    n_col = D_RNN // 128
    rx_refs, ry_refs = refs[:n_col], refs[n_col:2 * n_col]
    cw_ref, cb_ref, wa_ref, ba_ref, wx_ref, bx_ref, sp_ref, o_ref, xprev_ref, h_ref, os_ref = refs[2 * n_col:]

    @pl.when(pl.program_id(1) == 0)
    def _():
        xprev_ref[...] = jnp.zeros(xprev_ref.shape, F32)
        h_ref[...] = jnp.zeros(h_ref.shape, F32)

    def strided(col_refs):
        return jnp.concatenate(
            [jnp.concatenate([ref[pl.ds(j, 8, stride=J), :] for j in range(J)], axis=0)
             for ref in col_refs], axis=1)

    xp = strided(rx_refs)
    yp = strided(ry_refs)
    sub = lax.broadcasted_iota(jnp.int32, (8, D_RNN), 0)

    def delayed(d):
        heads = []
        for j in range(d):
            src = pltpu.roll(xp[(J + j - d) * 8:(J + j - d + 1) * 8], 1, axis=0)
            heads.append(jnp.where(sub == 0, xprev_ref[8 + j - d:9 + j - d, :], src))
        return jnp.concatenate(heads + [xp[:L - 8 * d]], axis=0)

    xc = (cb_ref[...] + cw_ref[0:1, :] * delayed(3) + cw_ref[1:2, :] * delayed(2)
          + cw_ref[2:3, :] * delayed(1) + cw_ref[3:4, :] * xp)
    xprev_ref[...] = jnp.concatenate([ref[L - 8:L, :] for ref in rx_refs], axis=1)
    xcb = xc.astype(BF16)
    r = jax.nn.sigmoid(jnp.dot(xcb, wa_ref[...], preferred_element_type=F32) + ba_ref[...])
    ig = jax.nn.sigmoid(jnp.dot(xcb, wx_ref[...], preferred_element_type=F32) + bx_ref[...])
    a = jnp.exp(-RG_C * r * sp_ref[...])
    bt = jnp.sqrt(1.0 - a * a) * (ig * xc)

    h = jnp.zeros((8, D_RNN), F32)
    dec = jnp.ones((8, D_RNN), F32)
    hs, decs = [], []
    for j in range(J):
        aj = a[j * 8:(j + 1) * 8]
        h = aj * h + bt[j * 8:(j + 1) * 8]
        dec = aj * dec
        hs.append(h)
        decs.append(dec)
    carry = h_ref[0:1, :]
    carries = []
    for s in range(8):
        carries.append(carry)
        carry = dec[s:s + 1, :] * carry + h[s:s + 1, :]
    h_ref[...] = jnp.broadcast_to(carry, h_ref.shape)
    h_in = jnp.concatenate(carries, axis=0)
    cdf_c = math.sqrt(2.0 / math.pi)
    for j in range(J):
        y = yp[j * 8:(j + 1) * 8]
        cdf = 0.5 * (1.0 + jnp.tanh(cdf_c * (y + 0.044715 * (y * y * y))))
        res = (hs[j] + decs[j] * h_in) * (y * cdf)
        for c in range(n_col):
            os_ref[c, pl.ds(j, 8, stride=J), :] = res[:, c * 128:(c + 1) * 128]
    for c in range(n_col):
        o_ref[:, c * 128:(c + 1) * 128] = os_ref[c]


def _rglru(rx, ry, conv_w, conv_b, wa_bd, b_a, wx_bd, b_x, sp, B, T):
    L = 512
    blk = pl.BlockSpec((None, L, D_RNN), lambda b, t: (b, t, 0))
    n_col = D_RNN // 128
    cols = [pl.BlockSpec((None, L, 128), functools.partial(lambda b, t, c: (b, t, c), c=c))
            for c in range(n_col)]
    full = lambda a: pl.BlockSpec(a.shape, lambda b, t: (0,) * a.ndim)
    return pl.pallas_call(
        functools.partial(_rglru_kernel, L=L),
        grid=(B, T // L),
        in_specs=cols + cols + [full(conv_w), full(conv_b), full(wa_bd), full(b_a), full(wx_bd),
                                full(b_x), full(sp)],
        out_specs=blk,
        out_shape=jax.ShapeDtypeStruct((B, T, D_RNN), F32),
        scratch_shapes=[pltpu.VMEM((8, D_RNN), F32), pltpu.VMEM((8, D_RNN), F32),
                        pltpu.VMEM((n_col, L, 128), F32)],
        compiler_params=_cparams(("parallel", "arbitrary"), VMEM_LIMIT),
        name="rglru",
    )(*([rx] * n_col + [ry] * n_col), conv_w, conv_b, wa_bd, b_a, wx_bd, b_x, sp)


def _layer_norm(y, g, b):
    mu = jnp.mean(y, axis=-1, keepdims=True)
    d = y - mu
    var = jnp.mean(d * d, axis=-1, keepdims=True)
    return d * lax.rsqrt(var + LN_EPS) * g + b


def _outproj_kernel(at_ref, rn_ref, x_ref, ga_ref, gr_ref, wo_ref, g1_ref, b1_ref, wrh_ref, wrl_ref,
                    br_ref, x1_ref, comb_ref):
    a = at_ref[...]
    rn = rn_ref[...]
    ha = a * lax.rsqrt(jnp.mean(a * a, axis=-1, keepdims=True) + RMS_EPS) * ga_ref[...]
    hr = rn * lax.rsqrt(jnp.mean(rn * rn, axis=-1, keepdims=True) + RMS_EPS) * gr_ref[...]
    heads = jnp.concatenate([ha, hr], axis=1).astype(BF16)
    mix = jnp.dot(heads, wo_ref[...], preferred_element_type=F32)
    x1 = _layer_norm(ALPHA * x_ref[...] + mix, g1_ref[...], b1_ref[...])
    x1_ref[...] = x1

    xh = x1.astype(BF16)
    xl = (x1 - xh.astype(F32)).astype(BF16)
    lg = (jnp.dot(xh, wrh_ref[...], preferred_element_type=F32)
          + jnp.dot(xl, wrh_ref[...], preferred_element_type=F32)
          + jnp.dot(xh, wrl_ref[...], preferred_element_type=F32)) + br_ref[...]
    lane = lax.broadcasted_iota(jnp.int32, lg.shape, 1)
    lane_f = lane.astype(F32)
    big = 1e9
    isg = lane < N_EXP_GROUPS
    gmax = jnp.max(jnp.where(isg, lg, -jnp.inf), axis=-1, keepdims=True)
    pg_top = 1.0 / jnp.sum(jnp.where(isg, jnp.exp(lg - gmax), 0.0), axis=-1, keepdims=True)
    gi = jnp.min(jnp.where(isg & (lg == gmax), lane_f, big), axis=-1, keepdims=True)
    egrp = ((lane - N_EXP_GROUPS) // EXPERTS_PER_GROUP).astype(F32)
    ise = (lane >= N_EXP_GROUPS) & (lane < N_EXP_GROUPS + N_EXPERTS) & (egrp == gi)
    emax = jnp.max(jnp.where(ise, lg, -jnp.inf), axis=-1, keepdims=True)
    i1 = jnp.min(jnp.where(ise & (lg == emax), lane_f, big), axis=-1, keepdims=True)
    rest = ise & (lane_f != i1)
    m2 = jnp.max(jnp.where(rest, lg, -jnp.inf), axis=-1, keepdims=True)
    i2 = jnp.min(jnp.where(rest & (lg == m2), lane_f, big), axis=-1, keepdims=True)
    e2 = jnp.exp(m2 - emax)
    inv = pg_top / (1.0 + e2)
    comb_ref[...] = jnp.where(lane_f == i1, inv, 0.0) + jnp.where(lane_f == i2, inv * e2, 0.0)


def _outproj(attn, rnn, x2, ga, gr, wo, g1, b1, wrh, wrl, br, N):
    tm = 512
    row = lambda w: pl.BlockSpec((tm, w), lambda i: (i, 0))
    full = lambda a: pl.BlockSpec(a.shape, lambda i: (0,) * a.ndim)
    return pl.pallas_call(
        _outproj_kernel,
        grid=(N // tm,),
        in_specs=[row(512), row(512), row(D_MODEL), full(ga), full(gr), full(wo), full(g1), full(b1),
                  full(wrh), full(wrl), full(br)],
        out_specs=[row(D_MODEL), row(128)],
        out_shape=[jax.ShapeDtypeStruct((N, D_MODEL), F32), jax.ShapeDtypeStruct((N, 128), F32)],
        compiler_params=_cparams(("parallel",), VMEM_LIMIT),
        name="outproj",
    )(attn, rnn, x2, ga, gr, wo, g1, b1, wrh, wrl, br)


def _moe_kernel(x1_ref, comb_ref, p_ref, tri_ref, wup_ref, wdn_ref, wg_ref, wp_ref, g2_ref, b2_ref, o_ref,
                acc_ref, xb_ref, rank_ref, rankt_ref, combt_ref):
    pi = pl.program_id(1)
    n_sub = x1_ref.shape[0] // MOE_SUB
    sub_rows = [slice(h * MOE_SUB, (h + 1) * MOE_SUB) for h in range(n_sub)]
    lane_col = N_EXP_GROUPS

    @pl.when(pi == 0)
    def _():
        x1 = x1_ref[...]
        xb = x1.astype(BF16)
        xb_ref[...] = xb
        gate = jax.nn.sigmoid(jnp.dot(xb, wg_ref[...], preferred_element_type=F32))
        ple = gate * jnp.dot(p_ref[...].astype(BF16), wp_ref[...], preferred_element_type=F32)
        acc_ref[...] = ALPHA * x1 + ple
        for rows in sub_rows:
            comb = comb_ref[rows, :]
            chosen = comb > 0.0
            rank = jnp.dot(tri_ref[...], jnp.where(chosen, 1.0, 0.0).astype(BF16),
                           preferred_element_type=F32)
            rank = jnp.where(chosen, rank, -1.0)
            rank_ref[rows, :] = rank
            rankt_ref[:, rows] = rank.T
            combt_ref[:, rows] = comb.T

    lane = lax.broadcasted_iota(jnp.int32, (MOE_SUB, 128), 1)
    slot_r = lax.broadcasted_iota(jnp.int32, (MOE_SLOTS, MOE_SUB), 0).astype(F32)
    slot_c = lax.broadcasted_iota(jnp.int32, (MOE_SUB, MOE_SLOTS), 1).astype(F32)

    def route_pair(base):
        chains = [(rows, k) for rows in sub_rows for k in range(2)]
        cmask, xc, u, hsw, y, ys, ct = {}, {}, {}, {}, {}, {}, {}
        for ch in chains:
            rows, k = ch
            rk_row = rankt_ref[pl.ds(2 * pi + k + lane_col, 1), rows] - base
            cmask[ch] = rk_row == slot_r
            xc[ch] = jnp.dot(jnp.where(cmask[ch], 1.0, 0.0).astype(BF16), xb_ref[rows, :],
                             preferred_element_type=F32).astype(BF16)
        for ch in chains:
            u[ch] = jnp.dot(xc[ch], wup_ref[ch[1]], preferred_element_type=F32)
        for ch in chains:
            ua = u[ch][:, :D_FF]
            hsw[ch] = (ua * jax.nn.sigmoid(ua) * u[ch][:, D_FF:]).astype(BF16)
        for ch in chains:
            y[ch] = jnp.dot(hsw[ch], wdn_ref[ch[1]], preferred_element_type=F32)
        for ch in chains:
            rows, k = ch
            e_lane = 2 * pi + k + lane_col
            w_row = combt_ref[pl.ds(e_lane, 1), rows]
            w_slot = jnp.sum(jnp.where(cmask[ch], w_row, 0.0), axis=1, keepdims=True)
            ys[ch] = (w_slot * y[ch]).astype(BF16)
            rk_col = jnp.sum(jnp.where(lane == e_lane, rank_ref[rows, :], 0.0), axis=1,
                             keepdims=True) - base
            ct[ch] = jnp.where(rk_col == slot_c, 1.0, 0.0).astype(BF16)
        for rows in sub_rows:
            pair = [(rows, k) for k in range(2)]
            ctp = jnp.concatenate([ct[ch] for ch in pair], axis=1)
            acc_ref[rows, :] = acc_ref[rows, :] + jnp.dot(
                ctp, jnp.concatenate([ys[ch] for ch in pair], axis=0), preferred_element_type=F32)

    route_pair(0.0)

    top_rank = jnp.max(jnp.maximum(rankt_ref[pl.ds(2 * pi + lane_col, 1), :],
                                   rankt_ref[pl.ds(2 * pi + 1 + lane_col, 1), :]))

    @pl.when(top_rank >= MOE_SLOTS)
    def _():
        def extra(sb, carry):
            route_pair((sb * MOE_SLOTS).astype(F32))
            return carry

        lax.fori_loop(1, top_rank.astype(jnp.int32) // MOE_SLOTS + 1, extra, 0)

    @pl.when(pi == N_EXPERTS // 2 - 1)
    def _():
        o_ref[...] = _layer_norm(acc_ref[...], g2_ref[...], b2_ref[...])


def _moe(x1, comb, p2, wup, wdn, wg, wp, g2, b2, N):
    tm = MOE_TILE
    tri = (jnp.arange(MOE_SUB)[None, :] < jnp.arange(MOE_SUB)[:, None]).astype(BF16)
    row = lambda w: pl.BlockSpec((tm, w), lambda i, e: (i, 0))
    full = lambda a: pl.BlockSpec(a.shape, lambda i, e: (0,) * a.ndim)
    return pl.pallas_call(
        _moe_kernel,
        grid=(N // tm, N_EXPERTS // 2),
        in_specs=[row(D_MODEL), row(128), row(D_PLE), full(tri),
                  pl.BlockSpec((2, D_MODEL, 2 * D_FF), lambda i, e: (e, 0, 0)),
                  pl.BlockSpec((2, D_FF, D_MODEL), lambda i, e: (e, 0, 0)),
                  full(wg), full(wp), full(g2), full(b2)],
        out_specs=row(D_MODEL),
        out_shape=jax.ShapeDtypeStruct((N, D_MODEL), F32),
        scratch_shapes=[pltpu.VMEM((tm, D_MODEL), F32),
                        pltpu.VMEM((tm, D_MODEL), BF16),
                        pltpu.VMEM((tm, 128), F32),
                        pltpu.VMEM((128, tm), F32),
                        pltpu.VMEM((128, tm), F32)],
        compiler_params=_cparams(("parallel", "arbitrary"), VMEM_LIMIT),
        name="moe",
    )(x1, comb, p2, tri, wup, wdn, wg, wp, g2, b2)


def _block_diag(w):
    n, k, j = w.shape
    eye = jnp.eye(n, dtype=w.dtype)
    return (w[:, :, None, :] * eye[:, None, :, None]).reshape(n * k, n * j)


def kernel(x, p, rel_bias, w_in, cmp_pe_k, cmp_pe_v, cmp_w_k, cmp_w_v, conv_w, conv_b, rg_w_a, rg_b_a,
           rg_w_x, rg_b_x, rg_lambda, attn_out_gain, rnn_out_gain, w_out, ln1_g, ln1_b, router_group_w,
           router_group_b, router_expert_w, router_expert_b, expert_w_up, expert_w_down, ple_w,
           ple_gate_w, ln2_g, ln2_b):
    B, T, _ = x.shape
    N = B * T
    assert T % FAR == 0 and T % (TQ * TILES) == 0 and T // SEL_BLOCK <= N_SLC and w_in.shape[0] == 1
    n_cmp = (T - CMP_LEN) // CMP_STRIDE + 1
    n_chunk = T // CMP_STRIDE
    assert C_OFF + n_chunk <= C_PAD
    row1 = lambda v: v.reshape(1, -1)

    w = w_in[0]
    c0 = D_ATTN
    kv = lambda k: w[:, c0 + k * D_KV:c0 + (k + 1) * D_KV]
    g0 = c0 + 6 * D_KV
    wq = w[:, :D_ATTN] * (HEAD_DIM ** -0.5 * LOG2E)
    w_main = jnp.concatenate([kv(2), kv(4), kv(0), kv(1), w[:, g0 + 24:]], axis=1).astype(BF16)
    w_t = jnp.concatenate([wq, kv(3), kv(5)], axis=1).T.astype(BF16)
    w_g = jnp.pad(w[:, g0:g0 + 24], ((0, 0), (0, GATE_ROWS - 24))).T.astype(BF16)

    x2 = x.reshape(N, D_MODEL)
    ks, kw, kc_raw, vc_raw, rx, ry, t_all, gates_t = _inproj(x2, w_main, w_t, w_g, B, T)

    def per_pos(wc):
        wl = wc.reshape(CMP_LEN, HEAD_DIM, HEAD_DIM)
        z = jnp.zeros_like(wl)
        return jnp.concatenate([jnp.concatenate([wl, z], axis=2), jnp.concatenate([z, wl], axis=2)],
                               axis=1).astype(BF16)

    pek = jnp.tile(cmp_pe_k[0], (1, N_GROUPS_KV))
    pev = jnp.tile(cmp_pe_v[0], (1, N_GROUPS_KV))
    grp_onehot = (jnp.arange(C_PAD)[:, None] // 8 == jnp.arange(128)[None, :]).astype(BF16)
    kc2, vct = _compress(kc_raw.reshape(B, T, 128), vc_raw.reshape(B, T, 128), pek, pev,
                         per_pos(cmp_w_k[0]), per_pos(cmp_w_v[0]), grp_onehot, B, T, n_cmp)

    onehot = (jnp.arange(T)[:, None] // SEL_BLOCK == jnp.arange(N_SLC)[None, :]).astype(BF16)
    kaug = jnp.concatenate([jnp.broadcast_to(onehot, (B, T, N_SLC)), ks.reshape(B, T, 128)], axis=2)
    lane256 = jnp.arange(256)
    pad_sel = jnp.broadcast_to((lane256 == N_SLC - 1).astype(BF16), (B, WINDOW, 256))
    pad_win = jnp.broadcast_to((lane256 == 128).astype(BF16), (B, WINDOW, 256))
    kaug = jnp.concatenate([pad_sel, kaug], axis=1)
    kw_pad = jnp.concatenate([pad_win, jnp.pad(kw.reshape(B, T, 128), ((0, 0), (0, 0), (0, 128)))], axis=1)
    ones_rows = jnp.concatenate([jnp.ones((B, N_GROUPS_KV, 1, T), BF16),
                                 jnp.zeros((B, N_GROUPS_KV, V_ROWS - HEAD_DIM - 1, T), BF16)], axis=2)

    def v_rows(vt):
        vt = jnp.concatenate([vt.reshape(B, N_GROUPS_KV, HEAD_DIM, T), ones_rows], axis=2)
        return jnp.pad(vt, ((0, 0), (0, 0), (0, 0), (WINDOW, 0)))

    vst = v_rows(t_all[:, 512:640])
    vwt = v_rows(t_all[:, 640:768])
    ds, dw, bc = _bias_tiles(rel_bias)

    attn = _attention(t_all, gates_t, kc2, vct, kaug, vst, kw_pad, vwt, ds, dw, bc, B, T)

    sp = jax.nn.softplus(-rg_lambda[0].astype(F32))
    rnn = _rglru(rx.reshape(B, T, D_RNN), ry.reshape(B, T, D_RNN), conv_w[0], row1(conv_b[0]),
                 _block_diag(rg_w_a[0]).astype(BF16), row1(rg_b_a[0]),
                 _block_diag(rg_w_x[0]).astype(BF16), row1(rg_b_x[0]), row1(sp), B, T)

    wr = jnp.pad(jnp.concatenate([router_group_w[0], router_expert_w[0]], axis=1), ((0, 0), (0, 108)))
    wrh = wr.astype(BF16)
    wrl = (wr - wrh.astype(F32)).astype(BF16)
    br = jnp.pad(jnp.concatenate([router_group_b[0], router_expert_b[0]]), (0, 108)).reshape(1, 128)
    x1, comb = _outproj(attn.reshape(N, 512), rnn.reshape(N, D_RNN), x2, row1(attn_out_gain[0]),
                        row1(rnn_out_gain[0]), w_out[0].astype(BF16), row1(ln1_g[0]), row1(ln1_b[0]),
                        wrh, wrl, br, N)

    out = _moe(x1, comb, p[0].reshape(N, D_PLE), expert_w_up[0].astype(BF16),
               expert_w_down[0].astype(BF16), ple_gate_w[0].astype(BF16), ple_w[0].astype(BF16),
               row1(ln2_g[0]), row1(ln2_b[0]), N)
    return out.reshape(B, T, D_MODEL)
```

```python
import functools
import math

import jax
import jax.numpy as jnp
from jax import lax
from jax.experimental import pallas as pl
from jax.experimental.pallas import tpu as pltpu

F32 = jnp.float32
BF16 = jnp.bfloat16
NEG = -1e30
LOG2E = 1.4426950408889634

D_MODEL = 1024
HEAD_DIM = 64
N_HEADS = 8
N_GROUPS_KV = 2
GQA_REP = 4
D_ATTN = 512
D_RNN = 512
D_KV = 128
CMP_LEN = 32
CMP_STRIDE = 16
SEL_BLOCK = 64
N_SEL = 16
WINDOW = 512
N_BUCKETS = 32
MAX_DISTANCE = 128
N_EXPERTS = 16
EXPERTS_PER_GROUP = 4
N_EXP_GROUPS = 4
D_FF = 512
D_PLE = 256
ALPHA = 2.0 ** 0.25
LN_EPS = 1e-5
RMS_EPS = 1e-6
FORCE_BONUS = 1e4
RG_C = 8.0

TQ = 128
TILES = 2
N_SLC = 128
C_PAD = 640
C_OFF = 16
FAR = 512
TAIL = WINDOW + TQ
BAND = 24
GATE_ROWS = 32
V_ROWS = 80
MOE_TILE = 1024
MOE_SUB = 512
MOE_SLOTS = 128
VMEM_LIMIT = 56 * 1024 * 1024


def _cparams(sem, vmem=None):
    return pltpu.CompilerParams(dimension_semantics=sem, vmem_limit_bytes=vmem)


def _inproj_kernel(x_ref, wm_ref, wt_ref, wg_ref, ks_ref, kw_ref, kc_ref, vc_ref, rx_ref, ry_ref,
                   t_ref, gt_ref):
    xb = x_ref[...].astype(BF16)

    def mm(lo, hi):
        return jnp.dot(xb, wm_ref[:, lo:hi], preferred_element_type=F32)

    ks_ref[...] = mm(0, 128).astype(BF16)
    kw_ref[...] = mm(128, 256).astype(BF16)
    kc_ref[...] = mm(256, 384)
    vc_ref[...] = mm(384, 512)
    rx_ref[...] = mm(512, 1024)
    ry_ref[...] = mm(1024, 1536)
    dn = (((1,), (1,)), ((), ()))
    t_ref[...] = lax.dot_general(wt_ref[...], xb, dn, preferred_element_type=F32).astype(BF16)
    gt_ref[...] = jax.nn.sigmoid(lax.dot_general(wg_ref[...], xb, dn, preferred_element_type=F32))


def _inproj(x2, w_main, w_t, w_g, B, T):
    N = B * T
    tm = 512
    nt = T // tm
    row = lambda w: pl.BlockSpec((tm, w), lambda i: (i, 0))
    full = lambda a: pl.BlockSpec(a.shape, lambda i: (0,) * a.ndim)
    tr = lambda r: pl.BlockSpec((None, r, tm), lambda i: (i // nt, 0, i % nt))
    return pl.pallas_call(
        _inproj_kernel,
        grid=(N // tm,),
        in_specs=[row(D_MODEL), full(w_main), full(w_t), full(w_g)],
        out_specs=[row(128), row(128), row(128), row(128), row(512), row(512), tr(768), tr(GATE_ROWS)],
        out_shape=[jax.ShapeDtypeStruct((N, 128), BF16),
                   jax.ShapeDtypeStruct((N, 128), BF16),
                   jax.ShapeDtypeStruct((N, 128), F32),
                   jax.ShapeDtypeStruct((N, 128), F32),
                   jax.ShapeDtypeStruct((N, 512), F32),
                   jax.ShapeDtypeStruct((N, 512), F32),
                   jax.ShapeDtypeStruct((B, 768, T), BF16),
                   jax.ShapeDtypeStruct((B, GATE_ROWS, T), F32)],
        compiler_params=_cparams(("parallel",), VMEM_LIMIT),
        name="inproj",
    )(x2, w_main, w_t, w_g)


def _compress_kernel(kr_ref, vr_ref, pek_ref, pev_ref, wk_ref, wv_ref, oh_ref, kc_ref, vct_ref, *,
                     n_cmp, n_chunk):
    def compress(raw_ref, pe_ref, w_ref):
        lo = jnp.zeros((n_chunk, 2 * HEAD_DIM), F32)
        hi = jnp.zeros((n_chunk, 2 * HEAD_DIM), F32)
        for j in range(CMP_STRIDE):
            a = raw_ref[pl.ds(j, n_chunk, stride=CMP_STRIDE), :]
            lo = lo + jnp.dot((a + pe_ref[j:j + 1, :]).astype(BF16), w_ref[j],
                              preferred_element_type=F32)
            hi = hi + jnp.dot((a + pe_ref[CMP_STRIDE + j:CMP_STRIDE + j + 1, :]).astype(BF16),
                              w_ref[CMP_STRIDE + j], preferred_element_type=F32)
        hi = pltpu.roll(hi, n_chunk - 1, axis=0)
        rid = lax.broadcasted_iota(jnp.int32, (n_chunk, 2 * HEAD_DIM), 0)
        out = jnp.where(rid < n_cmp, lo + hi, 0.0)
        return jnp.concatenate([jnp.zeros((C_OFF, 2 * HEAD_DIM), F32), out,
                                jnp.zeros((C_PAD - C_OFF - n_chunk, 2 * HEAD_DIM), F32)], axis=0)

    kc_ref[:, 0:2 * HEAD_DIM] = compress(kr_ref, pek_ref, wk_ref).astype(BF16)
    kc_ref[:, 2 * HEAD_DIM:] = oh_ref[...]
    vct_ref[...] = compress(vr_ref, pev_ref, wv_ref).T.astype(BF16)


def _compress(kc_raw, vc_raw, pek, pev, wk, wv, grp_onehot, B, T, n_cmp):
    n_chunk = T // CMP_STRIDE
    blk = pl.BlockSpec((None, T, 2 * HEAD_DIM), lambda b: (b, 0, 0))
    full = lambda a: pl.BlockSpec(a.shape, lambda b: (0,) * a.ndim)
    return pl.pallas_call(
        functools.partial(_compress_kernel, n_cmp=n_cmp, n_chunk=n_chunk),
        grid=(B,),
        in_specs=[blk, blk, full(pek), full(pev), full(wk), full(wv), full(grp_onehot)],
        out_specs=[pl.BlockSpec((None, C_PAD, 256), lambda b: (b, 0, 0)),
                   pl.BlockSpec((None, 2 * HEAD_DIM, C_PAD), lambda b: (b, 0, 0))],
        out_shape=[jax.ShapeDtypeStruct((B, C_PAD, 256), BF16),
                   jax.ShapeDtypeStruct((B, 2 * HEAD_DIM, C_PAD), BF16)],
        compiler_params=_cparams(("parallel",), VMEM_LIMIT),
        name="compress",
    )(kc_raw, vc_raw, pek, pev, wk, wv, grp_onehot)


def _bias_kernel(tab_ref, dt_ref, bt_ref, dc_ref, bc_ref, os_ref, ow_ref, oc_ref):
    h = pl.program_id(0) * GQA_REP + pl.program_id(1)

    def lookup(bk):
        out = jnp.zeros(bk.shape, F32)
        for b in range(N_BUCKETS):
            out = jnp.where(bk == b, tab_ref[h, b], out)
        return (out - tab_ref[h, N_BUCKETS - 1]) * LOG2E

    dt = dt_ref[...]
    v = lookup(bt_ref[...])
    os_ref[...] = jnp.where(dt >= 0, v, NEG)
    ow_ref[...] = jnp.where((dt >= 0) & (dt < WINDOW), v, NEG)
    oc_ref[...] = jnp.where(dc_ref[...] >= 0, lookup(bc_ref[...]), NEG)


def _t5_bucket(dist):
    max_exact = N_BUCKETS // 2
    d = jnp.maximum(dist, 0)
    df = jnp.maximum(d, 1).astype(F32)
    large = max_exact + (jnp.log(df / max_exact) / math.log(MAX_DISTANCE / max_exact)
                         * (N_BUCKETS - max_exact)).astype(jnp.int32)
    large = jnp.minimum(large, N_BUCKETS - 1)
    return jnp.where(d < max_exact, d, large)


def _bias_tiles(rel_bias):
    tq = jnp.arange(TQ, dtype=jnp.int32)
    dt = tq[None, :] + WINDOW - jnp.arange(TAIL, dtype=jnp.int32)[:, None]
    cc = jnp.arange(BAND, dtype=jnp.int32) - C_OFF
    dc = tq[None, :] - (cc[:, None] * CMP_STRIDE + CMP_LEN - 1)
    full = lambda a: pl.BlockSpec(a.shape, lambda g, r: (0,) * a.ndim)
    out = lambda rows: pl.BlockSpec((None, rows, TQ), lambda g, r: (g, 0, r))
    return pl.pallas_call(
        _bias_kernel,
        grid=(N_GROUPS_KV, GQA_REP),
        in_specs=[pl.BlockSpec(memory_space=pltpu.SMEM), full(dt), full(dt), full(dc), full(dc)],
        out_specs=[out(TAIL), out(TAIL), out(BAND)],
        out_shape=[jax.ShapeDtypeStruct((N_GROUPS_KV, TAIL, GQA_REP * TQ), F32),
                   jax.ShapeDtypeStruct((N_GROUPS_KV, TAIL, GQA_REP * TQ), F32),
                   jax.ShapeDtypeStruct((N_GROUPS_KV, BAND, GQA_REP * TQ), F32)],
        compiler_params=_cparams(("arbitrary", "arbitrary")),
        name="biastile",
    )(rel_bias.T, dt, _t5_bucket(dt), dc, _t5_bucket(dc))


def _attn_kernel(qt_ref, gt_ref, kc_ref, vct_ref, kaug_ref, vst_ref, kw_ref, vwt_ref, ds_ref, dw_ref,
                 bc_ref, o_ref, *scratch, n_far_max):
    n_hg = TILES * N_GROUPS_KV
    sc_ref, sw_ref, ps_ref = (scratch[n_hg * k:n_hg * (k + 1)] for k in range(3))
    qa_ref, m_ref, acc_ref, sb_ref, sm_ref = scratch[3 * n_hg:]
    cols4 = GQA_REP * TQ
    groups = range(N_GROUPS_KV)
    zeros_q = jnp.zeros((HEAD_DIM, cols4), BF16)
    grp = lax.broadcasted_iota(jnp.int32, (128, cols4), 0)
    neg_row = jnp.where(grp == 0, NEG, 0.0).astype(BF16)
    tq_col = lax.broadcasted_iota(jnp.int32, (1, cols4), 1) % TQ
    sidx = lax.broadcasted_iota(jnp.int32, (N_SLC, TQ), 0)
    sidx_f = sidx.astype(F32)
    tq_lane = lax.broadcasted_iota(jnp.int32, (N_SLC, TQ), 1)

    def chunk_start(c):
        return pl.multiple_of(WINDOW + jnp.minimum(c, n_far_max - 1) * FAR, FAR)

    def scores(h, c, par):
        kch = kaug_ref[pl.ds(chunk_start(c), FAR), :]
        for g in groups:
            sf = jnp.dot(kch, qa_ref[2 * h + g], preferred_element_type=F32)
            sb_ref[4 * h + 2 * g + par] = sf
            sm_ref[4 * h + 2 * g + par] = jnp.max(sf, axis=0, keepdims=True)

    def consume(h, nfar, c, par):
        c0 = chunk_start(c)
        live = c < nfar
        for g in groups:
            hg = 2 * h + g
            sf = sb_ref[4 * h + 2 * g + par]
            mprev = m_ref[hg]
            mcand = jnp.maximum(mprev, sm_ref[4 * h + 2 * g + par])
            mnew = jnp.where(live, mcand, mprev)
            pf = jnp.exp2(sf - jnp.where(live, mcand, -NEG)).astype(BF16)
            acc_ref[hg] = jnp.exp2(mprev - mnew) * acc_ref[hg] + jnp.dot(
                vst_ref[g, :, pl.ds(c0, FAR)], pf, preferred_element_type=F32)
            m_ref[hg] = mnew

    tiles = range(TILES)
    chains = [(h, g) for h in tiles for g in groups]
    t_i = [pl.program_id(1) * TILES + h for h in tiles]
    t_qs = [pl.multiple_of(i * TQ, TQ) for i in t_i]
    t_nfar = [jnp.maximum(i - 1, 0) // 4 for i in t_i]
    qpad = {}
    for h, g in chains:
        qgt = jnp.concatenate(
            [qt_ref[(GQA_REP * g + r) * HEAD_DIM:(GQA_REP * g + r + 1) * HEAD_DIM, h * TQ:(h + 1) * TQ]
             for r in range(GQA_REP)], axis=1)
        qpad[h, g] = jnp.concatenate([qgt, zeros_q] if g == 0 else [zeros_q, qgt], axis=0)

    for h, g in chains:
        i = t_i[h]
        gmask = jnp.where((grp < C_OFF // 8) | (grp >= i + BAND // 8), NEG, 0.0).astype(BF16)
        band0 = pl.multiple_of(i * 8, 8)
        sc_ref[2 * h + g][...] = jnp.dot(kc_ref[...], jnp.concatenate([qpad[h, g], gmask], axis=0),
                                         preferred_element_type=F32)
        sc_ref[2 * h + g][pl.ds(band0, BAND), :] = sc_ref[2 * h + g][pl.ds(band0, BAND), :] + bc_ref[g]
    w_max = {}
    for h, g in chains:
        sw = jnp.dot(kw_ref[pl.ds(t_qs[h], TAIL), :], jnp.concatenate([qpad[h, g], neg_row], axis=0),
                     preferred_element_type=F32) + dw_ref[g]
        sw_ref[2 * h + g][...] = sw
        w_max[h, g] = jnp.max(sw, axis=0, keepdims=True)
    o_c, o_w = {}, {}
    for h, g in chains:
        s = sc_ref[2 * h + g][...]
        e = jnp.exp2(s - jnp.max(s, axis=0, keepdims=True))
        has_cmp = t_qs[h] + tq_col >= CMP_LEN - 1
        pinv = jnp.where(has_cmp, 1.0 / jnp.sum(e, axis=0, keepdims=True), 0.0)
        o_c[h, g] = jnp.dot(vct_ref[...], e.astype(BF16), preferred_element_type=F32) * pinv
        ps_ref[2 * h + g][...] = sum(e[:, r * TQ:(r + 1) * TQ] * pinv[:, r * TQ:(r + 1) * TQ]
                                     for r in range(GQA_REP))
    for h, g in chains:
        pw = jnp.exp2(sw_ref[2 * h + g][...] - w_max[h, g]).astype(BF16)
        acc_w = jnp.dot(vwt_ref[g, :, pl.ds(t_qs[h], TAIL)], pw, preferred_element_type=F32)
        o_w[h, g] = acc_w[0:HEAD_DIM, :] * (1.0 / acc_w[HEAD_DIM:HEAD_DIM + 1, :])

    score, mbt = {}, {}
    for h, g in chains:
        cur = (t_qs[h] + tq_lane) // SEL_BLOCK
        forced = (sidx == 0) | (sidx == cur) | (sidx == cur - 1)
        imp = ps_ref[2 * h + g][pl.ds(C_OFF - 1, N_SLC, stride=4), :]
        for d in range(1, 5):
            imp = imp + ps_ref[2 * h + g][pl.ds(C_OFF - 1 + d, N_SLC, stride=4), :]
        score[h, g] = jnp.where(forced, -jnp.inf, jnp.where(sidx <= cur, imp, NEG))
        mbt[h, g] = jnp.where(forced, 0.0, NEG)
    for _ in range(N_SEL - 3):
        for ch in chains:
            cm = jnp.max(score[ch], axis=0, keepdims=True)
            first = jnp.min(jnp.where(score[ch] == cm, sidx_f, float(N_SLC)), axis=0, keepdims=True)
            pick = sidx_f == first
            mbt[ch] = jnp.where(pick, 0.0, mbt[ch])
            score[ch] = jnp.where(pick, -jnp.inf, score[ch])

    for h, g in chains:
        qa_ref[2 * h + g] = jnp.concatenate(
            [jnp.concatenate([mbt[h, g].astype(BF16)] * GQA_REP, axis=1), qpad[h, g]], axis=0)
        mbt_tail = jnp.where(sidx < t_nfar[h] * (FAR // SEL_BLOCK), NEG, mbt[h, g]).astype(BF16)
        qa_tail = jnp.concatenate([jnp.concatenate([mbt_tail] * GQA_REP, axis=1), qpad[h, g]], axis=0)
        st = jnp.dot(kaug_ref[pl.ds(t_qs[h], TAIL), :], qa_tail, preferred_element_type=F32) + ds_ref[g]
        sc_ref[2 * h + g][...] = st
        m_ref[2 * h + g] = jnp.max(st, axis=0, keepdims=True)
    for h in tiles:
        scores(h, 0, 0)
    for h, g in chains:
        pt = jnp.exp2(sc_ref[2 * h + g][...] - m_ref[2 * h + g]).astype(BF16)
        acc_ref[2 * h + g] = jnp.dot(vst_ref[g, :, pl.ds(t_qs[h], TAIL)], pt, preferred_element_type=F32)

    for h in tiles:
        nfar = t_nfar[h]

        def far_pair(kk, carry, h=h, nfar=nfar):
            c = 2 * kk
            scores(h, c + 1, 1)
            consume(h, nfar, c, 0)
            scores(h, c + 2, 0)
            consume(h, nfar, c + 1, 1)
            return carry

        lax.fori_loop(0, (nfar + 1) // 2, far_pair, 0)

    for h in tiles:
        lanes = slice(h * TQ, (h + 1) * TQ)
        out_rows = []
        for g in groups:
            dsl = slice(g * HEAD_DIM, (g + 1) * HEAD_DIM)
            hg = 2 * h + g
            o_s = acc_ref[hg, 0:HEAD_DIM, :] * (1.0 / acc_ref[hg, HEAD_DIM:HEAD_DIM + 1, :])
            for r in range(GQA_REP):
                cs = slice(r * TQ, (r + 1) * TQ)
                gc = 3 * (GQA_REP * g + r)
                out_rows.append(gt_ref[gc:gc + 1, lanes] * o_c[h, g][dsl, cs]
                                + gt_ref[gc + 1:gc + 2, lanes] * o_s[:, cs]
                                + gt_ref[gc + 2:gc + 3, lanes] * o_w[h, g][:, cs])
        o_ref[h * TQ:(h + 1) * TQ, :] = jnp.concatenate(out_rows, axis=0).T


def _attention(t_all, gates_t, kc2, vct, kaug, vst, kw_pad, vwt, ds, dw, bc, B, T):
    tp = T + WINDOW
    tqs = TQ * TILES
    per_b = lambda shape: pl.BlockSpec((None,) + shape, lambda b, i: (b,) + (0,) * len(shape),
                                      pipeline_mode=pl.Buffered(1))
    const = lambda a: pl.BlockSpec(a.shape, lambda b, i: (0,) * a.ndim, pipeline_mode=pl.Buffered(1))
    cols4 = GQA_REP * TQ
    n_hg = TILES * N_GROUPS_KV
    return pl.pallas_call(
        functools.partial(_attn_kernel, n_far_max=T // FAR),
        grid=(B, T // tqs),
        in_specs=[pl.BlockSpec((None, 512, tqs), lambda b, i: (b, 0, i)),
                  pl.BlockSpec((None, GATE_ROWS, tqs), lambda b, i: (b, 0, i)),
                  per_b((C_PAD, 256)),
                  per_b((128, C_PAD)),
                  per_b((tp, 256)),
                  per_b((N_GROUPS_KV, V_ROWS, tp)),
                  per_b((tp, 256)),
                  per_b((N_GROUPS_KV, V_ROWS, tp)),
                  const(ds), const(dw), const(bc)],
        out_specs=pl.BlockSpec((None, tqs, 512), lambda b, i: (b, i, 0)),
        out_shape=jax.ShapeDtypeStruct((B, T, 512), F32),
        scratch_shapes=([pltpu.VMEM((C_PAD, cols4), F32)] * n_hg
                        + [pltpu.VMEM((TAIL, cols4), F32)] * n_hg
                        + [pltpu.VMEM((C_PAD, TQ), F32)] * n_hg
                        + [pltpu.VMEM((n_hg, 256, cols4), BF16),
                           pltpu.VMEM((n_hg, 1, cols4), F32),
                           pltpu.VMEM((n_hg, V_ROWS, cols4), F32),
                           pltpu.VMEM((2 * n_hg, FAR, cols4), F32),
                           pltpu.VMEM((2 * n_hg, 1, cols4), F32)]),
        compiler_params=_cparams(("parallel", "arbitrary"), VMEM_LIMIT),
        name="attn",
    )(t_all, gates_t, kc2, vct, kaug, vst, kw_pad, vwt, ds, dw, bc)


def _rglru_kernel(*refs, L):
    J = L // 8
    n_col = D_RNN // 128
    rx_refs, ry_refs = refs[:n_col], refs[n_col:2 * n_col]
    cw_ref, cb_ref, wa_ref, ba_ref, wx_ref, bx_ref, sp_ref, o_ref, xprev_ref, h_ref, os_ref = refs[2 * n_col:]

    @pl.when(pl.program_id(1) == 0)
    def _():
        xprev_ref[...] = jnp.zeros(xprev_ref.shape, F32)
        h_ref[...] = jnp.zeros(h_ref.shape, F32)

    def strided(col_refs):
        return jnp.concatenate(
            [jnp.concatenate([ref[pl.ds(j, 8, stride=J), :] for j in range(J)], axis=0)
             for ref in col_refs], axis=1)

    xp = strided(rx_refs)
    yp = strided(ry_refs)
    sub = lax.broadcasted_iota(jnp.int32, (8, D_RNN), 0)

    def delayed(d):
        heads = []
        for j in range(d):
            src = pltpu.roll(xp[(J + j - d) * 8:(J + j - d + 1) * 8], 1, axis=0)
            heads.append(jnp.where(sub == 0, xprev_ref[8 + j - d:9 + j - d, :], src))
        return jnp.concatenate(heads + [xp[:L - 8 * d]], axis=0)

    xc = (cb_ref[...] + cw_ref[0:1, :] * delayed(3) + cw_ref[1:2, :] * delayed(2)
          + cw_ref[2:3, :] * delayed(1) + cw_ref[3:4, :] * xp)
    xprev_ref[...] = jnp.concatenate([ref[L - 8:L, :] for ref in rx_refs], axis=1)
    xcb = xc.astype(BF16)
    r = jax.nn.sigmoid(jnp.dot(xcb, wa_ref[...], preferred_element_type=F32) + ba_ref[...])
    ig = jax.nn.sigmoid(jnp.dot(xcb, wx_ref[...], preferred_element_type=F32) + bx_ref[...])
    a = jnp.exp(-RG_C * r * sp_ref[...])
    bt = jnp.sqrt(1.0 - a * a) * (ig * xc)

    h = jnp.zeros((8, D_RNN), F32)
    dec = jnp.ones((8, D_RNN), F32)
    hs, decs = [], []
    for j in range(J):
        aj = a[j * 8:(j + 1) * 8]
        h = aj * h + bt[j * 8:(j + 1) * 8]
        dec = aj * dec
        hs.append(h)
        decs.append(dec)
    carry = h_ref[0:1, :]
    carries = []
    for s in range(8):
        carries.append(carry)
        carry = dec[s:s + 1, :] * carry + h[s:s + 1, :]
    h_ref[...] = jnp.broadcast_to(carry, h_ref.shape)
    h_in = jnp.concatenate(carries, axis=0)
    cdf_c = math.sqrt(2.0 / math.pi)
    for j in range(J):
        y = yp[j * 8:(j + 1) * 8]
        cdf = 0.5 * (1.0 + jnp.tanh(cdf_c * (y + 0.044715 * (y * y * y))))
        res = (hs[j] + decs[j] * h_in) * (y * cdf)
        for c in range(n_col):
            os_ref[c, pl.ds(j, 8, stride=J), :] = res[:, c * 128:(c + 1) * 128]
    for c in range(n_col):
        o_ref[:, c * 128:(c + 1) * 128] = os_ref[c]


def _rglru(rx, ry, conv_w, conv_b, wa_bd, b_a, wx_bd, b_x, sp, B, T):
    L = 512
    blk = pl.BlockSpec((None, L, D_RNN), lambda b, t: (b, t, 0))
    n_col = D_RNN // 128
    cols = [pl.BlockSpec((None, L, 128), functools.partial(lambda b, t, c: (b, t, c), c=c))
            for c in range(n_col)]
    full = lambda a: pl.BlockSpec(a.shape, lambda b, t: (0,) * a.ndim)
    return pl.pallas_call(
        functools.partial(_rglru_kernel, L=L),
        grid=(B, T // L),
        in_specs=cols + cols + [full(conv_w), full(conv_b), full(wa_bd), full(b_a), full(wx_bd),
                                full(b_x), full(sp)],
        out_specs=blk,
        out_shape=jax.ShapeDtypeStruct((B, T, D_RNN), F32),
        scratch_shapes=[pltpu.VMEM((8, D_RNN), F32), pltpu.VMEM((8, D_RNN), F32),
                        pltpu.VMEM((n_col, L, 128), F32)],
        compiler_params=_cparams(("parallel", "arbitrary"), VMEM_LIMIT),
        name="rglru",
    )(*([rx] * n_col + [ry] * n_col), conv_w, conv_b, wa_bd, b_a, wx_bd, b_x, sp)


def _layer_norm(y, g, b):
    mu = jnp.mean(y, axis=-1, keepdims=True)
    d = y - mu
    var = jnp.mean(d * d, axis=-1, keepdims=True)
    return d * lax.rsqrt(var + LN_EPS) * g + b


def _outproj_kernel(at_ref, rn_ref, x_ref, ga_ref, gr_ref, wo_ref, g1_ref, b1_ref, wrh_ref, wrl_ref,
                    br_ref, x1_ref, comb_ref):
    a = at_ref[...]
    rn = rn_ref[...]
    ha = a * lax.rsqrt(jnp.mean(a * a, axis=-1, keepdims=True) + RMS_EPS) * ga_ref[...]
    hr = rn * lax.rsqrt(jnp.mean(rn * rn, axis=-1, keepdims=True) + RMS_EPS) * gr_ref[...]
    heads = jnp.concatenate([ha, hr], axis=1).astype(BF16)
    mix = jnp.dot(heads, wo_ref[...], preferred_element_type=F32)
    x1 = _layer_norm(ALPHA * x_ref[...] + mix, g1_ref[...], b1_ref[...])
    x1_ref[...] = x1

    xh = x1.astype(BF16)
    xl = (x1 - xh.astype(F32)).astype(BF16)
    lg = (jnp.dot(xh, wrh_ref[...], preferred_element_type=F32)
          + jnp.dot(xl, wrh_ref[...], preferred_element_type=F32)
          + jnp.dot(xh, wrl_ref[...], preferred_element_type=F32)) + br_ref[...]
    lane = lax.broadcasted_iota(jnp.int32, lg.shape, 1)
    lane_f = lane.astype(F32)
    big = 1e9
    isg = lane < N_EXP_GROUPS
    gmax = jnp.max(jnp.where(isg, lg, -jnp.inf), axis=-1, keepdims=True)
    pg_top = 1.0 / jnp.sum(jnp.where(isg, jnp.exp(lg - gmax), 0.0), axis=-1, keepdims=True)
    gi = jnp.min(jnp.where(isg & (lg == gmax), lane_f, big), axis=-1, keepdims=True)
    egrp = ((lane - N_EXP_GROUPS) // EXPERTS_PER_GROUP).astype(F32)
    ise = (lane >= N_EXP_GROUPS) & (lane < N_EXP_GROUPS + N_EXPERTS) & (egrp == gi)
    emax = jnp.max(jnp.where(ise, lg, -jnp.inf), axis=-1, keepdims=True)
    i1 = jnp.min(jnp.where(ise & (lg == emax), lane_f, big), axis=-1, keepdims=True)
    rest = ise & (lane_f != i1)
    m2 = jnp.max(jnp.where(rest, lg, -jnp.inf), axis=-1, keepdims=True)
    i2 = jnp.min(jnp.where(rest & (lg == m2), lane_f, big), axis=-1, keepdims=True)
    e2 = jnp.exp(m2 - emax)
    inv = pg_top / (1.0 + e2)
    comb_ref[...] = jnp.where(lane_f == i1, inv, 0.0) + jnp.where(lane_f == i2, inv * e2, 0.0)


def _outproj(attn, rnn, x2, ga, gr, wo, g1, b1, wrh, wrl, br, N):
    tm = 512
    row = lambda w: pl.BlockSpec((tm, w), lambda i: (i, 0))
    full = lambda a: pl.BlockSpec(a.shape, lambda i: (0,) * a.ndim)
    return pl.pallas_call(
        _outproj_kernel,
        grid=(N // tm,),
        in_specs=[row(512), row(512), row(D_MODEL), full(ga), full(gr), full(wo), full(g1), full(b1),
                  full(wrh), full(wrl), full(br)],
        out_specs=[row(D_MODEL), row(128)],
        out_shape=[jax.ShapeDtypeStruct((N, D_MODEL), F32), jax.ShapeDtypeStruct((N, 128), F32)],
        compiler_params=_cparams(("parallel",), VMEM_LIMIT),
        name="outproj",
    )(attn, rnn, x2, ga, gr, wo, g1, b1, wrh, wrl, br)


def _moe_kernel(x1_ref, comb_ref, p_ref, tri_ref, wup_ref, wdn_ref, wg_ref, wp_ref, g2_ref, b2_ref, o_ref,
                acc_ref, xb_ref, rank_ref, rankt_ref, combt_ref):
    pi = pl.program_id(1)
    n_sub = x1_ref.shape[0] // MOE_SUB
    sub_rows = [slice(h * MOE_SUB, (h + 1) * MOE_SUB) for h in range(n_sub)]
    lane_col = N_EXP_GROUPS

    @pl.when(pi == 0)
    def _():
        x1 = x1_ref[...]
        xb = x1.astype(BF16)
        xb_ref[...] = xb
        gate = jax.nn.sigmoid(jnp.dot(xb, wg_ref[...], preferred_element_type=F32))
        ple = gate * jnp.dot(p_ref[...].astype(BF16), wp_ref[...], preferred_element_type=F32)
        acc_ref[...] = ALPHA * x1 + ple
        for rows in sub_rows:
            comb = comb_ref[rows, :]
            chosen = comb > 0.0
            rank = jnp.dot(tri_ref[...], jnp.where(chosen, 1.0, 0.0).astype(BF16),
                           preferred_element_type=F32)
            rank = jnp.where(chosen, rank, -1.0)
            rank_ref[rows, :] = rank
            rankt_ref[:, rows] = rank.T
            combt_ref[:, rows] = comb.T

    lane = lax.broadcasted_iota(jnp.int32, (MOE_SUB, 128), 1)
    slot_r = lax.broadcasted_iota(jnp.int32, (MOE_SLOTS, MOE_SUB), 0).astype(F32)
    slot_c = lax.broadcasted_iota(jnp.int32, (MOE_SUB, MOE_SLOTS), 1).astype(F32)

    def route_pair(base):
        chains = [(rows, k) for rows in sub_rows for k in range(2)]
        cmask, xc, u, hsw, y, ys, ct = {}, {}, {}, {}, {}, {}, {}
        for ch in chains:
            rows, k = ch
            rk_row = rankt_ref[pl.ds(2 * pi + k + lane_col, 1), rows] - base
            cmask[ch] = rk_row == slot_r
            xc[ch] = jnp.dot(jnp.where(cmask[ch], 1.0, 0.0).astype(BF16), xb_ref[rows, :],
                             preferred_element_type=F32).astype(BF16)
        for ch in chains:
            u[ch] = jnp.dot(xc[ch], wup_ref[ch[1]], preferred_element_type=F32)
        for ch in chains:
            ua = u[ch][:, :D_FF]
            hsw[ch] = (ua * jax.nn.sigmoid(ua) * u[ch][:, D_FF:]).astype(BF16)
        for ch in chains:
            y[ch] = jnp.dot(hsw[ch], wdn_ref[ch[1]], preferred_element_type=F32)
        for ch in chains:
            rows, k = ch
            e_lane = 2 * pi + k + lane_col
            w_row = combt_ref[pl.ds(e_lane, 1), rows]
            w_slot = jnp.sum(jnp.where(cmask[ch], w_row, 0.0), axis=1, keepdims=True)
            ys[ch] = (w_slot * y[ch]).astype(BF16)
            rk_col = jnp.sum(jnp.where(lane == e_lane, rank_ref[rows, :], 0.0), axis=1,
                             keepdims=True) - base
            ct[ch] = jnp.where(rk_col == slot_c, 1.0, 0.0).astype(BF16)
        for rows in sub_rows:
            pair = [(rows, k) for k in range(2)]
            ctp = jnp.concatenate([ct[ch] for ch in pair], axis=1)
            acc_ref[rows, :] = acc_ref[rows, :] + jnp.dot(
                ctp, jnp.concatenate([ys[ch] for ch in pair], axis=0), preferred_element_type=F32)

    route_pair(0.0)

    top_rank = jnp.max(jnp.maximum(rankt_ref[pl.ds(2 * pi + lane_col, 1), :],
                                   rankt_ref[pl.ds(2 * pi + 1 + lane_col, 1), :]))

    @pl.when(top_rank >= MOE_SLOTS)
    def _():
        def extra(sb, carry):
            route_pair((sb * MOE_SLOTS).astype(F32))
            return carry

        lax.fori_loop(1, top_rank.astype(jnp.int32) // MOE_SLOTS + 1, extra, 0)

    @pl.when(pi == N_EXPERTS // 2 - 1)
    def _():
        o_ref[...] = _layer_norm(acc_ref[...], g2_ref[...], b2_ref[...])


def _moe(x1, comb, p2, wup, wdn, wg, wp, g2, b2, N):
    tm = MOE_TILE
    tri = (jnp.arange(MOE_SUB)[None, :] < jnp.arange(MOE_SUB)[:, None]).astype(BF16)
    row = lambda w: pl.BlockSpec((tm, w), lambda i, e: (i, 0))
    full = lambda a: pl.BlockSpec(a.shape, lambda i, e: (0,) * a.ndim)
    return pl.pallas_call(
        _moe_kernel,
        grid=(N // tm, N_EXPERTS // 2),
        in_specs=[row(D_MODEL), row(128), row(D_PLE), full(tri),
                  pl.BlockSpec((2, D_MODEL, 2 * D_FF), lambda i, e: (e, 0, 0)),
                  pl.BlockSpec((2, D_FF, D_MODEL), lambda i, e: (e, 0, 0)),
                  full(wg), full(wp), full(g2), full(b2)],
        out_specs=row(D_MODEL),
        out_shape=jax.ShapeDtypeStruct((N, D_MODEL), F32),
        scratch_shapes=[pltpu.VMEM((tm, D_MODEL), F32),
                        pltpu.VMEM((tm, D_MODEL), BF16),
                        pltpu.VMEM((tm, 128), F32),
                        pltpu.VMEM((128, tm), F32),
                        pltpu.VMEM((128, tm), F32)],
        compiler_params=_cparams(("parallel", "arbitrary"), VMEM_LIMIT),
        name="moe",
    )(x1, comb, p2, tri, wup, wdn, wg, wp, g2, b2)


def _block_diag(w):
    n, k, j = w.shape
    eye = jnp.eye(n, dtype=w.dtype)
    return (w[:, :, None, :] * eye[:, None, :, None]).reshape(n * k, n * j)


def kernel(x, p, rel_bias, w_in, cmp_pe_k, cmp_pe_v, cmp_w_k, cmp_w_v, conv_w, conv_b, rg_w_a, rg_b_a,
           rg_w_x, rg_b_x, rg_lambda, attn_out_gain, rnn_out_gain, w_out, ln1_g, ln1_b, router_group_w,
           router_group_b, router_expert_w, router_expert_b, expert_w_up, expert_w_down, ple_w,
           ple_gate_w, ln2_g, ln2_b):
    B, T, _ = x.shape
    N = B * T
    assert T % FAR == 0 and T % (TQ * TILES) == 0 and T // SEL_BLOCK <= N_SLC and w_in.shape[0] == 1
    n_cmp = (T - CMP_LEN) // CMP_STRIDE + 1
    n_chunk = T // CMP_STRIDE
    assert C_OFF + n_chunk <= C_PAD
    row1 = lambda v: v.reshape(1, -1)

    w = w_in[0]
    c0 = D_ATTN
    kv = lambda k: w[:, c0 + k * D_KV:c0 + (k + 1) * D_KV]
    g0 = c0 + 6 * D_KV
    wq = w[:, :D_ATTN] * (HEAD_DIM ** -0.5 * LOG2E)
    w_main = jnp.concatenate([kv(2), kv(4), kv(0), kv(1), w[:, g0 + 24:]], axis=1).astype(BF16)
    w_t = jnp.concatenate([wq, kv(3), kv(5)], axis=1).T.astype(BF16)
    w_g = jnp.pad(w[:, g0:g0 + 24], ((0, 0), (0, GATE_ROWS - 24))).T.astype(BF16)

    x2 = x.reshape(N, D_MODEL)
    ks, kw, kc_raw, vc_raw, rx, ry, t_all, gates_t = _inproj(x2, w_main, w_t, w_g, B, T)

    def per_pos(wc):
        wl = wc.reshape(CMP_LEN, HEAD_DIM, HEAD_DIM)
        z = jnp.zeros_like(wl)
        return jnp.concatenate([jnp.concatenate([wl, z], axis=2), jnp.concatenate([z, wl], axis=2)],
                               axis=1).astype(BF16)

    pek = jnp.tile(cmp_pe_k[0], (1, N_GROUPS_KV))
    pev = jnp.tile(cmp_pe_v[0], (1, N_GROUPS_KV))
    grp_onehot = (jnp.arange(C_PAD)[:, None] // 8 == jnp.arange(128)[None, :]).astype(BF16)
    kc2, vct = _compress(kc_raw.reshape(B, T, 128), vc_raw.reshape(B, T, 128), pek, pev,
                         per_pos(cmp_w_k[0]), per_pos(cmp_w_v[0]), grp_onehot, B, T, n_cmp)

    onehot = (jnp.arange(T)[:, None] // SEL_BLOCK == jnp.arange(N_SLC)[None, :]).astype(BF16)
    kaug = jnp.concatenate([jnp.broadcast_to(onehot, (B, T, N_SLC)), ks.reshape(B, T, 128)], axis=2)
    lane256 = jnp.arange(256)
    pad_sel = jnp.broadcast_to((lane256 == N_SLC - 1).astype(BF16), (B, WINDOW, 256))
    pad_win = jnp.broadcast_to((lane256 == 128).astype(BF16), (B, WINDOW, 256))
    kaug = jnp.concatenate([pad_sel, kaug], axis=1)
    kw_pad = jnp.concatenate([pad_win, jnp.pad(kw.reshape(B, T, 128), ((0, 0), (0, 0), (0, 128)))], axis=1)
    ones_rows = jnp.concatenate([jnp.ones((B, N_GROUPS_KV, 1, T), BF16),
                                 jnp.zeros((B, N_GROUPS_KV, V_ROWS - HEAD_DIM - 1, T), BF16)], axis=2)

    def v_rows(vt):
        vt = jnp.concatenate([vt.reshape(B, N_GROUPS_KV, HEAD_DIM, T), ones_rows], axis=2)
        return jnp.pad(vt, ((0, 0), (0, 0), (0, 0), (WINDOW, 0)))

    vst = v_rows(t_all[:, 512:640])
    vwt = v_rows(t_all[:, 640:768])
    ds, dw, bc = _bias_tiles(rel_bias)

    attn = _attention(t_all, gates_t, kc2, vct, kaug, vst, kw_pad, vwt, ds, dw, bc, B, T)

    sp = jax.nn.softplus(-rg_lambda[0].astype(F32))
    rnn = _rglru(rx.reshape(B, T, D_RNN), ry.reshape(B, T, D_RNN), conv_w[0], row1(conv_b[0]),
                 _block_diag(rg_w_a[0]).astype(BF16), row1(rg_b_a[0]),
                 _block_diag(rg_w_x[0]).astype(BF16), row1(rg_b_x[0]), row1(sp), B, T)

    wr = jnp.pad(jnp.concatenate([router_group_w[0], router_expert_w[0]], axis=1), ((0, 0), (0, 108)))
    wrh = wr.astype(BF16)
    wrl = (wr - wrh.astype(F32)).astype(BF16)
    br = jnp.pad(jnp.concatenate([router_group_b[0], router_expert_b[0]]), (0, 108)).reshape(1, 128)
    x1, comb = _outproj(attn.reshape(N, 512), rnn.reshape(N, D_RNN), x2, row1(attn_out_gain[0]),
                        row1(rnn_out_gain[0]), w_out[0].astype(BF16), row1(ln1_g[0]), row1(ln1_b[0]),
                        wrh, wrl, br, N)

    out = _moe(x1, comb, p[0].reshape(N, D_PLE), expert_w_up[0].astype(BF16),
               expert_w_down[0].astype(BF16), ple_gate_w[0].astype(BF16), ple_w[0].astype(BF16),
               row1(ln2_g[0]), row1(ln2_b[0]), N)
    return out.reshape(B, T, D_MODEL)
```

```python
import functools
import math

import jax
import jax.numpy as jnp
from jax import lax
from jax.experimental import pallas as pl
from jax.experimental.pallas import tpu as pltpu

F32 = jnp.float32
BF16 = jnp.bfloat16
NEG = -1e30
LOG2E = 1.4426950408889634

D_MODEL = 1024
HEAD_DIM = 64
N_HEADS = 8
N_GROUPS_KV = 2
GQA_REP = 4
D_ATTN = 512
D_RNN = 512
D_KV = 128
CMP_LEN = 32
CMP_STRIDE = 16
SEL_BLOCK = 64
N_SEL = 16
WINDOW = 512
N_BUCKETS = 32
MAX_DISTANCE = 128
N_EXPERTS = 16
EXPERTS_PER_GROUP = 4
N_EXP_GROUPS = 4
D_FF = 512
D_PLE = 256
ALPHA = 2.0 ** 0.25
LN_EPS = 1e-5
RMS_EPS = 1e-6
FORCE_BONUS = 1e4
RG_C = 8.0

TQ = 128
TILES = 2
N_SLC = 128
C_PAD = 640
C_OFF = 16
FAR = 512
TAIL = WINDOW + TQ
BAND = 24
GATE_ROWS = 32
V_ROWS = 80
MOE_TILE = 1024
MOE_SUB = 512
MOE_SLOTS = 128
VMEM_LIMIT = 56 * 1024 * 1024


def _cparams(sem, vmem=None):
    return pltpu.CompilerParams(dimension_semantics=sem, vmem_limit_bytes=vmem)


def _inproj_kernel(x_ref, wm_ref, wt_ref, wg_ref, ks_ref, kw_ref, kc_ref, vc_ref, rx_ref, ry_ref,
                   t_ref, gt_ref):
    xb = x_ref[...].astype(BF16)

    def mm(lo, hi):
        return jnp.dot(xb, wm_ref[:, lo:hi], preferred_element_type=F32)

    ks_ref[...] = mm(0, 128).astype(BF16)
    kw_ref[...] = mm(128, 256).astype(BF16)
    kc_ref[...] = mm(256, 384)
    vc_ref[...] = mm(384, 512)
    rx_ref[...] = mm(512, 1024)
    ry_ref[...] = mm(1024, 1536)
    dn = (((1,), (1,)), ((), ()))
    t_ref[...] = lax.dot_general(wt_ref[...], xb, dn, preferred_element_type=F32).astype(BF16)
    gt_ref[...] = jax.nn.sigmoid(lax.dot_general(wg_ref[...], xb, dn, preferred_element_type=F32))


def _inproj(x2, w_main, w_t, w_g, B, T):
    N = B * T
    tm = 512
    nt = T // tm
    row = lambda w: pl.BlockSpec((tm, w), lambda i: (i, 0))
    full = lambda a: pl.BlockSpec(a.shape, lambda i: (0,) * a.ndim)
    tr = lambda r: pl.BlockSpec((None, r, tm), lambda i: (i // nt, 0, i % nt))
    return pl.pallas_call(
        _inproj_kernel,
        grid=(N // tm,),
        in_specs=[row(D_MODEL), full(w_main), full(w_t), full(w_g)],
        out_specs=[row(128), row(128), row(128), row(128), row(512), row(512), tr(768), tr(GATE_ROWS)],
        out_shape=[jax.ShapeDtypeStruct((N, 128), BF16),
                   jax.ShapeDtypeStruct((N, 128), BF16),
                   jax.ShapeDtypeStruct((N, 128), F32),
                   jax.ShapeDtypeStruct((N, 128), F32),
                   jax.ShapeDtypeStruct((N, 512), F32),
                   jax.ShapeDtypeStruct((N, 512), F32),
                   jax.ShapeDtypeStruct((B, 768, T), BF16),
                   jax.ShapeDtypeStruct((B, GATE_ROWS, T), F32)],
        compiler_params=_cparams(("parallel",), VMEM_LIMIT),
        name="inproj",
    )(x2, w_main, w_t, w_g)


def _compress_kernel(kr_ref, vr_ref, pek_ref, pev_ref, wk_ref, wv_ref, oh_ref, kc_ref, vct_ref, *,
                     n_cmp, n_chunk):
    def compress(raw_ref, pe_ref, w_ref):
        lo = jnp.zeros((n_chunk, 2 * HEAD_DIM), F32)
        hi = jnp.zeros((n_chunk, 2 * HEAD_DIM), F32)
        for j in range(CMP_STRIDE):
            a = raw_ref[pl.ds(j, n_chunk, stride=CMP_STRIDE), :]
            lo = lo + jnp.dot((a + pe_ref[j:j + 1, :]).astype(BF16), w_ref[j],
                              preferred_element_type=F32)
            hi = hi + jnp.dot((a + pe_ref[CMP_STRIDE + j:CMP_STRIDE + j + 1, :]).astype(BF16),
                              w_ref[CMP_STRIDE + j], preferred_element_type=F32)
        hi = pltpu.roll(hi, n_chunk - 1, axis=0)
        rid = lax.broadcasted_iota(jnp.int32, (n_chunk, 2 * HEAD_DIM), 0)
        out = jnp.where(rid < n_cmp, lo + hi, 0.0)
        return jnp.concatenate([jnp.zeros((C_OFF, 2 * HEAD_DIM), F32), out,
                                jnp.zeros((C_PAD - C_OFF - n_chunk, 2 * HEAD_DIM), F32)], axis=0)

    kc_ref[:, 0:2 * HEAD_DIM] = compress(kr_ref, pek_ref, wk_ref).astype(BF16)
    kc_ref[:, 2 * HEAD_DIM:] = oh_ref[...]
    vct_ref[...] = compress(vr_ref, pev_ref, wv_ref).T.astype(BF16)


def _compress(kc_raw, vc_raw, pek, pev, wk, wv, grp_onehot, B, T, n_cmp):
    n_chunk = T // CMP_STRIDE
    blk = pl.BlockSpec((None, T, 2 * HEAD_DIM), lambda b: (b, 0, 0))
    full = lambda a: pl.BlockSpec(a.shape, lambda b: (0,) * a.ndim)
    return pl.pallas_call(
        functools.partial(_compress_kernel, n_cmp=n_cmp, n_chunk=n_chunk),
        grid=(B,),
        in_specs=[blk, blk, full(pek), full(pev), full(wk), full(wv), full(grp_onehot)],
        out_specs=[pl.BlockSpec((None, C_PAD, 256), lambda b: (b, 0, 0)),
                   pl.BlockSpec((None, 2 * HEAD_DIM, C_PAD), lambda b: (b, 0, 0))],
        out_shape=[jax.ShapeDtypeStruct((B, C_PAD, 256), BF16),
                   jax.ShapeDtypeStruct((B, 2 * HEAD_DIM, C_PAD), BF16)],
        compiler_params=_cparams(("parallel",), VMEM_LIMIT),
        name="compress",
    )(kc_raw, vc_raw, pek, pev, wk, wv, grp_onehot)


def _bias_kernel(tab_ref, dt_ref, bt_ref, dc_ref, bc_ref, os_ref, ow_ref, oc_ref):
    h = pl.program_id(0) * GQA_REP + pl.program_id(1)

    def lookup(bk):
        out = jnp.zeros(bk.shape, F32)
        for b in range(N_BUCKETS):
            out = jnp.where(bk == b, tab_ref[h, b], out)
        return (out - tab_ref[h, N_BUCKETS - 1]) * LOG2E

    dt = dt_ref[...]
    v = lookup(bt_ref[...])
    os_ref[...] = jnp.where(dt >= 0, v, NEG)
    ow_ref[...] = jnp.where((dt >= 0) & (dt < WINDOW), v, NEG)
    oc_ref[...] = jnp.where(dc_ref[...] >= 0, lookup(bc_ref[...]), NEG)


def _t5_bucket(dist):
    max_exact = N_BUCKETS // 2
    d = jnp.maximum(dist, 0)
    df = jnp.maximum(d, 1).astype(F32)
    large = max_exact + (jnp.log(df / max_exact) / math.log(MAX_DISTANCE / max_exact)
                         * (N_BUCKETS - max_exact)).astype(jnp.int32)
    large = jnp.minimum(large, N_BUCKETS - 1)
    return jnp.where(d < max_exact, d, large)


def _bias_tiles(rel_bias):
    tq = jnp.arange(TQ, dtype=jnp.int32)
    dt = tq[None, :] + WINDOW - jnp.arange(TAIL, dtype=jnp.int32)[:, None]
    cc = jnp.arange(BAND, dtype=jnp.int32) - C_OFF
    dc = tq[None, :] - (cc[:, None] * CMP_STRIDE + CMP_LEN - 1)
    full = lambda a: pl.BlockSpec(a.shape, lambda g, r: (0,) * a.ndim)
    out = lambda rows: pl.BlockSpec((None, rows, TQ), lambda g, r: (g, 0, r))
    return pl.pallas_call(
        _bias_kernel,
        grid=(N_GROUPS_KV, GQA_REP),
        in_specs=[pl.BlockSpec(memory_space=pltpu.SMEM), full(dt), full(dt), full(dc), full(dc)],
        out_specs=[out(TAIL), out(TAIL), out(BAND)],
        out_shape=[jax.ShapeDtypeStruct((N_GROUPS_KV, TAIL, GQA_REP * TQ), F32),
                   jax.ShapeDtypeStruct((N_GROUPS_KV, TAIL, GQA_REP * TQ), F32),
                   jax.ShapeDtypeStruct((N_GROUPS_KV, BAND, GQA_REP * TQ), F32)],
        compiler_params=_cparams(("arbitrary", "arbitrary")),
        name="biastile",
    )(rel_bias.T, dt, _t5_bucket(dt), dc, _t5_bucket(dc))


def _attn_kernel(qt_ref, gt_ref, kc_ref, vct_ref, kaug_ref, vst_ref, kw_ref, vwt_ref, ds_ref, dw_ref,
                 bc_ref, o_ref, *scratch, n_far_max):
    n_hg = TILES * N_GROUPS_KV
    sc_ref, sw_ref = (scratch[n_hg * k:n_hg * (k + 1)] for k in range(2))
    qa_ref, m_ref, acc_ref, sb_ref, sm_ref = scratch[2 * n_hg:]
    cols4 = GQA_REP * TQ
    groups = range(N_GROUPS_KV)
    zeros_q = jnp.zeros((HEAD_DIM, cols4), BF16)
    grp = lax.broadcasted_iota(jnp.int32, (128, cols4), 0)
    neg_row = jnp.where(grp == 0, NEG, 0.0).astype(BF16)
    tq_col = lax.broadcasted_iota(jnp.int32, (1, cols4), 1) % TQ
    sidx = lax.broadcasted_iota(jnp.int32, (N_SLC, TQ), 0)
    sidx_f = sidx.astype(F32)
    tq_lane = lax.broadcasted_iota(jnp.int32, (N_SLC, TQ), 1)

    def chunk_start(c):
        return pl.multiple_of(WINDOW + jnp.minimum(c, n_far_max - 1) * FAR, FAR)

    def scores(h, c, par):
        kch = kaug_ref[pl.ds(chunk_start(c), FAR), :]
        for g in groups:
            sf = jnp.dot(kch, qa_ref[2 * h + g], preferred_element_type=F32)
            sb_ref[4 * h + 2 * g + par] = sf
            sm_ref[4 * h + 2 * g + par] = jnp.max(sf, axis=0, keepdims=True)

    def consume(h, nfar, c, par):
        c0 = chunk_start(c)
        live = c < nfar
        for g in groups:
            hg = 2 * h + g
            sf = sb_ref[4 * h + 2 * g + par]
            mprev = m_ref[hg]
            mcand = jnp.maximum(mprev, sm_ref[4 * h + 2 * g + par])
            mnew = jnp.where(live, mcand, mprev)
            pf = jnp.exp2(sf - jnp.where(live, mcand, -NEG)).astype(BF16)
            acc_ref[hg] = jnp.exp2(mprev - mnew) * acc_ref[hg] + jnp.dot(
                vst_ref[g, :, pl.ds(c0, FAR)], pf, preferred_element_type=F32)
            m_ref[hg] = mnew

    tiles = range(TILES)
    chains = [(h, g) for h in tiles for g in groups]
    t_i = [pl.program_id(1) * TILES + h for h in tiles]
    t_qs = [pl.multiple_of(i * TQ, TQ) for i in t_i]
    t_nfar = [jnp.maximum(i - 1, 0) // 4 for i in t_i]
    qpad = {}
    for h, g in chains:
        qgt = jnp.concatenate(
            [qt_ref[(GQA_REP * g + r) * HEAD_DIM:(GQA_REP * g + r + 1) * HEAD_DIM, h * TQ:(h + 1) * TQ]
             for r in range(GQA_REP)], axis=1)
        qpad[h, g] = jnp.concatenate([qgt, zeros_q] if g == 0 else [zeros_q, qgt], axis=0)

    for h, g in chains:
        i = t_i[h]
        gmask = jnp.where((grp < C_OFF // 8) | (grp >= i + BAND // 8), NEG, 0.0).astype(BF16)
        band0 = pl.multiple_of(i * 8, 8)
        sc_ref[2 * h + g][...] = jnp.dot(kc_ref[...], jnp.concatenate([qpad[h, g], gmask], axis=0),
                                         preferred_element_type=F32)
        sc_ref[2 * h + g][pl.ds(band0, BAND), :] = sc_ref[2 * h + g][pl.ds(band0, BAND), :] + bc_ref[g]
    w_max = {}
    for h, g in chains:
        sw = jnp.dot(kw_ref[pl.ds(t_qs[h], TAIL), :], jnp.concatenate([qpad[h, g], neg_row], axis=0),
                     preferred_element_type=F32) + dw_ref[g]
        sw_ref[2 * h + g][...] = sw
        w_max[h, g] = jnp.max(sw, axis=0, keepdims=True)
    o_c, o_w, imp = {}, {}, {}
    for h, g in chains:
        s = sc_ref[2 * h + g][...]
        e = jnp.exp2(s - jnp.max(s, axis=0, keepdims=True))
        has_cmp = t_qs[h] + tq_col >= CMP_LEN - 1
        pinv = jnp.where(has_cmp, 1.0 / jnp.sum(e, axis=0, keepdims=True), 0.0)
        res = jnp.dot(vct_ref[...], e.astype(BF16), preferred_element_type=F32)
        o_c[h, g] = res[0:2 * HEAD_DIM, :] * pinv
        imp[h, g] = sum(res[2 * HEAD_DIM:, r * TQ:(r + 1) * TQ] * pinv[:, r * TQ:(r + 1) * TQ]
                        for r in range(GQA_REP))
    for h, g in chains:
        pw = jnp.exp2(sw_ref[2 * h + g][...] - w_max[h, g]).astype(BF16)
        acc_w = jnp.dot(vwt_ref[g, :, pl.ds(t_qs[h], TAIL)], pw, preferred_element_type=F32)
        o_w[h, g] = acc_w[0:HEAD_DIM, :] * (1.0 / acc_w[HEAD_DIM:HEAD_DIM + 1, :])

    score, mbt = {}, {}
    for h, g in chains:
        cur = (t_qs[h] + tq_lane) // SEL_BLOCK
        forced = (sidx == 0) | (sidx == cur) | (sidx == cur - 1)
        score[h, g] = jnp.where(forced, -jnp.inf, jnp.where(sidx <= cur, imp[h, g], NEG))
        mbt[h, g] = jnp.where(forced, 0.0, NEG)
    for _ in range(N_SEL - 3):
        for ch in chains:
            cm = jnp.max(score[ch], axis=0, keepdims=True)
            first = jnp.min(jnp.where(score[ch] == cm, sidx_f, float(N_SLC)), axis=0, keepdims=True)
            pick = sidx_f == first
            mbt[ch] = jnp.where(pick, 0.0, mbt[ch])
            score[ch] = jnp.where(pick, -jnp.inf, score[ch])

    for h, g in chains:
        qa_ref[2 * h + g] = jnp.concatenate(
            [jnp.concatenate([mbt[h, g].astype(BF16)] * GQA_REP, axis=1), qpad[h, g]], axis=0)
        mbt_tail = jnp.where(sidx < t_nfar[h] * (FAR // SEL_BLOCK), NEG, mbt[h, g]).astype(BF16)
        qa_tail = jnp.concatenate([jnp.concatenate([mbt_tail] * GQA_REP, axis=1), qpad[h, g]], axis=0)
        st = jnp.dot(kaug_ref[pl.ds(t_qs[h], TAIL), :], qa_tail, preferred_element_type=F32) + ds_ref[g]
        sc_ref[2 * h + g][...] = st
        m_ref[2 * h + g] = jnp.max(st, axis=0, keepdims=True)
    for h in tiles:
        scores(h, 0, 0)
    for h, g in chains:
        pt = jnp.exp2(sc_ref[2 * h + g][...] - m_ref[2 * h + g]).astype(BF16)
        acc_ref[2 * h + g] = jnp.dot(vst_ref[g, :, pl.ds(t_qs[h], TAIL)], pt, preferred_element_type=F32)

    for h in tiles:
        nfar = t_nfar[h]

        def far_pair(kk, carry, h=h, nfar=nfar):
            c = 2 * kk
            scores(h, c + 1, 1)
            consume(h, nfar, c, 0)
            scores(h, c + 2, 0)
            consume(h, nfar, c + 1, 1)
            return carry

        lax.fori_loop(0, (nfar + 1) // 2, far_pair, 0)

    for h in tiles:
        lanes = slice(h * TQ, (h + 1) * TQ)
        out_rows = []
        for g in groups:
            dsl = slice(g * HEAD_DIM, (g + 1) * HEAD_DIM)
            hg = 2 * h + g
            o_s = acc_ref[hg, 0:HEAD_DIM, :] * (1.0 / acc_ref[hg, HEAD_DIM:HEAD_DIM + 1, :])
            for r in range(GQA_REP):
                cs = slice(r * TQ, (r + 1) * TQ)
                gc = 3 * (GQA_REP * g + r)
                out_rows.append(gt_ref[gc:gc + 1, lanes] * o_c[h, g][dsl, cs]
                                + gt_ref[gc + 1:gc + 2, lanes] * o_s[:, cs]
                                + gt_ref[gc + 2:gc + 3, lanes] * o_w[h, g][:, cs])
        o_ref[h * TQ:(h + 1) * TQ, :] = jnp.concatenate(out_rows, axis=0).T


def _attention(t_all, gates_t, kc2, vct, kaug, vst, kw_pad, vwt, ds, dw, bc, B, T):
    tp = T + WINDOW
    tqs = TQ * TILES
    per_b = lambda shape: pl.BlockSpec((None,) + shape, lambda b, i: (b,) + (0,) * len(shape),
                                      pipeline_mode=pl.Buffered(1))
    const = lambda a: pl.BlockSpec(a.shape, lambda b, i: (0,) * a.ndim, pipeline_mode=pl.Buffered(1))
    cols4 = GQA_REP * TQ
    n_hg = TILES * N_GROUPS_KV
    return pl.pallas_call(
        functools.partial(_attn_kernel, n_far_max=T // FAR),
        grid=(B, T // tqs),
        in_specs=[pl.BlockSpec((None, 512, tqs), lambda b, i: (b, 0, i)),
                  pl.BlockSpec((None, GATE_ROWS, tqs), lambda b, i: (b, 0, i)),
                  per_b((C_PAD, 256)),
                  per_b((256, C_PAD)),
                  per_b((tp, 256)),
                  per_b((N_GROUPS_KV, V_ROWS, tp)),
                  per_b((tp, 256)),
                  per_b((N_GROUPS_KV, V_ROWS, tp)),
                  const(ds), const(dw), const(bc)],
        out_specs=pl.BlockSpec((None, tqs, 512), lambda b, i: (b, i, 0)),
        out_shape=jax.ShapeDtypeStruct((B, T, 512), F32),
        scratch_shapes=([pltpu.VMEM((C_PAD, cols4), F32)] * n_hg
                        + [pltpu.VMEM((TAIL, cols4), F32)] * n_hg
                        + [pltpu.VMEM((n_hg, 256, cols4), BF16),
                           pltpu.VMEM((n_hg, 1, cols4), F32),
                           pltpu.VMEM((n_hg, V_ROWS, cols4), F32),
                           pltpu.VMEM((2 * n_hg, FAR, cols4), F32),
                           pltpu.VMEM((2 * n_hg, 1, cols4), F32)]),
        compiler_params=_cparams(("parallel", "arbitrary"), VMEM_LIMIT),
        name="attn",
    )(t_all, gates_t, kc2, vct, kaug, vst, kw_pad, vwt, ds, dw, bc)


def _rglru_kernel(*refs, L):
    J = L // 8
    n_col = D_RNN // 128
    rx_refs, ry_refs = refs[:n_col], refs[n_col:2 * n_col]
    cw_ref, cb_ref, wa_ref, ba_ref, wx_ref, bx_ref, sp_ref, o_ref, xprev_ref, h_ref, os_ref = refs[2 * n_col:]

    @pl.when(pl.program_id(1) == 0)
    def _():
        xprev_ref[...] = jnp.zeros(xprev_ref.shape, F32)
        h_ref[...] = jnp.zeros(h_ref.shape, F32)

    def strided(col_refs):
        return jnp.concatenate(
            [jnp.concatenate([ref[pl.ds(j, 8, stride=J), :] for j in range(J)], axis=0)
             for ref in col_refs], axis=1)

    xp = strided(rx_refs)
    yp = strided(ry_refs)
    sub = lax.broadcasted_iota(jnp.int32, (8, D_RNN), 0)

    def delayed(d):
        heads = []
        for j in range(d):
            src = pltpu.roll(xp[(J + j - d) * 8:(J + j - d + 1) * 8], 1, axis=0)
            heads.append(jnp.where(sub == 0, xprev_ref[8 + j - d:9 + j - d, :], src))
        return jnp.concatenate(heads + [xp[:L - 8 * d]], axis=0)

    xc = (cb_ref[...] + cw_ref[0:1, :] * delayed(3) + cw_ref[1:2, :] * delayed(2)
          + cw_ref[2:3, :] * delayed(1) + cw_ref[3:4, :] * xp)
    xprev_ref[...] = jnp.concatenate([ref[L - 8:L, :] for ref in rx_refs], axis=1)
    xcb = xc.astype(BF16)
    r = jax.nn.sigmoid(jnp.dot(xcb, wa_ref[...], preferred_element_type=F32) + ba_ref[...])
    ig = jax.nn.sigmoid(jnp.dot(xcb, wx_ref[...], preferred_element_type=F32) + bx_ref[...])
    a = jnp.exp(-RG_C * r * sp_ref[...])
    bt = jnp.sqrt(1.0 - a * a) * (ig * xc)

    h = jnp.zeros((8, D_RNN), F32)
    dec = jnp.ones((8, D_RNN), F32)
    hs, decs = [], []
    for j in range(J):
        aj = a[j * 8:(j + 1) * 8]
        h = aj * h + bt[j * 8:(j + 1) * 8]
        dec = aj * dec
        hs.append(h)
        decs.append(dec)
    carry = h_ref[0:1, :]
    carries = []
    for s in range(8):
        carries.append(carry)
        carry = dec[s:s + 1, :] * carry + h[s:s + 1, :]
    h_ref[...] = jnp.broadcast_to(carry, h_ref.shape)
    h_in = jnp.concatenate(carries, axis=0)
    cdf_c = math.sqrt(2.0 / math.pi)
    for j in range(J):
        y = yp[j * 8:(j + 1) * 8]
        cdf = 0.5 * (1.0 + jnp.tanh(cdf_c * (y + 0.044715 * (y * y * y))))
        res = (hs[j] + decs[j] * h_in) * (y * cdf)
        for c in range(n_col):
            os_ref[c, pl.ds(j, 8, stride=J), :] = res[:, c * 128:(c + 1) * 128]
    for c in range(n_col):
        o_ref[:, c * 128:(c + 1) * 128] = os_ref[c]


def _rglru(rx, ry, conv_w, conv_b, wa_bd, b_a, wx_bd, b_x, sp, B, T):
    L = 512
    blk = pl.BlockSpec((None, L, D_RNN), lambda b, t: (b, t, 0))
    n_col = D_RNN // 128
    cols = [pl.BlockSpec((None, L, 128), functools.partial(lambda b, t, c: (b, t, c), c=c))
            for c in range(n_col)]
    full = lambda a: pl.BlockSpec(a.shape, lambda b, t: (0,) * a.ndim)
    return pl.pallas_call(
        functools.partial(_rglru_kernel, L=L),
        grid=(B, T // L),
        in_specs=cols + cols + [full(conv_w), full(conv_b), full(wa_bd), full(b_a), full(wx_bd),
                                full(b_x), full(sp)],
        out_specs=blk,
        out_shape=jax.ShapeDtypeStruct((B, T, D_RNN), F32),
        scratch_shapes=[pltpu.VMEM((8, D_RNN), F32), pltpu.VMEM((8, D_RNN), F32),
                        pltpu.VMEM((n_col, L, 128), F32)],
        compiler_params=_cparams(("parallel", "arbitrary"), VMEM_LIMIT),
        name="rglru",
    )(*([rx] * n_col + [ry] * n_col), conv_w, conv_b, wa_bd, b_a, wx_bd, b_x, sp)


def _layer_norm(y, g, b):
    mu = jnp.mean(y, axis=-1, keepdims=True)
    d = y - mu
    var = jnp.mean(d * d, axis=-1, keepdims=True)
    return d * lax.rsqrt(var + LN_EPS) * g + b


def _outproj_kernel(at_ref, rn_ref, x_ref, ga_ref, gr_ref, wo_ref, g1_ref, b1_ref, wrh_ref, wrl_ref,
                    br_ref, x1_ref, comb_ref):
    a = at_ref[...]
    rn = rn_ref[...]
    ha = a * lax.rsqrt(jnp.mean(a * a, axis=-1, keepdims=True) + RMS_EPS) * ga_ref[...]
    hr = rn * lax.rsqrt(jnp.mean(rn * rn, axis=-1, keepdims=True) + RMS_EPS) * gr_ref[...]
    heads = jnp.concatenate([ha, hr], axis=1).astype(BF16)
    mix = jnp.dot(heads, wo_ref[...], preferred_element_type=F32)
    x1 = _layer_norm(ALPHA * x_ref[...] + mix, g1_ref[...], b1_ref[...])
    x1_ref[...] = x1

    xh = x1.astype(BF16)
    xl = (x1 - xh.astype(F32)).astype(BF16)
    lg = (jnp.dot(xh, wrh_ref[...], preferred_element_type=F32)
          + jnp.dot(xl, wrh_ref[...], preferred_element_type=F32)
          + jnp.dot(xh, wrl_ref[...], preferred_element_type=F32)) + br_ref[...]
    lane = lax.broadcasted_iota(jnp.int32, lg.shape, 1)
    lane_f = lane.astype(F32)
    big = 1e9
    isg = lane < N_EXP_GROUPS
    gmax = jnp.max(jnp.where(isg, lg, -jnp.inf), axis=-1, keepdims=True)
    pg_top = 1.0 / jnp.sum(jnp.where(isg, jnp.exp(lg - gmax), 0.0), axis=-1, keepdims=True)
    gi = jnp.min(jnp.where(isg & (lg == gmax), lane_f, big), axis=-1, keepdims=True)
    egrp = ((lane - N_EXP_GROUPS) // EXPERTS_PER_GROUP).astype(F32)
    ise = (lane >= N_EXP_GROUPS) & (lane < N_EXP_GROUPS + N_EXPERTS) & (egrp == gi)
    emax = jnp.max(jnp.where(ise, lg, -jnp.inf), axis=-1, keepdims=True)
    i1 = jnp.min(jnp.where(ise & (lg == emax), lane_f, big), axis=-1, keepdims=True)
    rest = ise & (lane_f != i1)
    m2 = jnp.max(jnp.where(rest, lg, -jnp.inf), axis=-1, keepdims=True)
    i2 = jnp.min(jnp.where(rest & (lg == m2), lane_f, big), axis=-1, keepdims=True)
    e2 = jnp.exp(m2 - emax)
    inv = pg_top / (1.0 + e2)
    comb_ref[...] = jnp.where(lane_f == i1, inv, 0.0) + jnp.where(lane_f == i2, inv * e2, 0.0)


def _outproj(attn, rnn, x2, ga, gr, wo, g1, b1, wrh, wrl, br, N):
    tm = 512
    row = lambda w: pl.BlockSpec((tm, w), lambda i: (i, 0))
    full = lambda a: pl.BlockSpec(a.shape, lambda i: (0,) * a.ndim)
    return pl.pallas_call(
        _outproj_kernel,
        grid=(N // tm,),
        in_specs=[row(512), row(512), row(D_MODEL), full(ga), full(gr), full(wo), full(g1), full(b1),
                  full(wrh), full(wrl), full(br)],
        out_specs=[row(D_MODEL), row(128)],
        out_shape=[jax.ShapeDtypeStruct((N, D_MODEL), F32), jax.ShapeDtypeStruct((N, 128), F32)],
        compiler_params=_cparams(("parallel",), VMEM_LIMIT),
        name="outproj",
    )(attn, rnn, x2, ga, gr, wo, g1, b1, wrh, wrl, br)


def _moe_kernel(x1_ref, comb_ref, p_ref, tri_ref, wup_ref, wdn_ref, wg_ref, wp_ref, g2_ref, b2_ref, o_ref,
                acc_ref, xb_ref, rank_ref, rankt_ref, combt_ref):
    pi = pl.program_id(1)
    n_sub = x1_ref.shape[0] // MOE_SUB
    sub_rows = [slice(h * MOE_SUB, (h + 1) * MOE_SUB) for h in range(n_sub)]
    lane_col = N_EXP_GROUPS

    @pl.when(pi == 0)
    def _():
        x1 = x1_ref[...]
        xb = x1.astype(BF16)
        xb_ref[...] = xb
        gate = jax.nn.sigmoid(jnp.dot(xb, wg_ref[...], preferred_element_type=F32))
        ple = gate * jnp.dot(p_ref[...].astype(BF16), wp_ref[...], preferred_element_type=F32)
        acc_ref[...] = ALPHA * x1 + ple
        for rows in sub_rows:
            comb = comb_ref[rows, :]
            chosen = comb > 0.0
            rank = jnp.dot(tri_ref[...], jnp.where(chosen, 1.0, 0.0).astype(BF16),
                           preferred_element_type=F32)
            rank = jnp.where(chosen, rank, -1.0)
            rank_ref[rows, :] = rank
            rankt_ref[:, rows] = rank.T
            combt_ref[:, rows] = comb.T

    lane = lax.broadcasted_iota(jnp.int32, (MOE_SUB, 128), 1)
    slot_r = lax.broadcasted_iota(jnp.int32, (MOE_SLOTS, MOE_SUB), 0).astype(F32)
    slot_c = lax.broadcasted_iota(jnp.int32, (MOE_SUB, MOE_SLOTS), 1).astype(F32)

    def route_pair(base):
        chains = [(rows, k) for rows in sub_rows for k in range(2)]
        cmask, xc, u, hsw, y, ys, ct = {}, {}, {}, {}, {}, {}, {}
        for ch in chains:
            rows, k = ch
            rk_row = rankt_ref[pl.ds(2 * pi + k + lane_col, 1), rows] - base
            cmask[ch] = rk_row == slot_r
            xc[ch] = jnp.dot(jnp.where(cmask[ch], 1.0, 0.0).astype(BF16), xb_ref[rows, :],
                             preferred_element_type=F32).astype(BF16)
        for ch in chains:
            u[ch] = jnp.dot(xc[ch], wup_ref[ch[1]], preferred_element_type=F32)
        for ch in chains:
            ua = u[ch][:, :D_FF]
            hsw[ch] = (ua * jax.nn.sigmoid(ua) * u[ch][:, D_FF:]).astype(BF16)
        for ch in chains:
            y[ch] = jnp.dot(hsw[ch], wdn_ref[ch[1]], preferred_element_type=F32)
        for ch in chains:
            rows, k = ch
            e_lane = 2 * pi + k + lane_col
            w_row = combt_ref[pl.ds(e_lane, 1), rows]
            w_slot = jnp.sum(jnp.where(cmask[ch], w_row, 0.0), axis=1, keepdims=True)
            ys[ch] = (w_slot * y[ch]).astype(BF16)
            rk_col = jnp.sum(jnp.where(lane == e_lane, rank_ref[rows, :], 0.0), axis=1,
                             keepdims=True) - base
            ct[ch] = jnp.where(rk_col == slot_c, 1.0, 0.0).astype(BF16)
        for rows in sub_rows:
            pair = [(rows, k) for k in range(2)]
            ctp = jnp.concatenate([ct[ch] for ch in pair], axis=1)
            acc_ref[rows, :] = acc_ref[rows, :] + jnp.dot(
                ctp, jnp.concatenate([ys[ch] for ch in pair], axis=0), preferred_element_type=F32)

    route_pair(0.0)

    top_rank = jnp.max(jnp.maximum(rankt_ref[pl.ds(2 * pi + lane_col, 1), :],
                                   rankt_ref[pl.ds(2 * pi + 1 + lane_col, 1), :]))

    @pl.when(top_rank >= MOE_SLOTS)
    def _():
        def extra(sb, carry):
            route_pair((sb * MOE_SLOTS).astype(F32))
            return carry

        lax.fori_loop(1, top_rank.astype(jnp.int32) // MOE_SLOTS + 1, extra, 0)

    @pl.when(pi == N_EXPERTS // 2 - 1)
    def _():
        o_ref[...] = _layer_norm(acc_ref[...], g2_ref[...], b2_ref[...])


def _moe(x1, comb, p2, wup, wdn, wg, wp, g2, b2, N):
    tm = MOE_TILE
    tri = (jnp.arange(MOE_SUB)[None, :] < jnp.arange(MOE_SUB)[:, None]).astype(BF16)
    row = lambda w: pl.BlockSpec((tm, w), lambda i, e: (i, 0))
    full = lambda a: pl.BlockSpec(a.shape, lambda i, e: (0,) * a.ndim)
    return pl.pallas_call(
        _moe_kernel,
        grid=(N // tm, N_EXPERTS // 2),
        in_specs=[row(D_MODEL), row(128), row(D_PLE), full(tri),
                  pl.BlockSpec((2, D_MODEL, 2 * D_FF), lambda i, e: (e, 0, 0)),
                  pl.BlockSpec((2, D_FF, D_MODEL), lambda i, e: (e, 0, 0)),
                  full(wg), full(wp), full(g2), full(b2)],
        out_specs=row(D_MODEL),
        out_shape=jax.ShapeDtypeStruct((N, D_MODEL), F32),
        scratch_shapes=[pltpu.VMEM((tm, D_MODEL), F32),
                        pltpu.VMEM((tm, D_MODEL), BF16),
                        pltpu.VMEM((tm, 128), F32),
                        pltpu.VMEM((128, tm), F32),
                        pltpu.VMEM((128, tm), F32)],
        compiler_params=_cparams(("parallel", "arbitrary"), VMEM_LIMIT),
        name="moe",
    )(x1, comb, p2, tri, wup, wdn, wg, wp, g2, b2)


def _block_diag(w):
    n, k, j = w.shape
    eye = jnp.eye(n, dtype=w.dtype)
    return (w[:, :, None, :] * eye[:, None, :, None]).reshape(n * k, n * j)


def kernel(x, p, rel_bias, w_in, cmp_pe_k, cmp_pe_v, cmp_w_k, cmp_w_v, conv_w, conv_b, rg_w_a, rg_b_a,
           rg_w_x, rg_b_x, rg_lambda, attn_out_gain, rnn_out_gain, w_out, ln1_g, ln1_b, router_group_w,
           router_group_b, router_expert_w, router_expert_b, expert_w_up, expert_w_down, ple_w,
           ple_gate_w, ln2_g, ln2_b):
    B, T, _ = x.shape
    N = B * T
    assert T % FAR == 0 and T % (TQ * TILES) == 0 and T // SEL_BLOCK <= N_SLC and w_in.shape[0] == 1
    n_cmp = (T - CMP_LEN) // CMP_STRIDE + 1
    n_chunk = T // CMP_STRIDE
    assert C_OFF + n_chunk <= C_PAD
    row1 = lambda v: v.reshape(1, -1)

    w = w_in[0]
    c0 = D_ATTN
    kv = lambda k: w[:, c0 + k * D_KV:c0 + (k + 1) * D_KV]
    g0 = c0 + 6 * D_KV
    wq = w[:, :D_ATTN] * (HEAD_DIM ** -0.5 * LOG2E)
    w_main = jnp.concatenate([kv(2), kv(4), kv(0), kv(1), w[:, g0 + 24:]], axis=1).astype(BF16)
    w_t = jnp.concatenate([wq, kv(3), kv(5)], axis=1).T.astype(BF16)
    w_g = jnp.pad(w[:, g0:g0 + 24], ((0, 0), (0, GATE_ROWS - 24))).T.astype(BF16)

    x2 = x.reshape(N, D_MODEL)
    ks, kw, kc_raw, vc_raw, rx, ry, t_all, gates_t = _inproj(x2, w_main, w_t, w_g, B, T)

    def per_pos(wc):
        wl = wc.reshape(CMP_LEN, HEAD_DIM, HEAD_DIM)
        z = jnp.zeros_like(wl)
        return jnp.concatenate([jnp.concatenate([wl, z], axis=2), jnp.concatenate([z, wl], axis=2)],
                               axis=1).astype(BF16)

    pek = jnp.tile(cmp_pe_k[0], (1, N_GROUPS_KV))
    pev = jnp.tile(cmp_pe_v[0], (1, N_GROUPS_KV))
    grp_onehot = (jnp.arange(C_PAD)[:, None] // 8 == jnp.arange(128)[None, :]).astype(BF16)
    kc2, vct = _compress(kc_raw.reshape(B, T, 128), vc_raw.reshape(B, T, 128), pek, pev,
                         per_pos(cmp_w_k[0]), per_pos(cmp_w_v[0]), grp_onehot, B, T, n_cmp)

    onehot = (jnp.arange(T)[:, None] // SEL_BLOCK == jnp.arange(N_SLC)[None, :]).astype(BF16)
    kaug = jnp.concatenate([jnp.broadcast_to(onehot, (B, T, N_SLC)), ks.reshape(B, T, 128)], axis=2)
    lane256 = jnp.arange(256)
    pad_sel = jnp.broadcast_to((lane256 == N_SLC - 1).astype(BF16), (B, WINDOW, 256))
    pad_win = jnp.broadcast_to((lane256 == 128).astype(BF16), (B, WINDOW, 256))
    kaug = jnp.concatenate([pad_sel, kaug], axis=1)
    kw_pad = jnp.concatenate([pad_win, jnp.pad(kw.reshape(B, T, 128), ((0, 0), (0, 0), (0, 128)))], axis=1)
    ones_rows = jnp.concatenate([jnp.ones((B, N_GROUPS_KV, 1, T), BF16),
                                 jnp.zeros((B, N_GROUPS_KV, V_ROWS - HEAD_DIM - 1, T), BF16)], axis=2)

    def v_rows(vt):
        vt = jnp.concatenate([vt.reshape(B, N_GROUPS_KV, HEAD_DIM, T), ones_rows], axis=2)
        return jnp.pad(vt, ((0, 0), (0, 0), (0, 0), (WINDOW, 0)))

    vst = v_rows(t_all[:, 512:640])
    vwt = v_rows(t_all[:, 640:768])
    ds, dw, bc = _bias_tiles(rel_bias)

    cidx = jnp.arange(C_PAD) - C_OFF
    s_lo = jnp.arange(N_SLC)[:, None] * SEL_BLOCK
    overlap = ((cidx[None, :] * CMP_STRIDE < s_lo + SEL_BLOCK) & (cidx[None, :] * CMP_STRIDE + CMP_LEN > s_lo)
               & (cidx[None, :] >= 0) & (cidx[None, :] < n_cmp)).astype(BF16)
    vct = jnp.concatenate([vct, jnp.broadcast_to(overlap, (B, N_SLC, C_PAD))], axis=1)
    attn = _attention(t_all, gates_t, kc2, vct, kaug, vst, kw_pad, vwt, ds, dw, bc, B, T)

    sp = jax.nn.softplus(-rg_lambda[0].astype(F32))
    rnn = _rglru(rx.reshape(B, T, D_RNN), ry.reshape(B, T, D_RNN), conv_w[0], row1(conv_b[0]),
                 _block_diag(rg_w_a[0]).astype(BF16), row1(rg_b_a[0]),
                 _block_diag(rg_w_x[0]).astype(BF16), row1(rg_b_x[0]), row1(sp), B, T)

    wr = jnp.pad(jnp.concatenate([router_group_w[0], router_expert_w[0]], axis=1), ((0, 0), (0, 108)))
    wrh = wr.astype(BF16)
    wrl = (wr - wrh.astype(F32)).astype(BF16)
    br = jnp.pad(jnp.concatenate([router_group_b[0], router_expert_b[0]]), (0, 108)).reshape(1, 128)
    x1, comb = _outproj(attn.reshape(N, 512), rnn.reshape(N, D_RNN), x2, row1(attn_out_gain[0]),
                        row1(rnn_out_gain[0]), w_out[0].astype(BF16), row1(ln1_g[0]), row1(ln1_b[0]),
                        wrh, wrl, br, N)

    out = _moe(x1, comb, p[0].reshape(N, D_PLE), expert_w_up[0].astype(BF16),
               expert_w_down[0].astype(BF16), ple_gate_w[0].astype(BF16), ple_w[0].astype(BF16),
               row1(ln2_g[0]), row1(ln2_b[0]), N)
    return out.reshape(B, T, D_MODEL)
```

```python
import functools
import math

import jax
import jax.numpy as jnp
from jax import lax
from jax.experimental import pallas as pl
from jax.experimental.pallas import tpu as pltpu

F32 = jnp.float32
BF16 = jnp.bfloat16
NEG = -1e30
LOG2E = 1.4426950408889634

D_MODEL = 1024
HEAD_DIM = 64
N_HEADS = 8
N_GROUPS_KV = 2
GQA_REP = 4
D_ATTN = 512
D_RNN = 512
D_KV = 128
CMP_LEN = 32
CMP_STRIDE = 16
SEL_BLOCK = 64
N_SEL = 16
WINDOW = 512
N_BUCKETS = 32
MAX_DISTANCE = 128
N_EXPERTS = 16
EXPERTS_PER_GROUP = 4
N_EXP_GROUPS = 4
D_FF = 512
D_PLE = 256
ALPHA = 2.0 ** 0.25
LN_EPS = 1e-5
RMS_EPS = 1e-6
FORCE_BONUS = 1e4
RG_C = 8.0

TQ = 128
TILES = 2
N_SLC = 128
C_PAD = 640
C_OFF = 16
FAR = 512
TAIL = WINDOW + TQ
BAND = 24
GATE_ROWS = 32
V_ROWS = 80
MOE_TILE = 1024
MOE_SUB = 512
MOE_SLOTS = 128
VMEM_LIMIT = 56 * 1024 * 1024


def _cparams(sem, vmem=None):
    return pltpu.CompilerParams(dimension_semantics=sem, vmem_limit_bytes=vmem)


def _inproj_kernel(x_ref, wm_ref, wt_ref, wg_ref, eu_ref, ed_ref, ks_ref, kw_ref, kc_ref, vc_ref, rx_ref,
                   ry_ref, t_ref, gt_ref, eub_ref, edb_ref):
    eub_ref[...] = eu_ref[...].astype(BF16)
    edb_ref[...] = ed_ref[...].astype(BF16)
    xb = x_ref[...].astype(BF16)

    def mm(lo, hi):
        return jnp.dot(xb, wm_ref[:, lo:hi], preferred_element_type=F32)

    ks_ref[...] = mm(0, 128).astype(BF16)
    kw_ref[...] = mm(128, 256).astype(BF16)
    kc_ref[...] = mm(256, 384)
    vc_ref[...] = mm(384, 512)
    rx_ref[...] = mm(512, 1024)
    ry_ref[...] = mm(1024, 1536)
    dn = (((1,), (1,)), ((), ()))
    t_ref[...] = lax.dot_general(wt_ref[...], xb, dn, preferred_element_type=F32).astype(BF16)
    gt_ref[...] = jax.nn.sigmoid(lax.dot_general(wg_ref[...], xb, dn, preferred_element_type=F32))


def _inproj(x2, w_main, w_t, w_g, w_up, w_dn, B, T):
    N = B * T
    tm = 512
    nt = T // tm
    steps = N // tm
    up_rows, dn_rows = w_up.shape[0] // steps, w_dn.shape[0] // steps
    assert up_rows * steps == w_up.shape[0] and dn_rows * steps == w_dn.shape[0] and dn_rows % 16 == 0
    row = lambda w: pl.BlockSpec((tm, w), lambda i: (i, 0))
    full = lambda a: pl.BlockSpec(a.shape, lambda i: (0,) * a.ndim)
    tr = lambda r: pl.BlockSpec((None, r, tm), lambda i: (i // nt, 0, i % nt))
    return pl.pallas_call(
        _inproj_kernel,
        grid=(N // tm,),
        in_specs=[row(D_MODEL), full(w_main), full(w_t), full(w_g),
                  pl.BlockSpec((up_rows, w_up.shape[1]), lambda i: (i, 0)),
                  pl.BlockSpec((dn_rows, w_dn.shape[1]), lambda i: (i, 0))],
        out_specs=[row(128), row(128), row(128), row(128), row(512), row(512), tr(768), tr(GATE_ROWS),
                   pl.BlockSpec((up_rows, w_up.shape[1]), lambda i: (i, 0)),
                   pl.BlockSpec((dn_rows, w_dn.shape[1]), lambda i: (i, 0))],
        out_shape=[jax.ShapeDtypeStruct((N, 128), BF16),
                   jax.ShapeDtypeStruct((N, 128), BF16),
                   jax.ShapeDtypeStruct((N, 128), F32),
                   jax.ShapeDtypeStruct((N, 128), F32),
                   jax.ShapeDtypeStruct((N, 512), F32),
                   jax.ShapeDtypeStruct((N, 512), F32),
                   jax.ShapeDtypeStruct((B, 768, T), BF16),
                   jax.ShapeDtypeStruct((B, GATE_ROWS, T), F32),
                   jax.ShapeDtypeStruct(w_up.shape, BF16),
                   jax.ShapeDtypeStruct(w_dn.shape, BF16)],
        compiler_params=_cparams(("parallel",), VMEM_LIMIT),
        name="inproj",
    )(x2, w_main, w_t, w_g, w_up, w_dn)


def _compress_kernel(kr_ref, vr_ref, pek_ref, pev_ref, wk_ref, wv_ref, oh_ref, kc_ref, vct_ref, *,
                     n_cmp, n_chunk):
    def compress(raw_ref, pe_ref, w_ref):
        lo = jnp.zeros((n_chunk, 2 * HEAD_DIM), F32)
        hi = jnp.zeros((n_chunk, 2 * HEAD_DIM), F32)
        for j in range(CMP_STRIDE):
            a = raw_ref[pl.ds(j, n_chunk, stride=CMP_STRIDE), :]
            lo = lo + jnp.dot((a + pe_ref[j:j + 1, :]).astype(BF16), w_ref[j],
                              preferred_element_type=F32)
            hi = hi + jnp.dot((a + pe_ref[CMP_STRIDE + j:CMP_STRIDE + j + 1, :]).astype(BF16),
                              w_ref[CMP_STRIDE + j], preferred_element_type=F32)
        hi = pltpu.roll(hi, n_chunk - 1, axis=0)
        rid = lax.broadcasted_iota(jnp.int32, (n_chunk, 2 * HEAD_DIM), 0)
        out = jnp.where(rid < n_cmp, lo + hi, 0.0)
        return jnp.concatenate([jnp.zeros((C_OFF, 2 * HEAD_DIM), F32), out,
                                jnp.zeros((C_PAD - C_OFF - n_chunk, 2 * HEAD_DIM), F32)], axis=0)

    kc_ref[:, 0:2 * HEAD_DIM] = compress(kr_ref, pek_ref, wk_ref).astype(BF16)
    kc_ref[:, 2 * HEAD_DIM:] = oh_ref[...]
    vct_ref[...] = compress(vr_ref, pev_ref, wv_ref).T.astype(BF16)


def _compress(kc_raw, vc_raw, pek, pev, wk, wv, grp_onehot, B, T, n_cmp):
    n_chunk = T // CMP_STRIDE
    blk = pl.BlockSpec((None, T, 2 * HEAD_DIM), lambda b: (b, 0, 0))
    full = lambda a: pl.BlockSpec(a.shape, lambda b: (0,) * a.ndim)
    return pl.pallas_call(
        functools.partial(_compress_kernel, n_cmp=n_cmp, n_chunk=n_chunk),
        grid=(B,),
        in_specs=[blk, blk, full(pek), full(pev), full(wk), full(wv), full(grp_onehot)],
        out_specs=[pl.BlockSpec((None, C_PAD, 256), lambda b: (b, 0, 0)),
                   pl.BlockSpec((None, 2 * HEAD_DIM, C_PAD), lambda b: (b, 0, 0))],
        out_shape=[jax.ShapeDtypeStruct((B, C_PAD, 256), BF16),
                   jax.ShapeDtypeStruct((B, 2 * HEAD_DIM, C_PAD), BF16)],
        compiler_params=_cparams(("parallel",), VMEM_LIMIT),
        name="compress",
    )(kc_raw, vc_raw, pek, pev, wk, wv, grp_onehot)


def _bias_kernel(tab_ref, dt_ref, bt_ref, dc_ref, bc_ref, os_ref, ow_ref, oc_ref):
    h = pl.program_id(0) * GQA_REP + pl.program_id(1)

    def lookup(bk):
        out = jnp.zeros(bk.shape, F32)
        for b in range(N_BUCKETS):
            out = jnp.where(bk == b, tab_ref[h, b], out)
        return (out - tab_ref[h, N_BUCKETS - 1]) * LOG2E

    dt = dt_ref[...]
    v = lookup(bt_ref[...])
    os_ref[...] = jnp.where(dt >= 0, v, NEG)
    ow_ref[...] = jnp.where((dt >= 0) & (dt < WINDOW), v, NEG)
    oc_ref[...] = jnp.where(dc_ref[...] >= 0, lookup(bc_ref[...]), NEG)


def _t5_bucket(dist):
    max_exact = N_BUCKETS // 2
    d = jnp.maximum(dist, 0)
    df = jnp.maximum(d, 1).astype(F32)
    large = max_exact + (jnp.log(df / max_exact) / math.log(MAX_DISTANCE / max_exact)
                         * (N_BUCKETS - max_exact)).astype(jnp.int32)
    large = jnp.minimum(large, N_BUCKETS - 1)
    return jnp.where(d < max_exact, d, large)


def _bias_tiles(rel_bias):
    tq = jnp.arange(TQ, dtype=jnp.int32)
    dt = tq[None, :] + WINDOW - jnp.arange(TAIL, dtype=jnp.int32)[:, None]
    cc = jnp.arange(BAND, dtype=jnp.int32) - C_OFF
    dc = tq[None, :] - (cc[:, None] * CMP_STRIDE + CMP_LEN - 1)
    full = lambda a: pl.BlockSpec(a.shape, lambda g, r: (0,) * a.ndim)
    out = lambda rows: pl.BlockSpec((None, rows, TQ), lambda g, r: (g, 0, r))
    return pl.pallas_call(
        _bias_kernel,
        grid=(N_GROUPS_KV, GQA_REP),
        in_specs=[pl.BlockSpec(memory_space=pltpu.SMEM), full(dt), full(dt), full(dc), full(dc)],
        out_specs=[out(TAIL), out(TAIL), out(BAND)],
        out_shape=[jax.ShapeDtypeStruct((N_GROUPS_KV, TAIL, GQA_REP * TQ), F32),
                   jax.ShapeDtypeStruct((N_GROUPS_KV, TAIL, GQA_REP * TQ), F32),
                   jax.ShapeDtypeStruct((N_GROUPS_KV, BAND, GQA_REP * TQ), F32)],
        compiler_params=_cparams(("arbitrary", "arbitrary")),
        name="biastile",
    )(rel_bias.T, dt, _t5_bucket(dt), dc, _t5_bucket(dc))


def _attn_kernel(qt_ref, gt_ref, kc_ref, vct_ref, kaug_ref, vst_ref, kw_ref, vwt_ref, ds_ref, dw_ref,
                 bc_ref, o_ref, *scratch, n_far_max):
    n_hg = TILES * N_GROUPS_KV
    sc_ref, sw_ref = (scratch[n_hg * k:n_hg * (k + 1)] for k in range(2))
    qa_ref, m_ref, acc_ref, sb_ref, sm_ref = scratch[2 * n_hg:]
    cols4 = GQA_REP * TQ
    groups = range(N_GROUPS_KV)
    zeros_q = jnp.zeros((HEAD_DIM, cols4), BF16)
    grp = lax.broadcasted_iota(jnp.int32, (128, cols4), 0)
    neg_row = jnp.where(grp == 0, NEG, 0.0).astype(BF16)
    tq_col = lax.broadcasted_iota(jnp.int32, (1, cols4), 1) % TQ
    sidx = lax.broadcasted_iota(jnp.int32, (N_SLC, TQ), 0)
    sidx_f = sidx.astype(F32)
    tq_lane = lax.broadcasted_iota(jnp.int32, (N_SLC, TQ), 1)

    def chunk_start(c):
        return pl.multiple_of(WINDOW + jnp.minimum(c, n_far_max - 1) * FAR, FAR)

    def scores(h, c, par):
        kch = kaug_ref[pl.ds(chunk_start(c), FAR), :]
        for g in groups:
            sf = jnp.dot(kch, qa_ref[2 * h + g], preferred_element_type=F32)
            sb_ref[4 * h + 2 * g + par] = sf
            sm_ref[4 * h + 2 * g + par] = jnp.max(sf, axis=0, keepdims=True)

    def consume(h, nfar, c, par):
        c0 = chunk_start(c)
        live = c < nfar
        for g in groups:
            hg = 2 * h + g
            sf = sb_ref[4 * h + 2 * g + par]
            mprev = m_ref[hg]
            mcand = jnp.maximum(mprev, sm_ref[4 * h + 2 * g + par])
            mnew = jnp.where(live, mcand, mprev)
            pf = jnp.exp2(sf - jnp.where(live, mcand, -NEG)).astype(BF16)
            acc_ref[hg] = jnp.exp2(mprev - mnew) * acc_ref[hg] + jnp.dot(
                vst_ref[g, :, pl.ds(c0, FAR)], pf, preferred_element_type=F32)
            m_ref[hg] = mnew

    tiles = range(TILES)
    chains = [(h, g) for h in tiles for g in groups]
    t_i = [pl.program_id(1) * TILES + h for h in tiles]
    t_qs = [pl.multiple_of(i * TQ, TQ) for i in t_i]
    t_nfar = [jnp.maximum(i - 1, 0) // 4 for i in t_i]
    qpad = {}
    for h, g in chains:
        qgt = jnp.concatenate(
            [qt_ref[(GQA_REP * g + r) * HEAD_DIM:(GQA_REP * g + r + 1) * HEAD_DIM, h * TQ:(h + 1) * TQ]
             for r in range(GQA_REP)], axis=1)
        qpad[h, g] = jnp.concatenate([qgt, zeros_q] if g == 0 else [zeros_q, qgt], axis=0)

    for h, g in chains:
        i = t_i[h]
        gmask = jnp.where((grp < C_OFF // 8) | (grp >= i + BAND // 8), NEG, 0.0).astype(BF16)
        band0 = pl.multiple_of(i * 8, 8)
        sc_ref[2 * h + g][...] = jnp.dot(kc_ref[...], jnp.concatenate([qpad[h, g], gmask], axis=0),
                                         preferred_element_type=F32)
        sc_ref[2 * h + g][pl.ds(band0, BAND), :] = sc_ref[2 * h + g][pl.ds(band0, BAND), :] + bc_ref[g]
    w_max = {}
    for h, g in chains:
        sw = jnp.dot(kw_ref[pl.ds(t_qs[h], TAIL), :], jnp.concatenate([qpad[h, g], neg_row], axis=0),
                     preferred_element_type=F32) + dw_ref[g]
        sw_ref[2 * h + g][...] = sw
        w_max[h, g] = jnp.max(sw, axis=0, keepdims=True)
    o_c, o_w, imp = {}, {}, {}
    for h, g in chains:
        s = sc_ref[2 * h + g][...]
        e = jnp.exp2(s - jnp.max(s, axis=0, keepdims=True))
        has_cmp = t_qs[h] + tq_col >= CMP_LEN - 1
        pinv = jnp.where(has_cmp, 1.0 / jnp.sum(e, axis=0, keepdims=True), 0.0)
        res = jnp.dot(vct_ref[...], e.astype(BF16), preferred_element_type=F32)
        o_c[h, g] = res[0:2 * HEAD_DIM, :] * pinv
        imp[h, g] = sum(res[2 * HEAD_DIM:, r * TQ:(r + 1) * TQ] * pinv[:, r * TQ:(r + 1) * TQ]
                        for r in range(GQA_REP))
    for h, g in chains:
        pw = jnp.exp2(sw_ref[2 * h + g][...] - w_max[h, g]).astype(BF16)
        acc_w = jnp.dot(vwt_ref[g, :, pl.ds(t_qs[h], TAIL)], pw, preferred_element_type=F32)
        o_w[h, g] = acc_w[0:HEAD_DIM, :] * (1.0 / acc_w[HEAD_DIM:HEAD_DIM + 1, :])

    score, mbt = {}, {}
    for h, g in chains:
        cur = (t_qs[h] + tq_lane) // SEL_BLOCK
        forced = (sidx == 0) | (sidx == cur) | (sidx == cur - 1)
        score[h, g] = jnp.where(forced, -jnp.inf, jnp.where(sidx <= cur, imp[h, g], NEG))
        mbt[h, g] = jnp.where(forced, 0.0, NEG)
    for _ in range(N_SEL - 3):
        for ch in chains:
            cm = jnp.max(score[ch], axis=0, keepdims=True)
            first = jnp.min(jnp.where(score[ch] == cm, sidx_f, float(N_SLC)), axis=0, keepdims=True)
            pick = sidx_f == first
            mbt[ch] = jnp.where(pick, 0.0, mbt[ch])
            score[ch] = jnp.where(pick, -jnp.inf, score[ch])

    for h, g in chains:
        qa_ref[2 * h + g] = jnp.concatenate(
            [jnp.concatenate([mbt[h, g].astype(BF16)] * GQA_REP, axis=1), qpad[h, g]], axis=0)
        mbt_tail = jnp.where(sidx < t_nfar[h] * (FAR // SEL_BLOCK), NEG, mbt[h, g]).astype(BF16)
        qa_tail = jnp.concatenate([jnp.concatenate([mbt_tail] * GQA_REP, axis=1), qpad[h, g]], axis=0)
        st = jnp.dot(kaug_ref[pl.ds(t_qs[h], TAIL), :], qa_tail, preferred_element_type=F32) + ds_ref[g]
        sc_ref[2 * h + g][...] = st
        m_ref[2 * h + g] = jnp.max(st, axis=0, keepdims=True)
    for h in tiles:
        scores(h, 0, 0)
    for h, g in chains:
        pt = jnp.exp2(sc_ref[2 * h + g][...] - m_ref[2 * h + g]).astype(BF16)
        acc_ref[2 * h + g] = jnp.dot(vst_ref[g, :, pl.ds(t_qs[h], TAIL)], pt, preferred_element_type=F32)

    for h in tiles:
        nfar = t_nfar[h]

        def far_pair(kk, carry, h=h, nfar=nfar):
            c = 2 * kk
            scores(h, c + 1, 1)
            consume(h, nfar, c, 0)
            scores(h, c + 2, 0)
            consume(h, nfar, c + 1, 1)
            return carry

        lax.fori_loop(0, (nfar + 1) // 2, far_pair, 0)

    for h in tiles:
        lanes = slice(h * TQ, (h + 1) * TQ)
        out_rows = []
        for g in groups:
            dsl = slice(g * HEAD_DIM, (g + 1) * HEAD_DIM)
            hg = 2 * h + g
            o_s = acc_ref[hg, 0:HEAD_DIM, :] * (1.0 / acc_ref[hg, HEAD_DIM:HEAD_DIM + 1, :])
            for r in range(GQA_REP):
                cs = slice(r * TQ, (r + 1) * TQ)
                gc = 3 * (GQA_REP * g + r)
                out_rows.append(gt_ref[gc:gc + 1, lanes] * o_c[h, g][dsl, cs]
                                + gt_ref[gc + 1:gc + 2, lanes] * o_s[:, cs]
                                + gt_ref[gc + 2:gc + 3, lanes] * o_w[h, g][:, cs])
        o_ref[h * TQ:(h + 1) * TQ, :] = jnp.concatenate(out_rows, axis=0).T


def _attention(t_all, gates_t, kc2, vct, kaug, vst, kw_pad, vwt, ds, dw, bc, B, T):
    tp = T + WINDOW
    tqs = TQ * TILES
    per_b = lambda shape: pl.BlockSpec((None,) + shape, lambda b, i: (b,) + (0,) * len(shape),
                                      pipeline_mode=pl.Buffered(1))
    const = lambda a: pl.BlockSpec(a.shape, lambda b, i: (0,) * a.ndim, pipeline_mode=pl.Buffered(1))
    cols4 = GQA_REP * TQ
    n_hg = TILES * N_GROUPS_KV
    return pl.pallas_call(
        functools.partial(_attn_kernel, n_far_max=T // FAR),
        grid=(B, T // tqs),
        in_specs=[pl.BlockSpec((None, 512, tqs), lambda b, i: (b, 0, i)),
                  pl.BlockSpec((None, GATE_ROWS, tqs), lambda b, i: (b, 0, i)),
                  per_b((C_PAD, 256)),
                  per_b((256, C_PAD)),
                  per_b((tp, 256)),
                  per_b((N_GROUPS_KV, V_ROWS, tp)),
                  per_b((tp, 256)),
                  per_b((N_GROUPS_KV, V_ROWS, tp)),
                  const(ds), const(dw), const(bc)],
        out_specs=pl.BlockSpec((None, tqs, 512), lambda b, i: (b, i, 0)),
        out_shape=jax.ShapeDtypeStruct((B, T, 512), F32),
        scratch_shapes=([pltpu.VMEM((C_PAD, cols4), F32)] * n_hg
                        + [pltpu.VMEM((TAIL, cols4), F32)] * n_hg
                        + [pltpu.VMEM((n_hg, 256, cols4), BF16),
                           pltpu.VMEM((n_hg, 1, cols4), F32),
                           pltpu.VMEM((n_hg, V_ROWS, cols4), F32),
                           pltpu.VMEM((2 * n_hg, FAR, cols4), F32),
                           pltpu.VMEM((2 * n_hg, 1, cols4), F32)]),
        compiler_params=_cparams(("parallel", "arbitrary"), VMEM_LIMIT),
        name="attn",
    )(t_all, gates_t, kc2, vct, kaug, vst, kw_pad, vwt, ds, dw, bc)


def _rglru_kernel(*refs, L):
    J = L // 8
    n_col = D_RNN // 128
    rx_refs, ry_refs = refs[:n_col], refs[n_col:2 * n_col]
    cw_ref, cb_ref, wa_ref, ba_ref, wx_ref, bx_ref, sp_ref, o_ref, xprev_ref, h_ref, os_ref = refs[2 * n_col:]

    @pl.when(pl.program_id(1) == 0)
    def _():
        xprev_ref[...] = jnp.zeros(xprev_ref.shape, F32)
        h_ref[...] = jnp.zeros(h_ref.shape, F32)

    def strided(col_refs):
        return jnp.concatenate(
            [jnp.concatenate([ref[pl.ds(j, 8, stride=J), :] for j in range(J)], axis=0)
             for ref in col_refs], axis=1)

    xp = strided(rx_refs)
    yp = strided(ry_refs)
    sub = lax.broadcasted_iota(jnp.int32, (8, D_RNN), 0)

    def delayed(d):
        heads = []
        for j in range(d):
            src = pltpu.roll(xp[(J + j - d) * 8:(J + j - d + 1) * 8], 1, axis=0)
            heads.append(jnp.where(sub == 0, xprev_ref[8 + j - d:9 + j - d, :], src))
        return jnp.concatenate(heads + [xp[:L - 8 * d]], axis=0)

    xc = (cb_ref[...] + cw_ref[0:1, :] * delayed(3) + cw_ref[1:2, :] * delayed(2)
          + cw_ref[2:3, :] * delayed(1) + cw_ref[3:4, :] * xp)
    xprev_ref[...] = jnp.concatenate([ref[L - 8:L, :] for ref in rx_refs], axis=1)
    xcb = xc.astype(BF16)
    r = jax.nn.sigmoid(jnp.dot(xcb, wa_ref[...], preferred_element_type=F32) + ba_ref[...])
    ig = jax.nn.sigmoid(jnp.dot(xcb, wx_ref[...], preferred_element_type=F32) + bx_ref[...])
    a = jnp.exp(-RG_C * r * sp_ref[...])
    bt = jnp.sqrt(1.0 - a * a) * (ig * xc)

    h = jnp.zeros((8, D_RNN), F32)
    dec = jnp.ones((8, D_RNN), F32)
    hs, decs = [], []
    for j in range(J):
        aj = a[j * 8:(j + 1) * 8]
        h = aj * h + bt[j * 8:(j + 1) * 8]
        dec = aj * dec
        hs.append(h)
        decs.append(dec)
    carry = h_ref[0:1, :]
    carries = []
    for s in range(8):
        carries.append(carry)
        carry = dec[s:s + 1, :] * carry + h[s:s + 1, :]
    h_ref[...] = jnp.broadcast_to(carry, h_ref.shape)
    h_in = jnp.concatenate(carries, axis=0)
    cdf_c = math.sqrt(2.0 / math.pi)
    for j in range(J):
        y = yp[j * 8:(j + 1) * 8]
        cdf = 0.5 * (1.0 + jnp.tanh(cdf_c * (y + 0.044715 * (y * y * y))))
        res = (hs[j] + decs[j] * h_in) * (y * cdf)
        for c in range(n_col):
            os_ref[c, pl.ds(j, 8, stride=J), :] = res[:, c * 128:(c + 1) * 128]
    for c in range(n_col):
        o_ref[:, c * 128:(c + 1) * 128] = os_ref[c]


def _rglru(rx, ry, conv_w, conv_b, wa_bd, b_a, wx_bd, b_x, sp, B, T):
    L = 512
    blk = pl.BlockSpec((None, L, D_RNN), lambda b, t: (b, t, 0))
    n_col = D_RNN // 128
    cols = [pl.BlockSpec((None, L, 128), functools.partial(lambda b, t, c: (b, t, c), c=c))
            for c in range(n_col)]
    full = lambda a: pl.BlockSpec(a.shape, lambda b, t: (0,) * a.ndim)
    return pl.pallas_call(
        functools.partial(_rglru_kernel, L=L),
        grid=(B, T // L),
        in_specs=cols + cols + [full(conv_w), full(conv_b), full(wa_bd), full(b_a), full(wx_bd),
                                full(b_x), full(sp)],
        out_specs=blk,
        out_shape=jax.ShapeDtypeStruct((B, T, D_RNN), F32),
        scratch_shapes=[pltpu.VMEM((8, D_RNN), F32), pltpu.VMEM((8, D_RNN), F32),
                        pltpu.VMEM((n_col, L, 128), F32)],
        compiler_params=_cparams(("parallel", "arbitrary"), VMEM_LIMIT),
        name="rglru",
    )(*([rx] * n_col + [ry] * n_col), conv_w, conv_b, wa_bd, b_a, wx_bd, b_x, sp)


def _layer_norm(y, g, b):
    mu = jnp.mean(y, axis=-1, keepdims=True)
    d = y - mu
    var = jnp.mean(d * d, axis=-1, keepdims=True)
    return d * lax.rsqrt(var + LN_EPS) * g + b


def _outproj_kernel(at_ref, rn_ref, x_ref, ga_ref, gr_ref, wo_ref, g1_ref, b1_ref, wrh_ref, wrl_ref,
                    br_ref, x1_ref, comb_ref):
    a = at_ref[...]
    rn = rn_ref[...]
    ha = a * lax.rsqrt(jnp.mean(a * a, axis=-1, keepdims=True) + RMS_EPS) * ga_ref[...]
    hr = rn * lax.rsqrt(jnp.mean(rn * rn, axis=-1, keepdims=True) + RMS_EPS) * gr_ref[...]
    heads = jnp.concatenate([ha, hr], axis=1).astype(BF16)
    mix = jnp.dot(heads, wo_ref[...], preferred_element_type=F32)
    x1 = _layer_norm(ALPHA * x_ref[...] + mix, g1_ref[...], b1_ref[...])
    x1_ref[...] = x1

    xh = x1.astype(BF16)
    xl = (x1 - xh.astype(F32)).astype(BF16)
    lg = (jnp.dot(xh, wrh_ref[...], preferred_element_type=F32)
          + jnp.dot(xl, wrh_ref[...], preferred_element_type=F32)
          + jnp.dot(xh, wrl_ref[...], preferred_element_type=F32)) + br_ref[...]
    lane = lax.broadcasted_iota(jnp.int32, lg.shape, 1)
    lane_f = lane.astype(F32)
    big = 1e9
    isg = lane < N_EXP_GROUPS
    gmax = jnp.max(jnp.where(isg, lg, -jnp.inf), axis=-1, keepdims=True)
    pg_top = 1.0 / jnp.sum(jnp.where(isg, jnp.exp(lg - gmax), 0.0), axis=-1, keepdims=True)
    gi = jnp.min(jnp.where(isg & (lg == gmax), lane_f, big), axis=-1, keepdims=True)
    egrp = ((lane - N_EXP_GROUPS) // EXPERTS_PER_GROUP).astype(F32)
    ise = (lane >= N_EXP_GROUPS) & (lane < N_EXP_GROUPS + N_EXPERTS) & (egrp == gi)
    emax = jnp.max(jnp.where(ise, lg, -jnp.inf), axis=-1, keepdims=True)
    i1 = jnp.min(jnp.where(ise & (lg == emax), lane_f, big), axis=-1, keepdims=True)
    rest = ise & (lane_f != i1)
    m2 = jnp.max(jnp.where(rest, lg, -jnp.inf), axis=-1, keepdims=True)
    i2 = jnp.min(jnp.where(rest & (lg == m2), lane_f, big), axis=-1, keepdims=True)
    e2 = jnp.exp(m2 - emax)
    inv = pg_top / (1.0 + e2)
    comb_ref[...] = jnp.where(lane_f == i1, inv, 0.0) + jnp.where(lane_f == i2, inv * e2, 0.0)


def _outproj(attn, rnn, x2, ga, gr, wo, g1, b1, wrh, wrl, br, N):
    tm = 512
    row = lambda w: pl.BlockSpec((tm, w), lambda i: (i, 0))
    full = lambda a: pl.BlockSpec(a.shape, lambda i: (0,) * a.ndim)
    return pl.pallas_call(
        _outproj_kernel,
        grid=(N // tm,),
        in_specs=[row(512), row(512), row(D_MODEL), full(ga), full(gr), full(wo), full(g1), full(b1),
                  full(wrh), full(wrl), full(br)],
        out_specs=[row(D_MODEL), row(128)],
        out_shape=[jax.ShapeDtypeStruct((N, D_MODEL), F32), jax.ShapeDtypeStruct((N, 128), F32)],
        compiler_params=_cparams(("parallel",), VMEM_LIMIT),
        name="outproj",
    )(attn, rnn, x2, ga, gr, wo, g1, b1, wrh, wrl, br)


def _moe_kernel(x1_ref, comb_ref, p_ref, tri_ref, wup_ref, wdn_ref, wg_ref, wp_ref, g2_ref, b2_ref, o_ref,
                acc_ref, xb_ref, rank_ref, rankt_ref, combt_ref):
    pi = pl.program_id(1)
    n_sub = x1_ref.shape[0] // MOE_SUB
    sub_rows = [slice(h * MOE_SUB, (h + 1) * MOE_SUB) for h in range(n_sub)]
    lane_col = N_EXP_GROUPS

    @pl.when(pi == 0)
    def _():
        x1 = x1_ref[...]
        xb = x1.astype(BF16)
        xb_ref[...] = xb
        gate = jax.nn.sigmoid(jnp.dot(xb, wg_ref[...], preferred_element_type=F32))
        ple = gate * jnp.dot(p_ref[...].astype(BF16), wp_ref[...], preferred_element_type=F32)
        acc_ref[...] = ALPHA * x1 + ple
        for rows in sub_rows:
            comb = comb_ref[rows, :]
            chosen = comb > 0.0
            rank = jnp.dot(tri_ref[...], jnp.where(chosen, 1.0, 0.0).astype(BF16),
                           preferred_element_type=F32)
            rank = jnp.where(chosen, rank, -1.0)
            rank_ref[rows, :] = rank
            rankt_ref[:, rows] = rank.T
            combt_ref[:, rows] = comb.T

    lane = lax.broadcasted_iota(jnp.int32, (MOE_SUB, 128), 1)
    slot_r = lax.broadcasted_iota(jnp.int32, (MOE_SLOTS, MOE_SUB), 0).astype(F32)
    slot_c = lax.broadcasted_iota(jnp.int32, (MOE_SUB, MOE_SLOTS), 1).astype(F32)

    def route_pair(base):
        chains = [(rows, k) for rows in sub_rows for k in range(2)]
        cmask, xc, u, hsw, y, ys, ct = {}, {}, {}, {}, {}, {}, {}
        for ch in chains:
            rows, k = ch
            rk_row = rankt_ref[pl.ds(2 * pi + k + lane_col, 1), rows] - base
            cmask[ch] = rk_row == slot_r
            xc[ch] = jnp.dot(jnp.where(cmask[ch], 1.0, 0.0).astype(BF16), xb_ref[rows, :],
                             preferred_element_type=F32).astype(BF16)
        for ch in chains:
            u[ch] = jnp.dot(xc[ch], wup_ref[ch[1]], preferred_element_type=F32)
        for ch in chains:
            ua = u[ch][:, :D_FF]
            hsw[ch] = (ua * jax.nn.sigmoid(ua) * u[ch][:, D_FF:]).astype(BF16)
        for ch in chains:
            y[ch] = jnp.dot(hsw[ch], wdn_ref[ch[1]], preferred_element_type=F32)
        for ch in chains:
            rows, k = ch
            e_lane = 2 * pi + k + lane_col
            w_row = combt_ref[pl.ds(e_lane, 1), rows]
            w_slot = jnp.sum(jnp.where(cmask[ch], w_row, 0.0), axis=1, keepdims=True)
            ys[ch] = (w_slot * y[ch]).astype(BF16)
            rk_col = jnp.sum(jnp.where(lane == e_lane, rank_ref[rows, :], 0.0), axis=1,
                             keepdims=True) - base
            ct[ch] = jnp.where(rk_col == slot_c, 1.0, 0.0).astype(BF16)
        for rows in sub_rows:
            pair = [(rows, k) for k in range(2)]
            ctp = jnp.concatenate([ct[ch] for ch in pair], axis=1)
            acc_ref[rows, :] = acc_ref[rows, :] + jnp.dot(
                ctp, jnp.concatenate([ys[ch] for ch in pair], axis=0), preferred_element_type=F32)

    route_pair(0.0)

    top_rank = jnp.max(jnp.maximum(rankt_ref[pl.ds(2 * pi + lane_col, 1), :],
                                   rankt_ref[pl.ds(2 * pi + 1 + lane_col, 1), :]))

    @pl.when(top_rank >= MOE_SLOTS)
    def _():
        def extra(sb, carry):
            route_pair((sb * MOE_SLOTS).astype(F32))
            return carry

        lax.fori_loop(1, top_rank.astype(jnp.int32) // MOE_SLOTS + 1, extra, 0)

    @pl.when(pi == N_EXPERTS // 2 - 1)
    def _():
        o_ref[...] = _layer_norm(acc_ref[...], g2_ref[...], b2_ref[...])


def _moe(x1, comb, p2, wup, wdn, wg, wp, g2, b2, N):
    tm = MOE_TILE
    tri = (jnp.arange(MOE_SUB)[None, :] < jnp.arange(MOE_SUB)[:, None]).astype(BF16)
    row = lambda w: pl.BlockSpec((tm, w), lambda i, e: (i, 0))
    full = lambda a: pl.BlockSpec(a.shape, lambda i, e: (0,) * a.ndim)
    return pl.pallas_call(
        _moe_kernel,
        grid=(N // tm, N_EXPERTS // 2),
        in_specs=[row(D_MODEL), row(128), row(D_PLE), full(tri),
                  pl.BlockSpec((2, D_MODEL, 2 * D_FF), lambda i, e: (e, 0, 0)),
                  pl.BlockSpec((2, D_FF, D_MODEL), lambda i, e: (e, 0, 0)),
                  full(wg), full(wp), full(g2), full(b2)],
        out_specs=row(D_MODEL),
        out_shape=jax.ShapeDtypeStruct((N, D_MODEL), F32),
        scratch_shapes=[pltpu.VMEM((tm, D_MODEL), F32),
                        pltpu.VMEM((tm, D_MODEL), BF16),
                        pltpu.VMEM((tm, 128), F32),
                        pltpu.VMEM((128, tm), F32),
                        pltpu.VMEM((128, tm), F32)],
        compiler_params=_cparams(("parallel", "arbitrary"), VMEM_LIMIT),
        name="moe",
    )(x1, comb, p2, tri, wup, wdn, wg, wp, g2, b2)


def _block_diag(w):
    n, k, j = w.shape
    eye = jnp.eye(n, dtype=w.dtype)
    return (w[:, :, None, :] * eye[:, None, :, None]).reshape(n * k, n * j)


def kernel(x, p, rel_bias, w_in, cmp_pe_k, cmp_pe_v, cmp_w_k, cmp_w_v, conv_w, conv_b, rg_w_a, rg_b_a,
           rg_w_x, rg_b_x, rg_lambda, attn_out_gain, rnn_out_gain, w_out, ln1_g, ln1_b, router_group_w,
           router_group_b, router_expert_w, router_expert_b, expert_w_up, expert_w_down, ple_w,
           ple_gate_w, ln2_g, ln2_b):
    B, T, _ = x.shape
    N = B * T
    assert T % FAR == 0 and T % (TQ * TILES) == 0 and T // SEL_BLOCK <= N_SLC and w_in.shape[0] == 1
    n_cmp = (T - CMP_LEN) // CMP_STRIDE + 1
    n_chunk = T // CMP_STRIDE
    assert C_OFF + n_chunk <= C_PAD
    row1 = lambda v: v.reshape(1, -1)

    w = w_in[0]
    c0 = D_ATTN
    kv = lambda k: w[:, c0 + k * D_KV:c0 + (k + 1) * D_KV]
    g0 = c0 + 6 * D_KV
    wq = w[:, :D_ATTN] * (HEAD_DIM ** -0.5 * LOG2E)
    w_main = jnp.concatenate([kv(2), kv(4), kv(0), kv(1), w[:, g0 + 24:]], axis=1).astype(BF16)
    w_t = jnp.concatenate([wq, kv(3), kv(5)], axis=1).T.astype(BF16)
    w_g = jnp.pad(w[:, g0:g0 + 24], ((0, 0), (0, GATE_ROWS - 24))).T.astype(BF16)

    x2 = x.reshape(N, D_MODEL)
    ks, kw, kc_raw, vc_raw, rx, ry, t_all, gates_t, wup_b, wdn_b = _inproj(
        x2, w_main, w_t, w_g, expert_w_up[0].reshape(N_EXPERTS * D_MODEL, 2 * D_FF),
        expert_w_down[0].reshape(N_EXPERTS * D_FF, D_MODEL), B, T)

    def per_pos(wc):
        wl = wc.reshape(CMP_LEN, HEAD_DIM, HEAD_DIM)
        z = jnp.zeros_like(wl)
        return jnp.concatenate([jnp.concatenate([wl, z], axis=2), jnp.concatenate([z, wl], axis=2)],
                               axis=1).astype(BF16)

    pek = jnp.tile(cmp_pe_k[0], (1, N_GROUPS_KV))
    pev = jnp.tile(cmp_pe_v[0], (1, N_GROUPS_KV))
    grp_onehot = (jnp.arange(C_PAD)[:, None] // 8 == jnp.arange(128)[None, :]).astype(BF16)
    kc2, vct = _compress(kc_raw.reshape(B, T, 128), vc_raw.reshape(B, T, 128), pek, pev,
                         per_pos(cmp_w_k[0]), per_pos(cmp_w_v[0]), grp_onehot, B, T, n_cmp)

    onehot = (jnp.arange(T)[:, None] // SEL_BLOCK == jnp.arange(N_SLC)[None, :]).astype(BF16)
    kaug = jnp.concatenate([jnp.broadcast_to(onehot, (B, T, N_SLC)), ks.reshape(B, T, 128)], axis=2)
    lane256 = jnp.arange(256)
    pad_sel = jnp.broadcast_to((lane256 == N_SLC - 1).astype(BF16), (B, WINDOW, 256))
    pad_win = jnp.broadcast_to((lane256 == 128).astype(BF16), (B, WINDOW, 256))
    kaug = jnp.concatenate([pad_sel, kaug], axis=1)
    kw_pad = jnp.concatenate([pad_win, jnp.pad(kw.reshape(B, T, 128), ((0, 0), (0, 0), (0, 128)))], axis=1)
    ones_rows = jnp.concatenate([jnp.ones((B, N_GROUPS_KV, 1, T), BF16),
                                 jnp.zeros((B, N_GROUPS_KV, V_ROWS - HEAD_DIM - 1, T), BF16)], axis=2)

    def v_rows(vt):
        vt = jnp.concatenate([vt.reshape(B, N_GROUPS_KV, HEAD_DIM, T), ones_rows], axis=2)
        return jnp.pad(vt, ((0, 0), (0, 0), (0, 0), (WINDOW, 0)))

    vst = v_rows(t_all[:, 512:640])
    vwt = v_rows(t_all[:, 640:768])
    ds, dw, bc = _bias_tiles(rel_bias)

    cidx = jnp.arange(C_PAD) - C_OFF
    s_lo = jnp.arange(N_SLC)[:, None] * SEL_BLOCK
    overlap = ((cidx[None, :] * CMP_STRIDE < s_lo + SEL_BLOCK) & (cidx[None, :] * CMP_STRIDE + CMP_LEN > s_lo)
               & (cidx[None, :] >= 0) & (cidx[None, :] < n_cmp)).astype(BF16)
    vct = jnp.concatenate([vct, jnp.broadcast_to(overlap, (B, N_SLC, C_PAD))], axis=1)
    attn = _attention(t_all, gates_t, kc2, vct, kaug, vst, kw_pad, vwt, ds, dw, bc, B, T)

    sp = jax.nn.softplus(-rg_lambda[0].astype(F32))
    rnn = _rglru(rx.reshape(B, T, D_RNN), ry.reshape(B, T, D_RNN), conv_w[0], row1(conv_b[0]),
                 _block_diag(rg_w_a[0]).astype(BF16), row1(rg_b_a[0]),
                 _block_diag(rg_w_x[0]).astype(BF16), row1(rg_b_x[0]), row1(sp), B, T)

    wr = jnp.pad(jnp.concatenate([router_group_w[0], router_expert_w[0]], axis=1), ((0, 0), (0, 108)))
    wrh = wr.astype(BF16)
    wrl = (wr - wrh.astype(F32)).astype(BF16)
    br = jnp.pad(jnp.concatenate([router_group_b[0], router_expert_b[0]]), (0, 108)).reshape(1, 128)
    x1, comb = _outproj(attn.reshape(N, 512), rnn.reshape(N, D_RNN), x2, row1(attn_out_gain[0]),
                        row1(rnn_out_gain[0]), w_out[0].astype(BF16), row1(ln1_g[0]), row1(ln1_b[0]),
                        wrh, wrl, br, N)

    out = _moe(x1, comb, p[0].reshape(N, D_PLE), wup_b.reshape(N_EXPERTS, D_MODEL, 2 * D_FF),
               wdn_b.reshape(N_EXPERTS, D_FF, D_MODEL), ple_gate_w[0].astype(BF16), ple_w[0].astype(BF16),
               row1(ln2_g[0]), row1(ln2_b[0]), N)
    return out.reshape(B, T, D_MODEL)
```

```python
import functools
import math

import jax
import jax.numpy as jnp
from jax import lax
from jax.experimental import pallas as pl
from jax.experimental.pallas import tpu as pltpu

F32 = jnp.float32
BF16 = jnp.bfloat16
NEG = -1e30
LOG2E = 1.4426950408889634

D_MODEL = 1024
HEAD_DIM = 64
N_HEADS = 8
N_GROUPS_KV = 2
GQA_REP = 4
D_ATTN = 512
D_RNN = 512
D_KV = 128
CMP_LEN = 32
CMP_STRIDE = 16
SEL_BLOCK = 64
N_SEL = 16
WINDOW = 512
N_BUCKETS = 32
MAX_DISTANCE = 128
N_EXPERTS = 16
EXPERTS_PER_GROUP = 4
N_EXP_GROUPS = 4
D_FF = 512
D_PLE = 256
ALPHA = 2.0 ** 0.25
LN_EPS = 1e-5
RMS_EPS = 1e-6
FORCE_BONUS = 1e4
RG_C = 8.0

TQ = 128
TILES = 2
N_SLC = 128
C_PAD = 640
C_OFF = 16
FAR = 512
TAIL = WINDOW + TQ
BAND = 24
GATE_ROWS = 32
V_ROWS = 80
MOE_TILE = 1024
MOE_SUB = 512
MOE_SLOTS = 128
VMEM_LIMIT = 56 * 1024 * 1024


def _cparams(sem, vmem=None):
    return pltpu.CompilerParams(dimension_semantics=sem, vmem_limit_bytes=vmem)


def _inproj_kernel(x_ref, wm_ref, wt_ref, wg_ref, eu_ref, ed_ref, ks_ref, kw_ref, kc_ref, vc_ref, rx_ref,
                   ry_ref, t_ref, gt_ref, eub_ref, edb_ref):
    eub_ref[...] = eu_ref[...].astype(BF16)
    edb_ref[...] = ed_ref[...].astype(BF16)
    xb = x_ref[...].astype(BF16)

    def mm(lo, hi):
        return jnp.dot(xb, wm_ref[:, lo:hi], preferred_element_type=F32)

    ks_ref[...] = mm(0, 128).astype(BF16)
    kw_ref[...] = mm(128, 256).astype(BF16)
    kc_ref[...] = mm(256, 384)
    vc_ref[...] = mm(384, 512)
    rx_ref[...] = mm(512, 1024)
    ry_ref[...] = mm(1024, 1536)
    dn = (((1,), (1,)), ((), ()))
    t_ref[...] = lax.dot_general(wt_ref[...], xb, dn, preferred_element_type=F32).astype(BF16)
    gt_ref[...] = jax.nn.sigmoid(lax.dot_general(wg_ref[...], xb, dn, preferred_element_type=F32))


def _inproj(x2, w_main, w_t, w_g, w_up, w_dn, B, T):
    N = B * T
    tm = 512
    nt = T // tm
    steps = N // tm
    up_rows, dn_rows = w_up.shape[0] // steps, w_dn.shape[0] // steps
    assert up_rows * steps == w_up.shape[0] and dn_rows * steps == w_dn.shape[0] and dn_rows % 16 == 0
    row = lambda w: pl.BlockSpec((tm, w), lambda i: (i, 0))
    full = lambda a: pl.BlockSpec(a.shape, lambda i: (0,) * a.ndim)
    tr = lambda r: pl.BlockSpec((None, r, tm), lambda i: (i // nt, 0, i % nt))
    return pl.pallas_call(
        _inproj_kernel,
        grid=(N // tm,),
        in_specs=[row(D_MODEL), full(w_main), full(w_t), full(w_g),
                  pl.BlockSpec((up_rows, w_up.shape[1]), lambda i: (i, 0)),
                  pl.BlockSpec((dn_rows, w_dn.shape[1]), lambda i: (i, 0))],
        out_specs=[row(128), row(128), row(128), row(128), row(512), row(512), tr(768), tr(GATE_ROWS),
                   pl.BlockSpec((up_rows, w_up.shape[1]), lambda i: (i, 0)),
                   pl.BlockSpec((dn_rows, w_dn.shape[1]), lambda i: (i, 0))],
        out_shape=[jax.ShapeDtypeStruct((N, 128), BF16),
                   jax.ShapeDtypeStruct((N, 128), BF16),
                   jax.ShapeDtypeStruct((N, 128), F32),
                   jax.ShapeDtypeStruct((N, 128), F32),
                   jax.ShapeDtypeStruct((N, 512), F32),
                   jax.ShapeDtypeStruct((N, 512), F32),
                   jax.ShapeDtypeStruct((B, 768, T), BF16),
                   jax.ShapeDtypeStruct((B, GATE_ROWS, T), F32),
                   jax.ShapeDtypeStruct(w_up.shape, BF16),
                   jax.ShapeDtypeStruct(w_dn.shape, BF16)],
        compiler_params=_cparams(("parallel",), VMEM_LIMIT),
        name="inproj",
    )(x2, w_main, w_t, w_g, w_up, w_dn)


def _compress_kernel(kr_ref, vr_ref, pek_ref, pev_ref, wk_ref, wv_ref, oh_ref, kc_ref, vct_ref, *,
                     n_cmp, n_chunk):
    def compress(raw_ref, pe_ref, w_ref):
        lo = jnp.zeros((n_chunk, 2 * HEAD_DIM), F32)
        hi = jnp.zeros((n_chunk, 2 * HEAD_DIM), F32)
        for j in range(CMP_STRIDE):
            a = raw_ref[pl.ds(j, n_chunk, stride=CMP_STRIDE), :]
            lo = lo + jnp.dot((a + pe_ref[j:j + 1, :]).astype(BF16), w_ref[j],
                              preferred_element_type=F32)
            hi = hi + jnp.dot((a + pe_ref[CMP_STRIDE + j:CMP_STRIDE + j + 1, :]).astype(BF16),
                              w_ref[CMP_STRIDE + j], preferred_element_type=F32)
        hi = pltpu.roll(hi, n_chunk - 1, axis=0)
        rid = lax.broadcasted_iota(jnp.int32, (n_chunk, 2 * HEAD_DIM), 0)
        out = jnp.where(rid < n_cmp, lo + hi, 0.0)
        return jnp.concatenate([jnp.zeros((C_OFF, 2 * HEAD_DIM), F32), out,
                                jnp.zeros((C_PAD - C_OFF - n_chunk, 2 * HEAD_DIM), F32)], axis=0)

    kc_ref[:, 0:2 * HEAD_DIM] = compress(kr_ref, pek_ref, wk_ref).astype(BF16)
    kc_ref[:, 2 * HEAD_DIM:] = oh_ref[...]
    vct_ref[...] = compress(vr_ref, pev_ref, wv_ref).T.astype(BF16)


def _compress(kc_raw, vc_raw, pek, pev, wk, wv, grp_onehot, B, T, n_cmp):
    n_chunk = T // CMP_STRIDE
    blk = pl.BlockSpec((None, T, 2 * HEAD_DIM), lambda b: (b, 0, 0))
    full = lambda a: pl.BlockSpec(a.shape, lambda b: (0,) * a.ndim)
    return pl.pallas_call(
        functools.partial(_compress_kernel, n_cmp=n_cmp, n_chunk=n_chunk),
        grid=(B,),
        in_specs=[blk, blk, full(pek), full(pev), full(wk), full(wv), full(grp_onehot)],
        out_specs=[pl.BlockSpec((None, C_PAD, 256), lambda b: (b, 0, 0)),
                   pl.BlockSpec((None, 2 * HEAD_DIM, C_PAD), lambda b: (b, 0, 0))],
        out_shape=[jax.ShapeDtypeStruct((B, C_PAD, 256), BF16),
                   jax.ShapeDtypeStruct((B, 2 * HEAD_DIM, C_PAD), BF16)],
        compiler_params=_cparams(("parallel",), VMEM_LIMIT),
        name="compress",
    )(kc_raw, vc_raw, pek, pev, wk, wv, grp_onehot)


def _bias_kernel(tab_ref, dt_ref, bt_ref, dc_ref, bc_ref, os_ref, ow_ref, oc_ref):
    h = pl.program_id(0) * GQA_REP + pl.program_id(1)

    def lookup(bk):
        out = jnp.zeros(bk.shape, F32)
        for b in range(N_BUCKETS):
            out = jnp.where(bk == b, tab_ref[h, b], out)
        return (out - tab_ref[h, N_BUCKETS - 1]) * LOG2E

    dt = dt_ref[...]
    v = lookup(bt_ref[...])
    os_ref[...] = jnp.where(dt >= 0, v, NEG)
    ow_ref[...] = jnp.where((dt >= 0) & (dt < WINDOW), v, NEG)
    oc_ref[...] = jnp.where(dc_ref[...] >= 0, lookup(bc_ref[...]), NEG)


def _t5_bucket(dist):
    max_exact = N_BUCKETS // 2
    d = jnp.maximum(dist, 0)
    df = jnp.maximum(d, 1).astype(F32)
    large = max_exact + (jnp.log(df / max_exact) / math.log(MAX_DISTANCE / max_exact)
                         * (N_BUCKETS - max_exact)).astype(jnp.int32)
    large = jnp.minimum(large, N_BUCKETS - 1)
    return jnp.where(d < max_exact, d, large)


def _bias_tiles(rel_bias):
    tq = jnp.arange(TQ, dtype=jnp.int32)
    dt = tq[None, :] + WINDOW - jnp.arange(TAIL, dtype=jnp.int32)[:, None]
    cc = jnp.arange(BAND, dtype=jnp.int32) - C_OFF
    dc = tq[None, :] - (cc[:, None] * CMP_STRIDE + CMP_LEN - 1)
    full = lambda a: pl.BlockSpec(a.shape, lambda g, r: (0,) * a.ndim)
    out = lambda rows: pl.BlockSpec((None, rows, TQ), lambda g, r: (g, 0, r))
    return pl.pallas_call(
        _bias_kernel,
        grid=(N_GROUPS_KV, GQA_REP),
        in_specs=[pl.BlockSpec(memory_space=pltpu.SMEM), full(dt), full(dt), full(dc), full(dc)],
        out_specs=[out(TAIL), out(TAIL), out(BAND)],
        out_shape=[jax.ShapeDtypeStruct((N_GROUPS_KV, TAIL, GQA_REP * TQ), F32),
                   jax.ShapeDtypeStruct((N_GROUPS_KV, TAIL, GQA_REP * TQ), F32),
                   jax.ShapeDtypeStruct((N_GROUPS_KV, BAND, GQA_REP * TQ), F32)],
        compiler_params=_cparams(("arbitrary", "arbitrary")),
        name="biastile",
    )(rel_bias.T, dt, _t5_bucket(dt), dc, _t5_bucket(dc))


def _attn_kernel(qt_ref, gt_ref, kc_ref, vct_ref, kaug_ref, vst_ref, kw_ref, vwt_ref, ds_ref, dw_ref,
                 bc_ref, o_ref, *scratch, n_far_max):
    n_hg = TILES * N_GROUPS_KV
    sc_ref, sw_ref = (scratch[n_hg * k:n_hg * (k + 1)] for k in range(2))
    qa_ref, m_ref, acc_ref, sb_ref, sm_ref = scratch[2 * n_hg:]
    cols4 = GQA_REP * TQ
    groups = range(N_GROUPS_KV)
    zeros_q = jnp.zeros((HEAD_DIM, cols4), BF16)
    grp = lax.broadcasted_iota(jnp.int32, (128, cols4), 0)
    neg_row = jnp.where(grp == 0, NEG, 0.0).astype(BF16)
    tq_col = lax.broadcasted_iota(jnp.int32, (1, cols4), 1) % TQ
    sidx = lax.broadcasted_iota(jnp.int32, (N_SLC, TQ), 0)
    sidx_f = sidx.astype(F32)
    tq_lane = lax.broadcasted_iota(jnp.int32, (N_SLC, TQ), 1)

    def chunk_start(c):
        return pl.multiple_of(WINDOW + jnp.minimum(c, n_far_max - 1) * FAR, FAR)

    def scores(h, c, par):
        kch = kaug_ref[pl.ds(chunk_start(c), FAR), :]
        for g in groups:
            sf = jnp.dot(kch, qa_ref[2 * h + g], preferred_element_type=F32)
            sb_ref[4 * h + 2 * g + par] = sf
            sm_ref[4 * h + 2 * g + par] = jnp.max(sf, axis=0, keepdims=True)

    def consume(h, nfar, c, par):
        c0 = chunk_start(c)
        live = c < nfar
        for g in groups:
            hg = 2 * h + g
            sf = sb_ref[4 * h + 2 * g + par]
            mprev = m_ref[hg]
            mcand = jnp.maximum(mprev, sm_ref[4 * h + 2 * g + par])
            mnew = jnp.where(live, mcand, mprev)
            pf = jnp.exp2(sf - jnp.where(live, mcand, -NEG)).astype(BF16)
            acc_ref[hg] = jnp.exp2(mprev - mnew) * acc_ref[hg] + jnp.dot(
                vst_ref[g, :, pl.ds(c0, FAR)], pf, preferred_element_type=F32)
            m_ref[hg] = mnew

    tiles = range(TILES)
    chains = [(h, g) for h in tiles for g in groups]
    t_i = [pl.program_id(1) * TILES + h for h in tiles]
    t_qs = [pl.multiple_of(i * TQ, TQ) for i in t_i]
    t_nfar = [jnp.maximum(i - 1, 0) // 4 for i in t_i]
    qpad = {}
    for h, g in chains:
        qgt = jnp.concatenate(
            [qt_ref[(GQA_REP * g + r) * HEAD_DIM:(GQA_REP * g + r + 1) * HEAD_DIM, h * TQ:(h + 1) * TQ]
             for r in range(GQA_REP)], axis=1)
        qpad[h, g] = jnp.concatenate([qgt, zeros_q] if g == 0 else [zeros_q, qgt], axis=0)

    for h, g in chains:
        i = t_i[h]
        gmask = jnp.where((grp < C_OFF // 8) | (grp >= i + BAND // 8), NEG, 0.0).astype(BF16)
        band0 = pl.multiple_of(i * 8, 8)
        sc_ref[2 * h + g][...] = jnp.dot(kc_ref[...], jnp.concatenate([qpad[h, g], gmask], axis=0),
                                         preferred_element_type=F32)
        sc_ref[2 * h + g][pl.ds(band0, BAND), :] = sc_ref[2 * h + g][pl.ds(band0, BAND), :] + bc_ref[g]
    w_max = {}
    for h, g in chains:
        sw = jnp.dot(kw_ref[pl.ds(t_qs[h], TAIL), :], jnp.concatenate([qpad[h, g], neg_row], axis=0),
                     preferred_element_type=F32) + dw_ref[g]
        sw_ref[2 * h + g][...] = sw
        w_max[h, g] = jnp.max(sw, axis=0, keepdims=True)
    o_c, o_w, imp = {}, {}, {}
    for h, g in chains:
        s = sc_ref[2 * h + g][...]
        e = jnp.exp2(s - jnp.max(s, axis=0, keepdims=True))
        has_cmp = t_qs[h] + tq_col >= CMP_LEN - 1
        pinv = jnp.where(has_cmp, 1.0 / jnp.sum(e, axis=0, keepdims=True), 0.0)
        res = jnp.dot(vct_ref[...], e.astype(BF16), preferred_element_type=F32)
        o_c[h, g] = res[0:2 * HEAD_DIM, :] * pinv
        imp[h, g] = sum(res[2 * HEAD_DIM:, r * TQ:(r + 1) * TQ] * pinv[:, r * TQ:(r + 1) * TQ]
                        for r in range(GQA_REP))
    for h, g in chains:
        pw = jnp.exp2(sw_ref[2 * h + g][...] - w_max[h, g]).astype(BF16)
        acc_w = jnp.dot(vwt_ref[g, :, pl.ds(t_qs[h], TAIL)], pw, preferred_element_type=F32)
        o_w[h, g] = acc_w[0:HEAD_DIM, :] * (1.0 / acc_w[HEAD_DIM:HEAD_DIM + 1, :])

    score, mbt = {}, {}
    for h, g in chains:
        cur = (t_qs[h] + tq_lane) // SEL_BLOCK
        forced = (sidx == 0) | (sidx == cur) | (sidx == cur - 1)
        score[h, g] = jnp.where(forced, -jnp.inf, jnp.where(sidx <= cur, imp[h, g], NEG))
        mbt[h, g] = jnp.where(forced, 0.0, NEG)
    for _ in range(N_SEL - 3):
        for ch in chains:
            cm = jnp.max(score[ch], axis=0, keepdims=True)
            first = jnp.min(jnp.where(score[ch] == cm, sidx_f, float(N_SLC)), axis=0, keepdims=True)
            pick = sidx_f == first
            mbt[ch] = jnp.where(pick, 0.0, mbt[ch])
            score[ch] = jnp.where(pick, -jnp.inf, score[ch])

    for h, g in chains:
        qa_ref[2 * h + g] = jnp.concatenate(
            [jnp.concatenate([mbt[h, g].astype(BF16)] * GQA_REP, axis=1), qpad[h, g]], axis=0)
        mbt_tail = jnp.where(sidx < t_nfar[h] * (FAR // SEL_BLOCK), NEG, mbt[h, g]).astype(BF16)
        qa_tail = jnp.concatenate([jnp.concatenate([mbt_tail] * GQA_REP, axis=1), qpad[h, g]], axis=0)
        st = jnp.dot(kaug_ref[pl.ds(t_qs[h], TAIL), :], qa_tail, preferred_element_type=F32) + ds_ref[g]
        sc_ref[2 * h + g][...] = st
        m_ref[2 * h + g] = jnp.max(st, axis=0, keepdims=True)
    for h in tiles:
        scores(h, 0, 0)
    for h, g in chains:
        pt = jnp.exp2(sc_ref[2 * h + g][...] - m_ref[2 * h + g]).astype(BF16)
        acc_ref[2 * h + g] = jnp.dot(vst_ref[g, :, pl.ds(t_qs[h], TAIL)], pt, preferred_element_type=F32)

    def far_pair(kk, carry):
        c = 2 * kk
        for h in tiles:
            scores(h, c + 1, 1)
        for h in tiles:
            consume(h, t_nfar[h], c, 0)
        for h in tiles:
            scores(h, c + 2, 0)
        for h in tiles:
            consume(h, t_nfar[h], c + 1, 1)
        return carry

    lax.fori_loop(0, (t_nfar[TILES - 1] + 1) // 2, far_pair, 0)

    for h in tiles:
        lanes = slice(h * TQ, (h + 1) * TQ)
        out_rows = []
        for g in groups:
            dsl = slice(g * HEAD_DIM, (g + 1) * HEAD_DIM)
            hg = 2 * h + g
            o_s = acc_ref[hg, 0:HEAD_DIM, :] * (1.0 / acc_ref[hg, HEAD_DIM:HEAD_DIM + 1, :])
            for r in range(GQA_REP):
                cs = slice(r * TQ, (r + 1) * TQ)
                gc = 3 * (GQA_REP * g + r)
                out_rows.append(gt_ref[gc:gc + 1, lanes] * o_c[h, g][dsl, cs]
                                + gt_ref[gc + 1:gc + 2, lanes] * o_s[:, cs]
                                + gt_ref[gc + 2:gc + 3, lanes] * o_w[h, g][:, cs])
        o_ref[h * TQ:(h + 1) * TQ, :] = jnp.concatenate(out_rows, axis=0).T


def _attention(t_all, gates_t, kc2, vct, kaug, vst, kw_pad, vwt, ds, dw, bc, B, T):
    tp = T + WINDOW
    tqs = TQ * TILES
    per_b = lambda shape: pl.BlockSpec((None,) + shape, lambda b, i: (b,) + (0,) * len(shape),
                                      pipeline_mode=pl.Buffered(1))
    const = lambda a: pl.BlockSpec(a.shape, lambda b, i: (0,) * a.ndim, pipeline_mode=pl.Buffered(1))
    cols4 = GQA_REP * TQ
    n_hg = TILES * N_GROUPS_KV
    return pl.pallas_call(
        functools.partial(_attn_kernel, n_far_max=T // FAR),
        grid=(B, T // tqs),
        in_specs=[pl.BlockSpec((None, 512, tqs), lambda b, i: (b, 0, i)),
                  pl.BlockSpec((None, GATE_ROWS, tqs), lambda b, i: (b, 0, i)),
                  per_b((C_PAD, 256)),
                  per_b((256, C_PAD)),
                  per_b((tp, 256)),
                  per_b((N_GROUPS_KV, V_ROWS, tp)),
                  per_b((tp, 256)),
                  per_b((N_GROUPS_KV, V_ROWS, tp)),
                  const(ds), const(dw), const(bc)],
        out_specs=pl.BlockSpec((None, tqs, 512), lambda b, i: (b, i, 0)),
        out_shape=jax.ShapeDtypeStruct((B, T, 512), F32),
        scratch_shapes=([pltpu.VMEM((C_PAD, cols4), F32)] * n_hg
                        + [pltpu.VMEM((TAIL, cols4), F32)] * n_hg
                        + [pltpu.VMEM((n_hg, 256, cols4), BF16),
                           pltpu.VMEM((n_hg, 1, cols4), F32),
                           pltpu.VMEM((n_hg, V_ROWS, cols4), F32),
                           pltpu.VMEM((2 * n_hg, FAR, cols4), F32),
                           pltpu.VMEM((2 * n_hg, 1, cols4), F32)]),
        compiler_params=_cparams(("parallel", "arbitrary"), VMEM_LIMIT),
        name="attn",
    )(t_all, gates_t, kc2, vct, kaug, vst, kw_pad, vwt, ds, dw, bc)


def _rglru_kernel(*refs, L):
    J = L // 8
    n_col = D_RNN // 128
    rx_refs, ry_refs = refs[:n_col], refs[n_col:2 * n_col]
    cw_ref, cb_ref, wa_ref, ba_ref, wx_ref, bx_ref, sp_ref, o_ref, xprev_ref, h_ref, os_ref = refs[2 * n_col:]

    @pl.when(pl.program_id(1) == 0)
    def _():
        xprev_ref[...] = jnp.zeros(xprev_ref.shape, F32)
        h_ref[...] = jnp.zeros(h_ref.shape, F32)

    def strided(col_refs):
        return jnp.concatenate(
            [jnp.concatenate([ref[pl.ds(j, 8, stride=J), :] for j in range(J)], axis=0)
             for ref in col_refs], axis=1)

    xp = strided(rx_refs)
    yp = strided(ry_refs)
    sub = lax.broadcasted_iota(jnp.int32, (8, D_RNN), 0)

    def delayed(d):
        heads = []
        for j in range(d):
            src = pltpu.roll(xp[(J + j - d) * 8:(J + j - d + 1) * 8], 1, axis=0)
            heads.append(jnp.where(sub == 0, xprev_ref[8 + j - d:9 + j - d, :], src))
        return jnp.concatenate(heads + [xp[:L - 8 * d]], axis=0)

    xc = (cb_ref[...] + cw_ref[0:1, :] * delayed(3) + cw_ref[1:2, :] * delayed(2)
          + cw_ref[2:3, :] * delayed(1) + cw_ref[3:4, :] * xp)
    xprev_ref[...] = jnp.concatenate([ref[L - 8:L, :] for ref in rx_refs], axis=1)
    xcb = xc.astype(BF16)
    r = jax.nn.sigmoid(jnp.dot(xcb, wa_ref[...], preferred_element_type=F32) + ba_ref[...])
    ig = jax.nn.sigmoid(jnp.dot(xcb, wx_ref[...], preferred_element_type=F32) + bx_ref[...])
    a = jnp.exp(-RG_C * r * sp_ref[...])
    bt = jnp.sqrt(1.0 - a * a) * (ig * xc)

    h = jnp.zeros((8, D_RNN), F32)
    dec = jnp.ones((8, D_RNN), F32)
    hs, decs = [], []
    for j in range(J):
        aj = a[j * 8:(j + 1) * 8]
        h = aj * h + bt[j * 8:(j + 1) * 8]
        dec = aj * dec
        hs.append(h)
        decs.append(dec)
    carry = h_ref[0:1, :]
    carries = []
    for s in range(8):
        carries.append(carry)
        carry = dec[s:s + 1, :] * carry + h[s:s + 1, :]
    h_ref[...] = jnp.broadcast_to(carry, h_ref.shape)
    h_in = jnp.concatenate(carries, axis=0)
    cdf_c = math.sqrt(2.0 / math.pi)
    for j in range(J):
        y = yp[j * 8:(j + 1) * 8]
        cdf = 0.5 * (1.0 + jnp.tanh(cdf_c * (y + 0.044715 * (y * y * y))))
        res = (hs[j] + decs[j] * h_in) * (y * cdf)
        for c in range(n_col):
            os_ref[c, pl.ds(j, 8, stride=J), :] = res[:, c * 128:(c + 1) * 128]
    for c in range(n_col):
        o_ref[:, c * 128:(c + 1) * 128] = os_ref[c]


def _rglru(rx, ry, conv_w, conv_b, wa_bd, b_a, wx_bd, b_x, sp, B, T):
    L = 512
    blk = pl.BlockSpec((None, L, D_RNN), lambda b, t: (b, t, 0))
    n_col = D_RNN // 128
    cols = [pl.BlockSpec((None, L, 128), functools.partial(lambda b, t, c: (b, t, c), c=c))
            for c in range(n_col)]
    full = lambda a: pl.BlockSpec(a.shape, lambda b, t: (0,) * a.ndim)
    return pl.pallas_call(
        functools.partial(_rglru_kernel, L=L),
        grid=(B, T // L),
        in_specs=cols + cols + [full(conv_w), full(conv_b), full(wa_bd), full(b_a), full(wx_bd),
                                full(b_x), full(sp)],
        out_specs=blk,
        out_shape=jax.ShapeDtypeStruct((B, T, D_RNN), F32),
        scratch_shapes=[pltpu.VMEM((8, D_RNN), F32), pltpu.VMEM((8, D_RNN), F32),
                        pltpu.VMEM((n_col, L, 128), F32)],
        compiler_params=_cparams(("parallel", "arbitrary"), VMEM_LIMIT),
        name="rglru",
    )(*([rx] * n_col + [ry] * n_col), conv_w, conv_b, wa_bd, b_a, wx_bd, b_x, sp)


def _layer_norm(y, g, b):
    mu = jnp.mean(y, axis=-1, keepdims=True)
    d = y - mu
    var = jnp.mean(d * d, axis=-1, keepdims=True)
    return d * lax.rsqrt(var + LN_EPS) * g + b


def _outproj_kernel(at_ref, rn_ref, x_ref, ga_ref, gr_ref, wo_ref, g1_ref, b1_ref, wr2_ref, br_ref,
                    x1_ref, comb_ref):
    a = at_ref[...]
    rn = rn_ref[...]
    ha = a * lax.rsqrt(jnp.mean(a * a, axis=-1, keepdims=True) + RMS_EPS) * ga_ref[...]
    hr = rn * lax.rsqrt(jnp.mean(rn * rn, axis=-1, keepdims=True) + RMS_EPS) * gr_ref[...]
    heads = jnp.concatenate([ha, hr], axis=1).astype(BF16)
    mix = jnp.dot(heads, wo_ref[...], preferred_element_type=F32)
    x1 = _layer_norm(ALPHA * x_ref[...] + mix, g1_ref[...], b1_ref[...])
    x1_ref[...] = x1

    xh = x1.astype(BF16)
    xl = (x1 - xh.astype(F32)).astype(BF16)
    hh_hl = jnp.dot(xh, wr2_ref[...], preferred_element_type=F32)
    lg = (hh_hl[:, :128] + jnp.dot(xl, wr2_ref[:, :128], preferred_element_type=F32)
          + hh_hl[:, 128:]) + br_ref[...]
    lane = lax.broadcasted_iota(jnp.int32, lg.shape, 1)
    lane_f = lane.astype(F32)
    big = 1e9
    isg = lane < N_EXP_GROUPS
    gmax = jnp.max(jnp.where(isg, lg, -jnp.inf), axis=-1, keepdims=True)
    pg_top = 1.0 / jnp.sum(jnp.where(isg, jnp.exp(lg - gmax), 0.0), axis=-1, keepdims=True)
    gi = jnp.min(jnp.where(isg & (lg == gmax), lane_f, big), axis=-1, keepdims=True)
    egrp = ((lane - N_EXP_GROUPS) // EXPERTS_PER_GROUP).astype(F32)
    ise = (lane >= N_EXP_GROUPS) & (lane < N_EXP_GROUPS + N_EXPERTS) & (egrp == gi)
    emax = jnp.max(jnp.where(ise, lg, -jnp.inf), axis=-1, keepdims=True)
    i1 = jnp.min(jnp.where(ise & (lg == emax), lane_f, big), axis=-1, keepdims=True)
    rest = ise & (lane_f != i1)
    m2 = jnp.max(jnp.where(rest, lg, -jnp.inf), axis=-1, keepdims=True)
    i2 = jnp.min(jnp.where(rest & (lg == m2), lane_f, big), axis=-1, keepdims=True)
    e2 = jnp.exp(m2 - emax)
    inv = pg_top / (1.0 + e2)
    comb_ref[...] = jnp.where(lane_f == i1, inv, 0.0) + jnp.where(lane_f == i2, inv * e2, 0.0)


def _outproj(attn, rnn, x2, ga, gr, wo, g1, b1, wr2, br, N):
    tm = 512
    row = lambda w: pl.BlockSpec((tm, w), lambda i: (i, 0))
    full = lambda a: pl.BlockSpec(a.shape, lambda i: (0,) * a.ndim)
    return pl.pallas_call(
        _outproj_kernel,
        grid=(N // tm,),
        in_specs=[row(512), row(512), row(D_MODEL), full(ga), full(gr), full(wo), full(g1), full(b1),
                  full(wr2), full(br)],
        out_specs=[row(D_MODEL), row(128)],
        out_shape=[jax.ShapeDtypeStruct((N, D_MODEL), F32), jax.ShapeDtypeStruct((N, 128), F32)],
        compiler_params=_cparams(("parallel",), VMEM_LIMIT),
        name="outproj",
    )(attn, rnn, x2, ga, gr, wo, g1, b1, wr2, br)


def _moe_kernel(x1_ref, comb_ref, p_ref, tri_ref, wup_ref, wdn_ref, wg_ref, wp_ref, g2_ref, b2_ref, o_ref,
                acc_ref, xb_ref, rank_ref, rankt_ref, combt_ref):
    pi = pl.program_id(1)
    n_sub = x1_ref.shape[0] // MOE_SUB
    sub_rows = [slice(h * MOE_SUB, (h + 1) * MOE_SUB) for h in range(n_sub)]
    lane_col = N_EXP_GROUPS

    @pl.when(pi == 0)
    def _():
        x1 = x1_ref[...]
        xb = x1.astype(BF16)
        xb_ref[...] = xb
        gate = jax.nn.sigmoid(jnp.dot(xb, wg_ref[...], preferred_element_type=F32))
        ple = gate * jnp.dot(p_ref[...].astype(BF16), wp_ref[...], preferred_element_type=F32)
        acc_ref[...] = ALPHA * x1 + ple
        for rows in sub_rows:
            comb = comb_ref[rows, :]
            chosen = comb > 0.0
            rank = jnp.dot(tri_ref[...], jnp.where(chosen, 1.0, 0.0).astype(BF16),
                           preferred_element_type=F32)
            rank = jnp.where(chosen, rank, -1.0)
            rank_ref[rows, :] = rank
            rankt_ref[:, rows] = rank.T
            combt_ref[:, rows] = comb.T

    lane = lax.broadcasted_iota(jnp.int32, (MOE_SUB, 128), 1)
    slot_r = lax.broadcasted_iota(jnp.int32, (MOE_SLOTS, MOE_SUB), 0).astype(F32)
    slot_c = lax.broadcasted_iota(jnp.int32, (MOE_SUB, MOE_SLOTS), 1).astype(F32)

    def route_pair(base):
        chains = [(rows, k) for rows in sub_rows for k in range(2)]
        cmask, xc, u, hsw, y, ys, ct = {}, {}, {}, {}, {}, {}, {}
        for ch in chains:
            rows, k = ch
            rk_row = rankt_ref[pl.ds(2 * pi + k + lane_col, 1), rows] - base
            cmask[ch] = rk_row == slot_r
            xc[ch] = jnp.dot(jnp.where(cmask[ch], 1.0, 0.0).astype(BF16), xb_ref[rows, :],
                             preferred_element_type=F32).astype(BF16)
        for ch in chains:
            u[ch] = jnp.dot(xc[ch], wup_ref[ch[1]], preferred_element_type=F32)
        for ch in chains:
            ua = u[ch][:, :D_FF]
            hsw[ch] = (ua * jax.nn.sigmoid(ua) * u[ch][:, D_FF:]).astype(BF16)
        for ch in chains:
            y[ch] = jnp.dot(hsw[ch], wdn_ref[ch[1]], preferred_element_type=F32)
        for ch in chains:
            rows, k = ch
            e_lane = 2 * pi + k + lane_col
            w_row = combt_ref[pl.ds(e_lane, 1), rows]
            w_slot = jnp.sum(jnp.where(cmask[ch], w_row, 0.0), axis=1, keepdims=True)
            ys[ch] = (w_slot * y[ch]).astype(BF16)
            rk_col = jnp.sum(jnp.where(lane == e_lane, rank_ref[rows, :], 0.0), axis=1,
                             keepdims=True) - base
            ct[ch] = jnp.where(rk_col == slot_c, 1.0, 0.0).astype(BF16)
        for rows in sub_rows:
            pair = [(rows, k) for k in range(2)]
            ctp = jnp.concatenate([ct[ch] for ch in pair], axis=1)
            acc_ref[rows, :] = acc_ref[rows, :] + jnp.dot(
                ctp, jnp.concatenate([ys[ch] for ch in pair], axis=0), preferred_element_type=F32)

    route_pair(0.0)

    top_rank = jnp.max(jnp.maximum(rankt_ref[pl.ds(2 * pi + lane_col, 1), :],
                                   rankt_ref[pl.ds(2 * pi + 1 + lane_col, 1), :]))

    @pl.when(top_rank >= MOE_SLOTS)
    def _():
        def extra(sb, carry):
            route_pair((sb * MOE_SLOTS).astype(F32))
            return carry

        lax.fori_loop(1, top_rank.astype(jnp.int32) // MOE_SLOTS + 1, extra, 0)

    @pl.when(pi == N_EXPERTS // 2 - 1)
    def _():
        o_ref[...] = _layer_norm(acc_ref[...], g2_ref[...], b2_ref[...])


def _moe(x1, comb, p2, wup, wdn, wg, wp, g2, b2, N):
    tm = MOE_TILE
    tri = (jnp.arange(MOE_SUB)[None, :] < jnp.arange(MOE_SUB)[:, None]).astype(BF16)
    row = lambda w: pl.BlockSpec((tm, w), lambda i, e: (i, 0))
    full = lambda a: pl.BlockSpec(a.shape, lambda i, e: (0,) * a.ndim)
    return pl.pallas_call(
        _moe_kernel,
        grid=(N // tm, N_EXPERTS // 2),
        in_specs=[row(D_MODEL), row(128), row(D_PLE), full(tri),
                  pl.BlockSpec((2, D_MODEL, 2 * D_FF), lambda i, e: (e, 0, 0)),
                  pl.BlockSpec((2, D_FF, D_MODEL), lambda i, e: (e, 0, 0)),
                  full(wg), full(wp), full(g2), full(b2)],
        out_specs=row(D_MODEL),
        out_shape=jax.ShapeDtypeStruct((N, D_MODEL), F32),
        scratch_shapes=[pltpu.VMEM((tm, D_MODEL), F32),
                        pltpu.VMEM((tm, D_MODEL), BF16),
                        pltpu.VMEM((tm, 128), F32),
                        pltpu.VMEM((128, tm), F32),
                        pltpu.VMEM((128, tm), F32)],
        compiler_params=_cparams(("parallel", "arbitrary"), VMEM_LIMIT),
        name="moe",
    )(x1, comb, p2, tri, wup, wdn, wg, wp, g2, b2)


def _block_diag(w):
    n, k, j = w.shape
    eye = jnp.eye(n, dtype=w.dtype)
    return (w[:, :, None, :] * eye[:, None, :, None]).reshape(n * k, n * j)


def kernel(x, p, rel_bias, w_in, cmp_pe_k, cmp_pe_v, cmp_w_k, cmp_w_v, conv_w, conv_b, rg_w_a, rg_b_a,
           rg_w_x, rg_b_x, rg_lambda, attn_out_gain, rnn_out_gain, w_out, ln1_g, ln1_b, router_group_w,
           router_group_b, router_expert_w, router_expert_b, expert_w_up, expert_w_down, ple_w,
           ple_gate_w, ln2_g, ln2_b):
    B, T, _ = x.shape
    N = B * T
    assert T % FAR == 0 and T % (TQ * TILES) == 0 and T // SEL_BLOCK <= N_SLC and w_in.shape[0] == 1
    n_cmp = (T - CMP_LEN) // CMP_STRIDE + 1
    n_chunk = T // CMP_STRIDE
    assert C_OFF + n_chunk <= C_PAD
    row1 = lambda v: v.reshape(1, -1)

    w = w_in[0]
    c0 = D_ATTN
    kv = lambda k: w[:, c0 + k * D_KV:c0 + (k + 1) * D_KV]
    g0 = c0 + 6 * D_KV
    wq = w[:, :D_ATTN] * (HEAD_DIM ** -0.5 * LOG2E)
    w_main = jnp.concatenate([kv(2), kv(4), kv(0), kv(1), w[:, g0 + 24:]], axis=1).astype(BF16)
    w_t = jnp.concatenate([wq, kv(3), kv(5)], axis=1).T.astype(BF16)
    w_g = jnp.pad(w[:, g0:g0 + 24], ((0, 0), (0, GATE_ROWS - 24))).T.astype(BF16)

    x2 = x.reshape(N, D_MODEL)
    ks, kw, kc_raw, vc_raw, rx, ry, t_all, gates_t, wup_b, wdn_b = _inproj(
        x2, w_main, w_t, w_g, expert_w_up[0].reshape(N_EXPERTS * D_MODEL, 2 * D_FF),
        expert_w_down[0].reshape(N_EXPERTS * D_FF, D_MODEL), B, T)

    def per_pos(wc):
        wl = wc.reshape(CMP_LEN, HEAD_DIM, HEAD_DIM)
        z = jnp.zeros_like(wl)
        return jnp.concatenate([jnp.concatenate([wl, z], axis=2), jnp.concatenate([z, wl], axis=2)],
                               axis=1).astype(BF16)

    pek = jnp.tile(cmp_pe_k[0], (1, N_GROUPS_KV))
    pev = jnp.tile(cmp_pe_v[0], (1, N_GROUPS_KV))
    grp_onehot = (jnp.arange(C_PAD)[:, None] // 8 == jnp.arange(128)[None, :]).astype(BF16)
    kc2, vct = _compress(kc_raw.reshape(B, T, 128), vc_raw.reshape(B, T, 128), pek, pev,
                         per_pos(cmp_w_k[0]), per_pos(cmp_w_v[0]), grp_onehot, B, T, n_cmp)

    onehot = (jnp.arange(T)[:, None] // SEL_BLOCK == jnp.arange(N_SLC)[None, :]).astype(BF16)
    kaug = jnp.concatenate([jnp.broadcast_to(onehot, (B, T, N_SLC)), ks.reshape(B, T, 128)], axis=2)
    lane256 = jnp.arange(256)
    pad_sel = jnp.broadcast_to((lane256 == N_SLC - 1).astype(BF16), (B, WINDOW, 256))
    pad_win = jnp.broadcast_to((lane256 == 128).astype(BF16), (B, WINDOW, 256))
    kaug = jnp.concatenate([pad_sel, kaug], axis=1)
    kw_pad = jnp.concatenate([pad_win, jnp.pad(kw.reshape(B, T, 128), ((0, 0), (0, 0), (0, 128)))], axis=1)
    ones_rows = jnp.concatenate([jnp.ones((B, N_GROUPS_KV, 1, T), BF16),
                                 jnp.zeros((B, N_GROUPS_KV, V_ROWS - HEAD_DIM - 1, T), BF16)], axis=2)

    def v_rows(vt):
        vt = jnp.concatenate([vt.reshape(B, N_GROUPS_KV, HEAD_DIM, T), ones_rows], axis=2)
        return jnp.pad(vt, ((0, 0), (0, 0), (0, 0), (WINDOW, 0)))

    vst = v_rows(t_all[:, 512:640])
    vwt = v_rows(t_all[:, 640:768])
    ds, dw, bc = _bias_tiles(rel_bias)

    cidx = jnp.arange(C_PAD) - C_OFF
    s_lo = jnp.arange(N_SLC)[:, None] * SEL_BLOCK
    overlap = ((cidx[None, :] * CMP_STRIDE < s_lo + SEL_BLOCK) & (cidx[None, :] * CMP_STRIDE + CMP_LEN > s_lo)
               & (cidx[None, :] >= 0) & (cidx[None, :] < n_cmp)).astype(BF16)
    vct = jnp.concatenate([vct, jnp.broadcast_to(overlap, (B, N_SLC, C_PAD))], axis=1)
    attn = _attention(t_all, gates_t, kc2, vct, kaug, vst, kw_pad, vwt, ds, dw, bc, B, T)

    sp = jax.nn.softplus(-rg_lambda[0].astype(F32))
    rnn = _rglru(rx.reshape(B, T, D_RNN), ry.reshape(B, T, D_RNN), conv_w[0], row1(conv_b[0]),
                 _block_diag(rg_w_a[0]).astype(BF16), row1(rg_b_a[0]),
                 _block_diag(rg_w_x[0]).astype(BF16), row1(rg_b_x[0]), row1(sp), B, T)

    wr = jnp.pad(jnp.concatenate([router_group_w[0], router_expert_w[0]], axis=1), ((0, 0), (0, 108)))
    wrh = wr.astype(BF16)
    wr2 = jnp.concatenate([wrh, (wr - wrh.astype(F32)).astype(BF16)], axis=1)
    br = jnp.pad(jnp.concatenate([router_group_b[0], router_expert_b[0]]), (0, 108)).reshape(1, 128)
    x1, comb = _outproj(attn.reshape(N, 512), rnn.reshape(N, D_RNN), x2, row1(attn_out_gain[0]),
                        row1(rnn_out_gain[0]), w_out[0].astype(BF16), row1(ln1_g[0]), row1(ln1_b[0]),
                        wr2, br, N)

    out = _moe(x1, comb, p[0].reshape(N, D_PLE), wup_b.reshape(N_EXPERTS, D_MODEL, 2 * D_FF),
               wdn_b.reshape(N_EXPERTS, D_FF, D_MODEL), ple_gate_w[0].astype(BF16), ple_w[0].astype(BF16),
               row1(ln2_g[0]), row1(ln2_b[0]), N)
    return out.reshape(B, T, D_MODEL)
```

```python
import functools
import math

import jax
import jax.numpy as jnp
from jax import lax
from jax.experimental import pallas as pl
from jax.experimental.pallas import tpu as pltpu

F32 = jnp.float32
BF16 = jnp.bfloat16
NEG = -1e30
LOG2E = 1.4426950408889634

D_MODEL = 1024
HEAD_DIM = 64
N_HEADS = 8
N_GROUPS_KV = 2
GQA_REP = 4
D_ATTN = 512
D_RNN = 512
D_KV = 128
CMP_LEN = 32
CMP_STRIDE = 16
SEL_BLOCK = 64
N_SEL = 16
WINDOW = 512
N_BUCKETS = 32
MAX_DISTANCE = 128
N_EXPERTS = 16
EXPERTS_PER_GROUP = 4
N_EXP_GROUPS = 4
D_FF = 512
D_PLE = 256
ALPHA = 2.0 ** 0.25
LN_EPS = 1e-5
RMS_EPS = 1e-6
FORCE_BONUS = 1e4
RG_C = 8.0

TQ = 128
TILES = 2
N_SLC = 128
C_PAD = 640
C_OFF = 16
FAR = 512
TAIL = WINDOW + TQ
BAND = 24
GATE_ROWS = 32
V_ROWS = 80
MOE_TILE = 1024
MOE_SUB = 512
MOE_SLOTS = 128
VMEM_LIMIT = 56 * 1024 * 1024


def _cparams(sem, vmem=None):
    return pltpu.CompilerParams(dimension_semantics=sem, vmem_limit_bytes=vmem)


def _inproj_kernel(x_ref, wm_ref, wt_ref, ks_ref, kw_ref, kc_ref, vc_ref, rx_ref, ry_ref, t_ref, gt_ref):
    xb = x_ref[...].astype(BF16)

    def mm(lo, hi):
        return jnp.dot(xb, wm_ref[:, lo:hi], preferred_element_type=F32)

    k_pair = mm(0, 256)
    ks_ref[...] = k_pair[:, :128].astype(BF16)
    kw_ref[...] = k_pair[:, 128:].astype(BF16)
    c_pair = mm(256, 512)
    kc_ref[...] = c_pair[:, :128]
    vc_ref[...] = c_pair[:, 128:]
    rx_ref[...] = mm(512, 1024)
    ry_ref[...] = mm(1024, 1536)
    tr = lax.dot_general(wt_ref[...], xb, (((1,), (1,)), ((), ())), preferred_element_type=F32)
    t_ref[...] = tr[:768].astype(BF16)
    gt_ref[...] = jax.nn.sigmoid(tr[768:])


def _inproj(x2, w_main, w_t, B, T):
    N = B * T
    tm = 512
    nt = T // tm
    row = lambda w: pl.BlockSpec((tm, w), lambda i: (i, 0))
    full = lambda a: pl.BlockSpec(a.shape, lambda i: (0,) * a.ndim)
    tr = lambda r: pl.BlockSpec((None, r, tm), lambda i: (i // nt, 0, i % nt))
    return pl.pallas_call(
        _inproj_kernel,
        grid=(N // tm,),
        in_specs=[row(D_MODEL), full(w_main), full(w_t)],
        out_specs=[row(128), row(128), row(128), row(128), row(512), row(512), tr(768), tr(GATE_ROWS)],
        out_shape=[jax.ShapeDtypeStruct((N, 128), BF16),
                   jax.ShapeDtypeStruct((N, 128), BF16),
                   jax.ShapeDtypeStruct((N, 128), F32),
                   jax.ShapeDtypeStruct((N, 128), F32),
                   jax.ShapeDtypeStruct((N, 512), F32),
                   jax.ShapeDtypeStruct((N, 512), F32),
                   jax.ShapeDtypeStruct((B, 768, T), BF16),
                   jax.ShapeDtypeStruct((B, GATE_ROWS, T), F32)],
        compiler_params=_cparams(("parallel",), VMEM_LIMIT),
        name="inproj",
    )(x2, w_main, w_t)


def _compress_kernel(kr_ref, vr_ref, pek_ref, pev_ref, wk_ref, wv_ref, oh_ref, kc_ref, vct_ref, *,
                     n_cmp, n_chunk):
    def compress(raw_ref, pe_ref, w_ref):
        lo = jnp.zeros((n_chunk, 2 * HEAD_DIM), F32)
        hi = jnp.zeros((n_chunk, 2 * HEAD_DIM), F32)
        for j in range(CMP_STRIDE):
            a = raw_ref[pl.ds(j, n_chunk, stride=CMP_STRIDE), :]
            lo = lo + jnp.dot((a + pe_ref[j:j + 1, :]).astype(BF16), w_ref[j],
                              preferred_element_type=F32)
            hi = hi + jnp.dot((a + pe_ref[CMP_STRIDE + j:CMP_STRIDE + j + 1, :]).astype(BF16),
                              w_ref[CMP_STRIDE + j], preferred_element_type=F32)
        hi = pltpu.roll(hi, n_chunk - 1, axis=0)
        rid = lax.broadcasted_iota(jnp.int32, (n_chunk, 2 * HEAD_DIM), 0)
        out = jnp.where(rid < n_cmp, lo + hi, 0.0)
        return jnp.concatenate([jnp.zeros((C_OFF, 2 * HEAD_DIM), F32), out,
                                jnp.zeros((C_PAD - C_OFF - n_chunk, 2 * HEAD_DIM), F32)], axis=0)

    kc_ref[:, 0:2 * HEAD_DIM] = compress(kr_ref, pek_ref, wk_ref).astype(BF16)
    kc_ref[:, 2 * HEAD_DIM:] = oh_ref[...]
    vct_ref[...] = compress(vr_ref, pev_ref, wv_ref).T.astype(BF16)


def _compress(kc_raw, vc_raw, pek, pev, wk, wv, grp_onehot, B, T, n_cmp):
    n_chunk = T // CMP_STRIDE
    blk = pl.BlockSpec((None, T, 2 * HEAD_DIM), lambda b: (b, 0, 0))
    full = lambda a: pl.BlockSpec(a.shape, lambda b: (0,) * a.ndim)
    return pl.pallas_call(
        functools.partial(_compress_kernel, n_cmp=n_cmp, n_chunk=n_chunk),
        grid=(B,),
        in_specs=[blk, blk, full(pek), full(pev), full(wk), full(wv), full(grp_onehot)],
        out_specs=[pl.BlockSpec((None, C_PAD, 256), lambda b: (b, 0, 0)),
                   pl.BlockSpec((None, 2 * HEAD_DIM, C_PAD), lambda b: (b, 0, 0))],
        out_shape=[jax.ShapeDtypeStruct((B, C_PAD, 256), BF16),
                   jax.ShapeDtypeStruct((B, 2 * HEAD_DIM, C_PAD), BF16)],
        compiler_params=_cparams(("parallel",), VMEM_LIMIT),
        name="compress",
    )(kc_raw, vc_raw, pek, pev, wk, wv, grp_onehot)


def _bias_kernel(tab_ref, dt_ref, bt_ref, dc_ref, bc_ref, os_ref, ow_ref, oc_ref):
    h = pl.program_id(0) * GQA_REP + pl.program_id(1)

    def lookup(bk):
        out = jnp.zeros(bk.shape, F32)
        for b in range(N_BUCKETS):
            out = jnp.where(bk == b, tab_ref[h, b], out)
        return (out - tab_ref[h, N_BUCKETS - 1]) * LOG2E

    dt = dt_ref[...]
    v = lookup(bt_ref[...])
    os_ref[...] = jnp.where(dt >= 0, v, NEG)
    ow_ref[...] = jnp.where((dt >= 0) & (dt < WINDOW), v, NEG)
    oc_ref[...] = jnp.where(dc_ref[...] >= 0, lookup(bc_ref[...]), NEG)


def _t5_bucket(dist):
    max_exact = N_BUCKETS // 2
    d = jnp.maximum(dist, 0)
    df = jnp.maximum(d, 1).astype(F32)
    large = max_exact + (jnp.log(df / max_exact) / math.log(MAX_DISTANCE / max_exact)
                         * (N_BUCKETS - max_exact)).astype(jnp.int32)
    large = jnp.minimum(large, N_BUCKETS - 1)
    return jnp.where(d < max_exact, d, large)


def _bias_tiles(rel_bias):
    tq = jnp.arange(TQ, dtype=jnp.int32)
    dt = tq[None, :] + WINDOW - jnp.arange(TAIL, dtype=jnp.int32)[:, None]
    cc = jnp.arange(BAND, dtype=jnp.int32) - C_OFF
    dc = tq[None, :] - (cc[:, None] * CMP_STRIDE + CMP_LEN - 1)
    full = lambda a: pl.BlockSpec(a.shape, lambda g, r: (0,) * a.ndim)
    out = lambda rows: pl.BlockSpec((None, rows, TQ), lambda g, r: (g, 0, r))
    return pl.pallas_call(
        _bias_kernel,
        grid=(N_GROUPS_KV, GQA_REP),
        in_specs=[pl.BlockSpec(memory_space=pltpu.SMEM), full(dt), full(dt), full(dc), full(dc)],
        out_specs=[out(TAIL), out(TAIL), out(BAND)],
        out_shape=[jax.ShapeDtypeStruct((N_GROUPS_KV, TAIL, GQA_REP * TQ), F32),
                   jax.ShapeDtypeStruct((N_GROUPS_KV, TAIL, GQA_REP * TQ), F32),
                   jax.ShapeDtypeStruct((N_GROUPS_KV, BAND, GQA_REP * TQ), F32)],
        compiler_params=_cparams(("arbitrary", "arbitrary")),
        name="biastile",
    )(rel_bias.T, dt, _t5_bucket(dt), dc, _t5_bucket(dc))


def _attn_kernel(qt_ref, gt_ref, kc_ref, vct_ref, kaug_ref, vst_ref, kw_ref, vwt_ref, ds_ref, dw_ref,
                 bc_ref, o_ref, *scratch, n_far_max):
    n_hg = TILES * N_GROUPS_KV
    sc_ref, sw_ref = (scratch[n_hg * k:n_hg * (k + 1)] for k in range(2))
    qa_ref, m_ref, acc_ref, sb_ref, sm_ref = scratch[2 * n_hg:]
    cols4 = GQA_REP * TQ
    groups = range(N_GROUPS_KV)
    zeros_q = jnp.zeros((HEAD_DIM, cols4), BF16)
    grp = lax.broadcasted_iota(jnp.int32, (128, cols4), 0)
    neg_row = jnp.where(grp == 0, NEG, 0.0).astype(BF16)
    tq_col = lax.broadcasted_iota(jnp.int32, (1, cols4), 1) % TQ
    sidx = lax.broadcasted_iota(jnp.int32, (N_SLC, TQ), 0)
    sidx_f = sidx.astype(F32)
    tq_lane = lax.broadcasted_iota(jnp.int32, (N_SLC, TQ), 1)

    def chunk_start(c):
        return pl.multiple_of(WINDOW + jnp.minimum(c, n_far_max - 1) * FAR, FAR)

    def scores(h, c, par):
        kch = kaug_ref[pl.ds(chunk_start(c), FAR), :]
        for g in groups:
            sf = jnp.dot(kch, qa_ref[2 * h + g], preferred_element_type=F32)
            sb_ref[4 * h + 2 * g + par] = sf
            sm_ref[4 * h + 2 * g + par] = jnp.max(sf, axis=0, keepdims=True)

    def consume(h, nfar, c, par):
        c0 = chunk_start(c)
        live = c < nfar
        for g in groups:
            hg = 2 * h + g
            sf = sb_ref[4 * h + 2 * g + par]
            mprev = m_ref[hg]
            mcand = jnp.maximum(mprev, sm_ref[4 * h + 2 * g + par])
            mnew = jnp.where(live, mcand, mprev)
            pf = jnp.exp2(sf - jnp.where(live, mcand, -NEG)).astype(BF16)
            acc_ref[hg] = jnp.exp2(mprev - mnew) * acc_ref[hg] + jnp.dot(
                vst_ref[g, :, pl.ds(c0, FAR)], pf, preferred_element_type=F32)
            m_ref[hg] = mnew

    tiles = range(TILES)
    chains = [(h, g) for h in tiles for g in groups]
    t_i = [pl.program_id(1) * TILES + h for h in tiles]
    t_qs = [pl.multiple_of(i * TQ, TQ) for i in t_i]
    t_nfar = [jnp.maximum(i - 1, 0) // 4 for i in t_i]
    qpad = {}
    for h, g in chains:
        qgt = jnp.concatenate(
            [qt_ref[(GQA_REP * g + r) * HEAD_DIM:(GQA_REP * g + r + 1) * HEAD_DIM, h * TQ:(h + 1) * TQ]
             for r in range(GQA_REP)], axis=1)
        qpad[h, g] = jnp.concatenate([qgt, zeros_q] if g == 0 else [zeros_q, qgt], axis=0)

    for h, g in chains:
        i = t_i[h]
        gmask = jnp.where((grp < C_OFF // 8) | (grp >= i + BAND // 8), NEG, 0.0).astype(BF16)
        band0 = pl.multiple_of(i * 8, 8)
        sc_ref[2 * h + g][...] = jnp.dot(kc_ref[...], jnp.concatenate([qpad[h, g], gmask], axis=0),
                                         preferred_element_type=F32)
        sc_ref[2 * h + g][pl.ds(band0, BAND), :] = sc_ref[2 * h + g][pl.ds(band0, BAND), :] + bc_ref[g]
    w_max = {}
    for h, g in chains:
        sw = jnp.dot(kw_ref[pl.ds(t_qs[h], TAIL), :], jnp.concatenate([qpad[h, g], neg_row], axis=0),
                     preferred_element_type=F32) + dw_ref[g]
        sw_ref[2 * h + g][...] = sw
        w_max[h, g] = jnp.max(sw, axis=0, keepdims=True)
    o_c, o_w, imp = {}, {}, {}
    for h, g in chains:
        s = sc_ref[2 * h + g][...]
        e = jnp.exp2(s - jnp.max(s, axis=0, keepdims=True))
        has_cmp = t_qs[h] + tq_col >= CMP_LEN - 1
        pinv = jnp.where(has_cmp, 1.0 / jnp.sum(e, axis=0, keepdims=True), 0.0)
        res = jnp.dot(vct_ref[...], e.astype(BF16), preferred_element_type=F32)
        o_c[h, g] = res[0:2 * HEAD_DIM, :] * pinv
        imp[h, g] = sum(res[2 * HEAD_DIM:, r * TQ:(r + 1) * TQ] * pinv[:, r * TQ:(r + 1) * TQ]
                        for r in range(GQA_REP))
    for h, g in chains:
        pw = jnp.exp2(sw_ref[2 * h + g][...] - w_max[h, g]).astype(BF16)
        acc_w = jnp.dot(vwt_ref[g, :, pl.ds(t_qs[h], TAIL)], pw, preferred_element_type=F32)
        o_w[h, g] = acc_w[0:HEAD_DIM, :] * (1.0 / acc_w[HEAD_DIM:HEAD_DIM + 1, :])

    score, mbt = {}, {}
    for h, g in chains:
        cur = (t_qs[h] + tq_lane) // SEL_BLOCK
        forced = (sidx == 0) | (sidx == cur) | (sidx == cur - 1)
        score[h, g] = jnp.where(forced, -jnp.inf, jnp.where(sidx <= cur, imp[h, g], NEG))
        mbt[h, g] = jnp.where(forced, 0.0, NEG)
    for _ in range(N_SEL - 3):
        for ch in chains:
            cm = jnp.max(score[ch], axis=0, keepdims=True)
            first = jnp.min(jnp.where(score[ch] == cm, sidx_f, float(N_SLC)), axis=0, keepdims=True)
            pick = sidx_f == first
            mbt[ch] = jnp.where(pick, 0.0, mbt[ch])
            score[ch] = jnp.where(pick, -jnp.inf, score[ch])

    for h, g in chains:
        qa_ref[2 * h + g] = jnp.concatenate(
            [jnp.concatenate([mbt[h, g].astype(BF16)] * GQA_REP, axis=1), qpad[h, g]], axis=0)
        mbt_tail = jnp.where(sidx < t_nfar[h] * (FAR // SEL_BLOCK), NEG, mbt[h, g]).astype(BF16)
        qa_tail = jnp.concatenate([jnp.concatenate([mbt_tail] * GQA_REP, axis=1), qpad[h, g]], axis=0)
        st = jnp.dot(kaug_ref[pl.ds(t_qs[h], TAIL), :], qa_tail, preferred_element_type=F32) + ds_ref[g]
        sc_ref[2 * h + g][...] = st
        m_ref[2 * h + g] = jnp.max(st, axis=0, keepdims=True)
    for h in tiles:
        scores(h, 0, 0)
    for h, g in chains:
        pt = jnp.exp2(sc_ref[2 * h + g][...] - m_ref[2 * h + g]).astype(BF16)
        acc_ref[2 * h + g] = jnp.dot(vst_ref[g, :, pl.ds(t_qs[h], TAIL)], pt, preferred_element_type=F32)

    def far_pair(kk, carry):
        c = 2 * kk
        for h in tiles:
            scores(h, c + 1, 1)
        for h in tiles:
            consume(h, t_nfar[h], c, 0)
        for h in tiles:
            scores(h, c + 2, 0)
        for h in tiles:
            consume(h, t_nfar[h], c + 1, 1)
        return carry

    lax.fori_loop(0, (t_nfar[TILES - 1] + 1) // 2, far_pair, 0)

    for h in tiles:
        lanes = slice(h * TQ, (h + 1) * TQ)
        out_rows = []
        for g in groups:
            dsl = slice(g * HEAD_DIM, (g + 1) * HEAD_DIM)
            hg = 2 * h + g
            o_s = acc_ref[hg, 0:HEAD_DIM, :] * (1.0 / acc_ref[hg, HEAD_DIM:HEAD_DIM + 1, :])
            for r in range(GQA_REP):
                cs = slice(r * TQ, (r + 1) * TQ)
                gc = 3 * (GQA_REP * g + r)
                out_rows.append(gt_ref[gc:gc + 1, lanes] * o_c[h, g][dsl, cs]
                                + gt_ref[gc + 1:gc + 2, lanes] * o_s[:, cs]
                                + gt_ref[gc + 2:gc + 3, lanes] * o_w[h, g][:, cs])
        o_ref[h * TQ:(h + 1) * TQ, :] = jnp.concatenate(out_rows, axis=0).T


def _attention(t_all, gates_t, kc2, vct, kaug, vst, kw_pad, vwt, ds, dw, bc, B, T):
    tp = T + WINDOW
    tqs = TQ * TILES
    per_b = lambda shape: pl.BlockSpec((None,) + shape, lambda b, i: (b,) + (0,) * len(shape),
                                      pipeline_mode=pl.Buffered(1))
    const = lambda a: pl.BlockSpec(a.shape, lambda b, i: (0,) * a.ndim, pipeline_mode=pl.Buffered(1))
    cols4 = GQA_REP * TQ
    n_hg = TILES * N_GROUPS_KV
    return pl.pallas_call(
        functools.partial(_attn_kernel, n_far_max=T // FAR),
        grid=(B, T // tqs),
        in_specs=[pl.BlockSpec((None, 512, tqs), lambda b, i: (b, 0, i)),
                  pl.BlockSpec((None, GATE_ROWS, tqs), lambda b, i: (b, 0, i)),
                  per_b((C_PAD, 256)),
                  per_b((256, C_PAD)),
                  per_b((tp, 256)),
                  per_b((N_GROUPS_KV, V_ROWS, tp)),
                  per_b((tp, 256)),
                  per_b((N_GROUPS_KV, V_ROWS, tp)),
                  const(ds), const(dw), const(bc)],
        out_specs=pl.BlockSpec((None, tqs, 512), lambda b, i: (b, i, 0)),
        out_shape=jax.ShapeDtypeStruct((B, T, 512), F32),
        scratch_shapes=([pltpu.VMEM((C_PAD, cols4), F32)] * n_hg
                        + [pltpu.VMEM((TAIL, cols4), F32)] * n_hg
                        + [pltpu.VMEM((n_hg, 256, cols4), BF16),
                           pltpu.VMEM((n_hg, 1, cols4), F32),
                           pltpu.VMEM((n_hg, V_ROWS, cols4), F32),
                           pltpu.VMEM((2 * n_hg, FAR, cols4), F32),
                           pltpu.VMEM((2 * n_hg, 1, cols4), F32)]),
        compiler_params=_cparams(("parallel", "arbitrary"), VMEM_LIMIT),
        name="attn",
    )(t_all, gates_t, kc2, vct, kaug, vst, kw_pad, vwt, ds, dw, bc)


def _rglru_kernel(*refs, L):
    J = L // 8
    n_col = D_RNN // 128
    rx_refs, ry_refs = refs[:n_col], refs[n_col:2 * n_col]
    (cw_ref, cb_ref, wa_ref, ba_ref, wx_ref, bx_ref, sp_ref, eu_ref, ed_ref, o_ref, eub_ref, edb_ref,
     xprev_ref, h_ref, os_ref) = refs[2 * n_col:]
    eub_ref[...] = eu_ref[...].astype(BF16)
    edb_ref[...] = ed_ref[...].astype(BF16)

    @pl.when(pl.program_id(1) == 0)
    def _():
        xprev_ref[...] = jnp.zeros(xprev_ref.shape, F32)
        h_ref[...] = jnp.zeros(h_ref.shape, F32)

    def strided(col_refs):
        return jnp.concatenate(
            [jnp.concatenate([ref[pl.ds(j, 8, stride=J), :] for j in range(J)], axis=0)
             for ref in col_refs], axis=1)

    xp = strided(rx_refs)
    yp = strided(ry_refs)
    sub = lax.broadcasted_iota(jnp.int32, (8, D_RNN), 0)

    def delayed(d):
        heads = []
        for j in range(d):
            src = pltpu.roll(xp[(J + j - d) * 8:(J + j - d + 1) * 8], 1, axis=0)
            heads.append(jnp.where(sub == 0, xprev_ref[8 + j - d:9 + j - d, :], src))
        return jnp.concatenate(heads + [xp[:L - 8 * d]], axis=0)

    xc = (cb_ref[...] + cw_ref[0:1, :] * delayed(3) + cw_ref[1:2, :] * delayed(2)
          + cw_ref[2:3, :] * delayed(1) + cw_ref[3:4, :] * xp)
    xprev_ref[...] = jnp.concatenate([ref[L - 8:L, :] for ref in rx_refs], axis=1)
    xcb = xc.astype(BF16)
    r = jax.nn.sigmoid(jnp.dot(xcb, wa_ref[...], preferred_element_type=F32) + ba_ref[...])
    ig = jax.nn.sigmoid(jnp.dot(xcb, wx_ref[...], preferred_element_type=F32) + bx_ref[...])
    a = jnp.exp(-RG_C * r * sp_ref[...])
    bt = jnp.sqrt(1.0 - a * a) * (ig * xc)

    h = jnp.zeros((8, D_RNN), F32)
    dec = jnp.ones((8, D_RNN), F32)
    hs, decs = [], []
    for j in range(J):
        aj = a[j * 8:(j + 1) * 8]
        h = aj * h + bt[j * 8:(j + 1) * 8]
        dec = aj * dec
        hs.append(h)
        decs.append(dec)
    carry = h_ref[0:1, :]
    carries = []
    for s in range(8):
        carries.append(carry)
        carry = dec[s:s + 1, :] * carry + h[s:s + 1, :]
    h_ref[...] = jnp.broadcast_to(carry, h_ref.shape)
    h_in = jnp.concatenate(carries, axis=0)
    cdf_c = math.sqrt(2.0 / math.pi)
    for j in range(J):
        y = yp[j * 8:(j + 1) * 8]
        cdf = 0.5 * (1.0 + jnp.tanh(cdf_c * (y + 0.044715 * (y * y * y))))
        res = (hs[j] + decs[j] * h_in) * (y * cdf)
        for c in range(n_col):
            os_ref[c, pl.ds(j, 8, stride=J), :] = res[:, c * 128:(c + 1) * 128]
    for c in range(n_col):
        o_ref[:, c * 128:(c + 1) * 128] = os_ref[c]


def _rglru(rx, ry, conv_w, conv_b, wa_bd, b_a, wx_bd, b_x, sp, w_up, w_dn, B, T):
    L = 512
    steps = B * T // L
    nt = T // L
    up_rows, dn_rows = w_up.shape[0] // steps, w_dn.shape[0] // steps
    assert up_rows * steps == w_up.shape[0] and dn_rows * steps == w_dn.shape[0] and dn_rows % 16 == 0
    slab = lambda rows, a: pl.BlockSpec((rows, a.shape[1]), lambda b, t: (b * nt + t, 0))
    blk = pl.BlockSpec((None, L, D_RNN), lambda b, t: (b, t, 0))
    n_col = D_RNN // 128
    cols = [pl.BlockSpec((None, L, 128), functools.partial(lambda b, t, c: (b, t, c), c=c))
            for c in range(n_col)]
    full = lambda a: pl.BlockSpec(a.shape, lambda b, t: (0,) * a.ndim)
    return pl.pallas_call(
        functools.partial(_rglru_kernel, L=L),
        grid=(B, T // L),
        in_specs=cols + cols + [full(conv_w), full(conv_b), full(wa_bd), full(b_a), full(wx_bd),
                                full(b_x), full(sp), slab(up_rows, w_up), slab(dn_rows, w_dn)],
        out_specs=[blk, slab(up_rows, w_up), slab(dn_rows, w_dn)],
        out_shape=[jax.ShapeDtypeStruct((B, T, D_RNN), F32),
                   jax.ShapeDtypeStruct(w_up.shape, BF16),
                   jax.ShapeDtypeStruct(w_dn.shape, BF16)],
        scratch_shapes=[pltpu.VMEM((8, D_RNN), F32), pltpu.VMEM((8, D_RNN), F32),
                        pltpu.VMEM((n_col, L, 128), F32)],
        compiler_params=_cparams(("parallel", "arbitrary"), VMEM_LIMIT),
        name="rglru",
    )(*([rx] * n_col + [ry] * n_col), conv_w, conv_b, wa_bd, b_a, wx_bd, b_x, sp, w_up, w_dn)


def _layer_norm(y, g, b):
    mu = jnp.mean(y, axis=-1, keepdims=True)
    d = y - mu
    var = jnp.mean(d * d, axis=-1, keepdims=True)
    return d * lax.rsqrt(var + LN_EPS) * g + b


def _outproj_kernel(at_ref, rn_ref, x_ref, ga_ref, gr_ref, wo_ref, g1_ref, b1_ref, wr2_ref, br_ref,
                    x1_ref, comb_ref):
    a = at_ref[...]
    rn = rn_ref[...]
    ha = a * lax.rsqrt(jnp.mean(a * a, axis=-1, keepdims=True) + RMS_EPS) * ga_ref[...]
    hr = rn * lax.rsqrt(jnp.mean(rn * rn, axis=-1, keepdims=True) + RMS_EPS) * gr_ref[...]
    heads = jnp.concatenate([ha, hr], axis=1).astype(BF16)
    mix = jnp.dot(heads, wo_ref[...], preferred_element_type=F32)
    x1 = _layer_norm(ALPHA * x_ref[...] + mix, g1_ref[...], b1_ref[...])
    x1_ref[...] = x1

    xh = x1.astype(BF16)
    xl = (x1 - xh.astype(F32)).astype(BF16)
    hh_hl = jnp.dot(xh, wr2_ref[...], preferred_element_type=F32)
    lg = (hh_hl[:, :128] + jnp.dot(xl, wr2_ref[:, :128], preferred_element_type=F32)
          + hh_hl[:, 128:]) + br_ref[...]
    lane = lax.broadcasted_iota(jnp.int32, lg.shape, 1)
    lane_f = lane.astype(F32)
    big = 1e9
    isg = lane < N_EXP_GROUPS
    gmax = jnp.max(jnp.where(isg, lg, -jnp.inf), axis=-1, keepdims=True)
    pg_top = 1.0 / jnp.sum(jnp.where(isg, jnp.exp(lg - gmax), 0.0), axis=-1, keepdims=True)
    gi = jnp.min(jnp.where(isg & (lg == gmax), lane_f, big), axis=-1, keepdims=True)
    egrp = ((lane - N_EXP_GROUPS) // EXPERTS_PER_GROUP).astype(F32)
    ise = (lane >= N_EXP_GROUPS) & (lane < N_EXP_GROUPS + N_EXPERTS) & (egrp == gi)
    emax = jnp.max(jnp.where(ise, lg, -jnp.inf), axis=-1, keepdims=True)
    i1 = jnp.min(jnp.where(ise & (lg == emax), lane_f, big), axis=-1, keepdims=True)
    rest = ise & (lane_f != i1)
    m2 = jnp.max(jnp.where(rest, lg, -jnp.inf), axis=-1, keepdims=True)
    i2 = jnp.min(jnp.where(rest & (lg == m2), lane_f, big), axis=-1, keepdims=True)
    e2 = jnp.exp(m2 - emax)
    inv = pg_top / (1.0 + e2)
    comb_ref[...] = jnp.where(lane_f == i1, inv, 0.0) + jnp.where(lane_f == i2, inv * e2, 0.0)


def _outproj(attn, rnn, x2, ga, gr, wo, g1, b1, wr2, br, N):
    tm = 512
    row = lambda w: pl.BlockSpec((tm, w), lambda i: (i, 0))
    full = lambda a: pl.BlockSpec(a.shape, lambda i: (0,) * a.ndim)
    return pl.pallas_call(
        _outproj_kernel,
        grid=(N // tm,),
        in_specs=[row(512), row(512), row(D_MODEL), full(ga), full(gr), full(wo), full(g1), full(b1),
                  full(wr2), full(br)],
        out_specs=[row(D_MODEL), row(128)],
        out_shape=[jax.ShapeDtypeStruct((N, D_MODEL), F32), jax.ShapeDtypeStruct((N, 128), F32)],
        compiler_params=_cparams(("parallel",), VMEM_LIMIT),
        name="outproj",
    )(attn, rnn, x2, ga, gr, wo, g1, b1, wr2, br)


def _moe_kernel(x1_ref, comb_ref, p_ref, tri_ref, wup_ref, wdn_ref, wg_ref, wp_ref, g2_ref, b2_ref, o_ref,
                acc_ref, xb_ref, rank_ref, rankt_ref, combt_ref):
    pi = pl.program_id(1)
    n_sub = x1_ref.shape[0] // MOE_SUB
    sub_rows = [slice(h * MOE_SUB, (h + 1) * MOE_SUB) for h in range(n_sub)]
    lane_col = N_EXP_GROUPS

    @pl.when(pi == 0)
    def _():
        x1 = x1_ref[...]
        xb = x1.astype(BF16)
        xb_ref[...] = xb
        gate = jax.nn.sigmoid(jnp.dot(xb, wg_ref[...], preferred_element_type=F32))
        ple = gate * jnp.dot(p_ref[...].astype(BF16), wp_ref[...], preferred_element_type=F32)
        acc_ref[...] = ALPHA * x1 + ple
        for rows in sub_rows:
            comb = comb_ref[rows, :]
            chosen = comb > 0.0
            rank = jnp.dot(tri_ref[...], jnp.where(chosen, 1.0, 0.0).astype(BF16),
                           preferred_element_type=F32)
            rank = jnp.where(chosen, rank, -1.0)
            rank_ref[rows, :] = rank
            rankt_ref[:, rows] = rank.T
            combt_ref[:, rows] = comb.T

    lane = lax.broadcasted_iota(jnp.int32, (MOE_SUB, 128), 1)
    slot_r = lax.broadcasted_iota(jnp.int32, (MOE_SLOTS, MOE_SUB), 0).astype(F32)
    slot_c = lax.broadcasted_iota(jnp.int32, (MOE_SUB, MOE_SLOTS), 1).astype(F32)

    def route_pair(base):
        chains = [(rows, k) for rows in sub_rows for k in range(2)]
        cmask, xc, u, hsw, y, ys, ct = {}, {}, {}, {}, {}, {}, {}
        for ch in chains:
            rows, k = ch
            rk_row = rankt_ref[pl.ds(2 * pi + k + lane_col, 1), rows] - base
            cmask[ch] = rk_row == slot_r
            xc[ch] = jnp.dot(jnp.where(cmask[ch], 1.0, 0.0).astype(BF16), xb_ref[rows, :],
                             preferred_element_type=F32).astype(BF16)
        for ch in chains:
            u[ch] = jnp.dot(xc[ch], wup_ref[ch[1]], preferred_element_type=F32)
        for ch in chains:
            ua = u[ch][:, :D_FF]
            hsw[ch] = (ua * jax.nn.sigmoid(ua) * u[ch][:, D_FF:]).astype(BF16)
        for ch in chains:
            y[ch] = jnp.dot(hsw[ch], wdn_ref[ch[1]], preferred_element_type=F32)
        for ch in chains:
            rows, k = ch
            e_lane = 2 * pi + k + lane_col
            w_row = combt_ref[pl.ds(e_lane, 1), rows]
            w_slot = jnp.sum(jnp.where(cmask[ch], w_row, 0.0), axis=1, keepdims=True)
            ys[ch] = (w_slot * y[ch]).astype(BF16)
            rk_col = jnp.sum(jnp.where(lane == e_lane, rank_ref[rows, :], 0.0), axis=1,
                             keepdims=True) - base
            ct[ch] = jnp.where(rk_col == slot_c, 1.0, 0.0).astype(BF16)
        for rows in sub_rows:
            pair = [(rows, k) for k in range(2)]
            ctp = jnp.concatenate([ct[ch] for ch in pair], axis=1)
            acc_ref[rows, :] = acc_ref[rows, :] + jnp.dot(
                ctp, jnp.concatenate([ys[ch] for ch in pair], axis=0), preferred_element_type=F32)

    route_pair(0.0)

    top_rank = jnp.max(jnp.maximum(rankt_ref[pl.ds(2 * pi + lane_col, 1), :],
                                   rankt_ref[pl.ds(2 * pi + 1 + lane_col, 1), :]))

    @pl.when(top_rank >= MOE_SLOTS)
    def _():
        def extra(sb, carry):
            route_pair((sb * MOE_SLOTS).astype(F32))
            return carry

        lax.fori_loop(1, top_rank.astype(jnp.int32) // MOE_SLOTS + 1, extra, 0)

    @pl.when(pi == N_EXPERTS // 2 - 1)
    def _():
        o_ref[...] = _layer_norm(acc_ref[...], g2_ref[...], b2_ref[...])


def _moe(x1, comb, p2, wup, wdn, wg, wp, g2, b2, N):
    tm = MOE_TILE
    tri = (jnp.arange(MOE_SUB)[None, :] < jnp.arange(MOE_SUB)[:, None]).astype(BF16)
    row = lambda w: pl.BlockSpec((tm, w), lambda i, e: (i, 0))
    full = lambda a: pl.BlockSpec(a.shape, lambda i, e: (0,) * a.ndim)
    return pl.pallas_call(
        _moe_kernel,
        grid=(N // tm, N_EXPERTS // 2),
        in_specs=[row(D_MODEL), row(128), row(D_PLE), full(tri),
                  pl.BlockSpec((2, D_MODEL, 2 * D_FF), lambda i, e: (e, 0, 0)),
                  pl.BlockSpec((2, D_FF, D_MODEL), lambda i, e: (e, 0, 0)),
                  full(wg), full(wp), full(g2), full(b2)],
        out_specs=row(D_MODEL),
        out_shape=jax.ShapeDtypeStruct((N, D_MODEL), F32),
        scratch_shapes=[pltpu.VMEM((tm, D_MODEL), F32),
                        pltpu.VMEM((tm, D_MODEL), BF16),
                        pltpu.VMEM((tm, 128), F32),
                        pltpu.VMEM((128, tm), F32),
                        pltpu.VMEM((128, tm), F32)],
        compiler_params=_cparams(("parallel", "arbitrary"), VMEM_LIMIT),
        name="moe",
    )(x1, comb, p2, tri, wup, wdn, wg, wp, g2, b2)


def _block_diag(w):
    n, k, j = w.shape
    eye = jnp.eye(n, dtype=w.dtype)
    return (w[:, :, None, :] * eye[:, None, :, None]).reshape(n * k, n * j)


def kernel(x, p, rel_bias, w_in, cmp_pe_k, cmp_pe_v, cmp_w_k, cmp_w_v, conv_w, conv_b, rg_w_a, rg_b_a,
           rg_w_x, rg_b_x, rg_lambda, attn_out_gain, rnn_out_gain, w_out, ln1_g, ln1_b, router_group_w,
           router_group_b, router_expert_w, router_expert_b, expert_w_up, expert_w_down, ple_w,
           ple_gate_w, ln2_g, ln2_b):
    B, T, _ = x.shape
    N = B * T
    assert T % FAR == 0 and T % (TQ * TILES) == 0 and T // SEL_BLOCK <= N_SLC and w_in.shape[0] == 1
    n_cmp = (T - CMP_LEN) // CMP_STRIDE + 1
    n_chunk = T // CMP_STRIDE
    assert C_OFF + n_chunk <= C_PAD
    row1 = lambda v: v.reshape(1, -1)

    w = w_in[0]
    c0 = D_ATTN
    kv = lambda k: w[:, c0 + k * D_KV:c0 + (k + 1) * D_KV]
    g0 = c0 + 6 * D_KV
    wq = w[:, :D_ATTN] * (HEAD_DIM ** -0.5 * LOG2E)
    w_main = jnp.concatenate([kv(2), kv(4), kv(0), kv(1), w[:, g0 + 24:]], axis=1).astype(BF16)
    w_gate = jnp.pad(w[:, g0:g0 + 24], ((0, 0), (0, GATE_ROWS - 24)))
    w_t = jnp.concatenate([wq, kv(3), kv(5), w_gate], axis=1).T.astype(BF16)

    x2 = x.reshape(N, D_MODEL)
    ks, kw, kc_raw, vc_raw, rx, ry, t_all, gates_t = _inproj(x2, w_main, w_t, B, T)

    def per_pos(wc):
        wl = wc.reshape(CMP_LEN, HEAD_DIM, HEAD_DIM)
        z = jnp.zeros_like(wl)
        return jnp.concatenate([jnp.concatenate([wl, z], axis=2), jnp.concatenate([z, wl], axis=2)],
                               axis=1).astype(BF16)

    pek = jnp.tile(cmp_pe_k[0], (1, N_GROUPS_KV))
    pev = jnp.tile(cmp_pe_v[0], (1, N_GROUPS_KV))
    grp_onehot = (jnp.arange(C_PAD)[:, None] // 8 == jnp.arange(128)[None, :]).astype(BF16)
    kc2, vct = _compress(kc_raw.reshape(B, T, 128), vc_raw.reshape(B, T, 128), pek, pev,
                         per_pos(cmp_w_k[0]), per_pos(cmp_w_v[0]), grp_onehot, B, T, n_cmp)

    onehot = (jnp.arange(T)[:, None] // SEL_BLOCK == jnp.arange(N_SLC)[None, :]).astype(BF16)
    kaug = jnp.concatenate([jnp.broadcast_to(onehot, (B, T, N_SLC)), ks.reshape(B, T, 128)], axis=2)
    lane256 = jnp.arange(256)
    pad_sel = jnp.broadcast_to((lane256 == N_SLC - 1).astype(BF16), (B, WINDOW, 256))
    pad_win = jnp.broadcast_to((lane256 == 128).astype(BF16), (B, WINDOW, 256))
    kaug = jnp.concatenate([pad_sel, kaug], axis=1)
    kw_pad = jnp.concatenate([pad_win, jnp.pad(kw.reshape(B, T, 128), ((0, 0), (0, 0), (0, 128)))], axis=1)
    ones_rows = jnp.concatenate([jnp.ones((B, N_GROUPS_KV, 1, T), BF16),
                                 jnp.zeros((B, N_GROUPS_KV, V_ROWS - HEAD_DIM - 1, T), BF16)], axis=2)

    def v_rows(vt):
        vt = jnp.concatenate([vt.reshape(B, N_GROUPS_KV, HEAD_DIM, T), ones_rows], axis=2)
        return jnp.pad(vt, ((0, 0), (0, 0), (0, 0), (WINDOW, 0)))

    vst = v_rows(t_all[:, 512:640])
    vwt = v_rows(t_all[:, 640:768])
    ds, dw, bc = _bias_tiles(rel_bias)

    cidx = jnp.arange(C_PAD) - C_OFF
    s_lo = jnp.arange(N_SLC)[:, None] * SEL_BLOCK
    overlap = ((cidx[None, :] * CMP_STRIDE < s_lo + SEL_BLOCK) & (cidx[None, :] * CMP_STRIDE + CMP_LEN > s_lo)
               & (cidx[None, :] >= 0) & (cidx[None, :] < n_cmp)).astype(BF16)
    vct = jnp.concatenate([vct, jnp.broadcast_to(overlap, (B, N_SLC, C_PAD))], axis=1)
    attn = _attention(t_all, gates_t, kc2, vct, kaug, vst, kw_pad, vwt, ds, dw, bc, B, T)

    sp = jax.nn.softplus(-rg_lambda[0].astype(F32))
    rnn, wup_b, wdn_b = _rglru(rx.reshape(B, T, D_RNN), ry.reshape(B, T, D_RNN), conv_w[0], row1(conv_b[0]),
                               _block_diag(rg_w_a[0]).astype(BF16), row1(rg_b_a[0]),
                               _block_diag(rg_w_x[0]).astype(BF16), row1(rg_b_x[0]), row1(sp),
                               expert_w_up[0].reshape(N_EXPERTS * D_MODEL, 2 * D_FF),
                               expert_w_down[0].reshape(N_EXPERTS * D_FF, D_MODEL), B, T)

    wr = jnp.pad(jnp.concatenate([router_group_w[0], router_expert_w[0]], axis=1), ((0, 0), (0, 108)))
    wrh = wr.astype(BF16)
    wr2 = jnp.concatenate([wrh, (wr - wrh.astype(F32)).astype(BF16)], axis=1)
    br = jnp.pad(jnp.concatenate([router_group_b[0], router_expert_b[0]]), (0, 108)).reshape(1, 128)
    x1, comb = _outproj(attn.reshape(N, 512), rnn.reshape(N, D_RNN), x2, row1(attn_out_gain[0]),
                        row1(rnn_out_gain[0]), w_out[0].astype(BF16), row1(ln1_g[0]), row1(ln1_b[0]),
                        wr2, br, N)

    out = _moe(x1, comb, p[0].reshape(N, D_PLE), wup_b.reshape(N_EXPERTS, D_MODEL, 2 * D_FF),
               wdn_b.reshape(N_EXPERTS, D_FF, D_MODEL), ple_gate_w[0].astype(BF16), ple_w[0].astype(BF16),
               row1(ln2_g[0]), row1(ln2_b[0]), N)
    return out.reshape(B, T, D_MODEL)
```

```python
import functools
import math

import jax
import jax.numpy as jnp
from jax import lax
from jax.experimental import pallas as pl
from jax.experimental.pallas import tpu as pltpu

F32 = jnp.float32
BF16 = jnp.bfloat16
NEG = -1e30
LOG2E = 1.4426950408889634

D_MODEL = 1024
HEAD_DIM = 64
N_HEADS = 8
N_GROUPS_KV = 2
GQA_REP = 4
D_ATTN = 512
D_RNN = 512
D_KV = 128
CMP_LEN = 32
CMP_STRIDE = 16
SEL_BLOCK = 64
N_SEL = 16
WINDOW = 512
N_BUCKETS = 32
MAX_DISTANCE = 128
N_EXPERTS = 16
EXPERTS_PER_GROUP = 4
N_EXP_GROUPS = 4
D_FF = 512
D_PLE = 256
ALPHA = 2.0 ** 0.25
LN_EPS = 1e-5
RMS_EPS = 1e-6
FORCE_BONUS = 1e4
RG_C = 8.0

TQ = 128
TILES = 2
N_SLC = 128
C_PAD = 640
C_OFF = 16
FAR = 512
TAIL = WINDOW + TQ
BAND = 24
GATE_ROWS = 32
V_ROWS = 80
MOE_TILE = 1024
MOE_SUB = 512
MOE_SLOTS = 128
VMEM_LIMIT = 56 * 1024 * 1024


def _cparams(sem, vmem=None):
    return pltpu.CompilerParams(dimension_semantics=sem, vmem_limit_bytes=vmem)


def _inproj_kernel(x_ref, wm_ref, wt_ref, ka_ref, kw_ref, kc_ref, vc_ref, rx_ref, ry_ref, t_ref, vs_ref,
                   vw_ref, gt_ref):
    tm = x_ref.shape[0]
    j = pl.program_id(1)
    ones_rows = jnp.where(lax.broadcasted_iota(jnp.int32, (V_ROWS - HEAD_DIM, tm), 0) == 0, 1.0, 0.0).astype(BF16)
    lane = lax.broadcasted_iota(jnp.int32, (tm, 2 * N_SLC), 1)

    @pl.when(j == 0)
    def _():
        ka_ref[...] = jnp.where(lane == N_SLC - 1, 1.0, 0.0).astype(BF16)
        kw_ref[...] = jnp.where(lane == N_SLC, 1.0, 0.0).astype(BF16)
        for v_ref in (vs_ref, vw_ref):
            for g in range(N_GROUPS_KV):
                v_ref[g, :HEAD_DIM, :] = jnp.zeros((HEAD_DIM, tm), BF16)
                v_ref[g, HEAD_DIM:, :] = ones_rows

    @pl.when(j > 0)
    def _():
        _inproj_tile(x_ref, wm_ref, wt_ref, ka_ref, kw_ref, kc_ref, vc_ref, rx_ref, ry_ref, t_ref, vs_ref,
                     vw_ref, gt_ref, (j - 1) * tm, ones_rows)


def _inproj_tile(x_ref, wm_ref, wt_ref, ka_ref, kw_ref, kc_ref, vc_ref, rx_ref, ry_ref, t_ref, vs_ref, vw_ref,
                 gt_ref, t0, ones_rows):
    tm = x_ref.shape[0]
    xb = x_ref[...].astype(BF16)

    def mm(lo, hi):
        return jnp.dot(xb, wm_ref[:, lo:hi], preferred_element_type=F32)

    k_pair = mm(0, 256)
    blk = (t0 + lax.broadcasted_iota(jnp.int32, (tm, N_SLC), 0)) // SEL_BLOCK
    ka_ref[:, :N_SLC] = jnp.where(blk == lax.broadcasted_iota(jnp.int32, (tm, N_SLC), 1), 1.0, 0.0).astype(BF16)
    ka_ref[:, N_SLC:] = k_pair[:, :128].astype(BF16)
    kw_ref[:, :N_SLC] = k_pair[:, 128:].astype(BF16)
    kw_ref[:, N_SLC:] = jnp.zeros((tm, N_SLC), BF16)
    c_pair = mm(256, 512)
    kc_ref[...] = c_pair[:, :128]
    vc_ref[...] = c_pair[:, 128:]
    rx_ref[...] = mm(512, 1024)
    ry_ref[...] = mm(1024, 1536)
    tr = lax.dot_general(wt_ref[...], xb, (((1,), (1,)), ((), ())), preferred_element_type=F32)
    t_ref[...] = tr[:D_ATTN].astype(BF16)
    for v_ref, base in ((vs_ref, D_ATTN), (vw_ref, D_ATTN + D_KV)):
        for g in range(N_GROUPS_KV):
            v_ref[g, :HEAD_DIM, :] = tr[base + g * HEAD_DIM:base + (g + 1) * HEAD_DIM].astype(BF16)
            v_ref[g, HEAD_DIM:, :] = ones_rows
    gt_ref[...] = jax.nn.sigmoid(tr[D_ATTN + 2 * D_KV:])


def _inproj(x2, w_main, w_t, B, T):
    N = B * T
    tm = WINDOW
    nt = T // tm
    tp = T + WINDOW
    data = lambda j: jnp.maximum(j - 1, 0)
    row = lambda w: pl.BlockSpec((tm, w), lambda b, j: (b * nt + data(j), 0))
    full = lambda a: pl.BlockSpec(a.shape, lambda b, j: (0,) * a.ndim)
    tr = lambda r: pl.BlockSpec((None, r, tm), lambda b, j: (b, 0, data(j)))
    lead = pl.BlockSpec((None, tm, 2 * N_SLC), lambda b, j: (b, j, 0))
    lead_t = pl.BlockSpec((None, N_GROUPS_KV, V_ROWS, tm), lambda b, j: (b, 0, 0, j))
    return pl.pallas_call(
        _inproj_kernel,
        grid=(B, nt + 1),
        in_specs=[row(D_MODEL), full(w_main), full(w_t)],
        out_specs=[lead, lead, row(128), row(128), row(512), row(512), tr(D_ATTN), lead_t, lead_t,
                   tr(GATE_ROWS)],
        out_shape=[jax.ShapeDtypeStruct((B, tp, 2 * N_SLC), BF16),
                   jax.ShapeDtypeStruct((B, tp, 2 * N_SLC), BF16),
                   jax.ShapeDtypeStruct((N, 128), F32),
                   jax.ShapeDtypeStruct((N, 128), F32),
                   jax.ShapeDtypeStruct((N, 512), F32),
                   jax.ShapeDtypeStruct((N, 512), F32),
                   jax.ShapeDtypeStruct((B, D_ATTN, T), BF16),
                   jax.ShapeDtypeStruct((B, N_GROUPS_KV, V_ROWS, tp), BF16),
                   jax.ShapeDtypeStruct((B, N_GROUPS_KV, V_ROWS, tp), BF16),
                   jax.ShapeDtypeStruct((B, GATE_ROWS, T), F32)],
        compiler_params=_cparams(("parallel", "arbitrary"), VMEM_LIMIT),
        name="inproj",
    )(x2, w_main, w_t)


def _compress_kernel(kr_ref, vr_ref, pek_ref, pev_ref, wk_ref, wv_ref, oh_ref, kc_ref, vct_ref, *,
                     n_cmp, n_chunk):
    def compress(raw_ref, pe_ref, w_ref):
        lo = jnp.zeros((n_chunk, 2 * HEAD_DIM), F32)
        hi = jnp.zeros((n_chunk, 2 * HEAD_DIM), F32)
        for j in range(CMP_STRIDE):
            a = raw_ref[pl.ds(j, n_chunk, stride=CMP_STRIDE), :]
            lo = lo + jnp.dot((a + pe_ref[j:j + 1, :]).astype(BF16), w_ref[j],
                              preferred_element_type=F32)
            hi = hi + jnp.dot((a + pe_ref[CMP_STRIDE + j:CMP_STRIDE + j + 1, :]).astype(BF16),
                              w_ref[CMP_STRIDE + j], preferred_element_type=F32)
        hi = pltpu.roll(hi, n_chunk - 1, axis=0)
        rid = lax.broadcasted_iota(jnp.int32, (n_chunk, 2 * HEAD_DIM), 0)
        out = jnp.where(rid < n_cmp, lo + hi, 0.0)
        return jnp.concatenate([jnp.zeros((C_OFF, 2 * HEAD_DIM), F32), out,
                                jnp.zeros((C_PAD - C_OFF - n_chunk, 2 * HEAD_DIM), F32)], axis=0)

    kc_ref[:, 0:2 * HEAD_DIM] = compress(kr_ref, pek_ref, wk_ref).astype(BF16)
    kc_ref[:, 2 * HEAD_DIM:] = oh_ref[...]
    vct_ref[...] = compress(vr_ref, pev_ref, wv_ref).T.astype(BF16)


def _compress(kc_raw, vc_raw, pek, pev, wk, wv, grp_onehot, B, T, n_cmp):
    n_chunk = T // CMP_STRIDE
    blk = pl.BlockSpec((None, T, 2 * HEAD_DIM), lambda b: (b, 0, 0))
    full = lambda a: pl.BlockSpec(a.shape, lambda b: (0,) * a.ndim)
    return pl.pallas_call(
        functools.partial(_compress_kernel, n_cmp=n_cmp, n_chunk=n_chunk),
        grid=(B,),
        in_specs=[blk, blk, full(pek), full(pev), full(wk), full(wv), full(grp_onehot)],
        out_specs=[pl.BlockSpec((None, C_PAD, 256), lambda b: (b, 0, 0)),
                   pl.BlockSpec((None, 2 * HEAD_DIM, C_PAD), lambda b: (b, 0, 0))],
        out_shape=[jax.ShapeDtypeStruct((B, C_PAD, 256), BF16),
                   jax.ShapeDtypeStruct((B, 2 * HEAD_DIM, C_PAD), BF16)],
        compiler_params=_cparams(("parallel",), VMEM_LIMIT),
        name="compress",
    )(kc_raw, vc_raw, pek, pev, wk, wv, grp_onehot)


def _bias_kernel(tab_ref, dt_ref, bt_ref, dc_ref, bc_ref, os_ref, ow_ref, oc_ref):
    h = pl.program_id(0) * GQA_REP + pl.program_id(1)

    def lookup(bk):
        out = jnp.zeros(bk.shape, F32)
        for b in range(N_BUCKETS):
            out = jnp.where(bk == b, tab_ref[h, b], out)
        return (out - tab_ref[h, N_BUCKETS - 1]) * LOG2E

    dt = dt_ref[...]
    v = lookup(bt_ref[...])
    os_ref[...] = jnp.where(dt >= 0, v, NEG)
    ow_ref[...] = jnp.where((dt >= 0) & (dt < WINDOW), v, NEG)
    oc_ref[...] = jnp.where(dc_ref[...] >= 0, lookup(bc_ref[...]), NEG)


def _t5_bucket(dist):
    max_exact = N_BUCKETS // 2
    d = jnp.maximum(dist, 0)
    df = jnp.maximum(d, 1).astype(F32)
    large = max_exact + (jnp.log(df / max_exact) / math.log(MAX_DISTANCE / max_exact)
                         * (N_BUCKETS - max_exact)).astype(jnp.int32)
    large = jnp.minimum(large, N_BUCKETS - 1)
    return jnp.where(d < max_exact, d, large)


def _bias_tiles(rel_bias):
    tq = jnp.arange(TQ, dtype=jnp.int32)
    dt = tq[None, :] + WINDOW - jnp.arange(TAIL, dtype=jnp.int32)[:, None]
    cc = jnp.arange(BAND, dtype=jnp.int32) - C_OFF
    dc = tq[None, :] - (cc[:, None] * CMP_STRIDE + CMP_LEN - 1)
    full = lambda a: pl.BlockSpec(a.shape, lambda g, r: (0,) * a.ndim)
    out = lambda rows: pl.BlockSpec((None, rows, TQ), lambda g, r: (g, 0, r))
    return pl.pallas_call(
        _bias_kernel,
        grid=(N_GROUPS_KV, GQA_REP),
        in_specs=[pl.BlockSpec(memory_space=pltpu.SMEM), full(dt), full(dt), full(dc), full(dc)],
        out_specs=[out(TAIL), out(TAIL), out(BAND)],
        out_shape=[jax.ShapeDtypeStruct((N_GROUPS_KV, TAIL, GQA_REP * TQ), F32),
                   jax.ShapeDtypeStruct((N_GROUPS_KV, TAIL, GQA_REP * TQ), F32),
                   jax.ShapeDtypeStruct((N_GROUPS_KV, BAND, GQA_REP * TQ), F32)],
        compiler_params=_cparams(("arbitrary", "arbitrary")),
        name="biastile",
    )(rel_bias.T, dt, _t5_bucket(dt), dc, _t5_bucket(dc))


def _attn_kernel(qt_ref, gt_ref, kc_ref, vct_ref, kaug_ref, vst_ref, kw_ref, vwt_ref, ds_ref, dw_ref,
                 bc_ref, o_ref, *scratch, n_far_max):
    n_hg = TILES * N_GROUPS_KV
    sc_ref, sw_ref = (scratch[n_hg * k:n_hg * (k + 1)] for k in range(2))
    qa_ref, m_ref, acc_ref, sb_ref, sm_ref = scratch[2 * n_hg:]
    cols4 = GQA_REP * TQ
    groups = range(N_GROUPS_KV)
    zeros_q = jnp.zeros((HEAD_DIM, cols4), BF16)
    grp = lax.broadcasted_iota(jnp.int32, (128, cols4), 0)
    neg_row = jnp.where(grp == 0, NEG, 0.0).astype(BF16)
    tq_col = lax.broadcasted_iota(jnp.int32, (1, cols4), 1) % TQ
    sidx = lax.broadcasted_iota(jnp.int32, (N_SLC, TQ), 0)
    sidx_f = sidx.astype(F32)
    tq_lane = lax.broadcasted_iota(jnp.int32, (N_SLC, TQ), 1)

    def chunk_start(c):
        return pl.multiple_of(WINDOW + jnp.minimum(c, n_far_max - 1) * FAR, FAR)

    def scores(h, c, par):
        kch = kaug_ref[pl.ds(chunk_start(c), FAR), :]
        for g in groups:
            sf = jnp.dot(kch, qa_ref[2 * h + g], preferred_element_type=F32)
            sb_ref[4 * h + 2 * g + par] = sf
            sm_ref[4 * h + 2 * g + par] = jnp.max(sf, axis=0, keepdims=True)

    def consume(h, nfar, c, par):
        c0 = chunk_start(c)
        live = c < nfar
        for g in groups:
            hg = 2 * h + g
            sf = sb_ref[4 * h + 2 * g + par]
            mprev = m_ref[hg]
            mcand = jnp.maximum(mprev, sm_ref[4 * h + 2 * g + par])
            mnew = jnp.where(live, mcand, mprev)
            pf = jnp.exp2(sf - jnp.where(live, mcand, -NEG)).astype(BF16)
            acc_ref[hg] = jnp.exp2(mprev - mnew) * acc_ref[hg] + jnp.dot(
                vst_ref[g, :, pl.ds(c0, FAR)], pf, preferred_element_type=F32)
            m_ref[hg] = mnew

    tiles = range(TILES)
    chains = [(h, g) for h in tiles for g in groups]
    t_i = [pl.program_id(1) * TILES + h for h in tiles]
    t_qs = [pl.multiple_of(i * TQ, TQ) for i in t_i]
    t_nfar = [jnp.maximum(i - 1, 0) // 4 for i in t_i]
    qpad = {}
    for h, g in chains:
        qgt = jnp.concatenate(
            [qt_ref[(GQA_REP * g + r) * HEAD_DIM:(GQA_REP * g + r + 1) * HEAD_DIM, h * TQ:(h + 1) * TQ]
             for r in range(GQA_REP)], axis=1)
        qpad[h, g] = jnp.concatenate([qgt, zeros_q] if g == 0 else [zeros_q, qgt], axis=0)

    for h, g in chains:
        i = t_i[h]
        gmask = jnp.where((grp < C_OFF // 8) | (grp >= i + BAND // 8), NEG, 0.0).astype(BF16)
        band0 = pl.multiple_of(i * 8, 8)
        sc_ref[2 * h + g][...] = jnp.dot(kc_ref[...], jnp.concatenate([qpad[h, g], gmask], axis=0),
                                         preferred_element_type=F32)
        sc_ref[2 * h + g][pl.ds(band0, BAND), :] = sc_ref[2 * h + g][pl.ds(band0, BAND), :] + bc_ref[g]
    w_max = {}
    for h, g in chains:
        sw = jnp.dot(kw_ref[pl.ds(t_qs[h], TAIL), :], jnp.concatenate([qpad[h, g], neg_row], axis=0),
                     preferred_element_type=F32) + dw_ref[g]
        sw_ref[2 * h + g][...] = sw
        w_max[h, g] = jnp.max(sw, axis=0, keepdims=True)
    o_c, o_w, imp = {}, {}, {}
    for h, g in chains:
        s = sc_ref[2 * h + g][...]
        e = jnp.exp2(s - jnp.max(s, axis=0, keepdims=True))
        has_cmp = t_qs[h] + tq_col >= CMP_LEN - 1
        pinv = jnp.where(has_cmp, 1.0 / jnp.sum(e, axis=0, keepdims=True), 0.0)
        res = jnp.dot(vct_ref[...], e.astype(BF16), preferred_element_type=F32)
        o_c[h, g] = res[0:2 * HEAD_DIM, :] * pinv
        imp[h, g] = sum(res[2 * HEAD_DIM:, r * TQ:(r + 1) * TQ] * pinv[:, r * TQ:(r + 1) * TQ]
                        for r in range(GQA_REP))
    for h, g in chains:
        pw = jnp.exp2(sw_ref[2 * h + g][...] - w_max[h, g]).astype(BF16)
        acc_w = jnp.dot(vwt_ref[g, :, pl.ds(t_qs[h], TAIL)], pw, preferred_element_type=F32)
        o_w[h, g] = acc_w[0:HEAD_DIM, :] * (1.0 / acc_w[HEAD_DIM:HEAD_DIM + 1, :])

    score, mbt = {}, {}
    for h, g in chains:
        cur = (t_qs[h] + tq_lane) // SEL_BLOCK
        forced = (sidx == 0) | (sidx == cur) | (sidx == cur - 1)
        score[h, g] = jnp.where(forced, -jnp.inf, jnp.where(sidx <= cur, imp[h, g], NEG))
        mbt[h, g] = jnp.where(forced, 0.0, NEG)
    for _ in range(N_SEL - 3):
        for ch in chains:
            cm = jnp.max(score[ch], axis=0, keepdims=True)
            first = jnp.min(jnp.where(score[ch] == cm, sidx_f, float(N_SLC)), axis=0, keepdims=True)
            pick = sidx_f == first
            mbt[ch] = jnp.where(pick, 0.0, mbt[ch])
            score[ch] = jnp.where(pick, -jnp.inf, score[ch])

    for h, g in chains:
        qa_ref[2 * h + g] = jnp.concatenate(
            [jnp.concatenate([mbt[h, g].astype(BF16)] * GQA_REP, axis=1), qpad[h, g]], axis=0)
        mbt_tail = jnp.where(sidx < t_nfar[h] * (FAR // SEL_BLOCK), NEG, mbt[h, g]).astype(BF16)
        qa_tail = jnp.concatenate([jnp.concatenate([mbt_tail] * GQA_REP, axis=1), qpad[h, g]], axis=0)
        st = jnp.dot(kaug_ref[pl.ds(t_qs[h], TAIL), :], qa_tail, preferred_element_type=F32) + ds_ref[g]
        sc_ref[2 * h + g][...] = st
        m_ref[2 * h + g] = jnp.max(st, axis=0, keepdims=True)
    for h in tiles:
        scores(h, 0, 0)
    for h, g in chains:
        pt = jnp.exp2(sc_ref[2 * h + g][...] - m_ref[2 * h + g]).astype(BF16)
        acc_ref[2 * h + g] = jnp.dot(vst_ref[g, :, pl.ds(t_qs[h], TAIL)], pt, preferred_element_type=F32)

    def far_pair(kk, carry):
        c = 2 * kk
        for h in tiles:
            scores(h, c + 1, 1)
        for h in tiles:
            consume(h, t_nfar[h], c, 0)
        for h in tiles:
            scores(h, c + 2, 0)
        for h in tiles:
            consume(h, t_nfar[h], c + 1, 1)
        return carry

    lax.fori_loop(0, (t_nfar[TILES - 1] + 1) // 2, far_pair, 0)

    for h in tiles:
        lanes = slice(h * TQ, (h + 1) * TQ)
        out_rows = []
        for g in groups:
            dsl = slice(g * HEAD_DIM, (g + 1) * HEAD_DIM)
            hg = 2 * h + g
            o_s = acc_ref[hg, 0:HEAD_DIM, :] * (1.0 / acc_ref[hg, HEAD_DIM:HEAD_DIM + 1, :])
            for r in range(GQA_REP):
                cs = slice(r * TQ, (r + 1) * TQ)
                gc = 3 * (GQA_REP * g + r)
                out_rows.append(gt_ref[gc:gc + 1, lanes] * o_c[h, g][dsl, cs]
                                + gt_ref[gc + 1:gc + 2, lanes] * o_s[:, cs]
                                + gt_ref[gc + 2:gc + 3, lanes] * o_w[h, g][:, cs])
        o_ref[h * TQ:(h + 1) * TQ, :] = jnp.concatenate(out_rows, axis=0).T


def _attention(t_all, gates_t, kc2, vct, kaug, vst, kw_pad, vwt, ds, dw, bc, B, T):
    tp = T + WINDOW
    tqs = TQ * TILES
    per_b = lambda shape: pl.BlockSpec((None,) + shape, lambda b, i: (b,) + (0,) * len(shape),
                                      pipeline_mode=pl.Buffered(1))
    const = lambda a: pl.BlockSpec(a.shape, lambda b, i: (0,) * a.ndim, pipeline_mode=pl.Buffered(1))
    cols4 = GQA_REP * TQ
    n_hg = TILES * N_GROUPS_KV
    return pl.pallas_call(
        functools.partial(_attn_kernel, n_far_max=T // FAR),
        grid=(B, T // tqs),
        in_specs=[pl.BlockSpec((None, 512, tqs), lambda b, i: (b, 0, i)),
                  pl.BlockSpec((None, GATE_ROWS, tqs), lambda b, i: (b, 0, i)),
                  per_b((C_PAD, 256)),
                  per_b((256, C_PAD)),
                  per_b((tp, 256)),
                  per_b((N_GROUPS_KV, V_ROWS, tp)),
                  per_b((tp, 256)),
                  per_b((N_GROUPS_KV, V_ROWS, tp)),
                  const(ds), const(dw), const(bc)],
        out_specs=pl.BlockSpec((None, tqs, 512), lambda b, i: (b, i, 0)),
        out_shape=jax.ShapeDtypeStruct((B, T, 512), F32),
        scratch_shapes=([pltpu.VMEM((C_PAD, cols4), F32)] * n_hg
                        + [pltpu.VMEM((TAIL, cols4), F32)] * n_hg
                        + [pltpu.VMEM((n_hg, 256, cols4), BF16),
                           pltpu.VMEM((n_hg, 1, cols4), F32),
                           pltpu.VMEM((n_hg, V_ROWS, cols4), F32),
                           pltpu.VMEM((2 * n_hg, FAR, cols4), F32),
                           pltpu.VMEM((2 * n_hg, 1, cols4), F32)]),
        compiler_params=_cparams(("parallel", "arbitrary"), VMEM_LIMIT),
        name="attn",
    )(t_all, gates_t, kc2, vct, kaug, vst, kw_pad, vwt, ds, dw, bc)


def _rglru_kernel(*refs, L):
    J = L // 8
    n_col = D_RNN // 128
    rx_refs, ry_refs = refs[:n_col], refs[n_col:2 * n_col]
    (cw_ref, cb_ref, wa_ref, ba_ref, wx_ref, bx_ref, sp_ref, eu_ref, ed_ref, o_ref, eub_ref, edb_ref,
     xprev_ref, h_ref, os_ref) = refs[2 * n_col:]
    eub_ref[...] = eu_ref[...].astype(BF16)
    edb_ref[...] = ed_ref[...].astype(BF16)

    @pl.when(pl.program_id(1) == 0)
    def _():
        xprev_ref[...] = jnp.zeros(xprev_ref.shape, F32)
        h_ref[...] = jnp.zeros(h_ref.shape, F32)

    def strided(col_refs):
        return jnp.concatenate(
            [jnp.concatenate([ref[pl.ds(j, 8, stride=J), :] for j in range(J)], axis=0)
             for ref in col_refs], axis=1)

    xp = strided(rx_refs)
    yp = strided(ry_refs)
    sub = lax.broadcasted_iota(jnp.int32, (8, D_RNN), 0)

    def delayed(d):
        heads = []
        for j in range(d):
            src = pltpu.roll(xp[(J + j - d) * 8:(J + j - d + 1) * 8], 1, axis=0)
            heads.append(jnp.where(sub == 0, xprev_ref[8 + j - d:9 + j - d, :], src))
        return jnp.concatenate(heads + [xp[:L - 8 * d]], axis=0)

    xc = (cb_ref[...] + cw_ref[0:1, :] * delayed(3) + cw_ref[1:2, :] * delayed(2)
          + cw_ref[2:3, :] * delayed(1) + cw_ref[3:4, :] * xp)
    xprev_ref[...] = jnp.concatenate([ref[L - 8:L, :] for ref in rx_refs], axis=1)
    xcb = xc.astype(BF16)
    r = jax.nn.sigmoid(jnp.dot(xcb, wa_ref[...], preferred_element_type=F32) + ba_ref[...])
    ig = jax.nn.sigmoid(jnp.dot(xcb, wx_ref[...], preferred_element_type=F32) + bx_ref[...])
    a = jnp.exp(-RG_C * r * sp_ref[...])
    bt = jnp.sqrt(1.0 - a * a) * (ig * xc)

    h = jnp.zeros((8, D_RNN), F32)
    dec = jnp.ones((8, D_RNN), F32)
    hs, decs = [], []
    for j in range(J):
        aj = a[j * 8:(j + 1) * 8]
        h = aj * h + bt[j * 8:(j + 1) * 8]
        dec = aj * dec
        hs.append(h)
        decs.append(dec)
    carry = h_ref[0:1, :]
    carries = []
    for s in range(8):
        carries.append(carry)
        carry = dec[s:s + 1, :] * carry + h[s:s + 1, :]
    h_ref[...] = jnp.broadcast_to(carry, h_ref.shape)
    h_in = jnp.concatenate(carries, axis=0)
    cdf_c = math.sqrt(2.0 / math.pi)
    for j in range(J):
        y = yp[j * 8:(j + 1) * 8]
        cdf = 0.5 * (1.0 + jnp.tanh(cdf_c * (y + 0.044715 * (y * y * y))))
        res = (hs[j] + decs[j] * h_in) * (y * cdf)
        for c in range(n_col):
            os_ref[c, pl.ds(j, 8, stride=J), :] = res[:, c * 128:(c + 1) * 128]
    for c in range(n_col):
        o_ref[:, c * 128:(c + 1) * 128] = os_ref[c]


def _rglru(rx, ry, conv_w, conv_b, wa_bd, b_a, wx_bd, b_x, sp, w_up, w_dn, B, T):
    L = 512
    steps = B * T // L
    nt = T // L
    up_rows, dn_rows = w_up.shape[0] // steps, w_dn.shape[0] // steps
    assert up_rows * steps == w_up.shape[0] and dn_rows * steps == w_dn.shape[0] and dn_rows % 16 == 0
    slab = lambda rows, a: pl.BlockSpec((rows, a.shape[1]), lambda b, t: (b * nt + t, 0))
    blk = pl.BlockSpec((None, L, D_RNN), lambda b, t: (b, t, 0))
    n_col = D_RNN // 128
    cols = [pl.BlockSpec((None, L, 128), functools.partial(lambda b, t, c: (b, t, c), c=c))
            for c in range(n_col)]
    full = lambda a: pl.BlockSpec(a.shape, lambda b, t: (0,) * a.ndim)
    return pl.pallas_call(
        functools.partial(_rglru_kernel, L=L),
        grid=(B, T // L),
        in_specs=cols + cols + [full(conv_w), full(conv_b), full(wa_bd), full(b_a), full(wx_bd),
                                full(b_x), full(sp), slab(up_rows, w_up), slab(dn_rows, w_dn)],
        out_specs=[blk, slab(up_rows, w_up), slab(dn_rows, w_dn)],
        out_shape=[jax.ShapeDtypeStruct((B, T, D_RNN), F32),
                   jax.ShapeDtypeStruct(w_up.shape, BF16),
                   jax.ShapeDtypeStruct(w_dn.shape, BF16)],
        scratch_shapes=[pltpu.VMEM((8, D_RNN), F32), pltpu.VMEM((8, D_RNN), F32),
                        pltpu.VMEM((n_col, L, 128), F32)],
        compiler_params=_cparams(("parallel", "arbitrary"), VMEM_LIMIT),
        name="rglru",
    )(*([rx] * n_col + [ry] * n_col), conv_w, conv_b, wa_bd, b_a, wx_bd, b_x, sp, w_up, w_dn)


def _layer_norm(y, g, b):
    mu = jnp.mean(y, axis=-1, keepdims=True)
    d = y - mu
    var = jnp.mean(d * d, axis=-1, keepdims=True)
    return d * lax.rsqrt(var + LN_EPS) * g + b


def _outproj_kernel(at_ref, rn_ref, x_ref, ga_ref, gr_ref, wo_ref, g1_ref, b1_ref, wr2_ref, br_ref,
                    x1_ref, comb_ref):
    a = at_ref[...]
    rn = rn_ref[...]
    ha = a * lax.rsqrt(jnp.mean(a * a, axis=-1, keepdims=True) + RMS_EPS) * ga_ref[...]
    hr = rn * lax.rsqrt(jnp.mean(rn * rn, axis=-1, keepdims=True) + RMS_EPS) * gr_ref[...]
    heads = jnp.concatenate([ha, hr], axis=1).astype(BF16)
    mix = jnp.dot(heads, wo_ref[...], preferred_element_type=F32)
    x1 = _layer_norm(ALPHA * x_ref[...] + mix, g1_ref[...], b1_ref[...])
    x1_ref[...] = x1

    xh = x1.astype(BF16)
    xl = (x1 - xh.astype(F32)).astype(BF16)
    hh_hl = jnp.dot(xh, wr2_ref[...], preferred_element_type=F32)
    lg = (hh_hl[:, :128] + jnp.dot(xl, wr2_ref[:, :128], preferred_element_type=F32)
          + hh_hl[:, 128:]) + br_ref[...]
    lane = lax.broadcasted_iota(jnp.int32, lg.shape, 1)
    lane_f = lane.astype(F32)
    big = 1e9
    isg = lane < N_EXP_GROUPS
    gmax = jnp.max(jnp.where(isg, lg, -jnp.inf), axis=-1, keepdims=True)
    pg_top = 1.0 / jnp.sum(jnp.where(isg, jnp.exp(lg - gmax), 0.0), axis=-1, keepdims=True)
    gi = jnp.min(jnp.where(isg & (lg == gmax), lane_f, big), axis=-1, keepdims=True)
    egrp = ((lane - N_EXP_GROUPS) // EXPERTS_PER_GROUP).astype(F32)
    ise = (lane >= N_EXP_GROUPS) & (lane < N_EXP_GROUPS + N_EXPERTS) & (egrp == gi)
    emax = jnp.max(jnp.where(ise, lg, -jnp.inf), axis=-1, keepdims=True)
    i1 = jnp.min(jnp.where(ise & (lg == emax), lane_f, big), axis=-1, keepdims=True)
    rest = ise & (lane_f != i1)
    m2 = jnp.max(jnp.where(rest, lg, -jnp.inf), axis=-1, keepdims=True)
    i2 = jnp.min(jnp.where(rest & (lg == m2), lane_f, big), axis=-1, keepdims=True)
    e2 = jnp.exp(m2 - emax)
    inv = pg_top / (1.0 + e2)
    comb_ref[...] = jnp.where(lane_f == i1, inv, 0.0) + jnp.where(lane_f == i2, inv * e2, 0.0)


def _outproj(attn, rnn, x2, ga, gr, wo, g1, b1, wr2, br, N):
    tm = 512
    row = lambda w: pl.BlockSpec((tm, w), lambda i: (i, 0))
    full = lambda a: pl.BlockSpec(a.shape, lambda i: (0,) * a.ndim)
    return pl.pallas_call(
        _outproj_kernel,
        grid=(N // tm,),
        in_specs=[row(512), row(512), row(D_MODEL), full(ga), full(gr), full(wo), full(g1), full(b1),
                  full(wr2), full(br)],
        out_specs=[row(D_MODEL), row(128)],
        out_shape=[jax.ShapeDtypeStruct((N, D_MODEL), F32), jax.ShapeDtypeStruct((N, 128), F32)],
        compiler_params=_cparams(("parallel",), VMEM_LIMIT),
        name="outproj",
    )(attn, rnn, x2, ga, gr, wo, g1, b1, wr2, br)


def _moe_kernel(x1_ref, comb_ref, p_ref, tri_ref, wup_ref, wdn_ref, wg_ref, wp_ref, g2_ref, b2_ref, o_ref,
                acc_ref, xb_ref, rank_ref, rankt_ref, combt_ref):
    pi = pl.program_id(1)
    n_sub = x1_ref.shape[0] // MOE_SUB
    sub_rows = [slice(h * MOE_SUB, (h + 1) * MOE_SUB) for h in range(n_sub)]
    lane_col = N_EXP_GROUPS

    @pl.when(pi == 0)
    def _():
        x1 = x1_ref[...]
        xb = x1.astype(BF16)
        xb_ref[...] = xb
        gate = jax.nn.sigmoid(jnp.dot(xb, wg_ref[...], preferred_element_type=F32))
        ple = gate * jnp.dot(p_ref[...].astype(BF16), wp_ref[...], preferred_element_type=F32)
        acc_ref[...] = ALPHA * x1 + ple
        for rows in sub_rows:
            comb = comb_ref[rows, :]
            chosen = comb > 0.0
            rank = jnp.dot(tri_ref[...], jnp.where(chosen, 1.0, 0.0).astype(BF16),
                           preferred_element_type=F32)
            rank = jnp.where(chosen, rank, -1.0)
            rank_ref[rows, :] = rank
            rankt_ref[:, rows] = rank.T
            combt_ref[:, rows] = comb.T

    lane = lax.broadcasted_iota(jnp.int32, (MOE_SUB, 128), 1)
    slot_r = lax.broadcasted_iota(jnp.int32, (MOE_SLOTS, MOE_SUB), 0).astype(F32)
    slot_c = lax.broadcasted_iota(jnp.int32, (MOE_SUB, MOE_SLOTS), 1).astype(F32)

    def route_pair(base):
        chains = [(rows, k) for rows in sub_rows for k in range(2)]
        cmask, xc, u, hsw, y, ys, ct = {}, {}, {}, {}, {}, {}, {}
        for ch in chains:
            rows, k = ch
            rk_row = rankt_ref[pl.ds(2 * pi + k + lane_col, 1), rows] - base
            cmask[ch] = rk_row == slot_r
            xc[ch] = jnp.dot(jnp.where(cmask[ch], 1.0, 0.0).astype(BF16), xb_ref[rows, :],
                             preferred_element_type=F32).astype(BF16)
        for ch in chains:
            u[ch] = jnp.dot(xc[ch], wup_ref[ch[1]], preferred_element_type=F32)
        for ch in chains:
            ua = u[ch][:, :D_FF]
            hsw[ch] = (ua * jax.nn.sigmoid(ua) * u[ch][:, D_FF:]).astype(BF16)
        for ch in chains:
            y[ch] = jnp.dot(hsw[ch], wdn_ref[ch[1]], preferred_element_type=F32)
        for ch in chains:
            rows, k = ch
            e_lane = 2 * pi + k + lane_col
            w_row = combt_ref[pl.ds(e_lane, 1), rows]
            w_slot = jnp.sum(jnp.where(cmask[ch], w_row, 0.0), axis=1, keepdims=True)
            ys[ch] = (w_slot * y[ch]).astype(BF16)
            rk_col = jnp.sum(jnp.where(lane == e_lane, rank_ref[rows, :], 0.0), axis=1,
                             keepdims=True) - base
            ct[ch] = jnp.where(rk_col == slot_c, 1.0, 0.0).astype(BF16)
        for rows in sub_rows:
            pair = [(rows, k) for k in range(2)]
            ctp = jnp.concatenate([ct[ch] for ch in pair], axis=1)
            acc_ref[rows, :] = acc_ref[rows, :] + jnp.dot(
                ctp, jnp.concatenate([ys[ch] for ch in pair], axis=0), preferred_element_type=F32)

    route_pair(0.0)

    top_rank = jnp.max(jnp.maximum(rankt_ref[pl.ds(2 * pi + lane_col, 1), :],
                                   rankt_ref[pl.ds(2 * pi + 1 + lane_col, 1), :]))

    @pl.when(top_rank >= MOE_SLOTS)
    def _():
        def extra(sb, carry):
            route_pair((sb * MOE_SLOTS).astype(F32))
            return carry

        lax.fori_loop(1, top_rank.astype(jnp.int32) // MOE_SLOTS + 1, extra, 0)

    @pl.when(pi == N_EXPERTS // 2 - 1)
    def _():
        o_ref[...] = _layer_norm(acc_ref[...], g2_ref[...], b2_ref[...])


def _moe(x1, comb, p2, wup, wdn, wg, wp, g2, b2, N):
    tm = MOE_TILE
    tri = (jnp.arange(MOE_SUB)[None, :] < jnp.arange(MOE_SUB)[:, None]).astype(BF16)
    row = lambda w: pl.BlockSpec((tm, w), lambda i, e: (i, 0))
    full = lambda a: pl.BlockSpec(a.shape, lambda i, e: (0,) * a.ndim)
    return pl.pallas_call(
        _moe_kernel,
        grid=(N // tm, N_EXPERTS // 2),
        in_specs=[row(D_MODEL), row(128), row(D_PLE), full(tri),
                  pl.BlockSpec((2, D_MODEL, 2 * D_FF), lambda i, e: (e, 0, 0)),
                  pl.BlockSpec((2, D_FF, D_MODEL), lambda i, e: (e, 0, 0)),
                  full(wg), full(wp), full(g2), full(b2)],
        out_specs=row(D_MODEL),
        out_shape=jax.ShapeDtypeStruct((N, D_MODEL), F32),
        scratch_shapes=[pltpu.VMEM((tm, D_MODEL), F32),
                        pltpu.VMEM((tm, D_MODEL), BF16),
                        pltpu.VMEM((tm, 128), F32),
                        pltpu.VMEM((128, tm), F32),
                        pltpu.VMEM((128, tm), F32)],
        compiler_params=_cparams(("parallel", "arbitrary"), VMEM_LIMIT),
        name="moe",
    )(x1, comb, p2, tri, wup, wdn, wg, wp, g2, b2)


def _block_diag(w):
    n, k, j = w.shape
    eye = jnp.eye(n, dtype=w.dtype)
    return (w[:, :, None, :] * eye[:, None, :, None]).reshape(n * k, n * j)


def kernel(x, p, rel_bias, w_in, cmp_pe_k, cmp_pe_v, cmp_w_k, cmp_w_v, conv_w, conv_b, rg_w_a, rg_b_a,
           rg_w_x, rg_b_x, rg_lambda, attn_out_gain, rnn_out_gain, w_out, ln1_g, ln1_b, router_group_w,
           router_group_b, router_expert_w, router_expert_b, expert_w_up, expert_w_down, ple_w,
           ple_gate_w, ln2_g, ln2_b):
    B, T, _ = x.shape
    N = B * T
    assert T % FAR == 0 and T % (TQ * TILES) == 0 and T % WINDOW == 0 and T // SEL_BLOCK <= N_SLC
    assert w_in.shape[0] == 1
    n_cmp = (T - CMP_LEN) // CMP_STRIDE + 1
    n_chunk = T // CMP_STRIDE
    assert C_OFF + n_chunk <= C_PAD
    row1 = lambda v: v.reshape(1, -1)

    w = w_in[0]
    c0 = D_ATTN
    kv = lambda k: w[:, c0 + k * D_KV:c0 + (k + 1) * D_KV]
    g0 = c0 + 6 * D_KV
    wq = w[:, :D_ATTN] * (HEAD_DIM ** -0.5 * LOG2E)
    w_main = jnp.concatenate([kv(2), kv(4), kv(0), kv(1), w[:, g0 + 24:]], axis=1).astype(BF16)
    w_gate = jnp.pad(w[:, g0:g0 + 24], ((0, 0), (0, GATE_ROWS - 24)))
    w_t = jnp.concatenate([wq, kv(3), kv(5), w_gate], axis=1).T.astype(BF16)

    x2 = x.reshape(N, D_MODEL)
    kaug, kw_pad, kc_raw, vc_raw, rx, ry, t_all, vst, vwt, gates_t = _inproj(x2, w_main, w_t, B, T)

    def per_pos(wc):
        wl = wc.reshape(CMP_LEN, HEAD_DIM, HEAD_DIM)
        z = jnp.zeros_like(wl)
        return jnp.concatenate([jnp.concatenate([wl, z], axis=2), jnp.concatenate([z, wl], axis=2)],
                               axis=1).astype(BF16)

    pek = jnp.tile(cmp_pe_k[0], (1, N_GROUPS_KV))
    pev = jnp.tile(cmp_pe_v[0], (1, N_GROUPS_KV))
    grp_onehot = (jnp.arange(C_PAD)[:, None] // 8 == jnp.arange(128)[None, :]).astype(BF16)
    kc2, vct = _compress(kc_raw.reshape(B, T, 128), vc_raw.reshape(B, T, 128), pek, pev,
                         per_pos(cmp_w_k[0]), per_pos(cmp_w_v[0]), grp_onehot, B, T, n_cmp)

    ds, dw, bc = _bias_tiles(rel_bias)

    cidx = jnp.arange(C_PAD) - C_OFF
    s_lo = jnp.arange(N_SLC)[:, None] * SEL_BLOCK
    overlap = ((cidx[None, :] * CMP_STRIDE < s_lo + SEL_BLOCK) & (cidx[None, :] * CMP_STRIDE + CMP_LEN > s_lo)
               & (cidx[None, :] >= 0) & (cidx[None, :] < n_cmp)).astype(BF16)
    vct = jnp.concatenate([vct, jnp.broadcast_to(overlap, (B, N_SLC, C_PAD))], axis=1)
    attn = _attention(t_all, gates_t, kc2, vct, kaug, vst, kw_pad, vwt, ds, dw, bc, B, T)

    sp = jax.nn.softplus(-rg_lambda[0].astype(F32))
    rnn, wup_b, wdn_b = _rglru(rx.reshape(B, T, D_RNN), ry.reshape(B, T, D_RNN), conv_w[0], row1(conv_b[0]),
                               _block_diag(rg_w_a[0]).astype(BF16), row1(rg_b_a[0]),
                               _block_diag(rg_w_x[0]).astype(BF16), row1(rg_b_x[0]), row1(sp),
                               expert_w_up[0].reshape(N_EXPERTS * D_MODEL, 2 * D_FF),
                               expert_w_down[0].reshape(N_EXPERTS * D_FF, D_MODEL), B, T)

    wr = jnp.pad(jnp.concatenate([router_group_w[0], router_expert_w[0]], axis=1), ((0, 0), (0, 108)))
    wrh = wr.astype(BF16)
    wr2 = jnp.concatenate([wrh, (wr - wrh.astype(F32)).astype(BF16)], axis=1)
    br = jnp.pad(jnp.concatenate([router_group_b[0], router_expert_b[0]]), (0, 108)).reshape(1, 128)
    x1, comb = _outproj(attn.reshape(N, 512), rnn.reshape(N, D_RNN), x2, row1(attn_out_gain[0]),
                        row1(rnn_out_gain[0]), w_out[0].astype(BF16), row1(ln1_g[0]), row1(ln1_b[0]),
                        wr2, br, N)

    out = _moe(x1, comb, p[0].reshape(N, D_PLE), wup_b.reshape(N_EXPERTS, D_MODEL, 2 * D_FF),
               wdn_b.reshape(N_EXPERTS, D_FF, D_MODEL), ple_gate_w[0].astype(BF16), ple_w[0].astype(BF16),
               row1(ln2_g[0]), row1(ln2_b[0]), N)
    return out.reshape(B, T, D_MODEL)
```

```python
import functools
import math

import jax
import jax.numpy as jnp
from jax import lax
from jax.experimental import pallas as pl
from jax.experimental.pallas import tpu as pltpu

F32 = jnp.float32
BF16 = jnp.bfloat16
NEG = -1e30
LOG2E = 1.4426950408889634

D_MODEL = 1024
HEAD_DIM = 64
N_HEADS = 8
N_GROUPS_KV = 2
GQA_REP = 4
D_ATTN = 512
D_RNN = 512
D_KV = 128
CMP_LEN = 32
CMP_STRIDE = 16
SEL_BLOCK = 64
N_SEL = 16
WINDOW = 512
N_BUCKETS = 32
MAX_DISTANCE = 128
N_EXPERTS = 16
EXPERTS_PER_GROUP = 4
N_EXP_GROUPS = 4
D_FF = 512
D_PLE = 256
ALPHA = 2.0 ** 0.25
LN_EPS = 1e-5
RMS_EPS = 1e-6
FORCE_BONUS = 1e4
RG_C = 8.0

TQ = 128
TILES = 2
N_SLC = 128
C_PAD = 640
C_OFF = 16
CMP_ROW_LEVELS = (256, 384, 512, C_PAD)
SEL_ROW_LEVELS = (32, 64, 96, N_SLC)
FAR = 512
TAIL = WINDOW + TQ
BAND = 24
GATE_ROWS = 32
V_ROWS = 80
MOE_TILE = 1024
MOE_SUB = 512
MOE_SLOTS = 128
VMEM_LIMIT = 56 * 1024 * 1024


def _cparams(sem, vmem=None):
    return pltpu.CompilerParams(dimension_semantics=sem, vmem_limit_bytes=vmem)


def _inproj_kernel(x_ref, wm_ref, wt_ref, ka_ref, kw_ref, kc_ref, vc_ref, rx_ref, ry_ref, t_ref, vs_ref,
                   vw_ref, gt_ref):
    tm = x_ref.shape[0]
    j = pl.program_id(1)
    ones_rows = jnp.where(lax.broadcasted_iota(jnp.int32, (V_ROWS - HEAD_DIM, tm), 0) == 0, 1.0, 0.0).astype(BF16)
    lane = lax.broadcasted_iota(jnp.int32, (tm, 2 * N_SLC), 1)

    @pl.when(j == 0)
    def _():
        ka_ref[...] = jnp.where(lane == N_SLC - 1, 1.0, 0.0).astype(BF16)
        kw_ref[...] = jnp.where(lane == N_SLC, 1.0, 0.0).astype(BF16)
        for v_ref in (vs_ref, vw_ref):
            for g in range(N_GROUPS_KV):
                v_ref[g, :HEAD_DIM, :] = jnp.zeros((HEAD_DIM, tm), BF16)
                v_ref[g, HEAD_DIM:, :] = ones_rows

    @pl.when(j > 0)
    def _():
        _inproj_tile(x_ref, wm_ref, wt_ref, ka_ref, kw_ref, kc_ref, vc_ref, rx_ref, ry_ref, t_ref, vs_ref,
                     vw_ref, gt_ref, (j - 1) * tm, ones_rows)


def _inproj_tile(x_ref, wm_ref, wt_ref, ka_ref, kw_ref, kc_ref, vc_ref, rx_ref, ry_ref, t_ref, vs_ref, vw_ref,
                 gt_ref, t0, ones_rows):
    tm = x_ref.shape[0]
    xb = x_ref[...].astype(BF16)

    def mm(lo, hi):
        return jnp.dot(xb, wm_ref[:, lo:hi], preferred_element_type=F32)

    k_pair = mm(0, 256)
    blk = (t0 + lax.broadcasted_iota(jnp.int32, (tm, N_SLC), 0)) // SEL_BLOCK
    ka_ref[:, :N_SLC] = jnp.where(blk == lax.broadcasted_iota(jnp.int32, (tm, N_SLC), 1), 1.0, 0.0).astype(BF16)
    ka_ref[:, N_SLC:] = k_pair[:, :128].astype(BF16)
    kw_ref[:, :N_SLC] = k_pair[:, 128:].astype(BF16)
    kw_ref[:, N_SLC:] = jnp.zeros((tm, N_SLC), BF16)
    c_pair = mm(256, 512)
    kc_ref[...] = c_pair[:, :128]
    vc_ref[...] = c_pair[:, 128:]
    rx_ref[...] = mm(512, 1024)
    ry_ref[...] = mm(1024, 1536)
    tr = lax.dot_general(wt_ref[...], xb, (((1,), (1,)), ((), ())), preferred_element_type=F32)
    t_ref[...] = tr[:D_ATTN].astype(BF16)
    for v_ref, base in ((vs_ref, D_ATTN), (vw_ref, D_ATTN + D_KV)):
        for g in range(N_GROUPS_KV):
            v_ref[g, :HEAD_DIM, :] = tr[base + g * HEAD_DIM:base + (g + 1) * HEAD_DIM].astype(BF16)
            v_ref[g, HEAD_DIM:, :] = ones_rows
    gt_ref[...] = jax.nn.sigmoid(tr[D_ATTN + 2 * D_KV:])


def _inproj(x2, w_main, w_t, B, T):
    N = B * T
    tm = WINDOW
    nt = T // tm
    tp = T + WINDOW
    data = lambda j: jnp.maximum(j - 1, 0)
    row = lambda w: pl.BlockSpec((tm, w), lambda b, j: (b * nt + data(j), 0))
    full = lambda a: pl.BlockSpec(a.shape, lambda b, j: (0,) * a.ndim)
    tr = lambda r: pl.BlockSpec((None, r, tm), lambda b, j: (b, 0, data(j)))
    lead = pl.BlockSpec((None, tm, 2 * N_SLC), lambda b, j: (b, j, 0))
    lead_t = pl.BlockSpec((None, N_GROUPS_KV, V_ROWS, tm), lambda b, j: (b, 0, 0, j))
    return pl.pallas_call(
        _inproj_kernel,
        grid=(B, nt + 1),
        in_specs=[row(D_MODEL), full(w_main), full(w_t)],
        out_specs=[lead, lead, row(128), row(128), row(512), row(512), tr(D_ATTN), lead_t, lead_t,
                   tr(GATE_ROWS)],
        out_shape=[jax.ShapeDtypeStruct((B, tp, 2 * N_SLC), BF16),
                   jax.ShapeDtypeStruct((B, tp, 2 * N_SLC), BF16),
                   jax.ShapeDtypeStruct((N, 128), F32),
                   jax.ShapeDtypeStruct((N, 128), F32),
                   jax.ShapeDtypeStruct((N, 512), F32),
                   jax.ShapeDtypeStruct((N, 512), F32),
                   jax.ShapeDtypeStruct((B, D_ATTN, T), BF16),
                   jax.ShapeDtypeStruct((B, N_GROUPS_KV, V_ROWS, tp), BF16),
                   jax.ShapeDtypeStruct((B, N_GROUPS_KV, V_ROWS, tp), BF16),
                   jax.ShapeDtypeStruct((B, GATE_ROWS, T), F32)],
        compiler_params=_cparams(("parallel", "arbitrary"), VMEM_LIMIT),
        name="inproj",
    )(x2, w_main, w_t)


def _compress_kernel(kr_ref, vr_ref, pek_ref, pev_ref, wk_ref, wv_ref, oh_ref, kc_ref, vct_ref, *,
                     n_cmp, n_chunk):
    def compress(raw_ref, pe_ref, w_ref):
        lo = jnp.zeros((n_chunk, 2 * HEAD_DIM), F32)
        hi = jnp.zeros((n_chunk, 2 * HEAD_DIM), F32)
        for j in range(CMP_STRIDE):
            a = raw_ref[pl.ds(j, n_chunk, stride=CMP_STRIDE), :]
            lo = lo + jnp.dot((a + pe_ref[j:j + 1, :]).astype(BF16), w_ref[j],
                              preferred_element_type=F32)
            hi = hi + jnp.dot((a + pe_ref[CMP_STRIDE + j:CMP_STRIDE + j + 1, :]).astype(BF16),
                              w_ref[CMP_STRIDE + j], preferred_element_type=F32)
        hi = pltpu.roll(hi, n_chunk - 1, axis=0)
        rid = lax.broadcasted_iota(jnp.int32, (n_chunk, 2 * HEAD_DIM), 0)
        out = jnp.where(rid < n_cmp, lo + hi, 0.0)
        return jnp.concatenate([jnp.zeros((C_OFF, 2 * HEAD_DIM), F32), out,
                                jnp.zeros((C_PAD - C_OFF - n_chunk, 2 * HEAD_DIM), F32)], axis=0)

    kc_ref[:, 0:2 * HEAD_DIM] = compress(kr_ref, pek_ref, wk_ref).astype(BF16)
    kc_ref[:, 2 * HEAD_DIM:] = oh_ref[...]
    vct_ref[...] = compress(vr_ref, pev_ref, wv_ref).T.astype(BF16)


def _compress(kc_raw, vc_raw, pek, pev, wk, wv, grp_onehot, B, T, n_cmp):
    n_chunk = T // CMP_STRIDE
    blk = pl.BlockSpec((None, T, 2 * HEAD_DIM), lambda b: (b, 0, 0))
    full = lambda a: pl.BlockSpec(a.shape, lambda b: (0,) * a.ndim)
    return pl.pallas_call(
        functools.partial(_compress_kernel, n_cmp=n_cmp, n_chunk=n_chunk),
        grid=(B,),
        in_specs=[blk, blk, full(pek), full(pev), full(wk), full(wv), full(grp_onehot)],
        out_specs=[pl.BlockSpec((None, C_PAD, 256), lambda b: (b, 0, 0)),
                   pl.BlockSpec((None, 2 * HEAD_DIM, C_PAD), lambda b: (b, 0, 0))],
        out_shape=[jax.ShapeDtypeStruct((B, C_PAD, 256), BF16),
                   jax.ShapeDtypeStruct((B, 2 * HEAD_DIM, C_PAD), BF16)],
        compiler_params=_cparams(("parallel",), VMEM_LIMIT),
        name="compress",
    )(kc_raw, vc_raw, pek, pev, wk, wv, grp_onehot)


def _bias_kernel(tab_ref, dt_ref, bt_ref, dc_ref, bc_ref, os_ref, ow_ref, oc_ref):
    h = pl.program_id(0) * GQA_REP + pl.program_id(1)

    def lookup(bk):
        out = jnp.zeros(bk.shape, F32)
        for b in range(N_BUCKETS):
            out = jnp.where(bk == b, tab_ref[h, b], out)
        return (out - tab_ref[h, N_BUCKETS - 1]) * LOG2E

    dt = dt_ref[...]
    v = lookup(bt_ref[...])
    os_ref[...] = jnp.where(dt >= 0, v, NEG)
    ow_ref[...] = jnp.where((dt >= 0) & (dt < WINDOW), v, NEG)
    oc_ref[...] = jnp.where(dc_ref[...] >= 0, lookup(bc_ref[...]), NEG)


def _t5_bucket(dist):
    max_exact = N_BUCKETS // 2
    d = jnp.maximum(dist, 0)
    df = jnp.maximum(d, 1).astype(F32)
    large = max_exact + (jnp.log(df / max_exact) / math.log(MAX_DISTANCE / max_exact)
                         * (N_BUCKETS - max_exact)).astype(jnp.int32)
    large = jnp.minimum(large, N_BUCKETS - 1)
    return jnp.where(d < max_exact, d, large)


def _bias_tiles(rel_bias):
    tq = jnp.arange(TQ, dtype=jnp.int32)
    dt = tq[None, :] + WINDOW - jnp.arange(TAIL, dtype=jnp.int32)[:, None]
    cc = jnp.arange(BAND, dtype=jnp.int32) - C_OFF
    dc = tq[None, :] - (cc[:, None] * CMP_STRIDE + CMP_LEN - 1)
    full = lambda a: pl.BlockSpec(a.shape, lambda g, r: (0,) * a.ndim)
    out = lambda rows: pl.BlockSpec((None, rows, TQ), lambda g, r: (g, 0, r))
    return pl.pallas_call(
        _bias_kernel,
        grid=(N_GROUPS_KV, GQA_REP),
        in_specs=[pl.BlockSpec(memory_space=pltpu.SMEM), full(dt), full(dt), full(dc), full(dc)],
        out_specs=[out(TAIL), out(TAIL), out(BAND)],
        out_shape=[jax.ShapeDtypeStruct((N_GROUPS_KV, TAIL, GQA_REP * TQ), F32),
                   jax.ShapeDtypeStruct((N_GROUPS_KV, TAIL, GQA_REP * TQ), F32),
                   jax.ShapeDtypeStruct((N_GROUPS_KV, BAND, GQA_REP * TQ), F32)],
        compiler_params=_cparams(("arbitrary", "arbitrary")),
        name="biastile",
    )(rel_bias.T, dt, _t5_bucket(dt), dc, _t5_bucket(dc))


def _attn_kernel(qt_ref, gt_ref, kc_ref, vct_ref, kaug_ref, vst_ref, kw_ref, vwt_ref, ds_ref, dw_ref,
                 bc_ref, o_ref, *scratch, n_far_max):
    n_hg = TILES * N_GROUPS_KV
    sc_ref, sw_ref = (scratch[n_hg * k:n_hg * (k + 1)] for k in range(2))
    qa_ref, qat_ref, m_ref, acc_ref, sb_ref, sm_ref, oc_ref, ow_ref, imp_ref = scratch[2 * n_hg:]
    cols4 = GQA_REP * TQ
    groups = range(N_GROUPS_KV)
    zeros_q = jnp.zeros((HEAD_DIM, cols4), BF16)
    grp = lax.broadcasted_iota(jnp.int32, (128, cols4), 0)
    neg_row = jnp.where(grp == 0, NEG, 0.0).astype(BF16)
    tq_col = lax.broadcasted_iota(jnp.int32, (1, cols4), 1) % TQ

    def chunk_start(c):
        return pl.multiple_of(WINDOW + jnp.minimum(c, n_far_max - 1) * FAR, FAR)

    def scores(h, c, par):
        kch = kaug_ref[pl.ds(chunk_start(c), FAR), :]
        for g in groups:
            sf = jnp.dot(kch, qa_ref[2 * h + g], preferred_element_type=F32)
            sb_ref[4 * h + 2 * g + par] = sf
            sm_ref[4 * h + 2 * g + par] = jnp.max(sf, axis=0, keepdims=True)

    def consume(h, nfar, c, par):
        c0 = chunk_start(c)
        live = c < nfar
        for g in groups:
            hg = 2 * h + g
            sf = sb_ref[4 * h + 2 * g + par]
            mprev = m_ref[hg]
            mcand = jnp.maximum(mprev, sm_ref[4 * h + 2 * g + par])
            mnew = jnp.where(live, mcand, mprev)
            pf = jnp.exp2(sf - jnp.where(live, mcand, -NEG)).astype(BF16)
            acc_ref[hg] = jnp.exp2(mprev - mnew) * acc_ref[hg] + jnp.dot(
                vst_ref[g, :, pl.ds(c0, FAR)], pf, preferred_element_type=F32)
            m_ref[hg] = mnew

    tiles = range(TILES)
    chains = [(h, g) for h in tiles for g in groups]
    t_i = [pl.program_id(1) * TILES + h for h in tiles]
    t_qs = [pl.multiple_of(i * TQ, TQ) for i in t_i]
    t_nfar = [jnp.maximum(i - 1, 0) // 4 for i in t_i]
    def qpad(h, g):
        qgt = jnp.concatenate(
            [qt_ref[(GQA_REP * g + r) * HEAD_DIM:(GQA_REP * g + r + 1) * HEAD_DIM, h * TQ:(h + 1) * TQ]
             for r in range(GQA_REP)], axis=1)
        return jnp.concatenate([qgt, zeros_q] if g == 0 else [zeros_q, qgt], axis=0)

    def compressed_and_window(rows):
        for h, g in chains:
            i = t_i[h]
            gmask = jnp.where((grp < C_OFF // 8) | (grp >= i + BAND // 8), NEG, 0.0).astype(BF16)
            band0 = pl.multiple_of(i * 8, 8)
            sc_ref[2 * h + g][0:rows, :] = jnp.dot(kc_ref[0:rows, :],
                                                   jnp.concatenate([qpad(h, g), gmask], axis=0),
                                                   preferred_element_type=F32)
            sc_ref[2 * h + g][pl.ds(band0, BAND), :] = sc_ref[2 * h + g][pl.ds(band0, BAND), :] + bc_ref[g]
        w_max = {}
        for h, g in chains:
            sw = jnp.dot(kw_ref[pl.ds(t_qs[h], TAIL), :], jnp.concatenate([qpad(h, g), neg_row], axis=0),
                         preferred_element_type=F32) + dw_ref[g]
            sw_ref[2 * h + g][...] = sw
            w_max[h, g] = jnp.max(sw, axis=0, keepdims=True)
        for h, g in chains:
            s = sc_ref[2 * h + g][0:rows, :]
            e = jnp.exp2(s - jnp.max(s, axis=0, keepdims=True))
            has_cmp = t_qs[h] + tq_col >= CMP_LEN - 1
            pinv = jnp.where(has_cmp, 1.0 / jnp.sum(e, axis=0, keepdims=True), 0.0)
            res = jnp.dot(vct_ref[:, 0:rows], e.astype(BF16), preferred_element_type=F32)
            oc_ref[2 * h + g] = res[0:2 * HEAD_DIM, :] * pinv
            imp_ref[2 * h + g] = sum(res[2 * HEAD_DIM:, r * TQ:(r + 1) * TQ] * pinv[:, r * TQ:(r + 1) * TQ]
                                     for r in range(GQA_REP))
        for h, g in chains:
            pw = jnp.exp2(sw_ref[2 * h + g][...] - w_max[h, g]).astype(BF16)
            acc_w = jnp.dot(vwt_ref[g, :, pl.ds(t_qs[h], TAIL)], pw, preferred_element_type=F32)
            ow_ref[2 * h + g] = acc_w[0:HEAD_DIM, :] * (1.0 / acc_w[HEAD_DIM:HEAD_DIM + 1, :])

    step = pl.program_id(1)
    first_step = 0
    for rows in CMP_ROW_LEVELS:
        end_step = ((rows - BAND) // 8 - TILES + 1) // TILES + 1
        in_level = step >= first_step if rows == C_PAD else (step >= first_step) & (step < end_step)
        pl.when(in_level)(functools.partial(compressed_and_window, rows))
        first_step = end_step

    def select_blocks(nsel):
        sid = lax.broadcasted_iota(jnp.int32, (nsel, TQ), 0)
        sid_f = sid.astype(F32)
        lane = lax.broadcasted_iota(jnp.int32, (nsel, TQ), 1)
        score, mbt = {}, {}
        for h, g in chains:
            cur = (t_qs[h] + lane) // SEL_BLOCK
            forced = (sid == 0) | (sid == cur) | (sid == cur - 1)
            score[h, g] = jnp.where(forced, -jnp.inf, jnp.where(sid <= cur, imp_ref[2 * h + g, 0:nsel, :], NEG))
            mbt[h, g] = jnp.where(forced, 0.0, NEG)
        for _ in range(N_SEL - 3):
            for ch in chains:
                cm = jnp.max(score[ch], axis=0, keepdims=True)
                first = jnp.min(jnp.where(score[ch] == cm, sid_f, float(N_SLC)), axis=0, keepdims=True)
                pick = sid_f == first
                mbt[ch] = jnp.where(pick, 0.0, mbt[ch])
                score[ch] = jnp.where(pick, -jnp.inf, score[ch])
        for h, g in chains:
            mbt_tail = jnp.where(sid < t_nfar[h] * (FAR // SEL_BLOCK), NEG, mbt[h, g])
            for dst, m in ((qa_ref, mbt[h, g]), (qat_ref, mbt_tail)):
                m = jnp.concatenate([m, jnp.full((N_SLC - nsel, TQ), NEG, F32)], axis=0) if nsel < N_SLC else m
                dst[2 * h + g, 0:N_SLC, :] = jnp.concatenate([m.astype(BF16)] * GQA_REP, axis=1)

    first_step = 0
    for nsel in SEL_ROW_LEVELS:
        end_step = nsel // (TILES * TQ // SEL_BLOCK)
        in_level = step >= first_step if nsel == N_SLC else (step >= first_step) & (step < end_step)
        pl.when(in_level)(functools.partial(select_blocks, nsel))
        first_step = end_step

    for h, g in chains:
        qa_ref[2 * h + g, N_SLC:, :] = qpad(h, g)
        qa_tail = jnp.concatenate([qat_ref[2 * h + g], qpad(h, g)], axis=0)
        st = jnp.dot(kaug_ref[pl.ds(t_qs[h], TAIL), :], qa_tail, preferred_element_type=F32) + ds_ref[g]
        sc_ref[2 * h + g][...] = st
        m_ref[2 * h + g] = jnp.max(st, axis=0, keepdims=True)
    for h in tiles:
        scores(h, 0, 0)
    for h, g in chains:
        pt = jnp.exp2(sc_ref[2 * h + g][...] - m_ref[2 * h + g]).astype(BF16)
        acc_ref[2 * h + g] = jnp.dot(vst_ref[g, :, pl.ds(t_qs[h], TAIL)], pt, preferred_element_type=F32)

    def far_pair(kk, carry):
        c = 2 * kk
        for h in tiles:
            scores(h, c + 1, 1)
        for h in tiles:
            consume(h, t_nfar[h], c, 0)
        for h in tiles:
            scores(h, c + 2, 0)
        for h in tiles:
            consume(h, t_nfar[h], c + 1, 1)
        return carry

    lax.fori_loop(0, (t_nfar[TILES - 1] + 1) // 2, far_pair, 0)

    for h in tiles:
        lanes = slice(h * TQ, (h + 1) * TQ)
        out_rows = []
        for g in groups:
            dsl = slice(g * HEAD_DIM, (g + 1) * HEAD_DIM)
            hg = 2 * h + g
            o_s = acc_ref[hg, 0:HEAD_DIM, :] * (1.0 / acc_ref[hg, HEAD_DIM:HEAD_DIM + 1, :])
            for r in range(GQA_REP):
                cs = slice(r * TQ, (r + 1) * TQ)
                gc = 3 * (GQA_REP * g + r)
                out_rows.append(gt_ref[gc:gc + 1, lanes] * oc_ref[hg, dsl, cs]
                                + gt_ref[gc + 1:gc + 2, lanes] * o_s[:, cs]
                                + gt_ref[gc + 2:gc + 3, lanes] * ow_ref[hg, :, cs])
        o_ref[h * TQ:(h + 1) * TQ, :] = jnp.concatenate(out_rows, axis=0).T


def _attention(t_all, gates_t, kc2, vct, kaug, vst, kw_pad, vwt, ds, dw, bc, B, T):
    tp = T + WINDOW
    tqs = TQ * TILES
    per_b = lambda shape: pl.BlockSpec((None,) + shape, lambda b, i: (b,) + (0,) * len(shape),
                                      pipeline_mode=pl.Buffered(1))
    const = lambda a: pl.BlockSpec(a.shape, lambda b, i: (0,) * a.ndim, pipeline_mode=pl.Buffered(1))
    cols4 = GQA_REP * TQ
    n_hg = TILES * N_GROUPS_KV
    return pl.pallas_call(
        functools.partial(_attn_kernel, n_far_max=T // FAR),
        grid=(B, T // tqs),
        in_specs=[pl.BlockSpec((None, 512, tqs), lambda b, i: (b, 0, i)),
                  pl.BlockSpec((None, GATE_ROWS, tqs), lambda b, i: (b, 0, i)),
                  per_b((C_PAD, 256)),
                  per_b((256, C_PAD)),
                  per_b((tp, 256)),
                  per_b((N_GROUPS_KV, V_ROWS, tp)),
                  per_b((tp, 256)),
                  per_b((N_GROUPS_KV, V_ROWS, tp)),
                  const(ds), const(dw), const(bc)],
        out_specs=pl.BlockSpec((None, tqs, 512), lambda b, i: (b, i, 0)),
        out_shape=jax.ShapeDtypeStruct((B, T, 512), F32),
        scratch_shapes=([pltpu.VMEM((C_PAD, cols4), F32)] * n_hg
                        + [pltpu.VMEM((TAIL, cols4), F32)] * n_hg
                        + [pltpu.VMEM((n_hg, N_SLC + 2 * HEAD_DIM, cols4), BF16),
                           pltpu.VMEM((n_hg, N_SLC, cols4), BF16),
                           pltpu.VMEM((n_hg, 1, cols4), F32),
                           pltpu.VMEM((n_hg, V_ROWS, cols4), F32),
                           pltpu.VMEM((2 * n_hg, FAR, cols4), F32),
                           pltpu.VMEM((2 * n_hg, 1, cols4), F32),
                           pltpu.VMEM((n_hg, 2 * HEAD_DIM, cols4), F32),
                           pltpu.VMEM((n_hg, HEAD_DIM, cols4), F32),
                           pltpu.VMEM((n_hg, N_SLC, TQ), F32)]),
        compiler_params=_cparams(("parallel", "arbitrary"), VMEM_LIMIT),
        name="attn",
    )(t_all, gates_t, kc2, vct, kaug, vst, kw_pad, vwt, ds, dw, bc)


def _rglru_kernel(*refs, L):
    J = L // 8
    n_col = D_RNN // 128
    rx_refs, ry_refs = refs[:n_col], refs[n_col:2 * n_col]
    (cw_ref, cb_ref, wa_ref, ba_ref, wx_ref, bx_ref, sp_ref, eu_ref, ed_ref, o_ref, eub_ref, edb_ref,
     xprev_ref, h_ref, os_ref) = refs[2 * n_col:]
    eub_ref[...] = eu_ref[...].astype(BF16)
    edb_ref[...] = ed_ref[...].astype(BF16)

    @pl.when(pl.program_id(1) == 0)
    def _():
        xprev_ref[...] = jnp.zeros(xprev_ref.shape, F32)
        h_ref[...] = jnp.zeros(h_ref.shape, F32)

    def strided(col_refs):
        return jnp.concatenate(
            [jnp.concatenate([ref[pl.ds(j, 8, stride=J), :] for j in range(J)], axis=0)
             for ref in col_refs], axis=1)

    xp = strided(rx_refs)
    yp = strided(ry_refs)
    sub = lax.broadcasted_iota(jnp.int32, (8, D_RNN), 0)

    def delayed(d):
        heads = []
        for j in range(d):
            src = pltpu.roll(xp[(J + j - d) * 8:(J + j - d + 1) * 8], 1, axis=0)
            heads.append(jnp.where(sub == 0, xprev_ref[8 + j - d:9 + j - d, :], src))
        return jnp.concatenate(heads + [xp[:L - 8 * d]], axis=0)

    xc = (cb_ref[...] + cw_ref[0:1, :] * delayed(3) + cw_ref[1:2, :] * delayed(2)
          + cw_ref[2:3, :] * delayed(1) + cw_ref[3:4, :] * xp)
    xprev_ref[...] = jnp.concatenate([ref[L - 8:L, :] for ref in rx_refs], axis=1)
    xcb = xc.astype(BF16)
    r = jax.nn.sigmoid(jnp.dot(xcb, wa_ref[...], preferred_element_type=F32) + ba_ref[...])
    ig = jax.nn.sigmoid(jnp.dot(xcb, wx_ref[...], preferred_element_type=F32) + bx_ref[...])
    a = jnp.exp(-RG_C * r * sp_ref[...])
    bt = jnp.sqrt(1.0 - a * a) * (ig * xc)

    h = jnp.zeros((8, D_RNN), F32)
    dec = jnp.ones((8, D_RNN), F32)
    hs, decs = [], []
    for j in range(J):
        aj = a[j * 8:(j + 1) * 8]
        h = aj * h + bt[j * 8:(j + 1) * 8]
        dec = aj * dec
        hs.append(h)
        decs.append(dec)
    carry = h_ref[0:1, :]
    carries = []
    for s in range(8):
        carries.append(carry)
        carry = dec[s:s + 1, :] * carry + h[s:s + 1, :]
    h_ref[...] = jnp.broadcast_to(carry, h_ref.shape)
    h_in = jnp.concatenate(carries, axis=0)
    cdf_c = math.sqrt(2.0 / math.pi)
    for j in range(J):
        y = yp[j * 8:(j + 1) * 8]
        cdf = 0.5 * (1.0 + jnp.tanh(cdf_c * (y + 0.044715 * (y * y * y))))
        res = (hs[j] + decs[j] * h_in) * (y * cdf)
        for c in range(n_col):
            os_ref[c, pl.ds(j, 8, stride=J), :] = res[:, c * 128:(c + 1) * 128]
    for c in range(n_col):
        o_ref[:, c * 128:(c + 1) * 128] = os_ref[c]


def _rglru(rx, ry, conv_w, conv_b, wa_bd, b_a, wx_bd, b_x, sp, w_up, w_dn, B, T):
    L = 512
    steps = B * T // L
    nt = T // L
    up_rows, dn_rows = w_up.shape[0] // steps, w_dn.shape[0] // steps
    assert up_rows * steps == w_up.shape[0] and dn_rows * steps == w_dn.shape[0] and dn_rows % 16 == 0
    slab = lambda rows, a: pl.BlockSpec((rows, a.shape[1]), lambda b, t: (b * nt + t, 0))
    blk = pl.BlockSpec((None, L, D_RNN), lambda b, t: (b, t, 0))
    n_col = D_RNN // 128
    cols = [pl.BlockSpec((None, L, 128), functools.partial(lambda b, t, c: (b, t, c), c=c))
            for c in range(n_col)]
    full = lambda a: pl.BlockSpec(a.shape, lambda b, t: (0,) * a.ndim)
    return pl.pallas_call(
        functools.partial(_rglru_kernel, L=L),
        grid=(B, T // L),
        in_specs=cols + cols + [full(conv_w), full(conv_b), full(wa_bd), full(b_a), full(wx_bd),
                                full(b_x), full(sp), slab(up_rows, w_up), slab(dn_rows, w_dn)],
        out_specs=[blk, slab(up_rows, w_up), slab(dn_rows, w_dn)],
        out_shape=[jax.ShapeDtypeStruct((B, T, D_RNN), F32),
                   jax.ShapeDtypeStruct(w_up.shape, BF16),
                   jax.ShapeDtypeStruct(w_dn.shape, BF16)],
        scratch_shapes=[pltpu.VMEM((8, D_RNN), F32), pltpu.VMEM((8, D_RNN), F32),
                        pltpu.VMEM((n_col, L, 128), F32)],
        compiler_params=_cparams(("parallel", "arbitrary"), VMEM_LIMIT),
        name="rglru",
    )(*([rx] * n_col + [ry] * n_col), conv_w, conv_b, wa_bd, b_a, wx_bd, b_x, sp, w_up, w_dn)


def _layer_norm(y, g, b):
    mu = jnp.mean(y, axis=-1, keepdims=True)
    d = y - mu
    var = jnp.mean(d * d, axis=-1, keepdims=True)
    return d * lax.rsqrt(var + LN_EPS) * g + b


def _outproj_kernel(at_ref, rn_ref, x_ref, ga_ref, gr_ref, wo_ref, g1_ref, b1_ref, wr2_ref, br_ref,
                    x1_ref, comb_ref):
    a = at_ref[...]
    rn = rn_ref[...]
    ha = a * lax.rsqrt(jnp.mean(a * a, axis=-1, keepdims=True) + RMS_EPS) * ga_ref[...]
    hr = rn * lax.rsqrt(jnp.mean(rn * rn, axis=-1, keepdims=True) + RMS_EPS) * gr_ref[...]
    heads = jnp.concatenate([ha, hr], axis=1).astype(BF16)
    mix = jnp.dot(heads, wo_ref[...], preferred_element_type=F32)
    x1 = _layer_norm(ALPHA * x_ref[...] + mix, g1_ref[...], b1_ref[...])
    x1_ref[...] = x1

    xh = x1.astype(BF16)
    xl = (x1 - xh.astype(F32)).astype(BF16)
    hh_hl = jnp.dot(xh, wr2_ref[...], preferred_element_type=F32)
    lg = (hh_hl[:, :128] + jnp.dot(xl, wr2_ref[:, :128], preferred_element_type=F32)
          + hh_hl[:, 128:]) + br_ref[...]
    lane = lax.broadcasted_iota(jnp.int32, lg.shape, 1)
    lane_f = lane.astype(F32)
    big = 1e9
    isg = lane < N_EXP_GROUPS
    gmax = jnp.max(jnp.where(isg, lg, -jnp.inf), axis=-1, keepdims=True)
    pg_top = 1.0 / jnp.sum(jnp.where(isg, jnp.exp(lg - gmax), 0.0), axis=-1, keepdims=True)
    gi = jnp.min(jnp.where(isg & (lg == gmax), lane_f, big), axis=-1, keepdims=True)
    egrp = ((lane - N_EXP_GROUPS) // EXPERTS_PER_GROUP).astype(F32)
    ise = (lane >= N_EXP_GROUPS) & (lane < N_EXP_GROUPS + N_EXPERTS) & (egrp == gi)
    emax = jnp.max(jnp.where(ise, lg, -jnp.inf), axis=-1, keepdims=True)
    i1 = jnp.min(jnp.where(ise & (lg == emax), lane_f, big), axis=-1, keepdims=True)
    rest = ise & (lane_f != i1)
    m2 = jnp.max(jnp.where(rest, lg, -jnp.inf), axis=-1, keepdims=True)
    i2 = jnp.min(jnp.where(rest & (lg == m2), lane_f, big), axis=-1, keepdims=True)
    e2 = jnp.exp(m2 - emax)
    inv = pg_top / (1.0 + e2)
    comb_ref[...] = jnp.where(lane_f == i1, inv, 0.0) + jnp.where(lane_f == i2, inv * e2, 0.0)


def _outproj(attn, rnn, x2, ga, gr, wo, g1, b1, wr2, br, N):
    tm = 512
    row = lambda w: pl.BlockSpec((tm, w), lambda i: (i, 0))
    full = lambda a: pl.BlockSpec(a.shape, lambda i: (0,) * a.ndim)
    return pl.pallas_call(
        _outproj_kernel,
        grid=(N // tm,),
        in_specs=[row(512), row(512), row(D_MODEL), full(ga), full(gr), full(wo), full(g1), full(b1),
                  full(wr2), full(br)],
        out_specs=[row(D_MODEL), row(128)],
        out_shape=[jax.ShapeDtypeStruct((N, D_MODEL), F32), jax.ShapeDtypeStruct((N, 128), F32)],
        compiler_params=_cparams(("parallel",), VMEM_LIMIT),
        name="outproj",
    )(attn, rnn, x2, ga, gr, wo, g1, b1, wr2, br)


def _moe_kernel(x1_ref, comb_ref, p_ref, tri_ref, wup_ref, wdn_ref, wg_ref, wp_ref, g2_ref, b2_ref, o_ref,
                acc_ref, xb_ref, rank_ref, rankt_ref, combt_ref):
    pi = pl.program_id(1)
    n_sub = x1_ref.shape[0] // MOE_SUB
    sub_rows = [slice(h * MOE_SUB, (h + 1) * MOE_SUB) for h in range(n_sub)]
    lane_col = N_EXP_GROUPS

    @pl.when(pi == 0)
    def _():
        x1 = x1_ref[...]
        xb = x1.astype(BF16)
        xb_ref[...] = xb
        gate = jax.nn.sigmoid(jnp.dot(xb, wg_ref[...], preferred_element_type=F32))
        ple = gate * jnp.dot(p_ref[...].astype(BF16), wp_ref[...], preferred_element_type=F32)
        acc_ref[...] = ALPHA * x1 + ple
        for rows in sub_rows:
            comb = comb_ref[rows, :]
            chosen = comb > 0.0
            rank = jnp.dot(tri_ref[...], jnp.where(chosen, 1.0, 0.0).astype(BF16),
                           preferred_element_type=F32)
            rank = jnp.where(chosen, rank, -1.0)
            rank_ref[rows, :] = rank
            rankt_ref[:, rows] = rank.T
            combt_ref[:, rows] = comb.T

    lane = lax.broadcasted_iota(jnp.int32, (MOE_SUB, 128), 1)
    slot_r = lax.broadcasted_iota(jnp.int32, (MOE_SLOTS, MOE_SUB), 0).astype(F32)
    slot_c = lax.broadcasted_iota(jnp.int32, (MOE_SUB, MOE_SLOTS), 1).astype(F32)

    def route_pair(base):
        chains = [(rows, k) for rows in sub_rows for k in range(2)]
        cmask, xc, u, hsw, y, ys, ct = {}, {}, {}, {}, {}, {}, {}
        for ch in chains:
            rows, k = ch
            rk_row = rankt_ref[pl.ds(2 * pi + k + lane_col, 1), rows] - base
            cmask[ch] = rk_row == slot_r
            xc[ch] = jnp.dot(jnp.where(cmask[ch], 1.0, 0.0).astype(BF16), xb_ref[rows, :],
                             preferred_element_type=F32).astype(BF16)
        for ch in chains:
            u[ch] = jnp.dot(xc[ch], wup_ref[ch[1]], preferred_element_type=F32)
        for ch in chains:
            ua = u[ch][:, :D_FF]
            hsw[ch] = (ua * jax.nn.sigmoid(ua) * u[ch][:, D_FF:]).astype(BF16)
        for ch in chains:
            y[ch] = jnp.dot(hsw[ch], wdn_ref[ch[1]], preferred_element_type=F32)
        for ch in chains:
            rows, k = ch
            e_lane = 2 * pi + k + lane_col
            w_row = combt_ref[pl.ds(e_lane, 1), rows]
            w_slot = jnp.sum(jnp.where(cmask[ch], w_row, 0.0), axis=1, keepdims=True)
            ys[ch] = (w_slot * y[ch]).astype(BF16)
            rk_col = jnp.sum(jnp.where(lane == e_lane, rank_ref[rows, :], 0.0), axis=1,
                             keepdims=True) - base
            ct[ch] = jnp.where(rk_col == slot_c, 1.0, 0.0).astype(BF16)
        for rows in sub_rows:
            pair = [(rows, k) for k in range(2)]
            ctp = jnp.concatenate([ct[ch] for ch in pair], axis=1)
            acc_ref[rows, :] = acc_ref[rows, :] + jnp.dot(
                ctp, jnp.concatenate([ys[ch] for ch in pair], axis=0), preferred_element_type=F32)

    route_pair(0.0)

    top_rank = jnp.max(jnp.maximum(rankt_ref[pl.ds(2 * pi + lane_col, 1), :],
                                   rankt_ref[pl.ds(2 * pi + 1 + lane_col, 1), :]))

    @pl.when(top_rank >= MOE_SLOTS)
    def _():
        def extra(sb, carry):
            route_pair((sb * MOE_SLOTS).astype(F32))
            return carry

        lax.fori_loop(1, top_rank.astype(jnp.int32) // MOE_SLOTS + 1, extra, 0)

    @pl.when(pi == N_EXPERTS // 2 - 1)
    def _():
        o_ref[...] = _layer_norm(acc_ref[...], g2_ref[...], b2_ref[...])


def _moe(x1, comb, p2, wup, wdn, wg, wp, g2, b2, N):
    tm = MOE_TILE
    tri = (jnp.arange(MOE_SUB)[None, :] < jnp.arange(MOE_SUB)[:, None]).astype(BF16)
    row = lambda w: pl.BlockSpec((tm, w), lambda i, e: (i, 0))
    full = lambda a: pl.BlockSpec(a.shape, lambda i, e: (0,) * a.ndim)
    return pl.pallas_call(
        _moe_kernel,
        grid=(N // tm, N_EXPERTS // 2),
        in_specs=[row(D_MODEL), row(128), row(D_PLE), full(tri),
                  pl.BlockSpec((2, D_MODEL, 2 * D_FF), lambda i, e: (e, 0, 0)),
                  pl.BlockSpec((2, D_FF, D_MODEL), lambda i, e: (e, 0, 0)),
                  full(wg), full(wp), full(g2), full(b2)],
        out_specs=row(D_MODEL),
        out_shape=jax.ShapeDtypeStruct((N, D_MODEL), F32),
        scratch_shapes=[pltpu.VMEM((tm, D_MODEL), F32),
                        pltpu.VMEM((tm, D_MODEL), BF16),
                        pltpu.VMEM((tm, 128), F32),
                        pltpu.VMEM((128, tm), F32),
                        pltpu.VMEM((128, tm), F32)],
        compiler_params=_cparams(("parallel", "arbitrary"), VMEM_LIMIT),
        name="moe",
    )(x1, comb, p2, tri, wup, wdn, wg, wp, g2, b2)


def _block_diag(w):
    n, k, j = w.shape
    eye = jnp.eye(n, dtype=w.dtype)
    return (w[:, :, None, :] * eye[:, None, :, None]).reshape(n * k, n * j)


def kernel(x, p, rel_bias, w_in, cmp_pe_k, cmp_pe_v, cmp_w_k, cmp_w_v, conv_w, conv_b, rg_w_a, rg_b_a,
           rg_w_x, rg_b_x, rg_lambda, attn_out_gain, rnn_out_gain, w_out, ln1_g, ln1_b, router_group_w,
           router_group_b, router_expert_w, router_expert_b, expert_w_up, expert_w_down, ple_w,
           ple_gate_w, ln2_g, ln2_b):
    B, T, _ = x.shape
    N = B * T
    assert T % FAR == 0 and T % (TQ * TILES) == 0 and T % WINDOW == 0 and T // SEL_BLOCK <= N_SLC
    assert w_in.shape[0] == 1
    n_cmp = (T - CMP_LEN) // CMP_STRIDE + 1
    n_chunk = T // CMP_STRIDE
    assert C_OFF + n_chunk <= C_PAD
    row1 = lambda v: v.reshape(1, -1)

    w = w_in[0]
    c0 = D_ATTN
    kv = lambda k: w[:, c0 + k * D_KV:c0 + (k + 1) * D_KV]
    g0 = c0 + 6 * D_KV
    wq = w[:, :D_ATTN] * (HEAD_DIM ** -0.5 * LOG2E)
    w_main = jnp.concatenate([kv(2), kv(4), kv(0), kv(1), w[:, g0 + 24:]], axis=1).astype(BF16)
    w_gate = jnp.pad(w[:, g0:g0 + 24], ((0, 0), (0, GATE_ROWS - 24)))
    w_t = jnp.concatenate([wq, kv(3), kv(5), w_gate], axis=1).T.astype(BF16)

    x2 = x.reshape(N, D_MODEL)
    kaug, kw_pad, kc_raw, vc_raw, rx, ry, t_all, vst, vwt, gates_t = _inproj(x2, w_main, w_t, B, T)

    def per_pos(wc):
        wl = wc.reshape(CMP_LEN, HEAD_DIM, HEAD_DIM)
        z = jnp.zeros_like(wl)
        return jnp.concatenate([jnp.concatenate([wl, z], axis=2), jnp.concatenate([z, wl], axis=2)],
                               axis=1).astype(BF16)

    pek = jnp.tile(cmp_pe_k[0], (1, N_GROUPS_KV))
    pev = jnp.tile(cmp_pe_v[0], (1, N_GROUPS_KV))
    grp_onehot = (jnp.arange(C_PAD)[:, None] // 8 == jnp.arange(128)[None, :]).astype(BF16)
    kc2, vct = _compress(kc_raw.reshape(B, T, 128), vc_raw.reshape(B, T, 128), pek, pev,
                         per_pos(cmp_w_k[0]), per_pos(cmp_w_v[0]), grp_onehot, B, T, n_cmp)

    ds, dw, bc = _bias_tiles(rel_bias)

    cidx = jnp.arange(C_PAD) - C_OFF
    s_lo = jnp.arange(N_SLC)[:, None] * SEL_BLOCK
    overlap = ((cidx[None, :] * CMP_STRIDE < s_lo + SEL_BLOCK) & (cidx[None, :] * CMP_STRIDE + CMP_LEN > s_lo)
               & (cidx[None, :] >= 0) & (cidx[None, :] < n_cmp)).astype(BF16)
    vct = jnp.concatenate([vct, jnp.broadcast_to(overlap, (B, N_SLC, C_PAD))], axis=1)
    attn = _attention(t_all, gates_t, kc2, vct, kaug, vst, kw_pad, vwt, ds, dw, bc, B, T)

    sp = jax.nn.softplus(-rg_lambda[0].astype(F32))
    rnn, wup_b, wdn_b = _rglru(rx.reshape(B, T, D_RNN), ry.reshape(B, T, D_RNN), conv_w[0], row1(conv_b[0]),
                               _block_diag(rg_w_a[0]).astype(BF16), row1(rg_b_a[0]),
                               _block_diag(rg_w_x[0]).astype(BF16), row1(rg_b_x[0]), row1(sp),
                               expert_w_up[0].reshape(N_EXPERTS * D_MODEL, 2 * D_FF),
                               expert_w_down[0].reshape(N_EXPERTS * D_FF, D_MODEL), B, T)

    wr = jnp.pad(jnp.concatenate([router_group_w[0], router_expert_w[0]], axis=1), ((0, 0), (0, 108)))
    wrh = wr.astype(BF16)
    wr2 = jnp.concatenate([wrh, (wr - wrh.astype(F32)).astype(BF16)], axis=1)
    br = jnp.pad(jnp.concatenate([router_group_b[0], router_expert_b[0]]), (0, 108)).reshape(1, 128)
    x1, comb = _outproj(attn.reshape(N, 512), rnn.reshape(N, D_RNN), x2, row1(attn_out_gain[0]),
                        row1(rnn_out_gain[0]), w_out[0].astype(BF16), row1(ln1_g[0]), row1(ln1_b[0]),
                        wr2, br, N)

    out = _moe(x1, comb, p[0].reshape(N, D_PLE), wup_b.reshape(N_EXPERTS, D_MODEL, 2 * D_FF),
               wdn_b.reshape(N_EXPERTS, D_FF, D_MODEL), ple_gate_w[0].astype(BF16), ple_w[0].astype(BF16),
               row1(ln2_g[0]), row1(ln2_b[0]), N)
    return out.reshape(B, T, D_MODEL)
```

```python
import functools
import math

import jax
import jax.numpy as jnp
from jax import lax
from jax.experimental import pallas as pl
from jax.experimental.pallas import tpu as pltpu

F32 = jnp.float32
BF16 = jnp.bfloat16
NEG = -1e30
LOG2E = 1.4426950408889634

D_MODEL = 1024
HEAD_DIM = 64
N_HEADS = 8
N_GROUPS_KV = 2
GQA_REP = 4
D_ATTN = 512
D_RNN = 512
D_KV = 128
CMP_LEN = 32
CMP_STRIDE = 16
SEL_BLOCK = 64
N_SEL = 16
WINDOW = 512
N_BUCKETS = 32
MAX_DISTANCE = 128
N_EXPERTS = 16
EXPERTS_PER_GROUP = 4
N_EXP_GROUPS = 4
D_FF = 512
D_PLE = 256
ALPHA = 2.0 ** 0.25
LN_EPS = 1e-5
RMS_EPS = 1e-6
FORCE_BONUS = 1e4
RG_C = 8.0

TQ = 128
TILES = 2
N_SLC = 128
C_PAD = 640
C_OFF = 16
CMP_ROW_LEVELS = (256, 384, 512, C_PAD)
SEL_ROW_LEVELS = (32, 64, 96, N_SLC)
FAR = 512
TAIL = WINDOW + TQ
BAND = 24
GATE_ROWS = 32
V_ROWS = 80
MOE_TILE = 1024
MOE_SUB = 512
MOE_SLOTS = 128
VMEM_LIMIT = 56 * 1024 * 1024


def _cparams(sem, vmem=None):
    return pltpu.CompilerParams(dimension_semantics=sem, vmem_limit_bytes=vmem)


def _inproj_kernel(x_ref, wm_ref, wt_ref, ka_ref, kw_ref, kc_ref, vc_ref, rx_ref, ry_ref, t_ref, vs_ref,
                   vw_ref, gt_ref):
    tm = x_ref.shape[0]
    j = pl.program_id(1)
    ones_rows = jnp.where(lax.broadcasted_iota(jnp.int32, (V_ROWS - HEAD_DIM, tm), 0) == 0, 1.0, 0.0).astype(BF16)
    lane = lax.broadcasted_iota(jnp.int32, (tm, 2 * N_SLC), 1)

    @pl.when(j == 0)
    def _():
        ka_ref[...] = jnp.where(lane == N_SLC - 1, 1.0, 0.0).astype(BF16)
        kw_ref[...] = jnp.where(lane == N_SLC, 1.0, 0.0).astype(BF16)
        for v_ref in (vs_ref, vw_ref):
            for g in range(N_GROUPS_KV):
                v_ref[g, :HEAD_DIM, :] = jnp.zeros((HEAD_DIM, tm), BF16)
                v_ref[g, HEAD_DIM:, :] = ones_rows

    @pl.when(j > 0)
    def _():
        _inproj_tile(x_ref, wm_ref, wt_ref, ka_ref, kw_ref, kc_ref, vc_ref, rx_ref, ry_ref, t_ref, vs_ref,
                     vw_ref, gt_ref, (j - 1) * tm, ones_rows)


def _inproj_tile(x_ref, wm_ref, wt_ref, ka_ref, kw_ref, kc_ref, vc_ref, rx_ref, ry_ref, t_ref, vs_ref, vw_ref,
                 gt_ref, t0, ones_rows):
    tm = x_ref.shape[0]
    xb = x_ref[...].astype(BF16)

    def mm(lo, hi):
        return jnp.dot(xb, wm_ref[:, lo:hi], preferred_element_type=F32)

    k_pair = mm(0, 256)
    blk = (t0 + lax.broadcasted_iota(jnp.int32, (tm, N_SLC), 0)) // SEL_BLOCK
    ka_ref[:, :N_SLC] = jnp.where(blk == lax.broadcasted_iota(jnp.int32, (tm, N_SLC), 1), 1.0, 0.0).astype(BF16)
    ka_ref[:, N_SLC:] = k_pair[:, :128].astype(BF16)
    kw_ref[:, :N_SLC] = k_pair[:, 128:].astype(BF16)
    kw_ref[:, N_SLC:] = jnp.zeros((tm, N_SLC), BF16)
    c_pair = mm(256, 512)
    kc_ref[...] = c_pair[:, :128]
    vc_ref[...] = c_pair[:, 128:]
    rx_ref[...] = mm(512, 1024)
    ry_ref[...] = mm(1024, 1536)
    tr = lax.dot_general(wt_ref[...], xb, (((1,), (1,)), ((), ())), preferred_element_type=F32)
    t_ref[...] = tr[:D_ATTN].astype(BF16)
    for v_ref, base in ((vs_ref, D_ATTN), (vw_ref, D_ATTN + D_KV)):
        for g in range(N_GROUPS_KV):
            v_ref[g, :HEAD_DIM, :] = tr[base + g * HEAD_DIM:base + (g + 1) * HEAD_DIM].astype(BF16)
            v_ref[g, HEAD_DIM:, :] = ones_rows
    gt_ref[...] = jax.nn.sigmoid(tr[D_ATTN + 2 * D_KV:])


def _inproj(x2, w_main, w_t, B, T):
    N = B * T
    tm = WINDOW
    nt = T // tm
    tp = T + WINDOW
    data = lambda j: jnp.maximum(j - 1, 0)
    row = lambda w: pl.BlockSpec((tm, w), lambda b, j: (b * nt + data(j), 0))
    full = lambda a: pl.BlockSpec(a.shape, lambda b, j: (0,) * a.ndim)
    tr = lambda r: pl.BlockSpec((None, r, tm), lambda b, j: (b, 0, data(j)))
    lead = pl.BlockSpec((None, tm, 2 * N_SLC), lambda b, j: (b, j, 0))
    lead_t = pl.BlockSpec((None, N_GROUPS_KV, V_ROWS, tm), lambda b, j: (b, 0, 0, j))
    return pl.pallas_call(
        _inproj_kernel,
        grid=(B, nt + 1),
        in_specs=[row(D_MODEL), full(w_main), full(w_t)],
        out_specs=[lead, lead, row(128), row(128), row(512), row(512), tr(D_ATTN), lead_t, lead_t,
                   tr(GATE_ROWS)],
        out_shape=[jax.ShapeDtypeStruct((B, tp, 2 * N_SLC), BF16),
                   jax.ShapeDtypeStruct((B, tp, 2 * N_SLC), BF16),
                   jax.ShapeDtypeStruct((N, 128), F32),
                   jax.ShapeDtypeStruct((N, 128), F32),
                   jax.ShapeDtypeStruct((N, 512), F32),
                   jax.ShapeDtypeStruct((N, 512), F32),
                   jax.ShapeDtypeStruct((B, D_ATTN, T), BF16),
                   jax.ShapeDtypeStruct((B, N_GROUPS_KV, V_ROWS, tp), BF16),
                   jax.ShapeDtypeStruct((B, N_GROUPS_KV, V_ROWS, tp), BF16),
                   jax.ShapeDtypeStruct((B, GATE_ROWS, T), F32)],
        compiler_params=_cparams(("parallel", "arbitrary"), VMEM_LIMIT),
        name="inproj",
    )(x2, w_main, w_t)


def _compress_kernel(kr_ref, vr_ref, pek_ref, pev_ref, wk_ref, wv_ref, oh_ref, kc_ref, vct_ref, *,
                     n_cmp, n_chunk):
    def compress(raw_ref, pe_ref, w_ref):
        lo = jnp.zeros((n_chunk, 2 * HEAD_DIM), F32)
        hi = jnp.zeros((n_chunk, 2 * HEAD_DIM), F32)
        for j in range(CMP_STRIDE):
            a = raw_ref[pl.ds(j, n_chunk, stride=CMP_STRIDE), :]
            lo = lo + jnp.dot((a + pe_ref[j:j + 1, :]).astype(BF16), w_ref[j],
                              preferred_element_type=F32)
            hi = hi + jnp.dot((a + pe_ref[CMP_STRIDE + j:CMP_STRIDE + j + 1, :]).astype(BF16),
                              w_ref[CMP_STRIDE + j], preferred_element_type=F32)
        hi = pltpu.roll(hi, n_chunk - 1, axis=0)
        rid = lax.broadcasted_iota(jnp.int32, (n_chunk, 2 * HEAD_DIM), 0)
        out = jnp.where(rid < n_cmp, lo + hi, 0.0)
        return jnp.concatenate([jnp.zeros((C_OFF, 2 * HEAD_DIM), F32), out,
                                jnp.zeros((C_PAD - C_OFF - n_chunk, 2 * HEAD_DIM), F32)], axis=0)

    kc_ref[:, 0:2 * HEAD_DIM] = compress(kr_ref, pek_ref, wk_ref).astype(BF16)
    kc_ref[:, 2 * HEAD_DIM:] = oh_ref[...]
    vct_ref[...] = compress(vr_ref, pev_ref, wv_ref).T.astype(BF16)


def _compress(kc_raw, vc_raw, pek, pev, wk, wv, grp_onehot, B, T, n_cmp):
    n_chunk = T // CMP_STRIDE
    blk = pl.BlockSpec((None, T, 2 * HEAD_DIM), lambda b: (b, 0, 0))
    full = lambda a: pl.BlockSpec(a.shape, lambda b: (0,) * a.ndim)
    return pl.pallas_call(
        functools.partial(_compress_kernel, n_cmp=n_cmp, n_chunk=n_chunk),
        grid=(B,),
        in_specs=[blk, blk, full(pek), full(pev), full(wk), full(wv), full(grp_onehot)],
        out_specs=[pl.BlockSpec((None, C_PAD, 256), lambda b: (b, 0, 0)),
                   pl.BlockSpec((None, 2 * HEAD_DIM, C_PAD), lambda b: (b, 0, 0))],
        out_shape=[jax.ShapeDtypeStruct((B, C_PAD, 256), BF16),
                   jax.ShapeDtypeStruct((B, 2 * HEAD_DIM, C_PAD), BF16)],
        compiler_params=_cparams(("parallel",), VMEM_LIMIT),
        name="compress",
    )(kc_raw, vc_raw, pek, pev, wk, wv, grp_onehot)


def _bias_kernel(tab_ref, dt_ref, bt_ref, dc_ref, bc_ref, os_ref, ow_ref, oc_ref):
    h = pl.program_id(0) * GQA_REP + pl.program_id(1)

    def lookup(bk):
        out = jnp.zeros(bk.shape, F32)
        for b in range(N_BUCKETS):
            out = jnp.where(bk == b, tab_ref[h, b], out)
        return (out - tab_ref[h, N_BUCKETS - 1]) * LOG2E

    dt = dt_ref[...]
    v = lookup(bt_ref[...])
    os_ref[...] = jnp.where(dt >= 0, v, NEG)
    ow_ref[...] = jnp.where((dt >= 0) & (dt < WINDOW), v, NEG)
    oc_ref[...] = jnp.where(dc_ref[...] >= 0, lookup(bc_ref[...]), NEG)


def _t5_bucket(dist):
    max_exact = N_BUCKETS // 2
    d = jnp.maximum(dist, 0)
    df = jnp.maximum(d, 1).astype(F32)
    large = max_exact + (jnp.log(df / max_exact) / math.log(MAX_DISTANCE / max_exact)
                         * (N_BUCKETS - max_exact)).astype(jnp.int32)
    large = jnp.minimum(large, N_BUCKETS - 1)
    return jnp.where(d < max_exact, d, large)


def _bias_tiles(rel_bias):
    tq = jnp.arange(TQ, dtype=jnp.int32)
    dt = tq[None, :] + WINDOW - jnp.arange(TAIL, dtype=jnp.int32)[:, None]
    cc = jnp.arange(BAND, dtype=jnp.int32) - C_OFF
    dc = tq[None, :] - (cc[:, None] * CMP_STRIDE + CMP_LEN - 1)
    full = lambda a: pl.BlockSpec(a.shape, lambda g, r: (0,) * a.ndim)
    out = lambda rows: pl.BlockSpec((None, rows, TQ), lambda g, r: (g, 0, r))
    return pl.pallas_call(
        _bias_kernel,
        grid=(N_GROUPS_KV, GQA_REP),
        in_specs=[pl.BlockSpec(memory_space=pltpu.SMEM), full(dt), full(dt), full(dc), full(dc)],
        out_specs=[out(TAIL), out(TAIL), out(BAND)],
        out_shape=[jax.ShapeDtypeStruct((N_GROUPS_KV, TAIL, GQA_REP * TQ), F32),
                   jax.ShapeDtypeStruct((N_GROUPS_KV, TAIL, GQA_REP * TQ), F32),
                   jax.ShapeDtypeStruct((N_GROUPS_KV, BAND, GQA_REP * TQ), F32)],
        compiler_params=_cparams(("arbitrary", "arbitrary")),
        name="biastile",
    )(rel_bias.T, dt, _t5_bucket(dt), dc, _t5_bucket(dc))


def _attn_kernel(qt_ref, gt_ref, kc_ref, vct_ref, kaug_ref, vst_ref, kw_ref, vwt_ref, ds_ref, dw_ref,
                 bc_ref, o_ref, *scratch, n_far_max):
    n_hg = TILES * N_GROUPS_KV
    sc_ref, sw_ref = (scratch[n_hg * k:n_hg * (k + 1)] for k in range(2))
    qa_ref, qat_ref, m_ref, acc_ref, sb_ref, sm_ref, oc_ref, ow_ref, imp_ref = scratch[2 * n_hg:]
    cols4 = GQA_REP * TQ
    groups = range(N_GROUPS_KV)
    zeros_q = jnp.zeros((HEAD_DIM, cols4), BF16)
    grp = lax.broadcasted_iota(jnp.int32, (128, cols4), 0)
    neg_row = jnp.where(grp == 0, NEG, 0.0).astype(BF16)
    tq_col = lax.broadcasted_iota(jnp.int32, (1, cols4), 1) % TQ

    def chunk_start(c):
        return pl.multiple_of(WINDOW + jnp.minimum(c, n_far_max - 1) * FAR, FAR)

    def scores(h, c, par):
        kch = kaug_ref[pl.ds(chunk_start(c), FAR), :]
        for g in groups:
            sf = jnp.dot(kch, qa_ref[2 * h + g], preferred_element_type=F32)
            sb_ref[4 * h + 2 * g + par] = sf
            sm_ref[4 * h + 2 * g + par] = jnp.max(sf, axis=0, keepdims=True)

    def consume(h, nfar, c, par):
        c0 = chunk_start(c)
        live = c < nfar
        for g in groups:
            hg = 2 * h + g
            sf = sb_ref[4 * h + 2 * g + par]
            mprev = m_ref[hg]
            mcand = jnp.maximum(mprev, sm_ref[4 * h + 2 * g + par])
            mnew = jnp.where(live, mcand, mprev)
            pf = jnp.exp2(sf - jnp.where(live, mcand, -NEG)).astype(BF16)
            acc_ref[hg] = jnp.exp2(mprev - mnew) * acc_ref[hg] + jnp.dot(
                vst_ref[g, :, pl.ds(c0, FAR)], pf, preferred_element_type=F32)
            m_ref[hg] = mnew

    tiles = range(TILES)
    chains = [(h, g) for h in tiles for g in groups]
    t_i = [pl.program_id(1) * TILES + h for h in tiles]
    t_qs = [pl.multiple_of(i * TQ, TQ) for i in t_i]
    t_nfar = [jnp.maximum(i - 1, 0) // 4 for i in t_i]
    def qpad(h, g):
        qgt = jnp.concatenate(
            [qt_ref[(GQA_REP * g + r) * HEAD_DIM:(GQA_REP * g + r + 1) * HEAD_DIM, h * TQ:(h + 1) * TQ]
             for r in range(GQA_REP)], axis=1)
        return jnp.concatenate([qgt, zeros_q] if g == 0 else [zeros_q, qgt], axis=0)

    def compressed_and_window(rows):
        for h, g in chains:
            i = t_i[h]
            gmask = jnp.where((grp < C_OFF // 8) | (grp >= i + BAND // 8), NEG, 0.0).astype(BF16)
            band0 = pl.multiple_of(i * 8, 8)
            sc_ref[2 * h + g][0:rows, :] = jnp.dot(kc_ref[0:rows, :],
                                                   jnp.concatenate([qpad(h, g), gmask], axis=0),
                                                   preferred_element_type=F32)
            sc_ref[2 * h + g][pl.ds(band0, BAND), :] = sc_ref[2 * h + g][pl.ds(band0, BAND), :] + bc_ref[g]
        w_max = {}
        for h, g in chains:
            sw = jnp.dot(kw_ref[pl.ds(t_qs[h], TAIL), :], jnp.concatenate([qpad(h, g), neg_row], axis=0),
                         preferred_element_type=F32) + dw_ref[g]
            sw_ref[2 * h + g][...] = sw
            w_max[h, g] = jnp.max(sw, axis=0, keepdims=True)
        for h, g in chains:
            s = sc_ref[2 * h + g][0:rows, :]
            e = jnp.exp2(s - jnp.max(s, axis=0, keepdims=True))
            has_cmp = t_qs[h] + tq_col >= CMP_LEN - 1
            pinv = jnp.where(has_cmp, 1.0 / jnp.sum(e, axis=0, keepdims=True), 0.0)
            res = jnp.dot(vct_ref[:, 0:rows], e.astype(BF16), preferred_element_type=F32)
            oc_ref[2 * h + g] = res[0:2 * HEAD_DIM, :] * pinv
            imp_ref[2 * h + g] = sum(res[2 * HEAD_DIM:, r * TQ:(r + 1) * TQ] * pinv[:, r * TQ:(r + 1) * TQ]
                                     for r in range(GQA_REP))
        for h, g in chains:
            pw = jnp.exp2(sw_ref[2 * h + g][...] - w_max[h, g]).astype(BF16)
            acc_w = jnp.dot(vwt_ref[g, :, pl.ds(t_qs[h], TAIL)], pw, preferred_element_type=F32)
            ow_ref[2 * h + g] = acc_w[0:HEAD_DIM, :] * (1.0 / acc_w[HEAD_DIM:HEAD_DIM + 1, :])

    step = pl.program_id(1)
    first_step = 0
    for rows in CMP_ROW_LEVELS:
        end_step = ((rows - BAND) // 8 - TILES + 1) // TILES + 1
        in_level = step >= first_step if rows == C_PAD else (step >= first_step) & (step < end_step)
        pl.when(in_level)(functools.partial(compressed_and_window, rows))
        first_step = end_step

    def select_blocks(nsel):
        sid = lax.broadcasted_iota(jnp.int32, (nsel, TQ), 0)
        sid_f = sid.astype(F32)
        lane = lax.broadcasted_iota(jnp.int32, (nsel, TQ), 1)
        score, mbt = {}, {}
        for h, g in chains:
            cur = (t_qs[h] + lane) // SEL_BLOCK
            forced = (sid == 0) | (sid == cur) | (sid == cur - 1)
            score[h, g] = jnp.where(forced, -jnp.inf, jnp.where(sid <= cur, imp_ref[2 * h + g, 0:nsel, :], NEG))
            mbt[h, g] = jnp.where(forced, 0.0, NEG)
        for _ in range(N_SEL - 3):
            for ch in chains:
                cm = jnp.max(score[ch], axis=0, keepdims=True)
                first = jnp.min(jnp.where(score[ch] == cm, sid_f, float(N_SLC)), axis=0, keepdims=True)
                pick = sid_f == first
                mbt[ch] = jnp.where(pick, 0.0, mbt[ch])
                score[ch] = jnp.where(pick, -jnp.inf, score[ch])
        for h, g in chains:
            mbt_tail = jnp.where(sid < t_nfar[h] * (FAR // SEL_BLOCK), NEG, mbt[h, g])
            for dst, m in ((qa_ref, mbt[h, g]), (qat_ref, mbt_tail)):
                m = jnp.concatenate([m, jnp.full((N_SLC - nsel, TQ), NEG, F32)], axis=0) if nsel < N_SLC else m
                dst[2 * h + g, 0:N_SLC, :] = jnp.concatenate([m.astype(BF16)] * GQA_REP, axis=1)

    first_step = 0
    for nsel in SEL_ROW_LEVELS:
        end_step = nsel // (TILES * TQ // SEL_BLOCK)
        in_level = step >= first_step if nsel == N_SLC else (step >= first_step) & (step < end_step)
        pl.when(in_level)(functools.partial(select_blocks, nsel))
        first_step = end_step

    for h, g in chains:
        qa_ref[2 * h + g, N_SLC:, :] = qpad(h, g)
        qa_tail = jnp.concatenate([qat_ref[2 * h + g], qpad(h, g)], axis=0)
        st = jnp.dot(kaug_ref[pl.ds(t_qs[h], TAIL), :], qa_tail, preferred_element_type=F32) + ds_ref[g]
        sc_ref[2 * h + g][...] = st
        m_ref[2 * h + g] = jnp.max(st, axis=0, keepdims=True)
    for h in tiles:
        scores(h, 0, 0)
    for h, g in chains:
        pt = jnp.exp2(sc_ref[2 * h + g][...] - m_ref[2 * h + g]).astype(BF16)
        acc_ref[2 * h + g] = jnp.dot(vst_ref[g, :, pl.ds(t_qs[h], TAIL)], pt, preferred_element_type=F32)

    def far_pair(kk, carry):
        c = 2 * kk
        for h in tiles:
            scores(h, c + 1, 1)
        for h in tiles:
            consume(h, t_nfar[h], c, 0)
        for h in tiles:
            scores(h, c + 2, 0)
        for h in tiles:
            consume(h, t_nfar[h], c + 1, 1)
        return carry

    n_chunks = t_nfar[TILES - 1]
    n_pairs = jnp.maximum(n_chunks - 1, 0) // 2
    lax.fori_loop(0, n_pairs, far_pair, 0)

    @pl.when(n_chunks - 2 * n_pairs == 1)
    def _():
        for h in tiles:
            consume(h, t_nfar[h], n_chunks - 1, 0)

    @pl.when(n_chunks - 2 * n_pairs == 2)
    def _():
        for h in tiles:
            scores(h, n_chunks - 1, 1)
        for h in tiles:
            consume(h, t_nfar[h], n_chunks - 2, 0)
        for h in tiles:
            consume(h, t_nfar[h], n_chunks - 1, 1)

    for h in tiles:
        lanes = slice(h * TQ, (h + 1) * TQ)
        out_rows = []
        for g in groups:
            dsl = slice(g * HEAD_DIM, (g + 1) * HEAD_DIM)
            hg = 2 * h + g
            o_s = acc_ref[hg, 0:HEAD_DIM, :] * (1.0 / acc_ref[hg, HEAD_DIM:HEAD_DIM + 1, :])
            for r in range(GQA_REP):
                cs = slice(r * TQ, (r + 1) * TQ)
                gc = 3 * (GQA_REP * g + r)
                out_rows.append(gt_ref[gc:gc + 1, lanes] * oc_ref[hg, dsl, cs]
                                + gt_ref[gc + 1:gc + 2, lanes] * o_s[:, cs]
                                + gt_ref[gc + 2:gc + 3, lanes] * ow_ref[hg, :, cs])
        o_ref[h * TQ:(h + 1) * TQ, :] = jnp.concatenate(out_rows, axis=0).T


def _attention(t_all, gates_t, kc2, vct, kaug, vst, kw_pad, vwt, ds, dw, bc, B, T):
    tp = T + WINDOW
    tqs = TQ * TILES
    per_b = lambda shape: pl.BlockSpec((None,) + shape, lambda b, i: (b,) + (0,) * len(shape),
                                      pipeline_mode=pl.Buffered(1))
    const = lambda a: pl.BlockSpec(a.shape, lambda b, i: (0,) * a.ndim, pipeline_mode=pl.Buffered(1))
    cols4 = GQA_REP * TQ
    n_hg = TILES * N_GROUPS_KV
    return pl.pallas_call(
        functools.partial(_attn_kernel, n_far_max=T // FAR),
        grid=(B, T // tqs),
        in_specs=[pl.BlockSpec((None, 512, tqs), lambda b, i: (b, 0, i)),
                  pl.BlockSpec((None, GATE_ROWS, tqs), lambda b, i: (b, 0, i)),
                  per_b((C_PAD, 256)),
                  per_b((256, C_PAD)),
                  per_b((tp, 256)),
                  per_b((N_GROUPS_KV, V_ROWS, tp)),
                  per_b((tp, 256)),
                  per_b((N_GROUPS_KV, V_ROWS, tp)),
                  const(ds), const(dw), const(bc)],
        out_specs=pl.BlockSpec((None, tqs, 512), lambda b, i: (b, i, 0)),
        out_shape=jax.ShapeDtypeStruct((B, T, 512), F32),
        scratch_shapes=([pltpu.VMEM((C_PAD, cols4), F32)] * n_hg
                        + [pltpu.VMEM((TAIL, cols4), F32)] * n_hg
                        + [pltpu.VMEM((n_hg, N_SLC + 2 * HEAD_DIM, cols4), BF16),
                           pltpu.VMEM((n_hg, N_SLC, cols4), BF16),
                           pltpu.VMEM((n_hg, 1, cols4), F32),
                           pltpu.VMEM((n_hg, V_ROWS, cols4), F32),
                           pltpu.VMEM((2 * n_hg, FAR, cols4), F32),
                           pltpu.VMEM((2 * n_hg, 1, cols4), F32),
                           pltpu.VMEM((n_hg, 2 * HEAD_DIM, cols4), F32),
                           pltpu.VMEM((n_hg, HEAD_DIM, cols4), F32),
                           pltpu.VMEM((n_hg, N_SLC, TQ), F32)]),
        compiler_params=_cparams(("parallel", "arbitrary"), VMEM_LIMIT),
        name="attn",
    )(t_all, gates_t, kc2, vct, kaug, vst, kw_pad, vwt, ds, dw, bc)


def _rglru_kernel(*refs, L):
    J = L // 8
    n_col = D_RNN // 128
    rx_refs, ry_refs = refs[:n_col], refs[n_col:2 * n_col]
    (cw_ref, cb_ref, wa_ref, ba_ref, wx_ref, bx_ref, sp_ref, eu_ref, ed_ref, o_ref, eub_ref, edb_ref,
     xprev_ref, h_ref, os_ref) = refs[2 * n_col:]
    eub_ref[...] = eu_ref[...].astype(BF16)
    edb_ref[...] = ed_ref[...].astype(BF16)

    @pl.when(pl.program_id(1) == 0)
    def _():
        xprev_ref[...] = jnp.zeros(xprev_ref.shape, F32)
        h_ref[...] = jnp.zeros(h_ref.shape, F32)

    def strided(col_refs):
        return jnp.concatenate(
            [jnp.concatenate([ref[pl.ds(j, 8, stride=J), :] for j in range(J)], axis=0)
             for ref in col_refs], axis=1)

    xp = strided(rx_refs)
    yp = strided(ry_refs)
    sub = lax.broadcasted_iota(jnp.int32, (8, D_RNN), 0)

    def delayed(d):
        heads = []
        for j in range(d):
            src = pltpu.roll(xp[(J + j - d) * 8:(J + j - d + 1) * 8], 1, axis=0)
            heads.append(jnp.where(sub == 0, xprev_ref[8 + j - d:9 + j - d, :], src))
        return jnp.concatenate(heads + [xp[:L - 8 * d]], axis=0)

    xc = (cb_ref[...] + cw_ref[0:1, :] * delayed(3) + cw_ref[1:2, :] * delayed(2)
          + cw_ref[2:3, :] * delayed(1) + cw_ref[3:4, :] * xp)
    xprev_ref[...] = jnp.concatenate([ref[L - 8:L, :] for ref in rx_refs], axis=1)
    xcb = xc.astype(BF16)
    r = jax.nn.sigmoid(jnp.dot(xcb, wa_ref[...], preferred_element_type=F32) + ba_ref[...])
    ig = jax.nn.sigmoid(jnp.dot(xcb, wx_ref[...], preferred_element_type=F32) + bx_ref[...])
    a = jnp.exp(-RG_C * r * sp_ref[...])
    bt = jnp.sqrt(1.0 - a * a) * (ig * xc)

    h = jnp.zeros((8, D_RNN), F32)
    dec = jnp.ones((8, D_RNN), F32)
    hs, decs = [], []
    for j in range(J):
        aj = a[j * 8:(j + 1) * 8]
        h = aj * h + bt[j * 8:(j + 1) * 8]
        dec = aj * dec
        hs.append(h)
        decs.append(dec)
    carry = h_ref[0:1, :]
    carries = []
    for s in range(8):
        carries.append(carry)
        carry = dec[s:s + 1, :] * carry + h[s:s + 1, :]
    h_ref[...] = jnp.broadcast_to(carry, h_ref.shape)
    h_in = jnp.concatenate(carries, axis=0)
    cdf_c = math.sqrt(2.0 / math.pi)
    for j in range(J):
        y = yp[j * 8:(j + 1) * 8]
        cdf = 0.5 * (1.0 + jnp.tanh(cdf_c * (y + 0.044715 * (y * y * y))))
        res = (hs[j] + decs[j] * h_in) * (y * cdf)
        for c in range(n_col):
            os_ref[c, pl.ds(j, 8, stride=J), :] = res[:, c * 128:(c + 1) * 128]
    for c in range(n_col):
        o_ref[:, c * 128:(c + 1) * 128] = os_ref[c]


def _rglru(rx, ry, conv_w, conv_b, wa_bd, b_a, wx_bd, b_x, sp, w_up, w_dn, B, T):
    L = 512
    steps = B * T // L
    nt = T // L
    up_rows, dn_rows = w_up.shape[0] // steps, w_dn.shape[0] // steps
    assert up_rows * steps == w_up.shape[0] and dn_rows * steps == w_dn.shape[0] and dn_rows % 16 == 0
    slab = lambda rows, a: pl.BlockSpec((rows, a.shape[1]), lambda b, t: (b * nt + t, 0))
    blk = pl.BlockSpec((None, L, D_RNN), lambda b, t: (b, t, 0))
    n_col = D_RNN // 128
    cols = [pl.BlockSpec((None, L, 128), functools.partial(lambda b, t, c: (b, t, c), c=c))
            for c in range(n_col)]
    full = lambda a: pl.BlockSpec(a.shape, lambda b, t: (0,) * a.ndim)
    return pl.pallas_call(
        functools.partial(_rglru_kernel, L=L),
        grid=(B, T // L),
        in_specs=cols + cols + [full(conv_w), full(conv_b), full(wa_bd), full(b_a), full(wx_bd),
                                full(b_x), full(sp), slab(up_rows, w_up), slab(dn_rows, w_dn)],
        out_specs=[blk, slab(up_rows, w_up), slab(dn_rows, w_dn)],
        out_shape=[jax.ShapeDtypeStruct((B, T, D_RNN), F32),
                   jax.ShapeDtypeStruct(w_up.shape, BF16),
                   jax.ShapeDtypeStruct(w_dn.shape, BF16)],
        scratch_shapes=[pltpu.VMEM((8, D_RNN), F32), pltpu.VMEM((8, D_RNN), F32),
                        pltpu.VMEM((n_col, L, 128), F32)],
        compiler_params=_cparams(("parallel", "arbitrary"), VMEM_LIMIT),
        name="rglru",
    )(*([rx] * n_col + [ry] * n_col), conv_w, conv_b, wa_bd, b_a, wx_bd, b_x, sp, w_up, w_dn)


def _layer_norm(y, g, b):
    mu = jnp.mean(y, axis=-1, keepdims=True)
    d = y - mu
    var = jnp.mean(d * d, axis=-1, keepdims=True)
    return d * lax.rsqrt(var + LN_EPS) * g + b


def _outproj_kernel(at_ref, rn_ref, x_ref, ga_ref, gr_ref, wo_ref, g1_ref, b1_ref, wr2_ref, br_ref,
                    x1_ref, comb_ref):
    a = at_ref[...]
    rn = rn_ref[...]
    ha = a * lax.rsqrt(jnp.mean(a * a, axis=-1, keepdims=True) + RMS_EPS) * ga_ref[...]
    hr = rn * lax.rsqrt(jnp.mean(rn * rn, axis=-1, keepdims=True) + RMS_EPS) * gr_ref[...]
    heads = jnp.concatenate([ha, hr], axis=1).astype(BF16)
    mix = jnp.dot(heads, wo_ref[...], preferred_element_type=F32)
    x1 = _layer_norm(ALPHA * x_ref[...] + mix, g1_ref[...], b1_ref[...])
    x1_ref[...] = x1

    xh = x1.astype(BF16)
    xl = (x1 - xh.astype(F32)).astype(BF16)
    hh_hl = jnp.dot(xh, wr2_ref[...], preferred_element_type=F32)
    lg = (hh_hl[:, :128] + jnp.dot(xl, wr2_ref[:, :128], preferred_element_type=F32)
          + hh_hl[:, 128:]) + br_ref[...]
    lane = lax.broadcasted_iota(jnp.int32, lg.shape, 1)
    lane_f = lane.astype(F32)
    big = 1e9
    isg = lane < N_EXP_GROUPS
    gmax = jnp.max(jnp.where(isg, lg, -jnp.inf), axis=-1, keepdims=True)
    pg_top = 1.0 / jnp.sum(jnp.where(isg, jnp.exp(lg - gmax), 0.0), axis=-1, keepdims=True)
    gi = jnp.min(jnp.where(isg & (lg == gmax), lane_f, big), axis=-1, keepdims=True)
    egrp = ((lane - N_EXP_GROUPS) // EXPERTS_PER_GROUP).astype(F32)
    ise = (lane >= N_EXP_GROUPS) & (lane < N_EXP_GROUPS + N_EXPERTS) & (egrp == gi)
    emax = jnp.max(jnp.where(ise, lg, -jnp.inf), axis=-1, keepdims=True)
    i1 = jnp.min(jnp.where(ise & (lg == emax), lane_f, big), axis=-1, keepdims=True)
    rest = ise & (lane_f != i1)
    m2 = jnp.max(jnp.where(rest, lg, -jnp.inf), axis=-1, keepdims=True)
    i2 = jnp.min(jnp.where(rest & (lg == m2), lane_f, big), axis=-1, keepdims=True)
    e2 = jnp.exp(m2 - emax)
    inv = pg_top / (1.0 + e2)
    comb_ref[...] = jnp.where(lane_f == i1, inv, 0.0) + jnp.where(lane_f == i2, inv * e2, 0.0)


def _outproj(attn, rnn, x2, ga, gr, wo, g1, b1, wr2, br, N):
    tm = 512
    row = lambda w: pl.BlockSpec((tm, w), lambda i: (i, 0))
    full = lambda a: pl.BlockSpec(a.shape, lambda i: (0,) * a.ndim)
    return pl.pallas_call(
        _outproj_kernel,
        grid=(N // tm,),
        in_specs=[row(512), row(512), row(D_MODEL), full(ga), full(gr), full(wo), full(g1), full(b1),
                  full(wr2), full(br)],
        out_specs=[row(D_MODEL), row(128)],
        out_shape=[jax.ShapeDtypeStruct((N, D_MODEL), F32), jax.ShapeDtypeStruct((N, 128), F32)],
        compiler_params=_cparams(("parallel",), VMEM_LIMIT),
        name="outproj",
    )(attn, rnn, x2, ga, gr, wo, g1, b1, wr2, br)


def _moe_kernel(x1_ref, comb_ref, p_ref, tri_ref, wup_ref, wdn_ref, wg_ref, wp_ref, g2_ref, b2_ref, o_ref,
                acc_ref, xb_ref, rank_ref, rankt_ref, combt_ref):
    pi = pl.program_id(1)
    n_sub = x1_ref.shape[0] // MOE_SUB
    sub_rows = [slice(h * MOE_SUB, (h + 1) * MOE_SUB) for h in range(n_sub)]
    lane_col = N_EXP_GROUPS

    @pl.when(pi == 0)
    def _():
        x1 = x1_ref[...]
        xb = x1.astype(BF16)
        xb_ref[...] = xb
        gate = jax.nn.sigmoid(jnp.dot(xb, wg_ref[...], preferred_element_type=F32))
        ple = gate * jnp.dot(p_ref[...].astype(BF16), wp_ref[...], preferred_element_type=F32)
        acc_ref[...] = ALPHA * x1 + ple
        for rows in sub_rows:
            comb = comb_ref[rows, :]
            chosen = comb > 0.0
            rank = jnp.dot(tri_ref[...], jnp.where(chosen, 1.0, 0.0).astype(BF16),
                           preferred_element_type=F32)
            rank = jnp.where(chosen, rank, -1.0)
            rank_ref[rows, :] = rank
            rankt_ref[:, rows] = rank.T
            combt_ref[:, rows] = comb.T

    lane = lax.broadcasted_iota(jnp.int32, (MOE_SUB, 128), 1)
    slot_r = lax.broadcasted_iota(jnp.int32, (MOE_SLOTS, MOE_SUB), 0).astype(F32)
    slot_c = lax.broadcasted_iota(jnp.int32, (MOE_SUB, MOE_SLOTS), 1).astype(F32)

    def route_pair(base):
        chains = [(rows, k) for rows in sub_rows for k in range(2)]
        cmask, xc, u, hsw, y, ys, ct = {}, {}, {}, {}, {}, {}, {}
        for ch in chains:
            rows, k = ch
            rk_row = rankt_ref[pl.ds(2 * pi + k + lane_col, 1), rows] - base
            cmask[ch] = rk_row == slot_r
            xc[ch] = jnp.dot(jnp.where(cmask[ch], 1.0, 0.0).astype(BF16), xb_ref[rows, :],
                             preferred_element_type=F32).astype(BF16)
        for ch in chains:
            u[ch] = jnp.dot(xc[ch], wup_ref[ch[1]], preferred_element_type=F32)
        for ch in chains:
            ua = u[ch][:, :D_FF]
            hsw[ch] = (ua * jax.nn.sigmoid(ua) * u[ch][:, D_FF:]).astype(BF16)
        for ch in chains:
            y[ch] = jnp.dot(hsw[ch], wdn_ref[ch[1]], preferred_element_type=F32)
        for ch in chains:
            rows, k = ch
            e_lane = 2 * pi + k + lane_col
            w_row = combt_ref[pl.ds(e_lane, 1), rows]
            w_slot = jnp.sum(jnp.where(cmask[ch], w_row, 0.0), axis=1, keepdims=True)
            ys[ch] = (w_slot * y[ch]).astype(BF16)
            rk_col = jnp.sum(jnp.where(lane == e_lane, rank_ref[rows, :], 0.0), axis=1,
                             keepdims=True) - base
            ct[ch] = jnp.where(rk_col == slot_c, 1.0, 0.0).astype(BF16)
        for rows in sub_rows:
            pair = [(rows, k) for k in range(2)]
            ctp = jnp.concatenate([ct[ch] for ch in pair], axis=1)
            acc_ref[rows, :] = acc_ref[rows, :] + jnp.dot(
                ctp, jnp.concatenate([ys[ch] for ch in pair], axis=0), preferred_element_type=F32)

    route_pair(0.0)

    top_rank = jnp.max(jnp.maximum(rankt_ref[pl.ds(2 * pi + lane_col, 1), :],
                                   rankt_ref[pl.ds(2 * pi + 1 + lane_col, 1), :]))

    @pl.when(top_rank >= MOE_SLOTS)
    def _():
        def extra(sb, carry):
            route_pair((sb * MOE_SLOTS).astype(F32))
            return carry

        lax.fori_loop(1, top_rank.astype(jnp.int32) // MOE_SLOTS + 1, extra, 0)

    @pl.when(pi == N_EXPERTS // 2 - 1)
    def _():
        o_ref[...] = _layer_norm(acc_ref[...], g2_ref[...], b2_ref[...])


def _moe(x1, comb, p2, wup, wdn, wg, wp, g2, b2, N):
    tm = MOE_TILE
    tri = (jnp.arange(MOE_SUB)[None, :] < jnp.arange(MOE_SUB)[:, None]).astype(BF16)
    row = lambda w: pl.BlockSpec((tm, w), lambda i, e: (i, 0))
    full = lambda a: pl.BlockSpec(a.shape, lambda i, e: (0,) * a.ndim)
    return pl.pallas_call(
        _moe_kernel,
        grid=(N // tm, N_EXPERTS // 2),
        in_specs=[row(D_MODEL), row(128), row(D_PLE), full(tri),
                  pl.BlockSpec((2, D_MODEL, 2 * D_FF), lambda i, e: (e, 0, 0)),
                  pl.BlockSpec((2, D_FF, D_MODEL), lambda i, e: (e, 0, 0)),
                  full(wg), full(wp), full(g2), full(b2)],
        out_specs=row(D_MODEL),
        out_shape=jax.ShapeDtypeStruct((N, D_MODEL), F32),
        scratch_shapes=[pltpu.VMEM((tm, D_MODEL), F32),
                        pltpu.VMEM((tm, D_MODEL), BF16),
                        pltpu.VMEM((tm, 128), F32),
                        pltpu.VMEM((128, tm), F32),
                        pltpu.VMEM((128, tm), F32)],
        compiler_params=_cparams(("parallel", "arbitrary"), VMEM_LIMIT),
        name="moe",
    )(x1, comb, p2, tri, wup, wdn, wg, wp, g2, b2)


def _block_diag(w):
    n, k, j = w.shape
    eye = jnp.eye(n, dtype=w.dtype)
    return (w[:, :, None, :] * eye[:, None, :, None]).reshape(n * k, n * j)


def kernel(x, p, rel_bias, w_in, cmp_pe_k, cmp_pe_v, cmp_w_k, cmp_w_v, conv_w, conv_b, rg_w_a, rg_b_a,
           rg_w_x, rg_b_x, rg_lambda, attn_out_gain, rnn_out_gain, w_out, ln1_g, ln1_b, router_group_w,
           router_group_b, router_expert_w, router_expert_b, expert_w_up, expert_w_down, ple_w,
           ple_gate_w, ln2_g, ln2_b):
    B, T, _ = x.shape
    N = B * T
    assert T % FAR == 0 and T % (TQ * TILES) == 0 and T % WINDOW == 0 and T // SEL_BLOCK <= N_SLC
    assert w_in.shape[0] == 1
    n_cmp = (T - CMP_LEN) // CMP_STRIDE + 1
    n_chunk = T // CMP_STRIDE
    assert C_OFF + n_chunk <= C_PAD
    row1 = lambda v: v.reshape(1, -1)

    w = w_in[0]
    c0 = D_ATTN
    kv = lambda k: w[:, c0 + k * D_KV:c0 + (k + 1) * D_KV]
    g0 = c0 + 6 * D_KV
    wq = w[:, :D_ATTN] * (HEAD_DIM ** -0.5 * LOG2E)
    w_main = jnp.concatenate([kv(2), kv(4), kv(0), kv(1), w[:, g0 + 24:]], axis=1).astype(BF16)
    w_gate = jnp.pad(w[:, g0:g0 + 24], ((0, 0), (0, GATE_ROWS - 24)))
    w_t = jnp.concatenate([wq, kv(3), kv(5), w_gate], axis=1).T.astype(BF16)

    x2 = x.reshape(N, D_MODEL)
    kaug, kw_pad, kc_raw, vc_raw, rx, ry, t_all, vst, vwt, gates_t = _inproj(x2, w_main, w_t, B, T)

    def per_pos(wc):
        wl = wc.reshape(CMP_LEN, HEAD_DIM, HEAD_DIM)
        z = jnp.zeros_like(wl)
        return jnp.concatenate([jnp.concatenate([wl, z], axis=2), jnp.concatenate([z, wl], axis=2)],
                               axis=1).astype(BF16)

    pek = jnp.tile(cmp_pe_k[0], (1, N_GROUPS_KV))
    pev = jnp.tile(cmp_pe_v[0], (1, N_GROUPS_KV))
    grp_onehot = (jnp.arange(C_PAD)[:, None] // 8 == jnp.arange(128)[None, :]).astype(BF16)
    kc2, vct = _compress(kc_raw.reshape(B, T, 128), vc_raw.reshape(B, T, 128), pek, pev,
                         per_pos(cmp_w_k[0]), per_pos(cmp_w_v[0]), grp_onehot, B, T, n_cmp)

    ds, dw, bc = _bias_tiles(rel_bias)

    cidx = jnp.arange(C_PAD) - C_OFF
    s_lo = jnp.arange(N_SLC)[:, None] * SEL_BLOCK
    overlap = ((cidx[None, :] * CMP_STRIDE < s_lo + SEL_BLOCK) & (cidx[None, :] * CMP_STRIDE + CMP_LEN > s_lo)
               & (cidx[None, :] >= 0) & (cidx[None, :] < n_cmp)).astype(BF16)
    vct = jnp.concatenate([vct, jnp.broadcast_to(overlap, (B, N_SLC, C_PAD))], axis=1)
    attn = _attention(t_all, gates_t, kc2, vct, kaug, vst, kw_pad, vwt, ds, dw, bc, B, T)

    sp = jax.nn.softplus(-rg_lambda[0].astype(F32))
    rnn, wup_b, wdn_b = _rglru(rx.reshape(B, T, D_RNN), ry.reshape(B, T, D_RNN), conv_w[0], row1(conv_b[0]),
                               _block_diag(rg_w_a[0]).astype(BF16), row1(rg_b_a[0]),
                               _block_diag(rg_w_x[0]).astype(BF16), row1(rg_b_x[0]), row1(sp),
                               expert_w_up[0].reshape(N_EXPERTS * D_MODEL, 2 * D_FF),
                               expert_w_down[0].reshape(N_EXPERTS * D_FF, D_MODEL), B, T)

    wr = jnp.pad(jnp.concatenate([router_group_w[0], router_expert_w[0]], axis=1), ((0, 0), (0, 108)))
    wrh = wr.astype(BF16)
    wr2 = jnp.concatenate([wrh, (wr - wrh.astype(F32)).astype(BF16)], axis=1)
    br = jnp.pad(jnp.concatenate([router_group_b[0], router_expert_b[0]]), (0, 108)).reshape(1, 128)
    x1, comb = _outproj(attn.reshape(N, 512), rnn.reshape(N, D_RNN), x2, row1(attn_out_gain[0]),
                        row1(rnn_out_gain[0]), w_out[0].astype(BF16), row1(ln1_g[0]), row1(ln1_b[0]),
                        wr2, br, N)

    out = _moe(x1, comb, p[0].reshape(N, D_PLE), wup_b.reshape(N_EXPERTS, D_MODEL, 2 * D_FF),
               wdn_b.reshape(N_EXPERTS, D_FF, D_MODEL), ple_gate_w[0].astype(BF16), ple_w[0].astype(BF16),
               row1(ln2_g[0]), row1(ln2_b[0]), N)
    return out.reshape(B, T, D_MODEL)
```

```python
import functools
import math

import jax
import jax.numpy as jnp
from jax import lax
from jax.experimental import pallas as pl
from jax.experimental.pallas import tpu as pltpu

F32 = jnp.float32
BF16 = jnp.bfloat16
NEG = -1e30
LOG2E = 1.4426950408889634

D_MODEL = 1024
HEAD_DIM = 64
N_HEADS = 8
N_GROUPS_KV = 2
GQA_REP = 4
D_ATTN = 512
D_RNN = 512
D_KV = 128
CMP_LEN = 32
CMP_STRIDE = 16
SEL_BLOCK = 64
N_SEL = 16
WINDOW = 512
N_BUCKETS = 32
MAX_DISTANCE = 128
N_EXPERTS = 16
EXPERTS_PER_GROUP = 4
N_EXP_GROUPS = 4
D_FF = 512
D_PLE = 256
ALPHA = 2.0 ** 0.25
LN_EPS = 1e-5
RMS_EPS = 1e-6
FORCE_BONUS = 1e4
RG_C = 8.0

TQ = 128
TILES = 2
N_SLC = 128
C_PAD = 640
C_OFF = 16
CMP_ROW_LEVELS = (256, 384, 512, C_PAD)
SEL_ROW_LEVELS = (32, 64, 96, N_SLC)
FAR = 512
TAIL = WINDOW + TQ
BAND = 24
GATE_ROWS = 32
V_ROWS = 80
MOE_TILE = 1024
MOE_SUB = 512
MOE_SLOTS = 112
MOE_PAIR_K = 256
VMEM_LIMIT = 56 * 1024 * 1024


def _cparams(sem, vmem=None):
    return pltpu.CompilerParams(dimension_semantics=sem, vmem_limit_bytes=vmem)


def _inproj_kernel(x_ref, wm_ref, wt_ref, ka_ref, kw_ref, kc_ref, vc_ref, rx_ref, ry_ref, t_ref, vs_ref,
                   vw_ref, gt_ref):
    tm = x_ref.shape[0]
    j = pl.program_id(1)
    ones_rows = jnp.where(lax.broadcasted_iota(jnp.int32, (V_ROWS - HEAD_DIM, tm), 0) == 0, 1.0, 0.0).astype(BF16)
    lane = lax.broadcasted_iota(jnp.int32, (tm, 2 * N_SLC), 1)

    @pl.when(j == 0)
    def _():
        ka_ref[...] = jnp.where(lane == N_SLC - 1, 1.0, 0.0).astype(BF16)
        kw_ref[...] = jnp.where(lane == N_SLC, 1.0, 0.0).astype(BF16)
        for v_ref in (vs_ref, vw_ref):
            for g in range(N_GROUPS_KV):
                v_ref[g, :HEAD_DIM, :] = jnp.zeros((HEAD_DIM, tm), BF16)
                v_ref[g, HEAD_DIM:, :] = ones_rows

    @pl.when(j > 0)
    def _():
        _inproj_tile(x_ref, wm_ref, wt_ref, ka_ref, kw_ref, kc_ref, vc_ref, rx_ref, ry_ref, t_ref, vs_ref,
                     vw_ref, gt_ref, (j - 1) * tm, ones_rows)


def _inproj_tile(x_ref, wm_ref, wt_ref, ka_ref, kw_ref, kc_ref, vc_ref, rx_ref, ry_ref, t_ref, vs_ref, vw_ref,
                 gt_ref, t0, ones_rows):
    tm = x_ref.shape[0]
    xb = x_ref[...].astype(BF16)

    def mm(lo, hi):
        return jnp.dot(xb, wm_ref[:, lo:hi], preferred_element_type=F32)

    k_pair = mm(0, 256)
    blk = (t0 + lax.broadcasted_iota(jnp.int32, (tm, N_SLC), 0)) // SEL_BLOCK
    ka_ref[:, :N_SLC] = jnp.where(blk == lax.broadcasted_iota(jnp.int32, (tm, N_SLC), 1), 1.0, 0.0).astype(BF16)
    ka_ref[:, N_SLC:] = k_pair[:, :128].astype(BF16)
    kw_ref[:, :N_SLC] = k_pair[:, 128:].astype(BF16)
    kw_ref[:, N_SLC:] = jnp.zeros((tm, N_SLC), BF16)
    c_pair = mm(256, 512)
    kc_ref[...] = c_pair[:, :128]
    vc_ref[...] = c_pair[:, 128:]
    rx_ref[...] = mm(512, 1024)
    ry_ref[...] = mm(1024, 1536)
    tr = lax.dot_general(wt_ref[...], xb, (((1,), (1,)), ((), ())), preferred_element_type=F32)
    t_ref[...] = tr[:D_ATTN].astype(BF16)
    for v_ref, base in ((vs_ref, D_ATTN), (vw_ref, D_ATTN + D_KV)):
        for g in range(N_GROUPS_KV):
            v_ref[g, :HEAD_DIM, :] = tr[base + g * HEAD_DIM:base + (g + 1) * HEAD_DIM].astype(BF16)
            v_ref[g, HEAD_DIM:, :] = ones_rows
    gt_ref[...] = jax.nn.sigmoid(tr[D_ATTN + 2 * D_KV:])


def _inproj(x2, w_main, w_t, B, T):
    N = B * T
    tm = WINDOW
    nt = T // tm
    tp = T + WINDOW
    data = lambda j: jnp.maximum(j - 1, 0)
    row = lambda w: pl.BlockSpec((tm, w), lambda b, j: (b * nt + data(j), 0))
    full = lambda a: pl.BlockSpec(a.shape, lambda b, j: (0,) * a.ndim)
    tr = lambda r: pl.BlockSpec((None, r, tm), lambda b, j: (b, 0, data(j)))
    lead = pl.BlockSpec((None, tm, 2 * N_SLC), lambda b, j: (b, j, 0))
    lead_t = pl.BlockSpec((None, N_GROUPS_KV, V_ROWS, tm), lambda b, j: (b, 0, 0, j))
    return pl.pallas_call(
        _inproj_kernel,
        grid=(B, nt + 1),
        in_specs=[row(D_MODEL), full(w_main), full(w_t)],
        out_specs=[lead, lead, row(128), row(128), row(512), row(512), tr(D_ATTN), lead_t, lead_t,
                   tr(GATE_ROWS)],
        out_shape=[jax.ShapeDtypeStruct((B, tp, 2 * N_SLC), BF16),
                   jax.ShapeDtypeStruct((B, tp, 2 * N_SLC), BF16),
                   jax.ShapeDtypeStruct((N, 128), F32),
                   jax.ShapeDtypeStruct((N, 128), F32),
                   jax.ShapeDtypeStruct((N, 512), F32),
                   jax.ShapeDtypeStruct((N, 512), F32),
                   jax.ShapeDtypeStruct((B, D_ATTN, T), BF16),
                   jax.ShapeDtypeStruct((B, N_GROUPS_KV, V_ROWS, tp), BF16),
                   jax.ShapeDtypeStruct((B, N_GROUPS_KV, V_ROWS, tp), BF16),
                   jax.ShapeDtypeStruct((B, GATE_ROWS, T), F32)],
        compiler_params=_cparams(("parallel", "arbitrary"), VMEM_LIMIT),
        name="inproj",
    )(x2, w_main, w_t)


def _compress_kernel(kr_ref, vr_ref, pek_ref, pev_ref, wk_ref, wv_ref, oh_ref, kc_ref, vct_ref, *,
                     n_cmp, n_chunk):
    def compress(raw_ref, pe_ref, w_ref):
        lo = jnp.zeros((n_chunk, 2 * HEAD_DIM), F32)
        hi = jnp.zeros((n_chunk, 2 * HEAD_DIM), F32)
        for j in range(CMP_STRIDE):
            a = raw_ref[pl.ds(j, n_chunk, stride=CMP_STRIDE), :]
            lo = lo + jnp.dot((a + pe_ref[j:j + 1, :]).astype(BF16), w_ref[j],
                              preferred_element_type=F32)
            hi = hi + jnp.dot((a + pe_ref[CMP_STRIDE + j:CMP_STRIDE + j + 1, :]).astype(BF16),
                              w_ref[CMP_STRIDE + j], preferred_element_type=F32)
        hi = pltpu.roll(hi, n_chunk - 1, axis=0)
        rid = lax.broadcasted_iota(jnp.int32, (n_chunk, 2 * HEAD_DIM), 0)
        out = jnp.where(rid < n_cmp, lo + hi, 0.0)
        return jnp.concatenate([jnp.zeros((C_OFF, 2 * HEAD_DIM), F32), out,
                                jnp.zeros((C_PAD - C_OFF - n_chunk, 2 * HEAD_DIM), F32)], axis=0)

    kc_ref[:, 0:2 * HEAD_DIM] = compress(kr_ref, pek_ref, wk_ref).astype(BF16)
    kc_ref[:, 2 * HEAD_DIM:] = oh_ref[...]
    vct_ref[...] = compress(vr_ref, pev_ref, wv_ref).T.astype(BF16)


def _compress(kc_raw, vc_raw, pek, pev, wk, wv, grp_onehot, B, T, n_cmp):
    n_chunk = T // CMP_STRIDE
    blk = pl.BlockSpec((None, T, 2 * HEAD_DIM), lambda b: (b, 0, 0))
    full = lambda a: pl.BlockSpec(a.shape, lambda b: (0,) * a.ndim)
    return pl.pallas_call(
        functools.partial(_compress_kernel, n_cmp=n_cmp, n_chunk=n_chunk),
        grid=(B,),
        in_specs=[blk, blk, full(pek), full(pev), full(wk), full(wv), full(grp_onehot)],
        out_specs=[pl.BlockSpec((None, C_PAD, 256), lambda b: (b, 0, 0)),
                   pl.BlockSpec((None, 2 * HEAD_DIM, C_PAD), lambda b: (b, 0, 0))],
        out_shape=[jax.ShapeDtypeStruct((B, C_PAD, 256), BF16),
                   jax.ShapeDtypeStruct((B, 2 * HEAD_DIM, C_PAD), BF16)],
        compiler_params=_cparams(("parallel",), VMEM_LIMIT),
        name="compress",
    )(kc_raw, vc_raw, pek, pev, wk, wv, grp_onehot)


def _bias_kernel(tab_ref, dt_ref, bt_ref, dc_ref, bc_ref, os_ref, ow_ref, oc_ref):
    h = pl.program_id(0) * GQA_REP + pl.program_id(1)

    def lookup(bk):
        out = jnp.zeros(bk.shape, F32)
        for b in range(N_BUCKETS):
            out = jnp.where(bk == b, tab_ref[h, b], out)
        return (out - tab_ref[h, N_BUCKETS - 1]) * LOG2E

    dt = dt_ref[...]
    v = lookup(bt_ref[...])
    os_ref[...] = jnp.where(dt >= 0, v, NEG)
    ow_ref[...] = jnp.where((dt >= 0) & (dt < WINDOW), v, NEG)
    oc_ref[...] = jnp.where(dc_ref[...] >= 0, lookup(bc_ref[...]), NEG)


def _t5_bucket(dist):
    max_exact = N_BUCKETS // 2
    d = jnp.maximum(dist, 0)
    df = jnp.maximum(d, 1).astype(F32)
    large = max_exact + (jnp.log(df / max_exact) / math.log(MAX_DISTANCE / max_exact)
                         * (N_BUCKETS - max_exact)).astype(jnp.int32)
    large = jnp.minimum(large, N_BUCKETS - 1)
    return jnp.where(d < max_exact, d, large)


def _bias_tiles(rel_bias):
    tq = jnp.arange(TQ, dtype=jnp.int32)
    dt = tq[None, :] + WINDOW - jnp.arange(TAIL, dtype=jnp.int32)[:, None]
    cc = jnp.arange(BAND, dtype=jnp.int32) - C_OFF
    dc = tq[None, :] - (cc[:, None] * CMP_STRIDE + CMP_LEN - 1)
    full = lambda a: pl.BlockSpec(a.shape, lambda g, r: (0,) * a.ndim)
    out = lambda rows: pl.BlockSpec((None, rows, TQ), lambda g, r: (g, 0, r))
    return pl.pallas_call(
        _bias_kernel,
        grid=(N_GROUPS_KV, GQA_REP),
        in_specs=[pl.BlockSpec(memory_space=pltpu.SMEM), full(dt), full(dt), full(dc), full(dc)],
        out_specs=[out(TAIL), out(TAIL), out(BAND)],
        out_shape=[jax.ShapeDtypeStruct((N_GROUPS_KV, TAIL, GQA_REP * TQ), F32),
                   jax.ShapeDtypeStruct((N_GROUPS_KV, TAIL, GQA_REP * TQ), F32),
                   jax.ShapeDtypeStruct((N_GROUPS_KV, BAND, GQA_REP * TQ), F32)],
        compiler_params=_cparams(("arbitrary", "arbitrary")),
        name="biastile",
    )(rel_bias.T, dt, _t5_bucket(dt), dc, _t5_bucket(dc))


def _attn_kernel(qt_ref, gt_ref, kc_ref, vct_ref, kaug_ref, vst_ref, kw_ref, vwt_ref, ds_ref, dw_ref,
                 bc_ref, o_ref, *scratch, n_far_max):
    n_hg = TILES * N_GROUPS_KV
    sc_ref, sw_ref = (scratch[n_hg * k:n_hg * (k + 1)] for k in range(2))
    qa_ref, qat_ref, m_ref, acc_ref, sb_ref, sm_ref, oc_ref, ow_ref, imp_ref = scratch[2 * n_hg:]
    cols4 = GQA_REP * TQ
    groups = range(N_GROUPS_KV)
    zeros_q = jnp.zeros((HEAD_DIM, cols4), BF16)
    grp = lax.broadcasted_iota(jnp.int32, (128, cols4), 0)
    neg_row = jnp.where(grp == 0, NEG, 0.0).astype(BF16)
    tq_col = lax.broadcasted_iota(jnp.int32, (1, cols4), 1) % TQ

    def chunk_start(c):
        return pl.multiple_of(WINDOW + jnp.minimum(c, n_far_max - 1) * FAR, FAR)

    def scores(h, c, par):
        kch = kaug_ref[pl.ds(chunk_start(c), FAR), :]
        for g in groups:
            sf = jnp.dot(kch, qa_ref[2 * h + g], preferred_element_type=F32)
            sb_ref[4 * h + 2 * g + par] = sf
            sm_ref[4 * h + 2 * g + par] = jnp.max(sf, axis=0, keepdims=True)

    def consume(h, nfar, c, par):
        c0 = chunk_start(c)
        live = c < nfar
        for g in groups:
            hg = 2 * h + g
            sf = sb_ref[4 * h + 2 * g + par]
            mprev = m_ref[hg]
            mcand = jnp.maximum(mprev, sm_ref[4 * h + 2 * g + par])
            mnew = jnp.where(live, mcand, mprev)
            pf = jnp.exp2(sf - jnp.where(live, mcand, -NEG)).astype(BF16)
            acc_ref[hg] = jnp.exp2(mprev - mnew) * acc_ref[hg] + jnp.dot(
                vst_ref[g, :, pl.ds(c0, FAR)], pf, preferred_element_type=F32)
            m_ref[hg] = mnew

    tiles = range(TILES)
    chains = [(h, g) for h in tiles for g in groups]
    t_i = [pl.program_id(1) * TILES + h for h in tiles]
    t_qs = [pl.multiple_of(i * TQ, TQ) for i in t_i]
    t_nfar = [jnp.maximum(i - 1, 0) // 4 for i in t_i]
    def qpad(h, g):
        qgt = jnp.concatenate(
            [qt_ref[(GQA_REP * g + r) * HEAD_DIM:(GQA_REP * g + r + 1) * HEAD_DIM, h * TQ:(h + 1) * TQ]
             for r in range(GQA_REP)], axis=1)
        return jnp.concatenate([qgt, zeros_q] if g == 0 else [zeros_q, qgt], axis=0)

    def compressed_and_window(rows):
        for h, g in chains:
            i = t_i[h]
            gmask = jnp.where((grp < C_OFF // 8) | (grp >= i + BAND // 8), NEG, 0.0).astype(BF16)
            band0 = pl.multiple_of(i * 8, 8)
            sc_ref[2 * h + g][0:rows, :] = jnp.dot(kc_ref[0:rows, :],
                                                   jnp.concatenate([qpad(h, g), gmask], axis=0),
                                                   preferred_element_type=F32)
            sc_ref[2 * h + g][pl.ds(band0, BAND), :] = sc_ref[2 * h + g][pl.ds(band0, BAND), :] + bc_ref[g]
        w_max = {}
        for h, g in chains:
            sw = jnp.dot(kw_ref[pl.ds(t_qs[h], TAIL), :], jnp.concatenate([qpad(h, g), neg_row], axis=0),
                         preferred_element_type=F32) + dw_ref[g]
            sw_ref[2 * h + g][...] = sw
            w_max[h, g] = jnp.max(sw, axis=0, keepdims=True)
        for h, g in chains:
            s = sc_ref[2 * h + g][0:rows, :]
            e = jnp.exp2(s - jnp.max(s, axis=0, keepdims=True))
            has_cmp = t_qs[h] + tq_col >= CMP_LEN - 1
            pinv = jnp.where(has_cmp, 1.0 / jnp.sum(e, axis=0, keepdims=True), 0.0)
            res = jnp.dot(vct_ref[:, 0:rows], e.astype(BF16), preferred_element_type=F32)
            oc_ref[2 * h + g] = res[0:2 * HEAD_DIM, :] * pinv
            imp_ref[2 * h + g] = sum(res[2 * HEAD_DIM:, r * TQ:(r + 1) * TQ] * pinv[:, r * TQ:(r + 1) * TQ]
                                     for r in range(GQA_REP))
        for h, g in chains:
            pw = jnp.exp2(sw_ref[2 * h + g][...] - w_max[h, g]).astype(BF16)
            acc_w = jnp.dot(vwt_ref[g, :, pl.ds(t_qs[h], TAIL)], pw, preferred_element_type=F32)
            ow_ref[2 * h + g] = acc_w[0:HEAD_DIM, :] * (1.0 / acc_w[HEAD_DIM:HEAD_DIM + 1, :])

    step = pl.program_id(1)
    first_step = 0
    for rows in CMP_ROW_LEVELS:
        end_step = ((rows - BAND) // 8 - TILES + 1) // TILES + 1
        in_level = step >= first_step if rows == C_PAD else (step >= first_step) & (step < end_step)
        pl.when(in_level)(functools.partial(compressed_and_window, rows))
        first_step = end_step

    def select_blocks(nsel):
        sid = lax.broadcasted_iota(jnp.int32, (nsel, TQ), 0)
        sid_f = sid.astype(F32)
        lane = lax.broadcasted_iota(jnp.int32, (nsel, TQ), 1)
        score, mbt = {}, {}
        for h, g in chains:
            cur = (t_qs[h] + lane) // SEL_BLOCK
            forced = (sid == 0) | (sid == cur) | (sid == cur - 1)
            score[h, g] = jnp.where(forced, -jnp.inf, jnp.where(sid <= cur, imp_ref[2 * h + g, 0:nsel, :], NEG))
            mbt[h, g] = jnp.where(forced, 0.0, NEG)
        for _ in range(N_SEL - 3):
            for ch in chains:
                cm = jnp.max(score[ch], axis=0, keepdims=True)
                first = jnp.min(jnp.where(score[ch] == cm, sid_f, float(N_SLC)), axis=0, keepdims=True)
                pick = sid_f == first
                mbt[ch] = jnp.where(pick, 0.0, mbt[ch])
                score[ch] = jnp.where(pick, -jnp.inf, score[ch])
        for h, g in chains:
            mbt_tail = jnp.where(sid < t_nfar[h] * (FAR // SEL_BLOCK), NEG, mbt[h, g])
            for dst, m in ((qa_ref, mbt[h, g]), (qat_ref, mbt_tail)):
                m = jnp.concatenate([m, jnp.full((N_SLC - nsel, TQ), NEG, F32)], axis=0) if nsel < N_SLC else m
                dst[2 * h + g, 0:N_SLC, :] = jnp.concatenate([m.astype(BF16)] * GQA_REP, axis=1)

    first_step = 0
    for nsel in SEL_ROW_LEVELS:
        end_step = nsel // (TILES * TQ // SEL_BLOCK)
        in_level = step >= first_step if nsel == N_SLC else (step >= first_step) & (step < end_step)
        pl.when(in_level)(functools.partial(select_blocks, nsel))
        first_step = end_step

    for h, g in chains:
        qa_ref[2 * h + g, N_SLC:, :] = qpad(h, g)
        qa_tail = jnp.concatenate([qat_ref[2 * h + g], qpad(h, g)], axis=0)
        st = jnp.dot(kaug_ref[pl.ds(t_qs[h], TAIL), :], qa_tail, preferred_element_type=F32) + ds_ref[g]
        sc_ref[2 * h + g][...] = st
        m_ref[2 * h + g] = jnp.max(st, axis=0, keepdims=True)
    for h in tiles:
        scores(h, 0, 0)
    for h, g in chains:
        pt = jnp.exp2(sc_ref[2 * h + g][...] - m_ref[2 * h + g]).astype(BF16)
        acc_ref[2 * h + g] = jnp.dot(vst_ref[g, :, pl.ds(t_qs[h], TAIL)], pt, preferred_element_type=F32)

    def far_pair(kk, carry):
        c = 2 * kk
        for h in tiles:
            scores(h, c + 1, 1)
        for h in tiles:
            consume(h, t_nfar[h], c, 0)
        for h in tiles:
            scores(h, c + 2, 0)
        for h in tiles:
            consume(h, t_nfar[h], c + 1, 1)
        return carry

    n_chunks = t_nfar[TILES - 1]
    n_pairs = jnp.maximum(n_chunks - 1, 0) // 2
    lax.fori_loop(0, n_pairs, far_pair, 0)

    @pl.when(n_chunks - 2 * n_pairs == 1)
    def _():
        for h in tiles:
            consume(h, t_nfar[h], n_chunks - 1, 0)

    @pl.when(n_chunks - 2 * n_pairs == 2)
    def _():
        for h in tiles:
            scores(h, n_chunks - 1, 1)
        for h in tiles:
            consume(h, t_nfar[h], n_chunks - 2, 0)
        for h in tiles:
            consume(h, t_nfar[h], n_chunks - 1, 1)

    for h in tiles:
        lanes = slice(h * TQ, (h + 1) * TQ)
        out_rows = []
        for g in groups:
            dsl = slice(g * HEAD_DIM, (g + 1) * HEAD_DIM)
            hg = 2 * h + g
            o_s = acc_ref[hg, 0:HEAD_DIM, :] * (1.0 / acc_ref[hg, HEAD_DIM:HEAD_DIM + 1, :])
            for r in range(GQA_REP):
                cs = slice(r * TQ, (r + 1) * TQ)
                gc = 3 * (GQA_REP * g + r)
                out_rows.append(gt_ref[gc:gc + 1, lanes] * oc_ref[hg, dsl, cs]
                                + gt_ref[gc + 1:gc + 2, lanes] * o_s[:, cs]
                                + gt_ref[gc + 2:gc + 3, lanes] * ow_ref[hg, :, cs])
        o_ref[h * TQ:(h + 1) * TQ, :] = jnp.concatenate(out_rows, axis=0).T


def _attention(t_all, gates_t, kc2, vct, kaug, vst, kw_pad, vwt, ds, dw, bc, B, T):
    tp = T + WINDOW
    tqs = TQ * TILES
    per_b = lambda shape: pl.BlockSpec((None,) + shape, lambda b, i: (b,) + (0,) * len(shape),
                                      pipeline_mode=pl.Buffered(1))
    const = lambda a: pl.BlockSpec(a.shape, lambda b, i: (0,) * a.ndim, pipeline_mode=pl.Buffered(1))
    cols4 = GQA_REP * TQ
    n_hg = TILES * N_GROUPS_KV
    return pl.pallas_call(
        functools.partial(_attn_kernel, n_far_max=T // FAR),
        grid=(B, T // tqs),
        in_specs=[pl.BlockSpec((None, 512, tqs), lambda b, i: (b, 0, i)),
                  pl.BlockSpec((None, GATE_ROWS, tqs), lambda b, i: (b, 0, i)),
                  per_b((C_PAD, 256)),
                  per_b((256, C_PAD)),
                  per_b((tp, 256)),
                  per_b((N_GROUPS_KV, V_ROWS, tp)),
                  per_b((tp, 256)),
                  per_b((N_GROUPS_KV, V_ROWS, tp)),
                  const(ds), const(dw), const(bc)],
        out_specs=pl.BlockSpec((None, tqs, 512), lambda b, i: (b, i, 0)),
        out_shape=jax.ShapeDtypeStruct((B, T, 512), F32),
        scratch_shapes=([pltpu.VMEM((C_PAD, cols4), F32)] * n_hg
                        + [pltpu.VMEM((TAIL, cols4), F32)] * n_hg
                        + [pltpu.VMEM((n_hg, N_SLC + 2 * HEAD_DIM, cols4), BF16),
                           pltpu.VMEM((n_hg, N_SLC, cols4), BF16),
                           pltpu.VMEM((n_hg, 1, cols4), F32),
                           pltpu.VMEM((n_hg, V_ROWS, cols4), F32),
                           pltpu.VMEM((2 * n_hg, FAR, cols4), F32),
                           pltpu.VMEM((2 * n_hg, 1, cols4), F32),
                           pltpu.VMEM((n_hg, 2 * HEAD_DIM, cols4), F32),
                           pltpu.VMEM((n_hg, HEAD_DIM, cols4), F32),
                           pltpu.VMEM((n_hg, N_SLC, TQ), F32)]),
        compiler_params=_cparams(("parallel", "arbitrary"), VMEM_LIMIT),
        name="attn",
    )(t_all, gates_t, kc2, vct, kaug, vst, kw_pad, vwt, ds, dw, bc)


def _rglru_kernel(*refs, L):
    J = L // 8
    n_col = D_RNN // 128
    rx_refs, ry_refs = refs[:n_col], refs[n_col:2 * n_col]
    (cw_ref, cb_ref, wa_ref, ba_ref, wx_ref, bx_ref, sp_ref, eu_ref, ed_ref, o_ref, eub_ref, edb_ref,
     xprev_ref, h_ref, os_ref) = refs[2 * n_col:]
    eub_ref[...] = eu_ref[...].astype(BF16)
    edb_ref[...] = ed_ref[...].astype(BF16)

    @pl.when(pl.program_id(1) == 0)
    def _():
        xprev_ref[...] = jnp.zeros(xprev_ref.shape, F32)
        h_ref[...] = jnp.zeros(h_ref.shape, F32)

    def strided(col_refs):
        return jnp.concatenate(
            [jnp.concatenate([ref[pl.ds(j, 8, stride=J), :] for j in range(J)], axis=0)
             for ref in col_refs], axis=1)

    xp = strided(rx_refs)
    yp = strided(ry_refs)
    sub = lax.broadcasted_iota(jnp.int32, (8, D_RNN), 0)

    def delayed(d):
        heads = []
        for j in range(d):
            src = pltpu.roll(xp[(J + j - d) * 8:(J + j - d + 1) * 8], 1, axis=0)
            heads.append(jnp.where(sub == 0, xprev_ref[8 + j - d:9 + j - d, :], src))
        return jnp.concatenate(heads + [xp[:L - 8 * d]], axis=0)

    xc = (cb_ref[...] + cw_ref[0:1, :] * delayed(3) + cw_ref[1:2, :] * delayed(2)
          + cw_ref[2:3, :] * delayed(1) + cw_ref[3:4, :] * xp)
    xprev_ref[...] = jnp.concatenate([ref[L - 8:L, :] for ref in rx_refs], axis=1)
    xcb = xc.astype(BF16)
    r = jax.nn.sigmoid(jnp.dot(xcb, wa_ref[...], preferred_element_type=F32) + ba_ref[...])
    ig = jax.nn.sigmoid(jnp.dot(xcb, wx_ref[...], preferred_element_type=F32) + bx_ref[...])
    a = jnp.exp(-RG_C * r * sp_ref[...])
    bt = jnp.sqrt(1.0 - a * a) * (ig * xc)

    h = jnp.zeros((8, D_RNN), F32)
    dec = jnp.ones((8, D_RNN), F32)
    hs, decs = [], []
    for j in range(J):
        aj = a[j * 8:(j + 1) * 8]
        h = aj * h + bt[j * 8:(j + 1) * 8]
        dec = aj * dec
        hs.append(h)
        decs.append(dec)
    carry = h_ref[0:1, :]
    carries = []
    for s in range(8):
        carries.append(carry)
        carry = dec[s:s + 1, :] * carry + h[s:s + 1, :]
    h_ref[...] = jnp.broadcast_to(carry, h_ref.shape)
    h_in = jnp.concatenate(carries, axis=0)
    cdf_c = math.sqrt(2.0 / math.pi)
    for j in range(J):
        y = yp[j * 8:(j + 1) * 8]
        cdf = 0.5 * (1.0 + jnp.tanh(cdf_c * (y + 0.044715 * (y * y * y))))
        res = (hs[j] + decs[j] * h_in) * (y * cdf)
        for c in range(n_col):
            os_ref[c, pl.ds(j, 8, stride=J), :] = res[:, c * 128:(c + 1) * 128]
    for c in range(n_col):
        o_ref[:, c * 128:(c + 1) * 128] = os_ref[c]


def _rglru(rx, ry, conv_w, conv_b, wa_bd, b_a, wx_bd, b_x, sp, w_up, w_dn, B, T):
    L = 512
    steps = B * T // L
    nt = T // L
    up_rows, dn_rows = w_up.shape[0] // steps, w_dn.shape[0] // steps
    assert up_rows * steps == w_up.shape[0] and dn_rows * steps == w_dn.shape[0] and dn_rows % 16 == 0
    slab = lambda rows, a: pl.BlockSpec((rows, a.shape[1]), lambda b, t: (b * nt + t, 0))
    blk = pl.BlockSpec((None, L, D_RNN), lambda b, t: (b, t, 0))
    n_col = D_RNN // 128
    cols = [pl.BlockSpec((None, L, 128), functools.partial(lambda b, t, c: (b, t, c), c=c))
            for c in range(n_col)]
    full = lambda a: pl.BlockSpec(a.shape, lambda b, t: (0,) * a.ndim)
    return pl.pallas_call(
        functools.partial(_rglru_kernel, L=L),
        grid=(B, T // L),
        in_specs=cols + cols + [full(conv_w), full(conv_b), full(wa_bd), full(b_a), full(wx_bd),
                                full(b_x), full(sp), slab(up_rows, w_up), slab(dn_rows, w_dn)],
        out_specs=[blk, slab(up_rows, w_up), slab(dn_rows, w_dn)],
        out_shape=[jax.ShapeDtypeStruct((B, T, D_RNN), F32),
                   jax.ShapeDtypeStruct(w_up.shape, BF16),
                   jax.ShapeDtypeStruct(w_dn.shape, BF16)],
        scratch_shapes=[pltpu.VMEM((8, D_RNN), F32), pltpu.VMEM((8, D_RNN), F32),
                        pltpu.VMEM((n_col, L, 128), F32)],
        compiler_params=_cparams(("parallel", "arbitrary"), VMEM_LIMIT),
        name="rglru",
    )(*([rx] * n_col + [ry] * n_col), conv_w, conv_b, wa_bd, b_a, wx_bd, b_x, sp, w_up, w_dn)


def _layer_norm(y, g, b):
    mu = jnp.mean(y, axis=-1, keepdims=True)
    d = y - mu
    var = jnp.mean(d * d, axis=-1, keepdims=True)
    return d * lax.rsqrt(var + LN_EPS) * g + b


def _outproj_kernel(at_ref, rn_ref, x_ref, ga_ref, gr_ref, wo_ref, g1_ref, b1_ref, wr2_ref, br_ref,
                    x1_ref, comb_ref):
    a = at_ref[...]
    rn = rn_ref[...]
    ha = a * lax.rsqrt(jnp.mean(a * a, axis=-1, keepdims=True) + RMS_EPS) * ga_ref[...]
    hr = rn * lax.rsqrt(jnp.mean(rn * rn, axis=-1, keepdims=True) + RMS_EPS) * gr_ref[...]
    heads = jnp.concatenate([ha, hr], axis=1).astype(BF16)
    mix = jnp.dot(heads, wo_ref[...], preferred_element_type=F32)
    x1 = _layer_norm(ALPHA * x_ref[...] + mix, g1_ref[...], b1_ref[...])
    x1_ref[...] = x1

    xh = x1.astype(BF16)
    xl = (x1 - xh.astype(F32)).astype(BF16)
    hh_hl = jnp.dot(xh, wr2_ref[...], preferred_element_type=F32)
    lg = (hh_hl[:, :128] + jnp.dot(xl, wr2_ref[:, :128], preferred_element_type=F32)
          + hh_hl[:, 128:]) + br_ref[...]
    lane = lax.broadcasted_iota(jnp.int32, lg.shape, 1)
    lane_f = lane.astype(F32)
    big = 1e9
    isg = lane < N_EXP_GROUPS
    gmax = jnp.max(jnp.where(isg, lg, -jnp.inf), axis=-1, keepdims=True)
    pg_top = 1.0 / jnp.sum(jnp.where(isg, jnp.exp(lg - gmax), 0.0), axis=-1, keepdims=True)
    gi = jnp.min(jnp.where(isg & (lg == gmax), lane_f, big), axis=-1, keepdims=True)
    egrp = ((lane - N_EXP_GROUPS) // EXPERTS_PER_GROUP).astype(F32)
    ise = (lane >= N_EXP_GROUPS) & (lane < N_EXP_GROUPS + N_EXPERTS) & (egrp == gi)
    emax = jnp.max(jnp.where(ise, lg, -jnp.inf), axis=-1, keepdims=True)
    i1 = jnp.min(jnp.where(ise & (lg == emax), lane_f, big), axis=-1, keepdims=True)
    rest = ise & (lane_f != i1)
    m2 = jnp.max(jnp.where(rest, lg, -jnp.inf), axis=-1, keepdims=True)
    i2 = jnp.min(jnp.where(rest & (lg == m2), lane_f, big), axis=-1, keepdims=True)
    e2 = jnp.exp(m2 - emax)
    inv = pg_top / (1.0 + e2)
    comb_ref[...] = jnp.where(lane_f == i1, inv, 0.0) + jnp.where(lane_f == i2, inv * e2, 0.0)


def _outproj(attn, rnn, x2, ga, gr, wo, g1, b1, wr2, br, N):
    tm = 512
    row = lambda w: pl.BlockSpec((tm, w), lambda i: (i, 0))
    full = lambda a: pl.BlockSpec(a.shape, lambda i: (0,) * a.ndim)
    return pl.pallas_call(
        _outproj_kernel,
        grid=(N // tm,),
        in_specs=[row(512), row(512), row(D_MODEL), full(ga), full(gr), full(wo), full(g1), full(b1),
                  full(wr2), full(br)],
        out_specs=[row(D_MODEL), row(128)],
        out_shape=[jax.ShapeDtypeStruct((N, D_MODEL), F32), jax.ShapeDtypeStruct((N, 128), F32)],
        compiler_params=_cparams(("parallel",), VMEM_LIMIT),
        name="outproj",
    )(attn, rnn, x2, ga, gr, wo, g1, b1, wr2, br)


def _moe_kernel(x1_ref, comb_ref, p_ref, tri_ref, wup_ref, wdn_ref, wg_ref, wp_ref, g2_ref, b2_ref, o_ref,
                acc_ref, xb_ref, rank_ref, rankt_ref, combt_ref):
    pi = pl.program_id(1)
    n_sub = x1_ref.shape[0] // MOE_SUB
    sub_rows = [slice(h * MOE_SUB, (h + 1) * MOE_SUB) for h in range(n_sub)]
    lane_col = N_EXP_GROUPS

    @pl.when(pi == 0)
    def _():
        x1 = x1_ref[...]
        xb = x1.astype(BF16)
        xb_ref[...] = xb
        gate = jax.nn.sigmoid(jnp.dot(xb, wg_ref[...], preferred_element_type=F32))
        ple = gate * jnp.dot(p_ref[...].astype(BF16), wp_ref[...], preferred_element_type=F32)
        acc_ref[...] = ALPHA * x1 + ple
        for rows in sub_rows:
            comb = comb_ref[rows, :]
            chosen = comb > 0.0
            rank = jnp.dot(tri_ref[...], jnp.where(chosen, 1.0, 0.0).astype(BF16),
                           preferred_element_type=F32)
            rank = jnp.where(chosen, rank, -1.0)
            rank_ref[rows, :] = rank
            rankt_ref[:, rows] = rank.T
            combt_ref[:, rows] = comb.T

    lane = lax.broadcasted_iota(jnp.int32, (MOE_SUB, 128), 1)
    slot_r = lax.broadcasted_iota(jnp.int32, (MOE_SLOTS, MOE_SUB), 0).astype(F32)
    slot_c = lax.broadcasted_iota(jnp.int32, (MOE_SUB, MOE_PAIR_K), 1).astype(F32)
    pad_rows = jnp.zeros((MOE_PAIR_K - 2 * MOE_SLOTS, D_MODEL), BF16)

    def route_pair(base):
        cmask, xc, y, ys, rk_col = {}, {}, {}, {}, {}
        for rows in sub_rows:
            for k in range(2):
                rk_row = rankt_ref[pl.ds(2 * pi + k + lane_col, 1), rows] - base
                cmask[rows, k] = rk_row == slot_r
            sel = jnp.concatenate([jnp.where(cmask[rows, k], 1.0, 0.0).astype(BF16) for k in range(2)], axis=0)
            xcs = jnp.dot(sel, xb_ref[rows, :], preferred_element_type=F32).astype(BF16)
            for k in range(2):
                xc[rows, k] = xcs[k * MOE_SLOTS:(k + 1) * MOE_SLOTS, :]
        u = [jnp.dot(jnp.concatenate([xc[rows, k] for rows in sub_rows], axis=0), wup_ref[k],
                     preferred_element_type=F32) for k in range(2)]
        hsw = []
        for k in range(2):
            ua = u[k][:, :D_FF]
            hsw.append((ua * jax.nn.sigmoid(ua) * u[k][:, D_FF:]).astype(BF16))
        for k in range(2):
            yk = jnp.dot(hsw[k], wdn_ref[k], preferred_element_type=F32)
            for h, rows in enumerate(sub_rows):
                y[rows, k] = yk[h * MOE_SLOTS:(h + 1) * MOE_SLOTS, :]
        for rows in sub_rows:
            for k in range(2):
                e_lane = 2 * pi + k + lane_col
                w_row = combt_ref[pl.ds(e_lane, 1), rows]
                w_slot = jnp.sum(jnp.where(cmask[rows, k], w_row, 0.0), axis=1, keepdims=True)
                ys[rows, k] = (w_slot * y[rows, k]).astype(BF16)
                rk_col[rows, k] = jnp.sum(jnp.where(lane == e_lane, rank_ref[rows, :], 0.0), axis=1,
                                          keepdims=True) - base
        for rows in sub_rows:
            slot_of = jnp.where(slot_c < MOE_SLOTS, rk_col[rows, 0], rk_col[rows, 1] + MOE_SLOTS)
            acc_ref[rows, :] = acc_ref[rows, :] + jnp.dot(
                jnp.where(slot_of == slot_c, 1.0, 0.0).astype(BF16),
                jnp.concatenate([ys[rows, 0], ys[rows, 1], pad_rows], axis=0), preferred_element_type=F32)

    route_pair(0.0)

    top_rank = jnp.max(jnp.maximum(rankt_ref[pl.ds(2 * pi + lane_col, 1), :],
                                   rankt_ref[pl.ds(2 * pi + 1 + lane_col, 1), :]))

    @pl.when(top_rank >= MOE_SLOTS)
    def _():
        def extra(sb, carry):
            route_pair((sb * MOE_SLOTS).astype(F32))
            return carry

        lax.fori_loop(1, top_rank.astype(jnp.int32) // MOE_SLOTS + 1, extra, 0)

    @pl.when(pi == N_EXPERTS // 2 - 1)
    def _():
        o_ref[...] = _layer_norm(acc_ref[...], g2_ref[...], b2_ref[...])


def _moe(x1, comb, p2, wup, wdn, wg, wp, g2, b2, N):
    tm = MOE_TILE
    tri = (jnp.arange(MOE_SUB)[None, :] < jnp.arange(MOE_SUB)[:, None]).astype(BF16)
    row = lambda w: pl.BlockSpec((tm, w), lambda i, e: (i, 0))
    full = lambda a: pl.BlockSpec(a.shape, lambda i, e: (0,) * a.ndim)
    return pl.pallas_call(
        _moe_kernel,
        grid=(N // tm, N_EXPERTS // 2),
        in_specs=[row(D_MODEL), row(128), row(D_PLE), full(tri),
                  pl.BlockSpec((2, D_MODEL, 2 * D_FF), lambda i, e: (e, 0, 0)),
                  pl.BlockSpec((2, D_FF, D_MODEL), lambda i, e: (e, 0, 0)),
                  full(wg), full(wp), full(g2), full(b2)],
        out_specs=row(D_MODEL),
        out_shape=jax.ShapeDtypeStruct((N, D_MODEL), F32),
        scratch_shapes=[pltpu.VMEM((tm, D_MODEL), F32),
                        pltpu.VMEM((tm, D_MODEL), BF16),
                        pltpu.VMEM((tm, 128), F32),
                        pltpu.VMEM((128, tm), F32),
                        pltpu.VMEM((128, tm), F32)],
        compiler_params=_cparams(("parallel", "arbitrary"), VMEM_LIMIT),
        name="moe",
    )(x1, comb, p2, tri, wup, wdn, wg, wp, g2, b2)


def _block_diag(w):
    n, k, j = w.shape
    eye = jnp.eye(n, dtype=w.dtype)
    return (w[:, :, None, :] * eye[:, None, :, None]).reshape(n * k, n * j)


def kernel(x, p, rel_bias, w_in, cmp_pe_k, cmp_pe_v, cmp_w_k, cmp_w_v, conv_w, conv_b, rg_w_a, rg_b_a,
           rg_w_x, rg_b_x, rg_lambda, attn_out_gain, rnn_out_gain, w_out, ln1_g, ln1_b, router_group_w,
           router_group_b, router_expert_w, router_expert_b, expert_w_up, expert_w_down, ple_w,
           ple_gate_w, ln2_g, ln2_b):
    B, T, _ = x.shape
    N = B * T
    assert T % FAR == 0 and T % (TQ * TILES) == 0 and T % WINDOW == 0 and T // SEL_BLOCK <= N_SLC
    assert w_in.shape[0] == 1
    n_cmp = (T - CMP_LEN) // CMP_STRIDE + 1
    n_chunk = T // CMP_STRIDE
    assert C_OFF + n_chunk <= C_PAD
    row1 = lambda v: v.reshape(1, -1)

    w = w_in[0]
    c0 = D_ATTN
    kv = lambda k: w[:, c0 + k * D_KV:c0 + (k + 1) * D_KV]
    g0 = c0 + 6 * D_KV
    wq = w[:, :D_ATTN] * (HEAD_DIM ** -0.5 * LOG2E)
    w_main = jnp.concatenate([kv(2), kv(4), kv(0), kv(1), w[:, g0 + 24:]], axis=1).astype(BF16)
    w_gate = jnp.pad(w[:, g0:g0 + 24], ((0, 0), (0, GATE_ROWS - 24)))
    w_t = jnp.concatenate([wq, kv(3), kv(5), w_gate], axis=1).T.astype(BF16)

    x2 = x.reshape(N, D_MODEL)
    kaug, kw_pad, kc_raw, vc_raw, rx, ry, t_all, vst, vwt, gates_t = _inproj(x2, w_main, w_t, B, T)

    def per_pos(wc):
        wl = wc.reshape(CMP_LEN, HEAD_DIM, HEAD_DIM)
        z = jnp.zeros_like(wl)
        return jnp.concatenate([jnp.concatenate([wl, z], axis=2), jnp.concatenate([z, wl], axis=2)],
                               axis=1).astype(BF16)

    pek = jnp.tile(cmp_pe_k[0], (1, N_GROUPS_KV))
    pev = jnp.tile(cmp_pe_v[0], (1, N_GROUPS_KV))
    grp_onehot = (jnp.arange(C_PAD)[:, None] // 8 == jnp.arange(128)[None, :]).astype(BF16)
    kc2, vct = _compress(kc_raw.reshape(B, T, 128), vc_raw.reshape(B, T, 128), pek, pev,
                         per_pos(cmp_w_k[0]), per_pos(cmp_w_v[0]), grp_onehot, B, T, n_cmp)

    ds, dw, bc = _bias_tiles(rel_bias)

    cidx = jnp.arange(C_PAD) - C_OFF
    s_lo = jnp.arange(N_SLC)[:, None] * SEL_BLOCK
    overlap = ((cidx[None, :] * CMP_STRIDE < s_lo + SEL_BLOCK) & (cidx[None, :] * CMP_STRIDE + CMP_LEN > s_lo)
               & (cidx[None, :] >= 0) & (cidx[None, :] < n_cmp)).astype(BF16)
    vct = jnp.concatenate([vct, jnp.broadcast_to(overlap, (B, N_SLC, C_PAD))], axis=1)
    attn = _attention(t_all, gates_t, kc2, vct, kaug, vst, kw_pad, vwt, ds, dw, bc, B, T)

    sp = jax.nn.softplus(-rg_lambda[0].astype(F32))
    rnn, wup_b, wdn_b = _rglru(rx.reshape(B, T, D_RNN), ry.reshape(B, T, D_RNN), conv_w[0], row1(conv_b[0]),
                               _block_diag(rg_w_a[0]).astype(BF16), row1(rg_b_a[0]),
                               _block_diag(rg_w_x[0]).astype(BF16), row1(rg_b_x[0]), row1(sp),
                               expert_w_up[0].reshape(N_EXPERTS * D_MODEL, 2 * D_FF),
                               expert_w_down[0].reshape(N_EXPERTS * D_FF, D_MODEL), B, T)

    wr = jnp.pad(jnp.concatenate([router_group_w[0], router_expert_w[0]], axis=1), ((0, 0), (0, 108)))
    wrh = wr.astype(BF16)
    wr2 = jnp.concatenate([wrh, (wr - wrh.astype(F32)).astype(BF16)], axis=1)
    br = jnp.pad(jnp.concatenate([router_group_b[0], router_expert_b[0]]), (0, 108)).reshape(1, 128)
    x1, comb = _outproj(attn.reshape(N, 512), rnn.reshape(N, D_RNN), x2, row1(attn_out_gain[0]),
                        row1(rnn_out_gain[0]), w_out[0].astype(BF16), row1(ln1_g[0]), row1(ln1_b[0]),
                        wr2, br, N)

    out = _moe(x1, comb, p[0].reshape(N, D_PLE), wup_b.reshape(N_EXPERTS, D_MODEL, 2 * D_FF),
               wdn_b.reshape(N_EXPERTS, D_FF, D_MODEL), ple_gate_w[0].astype(BF16), ple_w[0].astype(BF16),
               row1(ln2_g[0]), row1(ln2_b[0]), N)
    return out.reshape(B, T, D_MODEL)
```

```python
import functools
import math

import jax
import jax.numpy as jnp
from jax import lax
from jax.experimental import pallas as pl
from jax.experimental.pallas import tpu as pltpu

F32 = jnp.float32
BF16 = jnp.bfloat16
NEG = -1e30
LOG2E = 1.4426950408889634

D_MODEL = 1024
HEAD_DIM = 64
N_HEADS = 8
N_GROUPS_KV = 2
GQA_REP = 4
D_ATTN = 512
D_RNN = 512
D_KV = 128
CMP_LEN = 32
CMP_STRIDE = 16
SEL_BLOCK = 64
N_SEL = 16
WINDOW = 512
N_BUCKETS = 32
MAX_DISTANCE = 128
N_EXPERTS = 16
EXPERTS_PER_GROUP = 4
N_EXP_GROUPS = 4
D_FF = 512
D_PLE = 256
ALPHA = 2.0 ** 0.25
LN_EPS = 1e-5
RMS_EPS = 1e-6
FORCE_BONUS = 1e4
RG_C = 8.0

TQ = 128
TILES = 2
N_SLC = 128
C_PAD = 640
C_OFF = 16
CMP_ROW_LEVELS = (256, 384, 512, C_PAD)
SEL_ROW_LEVELS = (32, 64, 96, N_SLC)
FAR = 512
TAIL = WINDOW + TQ
BAND = 24
GATE_ROWS = 32
V_ROWS = 80
OUT_SUBS = 2
MOE_TILE = 1024
MOE_SUB = 512
MOE_SLOTS = 112
MOE_PAIR_K = 256
VMEM_LIMIT = 56 * 1024 * 1024


def _cparams(sem, vmem=None):
    return pltpu.CompilerParams(dimension_semantics=sem, vmem_limit_bytes=vmem)


def _inproj_kernel(x_ref, wm_ref, wt_ref, ka_ref, kw_ref, kc_ref, vc_ref, rx_ref, ry_ref, t_ref, vs_ref,
                   vw_ref, gt_ref):
    tm = x_ref.shape[0]
    j = pl.program_id(1)
    ones_rows = jnp.where(lax.broadcasted_iota(jnp.int32, (V_ROWS - HEAD_DIM, tm), 0) == 0, 1.0, 0.0).astype(BF16)
    lane = lax.broadcasted_iota(jnp.int32, (tm, 2 * N_SLC), 1)

    @pl.when(j == 0)
    def _():
        ka_ref[...] = jnp.where(lane == N_SLC - 1, 1.0, 0.0).astype(BF16)
        kw_ref[...] = jnp.where(lane == N_SLC, 1.0, 0.0).astype(BF16)
        for v_ref in (vs_ref, vw_ref):
            for g in range(N_GROUPS_KV):
                v_ref[g, :HEAD_DIM, :] = jnp.zeros((HEAD_DIM, tm), BF16)
                v_ref[g, HEAD_DIM:, :] = ones_rows

    @pl.when(j > 0)
    def _():
        _inproj_tile(x_ref, wm_ref, wt_ref, ka_ref, kw_ref, kc_ref, vc_ref, rx_ref, ry_ref, t_ref, vs_ref,
                     vw_ref, gt_ref, (j - 1) * tm, ones_rows)


def _inproj_tile(x_ref, wm_ref, wt_ref, ka_ref, kw_ref, kc_ref, vc_ref, rx_ref, ry_ref, t_ref, vs_ref, vw_ref,
                 gt_ref, t0, ones_rows):
    tm = x_ref.shape[0]
    xb = x_ref[...].astype(BF16)

    def mm(lo, hi):
        return jnp.dot(xb, wm_ref[:, lo:hi], preferred_element_type=F32)

    k_pair = mm(0, 256)
    blk = (t0 + lax.broadcasted_iota(jnp.int32, (tm, N_SLC), 0)) // SEL_BLOCK
    ka_ref[:, :N_SLC] = jnp.where(blk == lax.broadcasted_iota(jnp.int32, (tm, N_SLC), 1), 1.0, 0.0).astype(BF16)
    ka_ref[:, N_SLC:] = k_pair[:, :128].astype(BF16)
    kw_ref[:, :N_SLC] = k_pair[:, 128:].astype(BF16)
    kw_ref[:, N_SLC:] = jnp.zeros((tm, N_SLC), BF16)
    c_pair = mm(256, 512)
    kc_ref[...] = c_pair[:, :128]
    vc_ref[...] = c_pair[:, 128:]
    rx_ref[...] = mm(512, 1024)
    ry_ref[...] = mm(1024, 1536)
    tr = lax.dot_general(wt_ref[...], xb, (((1,), (1,)), ((), ())), preferred_element_type=F32)
    t_ref[...] = tr[:D_ATTN].astype(BF16)
    for v_ref, base in ((vs_ref, D_ATTN), (vw_ref, D_ATTN + D_KV)):
        for g in range(N_GROUPS_KV):
            v_ref[g, :HEAD_DIM, :] = tr[base + g * HEAD_DIM:base + (g + 1) * HEAD_DIM].astype(BF16)
            v_ref[g, HEAD_DIM:, :] = ones_rows
    gt_ref[...] = jax.nn.sigmoid(tr[D_ATTN + 2 * D_KV:])


def _inproj(x2, w_main, w_t, B, T):
    N = B * T
    tm = WINDOW
    nt = T // tm
    tp = T + WINDOW
    data = lambda j: jnp.maximum(j - 1, 0)
    row = lambda w: pl.BlockSpec((tm, w), lambda b, j: (b * nt + data(j), 0))
    full = lambda a: pl.BlockSpec(a.shape, lambda b, j: (0,) * a.ndim)
    tr = lambda r: pl.BlockSpec((None, r, tm), lambda b, j: (b, 0, data(j)))
    lead = pl.BlockSpec((None, tm, 2 * N_SLC), lambda b, j: (b, j, 0))
    lead_t = pl.BlockSpec((None, N_GROUPS_KV, V_ROWS, tm), lambda b, j: (b, 0, 0, j))
    return pl.pallas_call(
        _inproj_kernel,
        grid=(B, nt + 1),
        in_specs=[row(D_MODEL), full(w_main), full(w_t)],
        out_specs=[lead, lead, row(128), row(128), row(512), row(512), tr(D_ATTN), lead_t, lead_t,
                   tr(GATE_ROWS)],
        out_shape=[jax.ShapeDtypeStruct((B, tp, 2 * N_SLC), BF16),
                   jax.ShapeDtypeStruct((B, tp, 2 * N_SLC), BF16),
                   jax.ShapeDtypeStruct((N, 128), F32),
                   jax.ShapeDtypeStruct((N, 128), F32),
                   jax.ShapeDtypeStruct((N, 512), F32),
                   jax.ShapeDtypeStruct((N, 512), F32),
                   jax.ShapeDtypeStruct((B, D_ATTN, T), BF16),
                   jax.ShapeDtypeStruct((B, N_GROUPS_KV, V_ROWS, tp), BF16),
                   jax.ShapeDtypeStruct((B, N_GROUPS_KV, V_ROWS, tp), BF16),
                   jax.ShapeDtypeStruct((B, GATE_ROWS, T), F32)],
        compiler_params=_cparams(("parallel", "arbitrary"), VMEM_LIMIT),
        name="inproj",
    )(x2, w_main, w_t)


def _compress_kernel(kr_ref, vr_ref, pek_ref, pev_ref, wk_ref, wv_ref, oh_ref, kc_ref, vct_ref, *,
                     n_cmp, n_chunk):
    def compress(raw_ref, pe_ref, w_ref):
        lo = jnp.zeros((n_chunk, 2 * HEAD_DIM), F32)
        hi = jnp.zeros((n_chunk, 2 * HEAD_DIM), F32)
        for j in range(CMP_STRIDE):
            a = raw_ref[pl.ds(j, n_chunk, stride=CMP_STRIDE), :]
            lo = lo + jnp.dot((a + pe_ref[j:j + 1, :]).astype(BF16), w_ref[j],
                              preferred_element_type=F32)
            hi = hi + jnp.dot((a + pe_ref[CMP_STRIDE + j:CMP_STRIDE + j + 1, :]).astype(BF16),
                              w_ref[CMP_STRIDE + j], preferred_element_type=F32)
        hi = pltpu.roll(hi, n_chunk - 1, axis=0)
        rid = lax.broadcasted_iota(jnp.int32, (n_chunk, 2 * HEAD_DIM), 0)
        out = jnp.where(rid < n_cmp, lo + hi, 0.0)
        return jnp.concatenate([jnp.zeros((C_OFF, 2 * HEAD_DIM), F32), out,
                                jnp.zeros((C_PAD - C_OFF - n_chunk, 2 * HEAD_DIM), F32)], axis=0)

    kc_ref[:, 0:2 * HEAD_DIM] = compress(kr_ref, pek_ref, wk_ref).astype(BF16)
    kc_ref[:, 2 * HEAD_DIM:] = oh_ref[...]
    vct_ref[...] = compress(vr_ref, pev_ref, wv_ref).T.astype(BF16)


def _compress(kc_raw, vc_raw, pek, pev, wk, wv, grp_onehot, B, T, n_cmp):
    n_chunk = T // CMP_STRIDE
    blk = pl.BlockSpec((None, T, 2 * HEAD_DIM), lambda b: (b, 0, 0))
    full = lambda a: pl.BlockSpec(a.shape, lambda b: (0,) * a.ndim)
    return pl.pallas_call(
        functools.partial(_compress_kernel, n_cmp=n_cmp, n_chunk=n_chunk),
        grid=(B,),
        in_specs=[blk, blk, full(pek), full(pev), full(wk), full(wv), full(grp_onehot)],
        out_specs=[pl.BlockSpec((None, C_PAD, 256), lambda b: (b, 0, 0)),
                   pl.BlockSpec((None, 2 * HEAD_DIM, C_PAD), lambda b: (b, 0, 0))],
        out_shape=[jax.ShapeDtypeStruct((B, C_PAD, 256), BF16),
                   jax.ShapeDtypeStruct((B, 2 * HEAD_DIM, C_PAD), BF16)],
        compiler_params=_cparams(("parallel",), VMEM_LIMIT),
        name="compress",
    )(kc_raw, vc_raw, pek, pev, wk, wv, grp_onehot)


def _bias_kernel(tab_ref, dt_ref, bt_ref, dc_ref, bc_ref, os_ref, ow_ref, oc_ref):
    h = pl.program_id(0) * GQA_REP + pl.program_id(1)

    def lookup(bk):
        out = jnp.zeros(bk.shape, F32)
        for b in range(N_BUCKETS):
            out = jnp.where(bk == b, tab_ref[h, b], out)
        return (out - tab_ref[h, N_BUCKETS - 1]) * LOG2E

    dt = dt_ref[...]
    v = lookup(bt_ref[...])
    os_ref[...] = jnp.where(dt >= 0, v, NEG)
    ow_ref[...] = jnp.where((dt >= 0) & (dt < WINDOW), v, NEG)
    oc_ref[...] = jnp.where(dc_ref[...] >= 0, lookup(bc_ref[...]), NEG)


def _t5_bucket(dist):
    max_exact = N_BUCKETS // 2
    d = jnp.maximum(dist, 0)
    df = jnp.maximum(d, 1).astype(F32)
    large = max_exact + (jnp.log(df / max_exact) / math.log(MAX_DISTANCE / max_exact)
                         * (N_BUCKETS - max_exact)).astype(jnp.int32)
    large = jnp.minimum(large, N_BUCKETS - 1)
    return jnp.where(d < max_exact, d, large)


def _bias_tiles(rel_bias):
    tq = jnp.arange(TQ, dtype=jnp.int32)
    dt = tq[None, :] + WINDOW - jnp.arange(TAIL, dtype=jnp.int32)[:, None]
    cc = jnp.arange(BAND, dtype=jnp.int32) - C_OFF
    dc = tq[None, :] - (cc[:, None] * CMP_STRIDE + CMP_LEN - 1)
    full = lambda a: pl.BlockSpec(a.shape, lambda g, r: (0,) * a.ndim)
    out = lambda rows: pl.BlockSpec((None, rows, TQ), lambda g, r: (g, 0, r))
    return pl.pallas_call(
        _bias_kernel,
        grid=(N_GROUPS_KV, GQA_REP),
        in_specs=[pl.BlockSpec(memory_space=pltpu.SMEM), full(dt), full(dt), full(dc), full(dc)],
        out_specs=[out(TAIL), out(TAIL), out(BAND)],
        out_shape=[jax.ShapeDtypeStruct((N_GROUPS_KV, TAIL, GQA_REP * TQ), F32),
                   jax.ShapeDtypeStruct((N_GROUPS_KV, TAIL, GQA_REP * TQ), F32),
                   jax.ShapeDtypeStruct((N_GROUPS_KV, BAND, GQA_REP * TQ), F32)],
        compiler_params=_cparams(("arbitrary", "arbitrary")),
        name="biastile",
    )(rel_bias.T, dt, _t5_bucket(dt), dc, _t5_bucket(dc))


def _attn_kernel(qt_ref, gt_ref, kc_ref, vct_ref, kaug_ref, vst_ref, kw_ref, vwt_ref, ds_ref, dw_ref,
                 bc_ref, o_ref, *scratch, n_far_max):
    n_hg = TILES * N_GROUPS_KV
    sc_ref, sw_ref = (scratch[n_hg * k:n_hg * (k + 1)] for k in range(2))
    qa_ref, qat_ref, m_ref, acc_ref, sb_ref, sm_ref, oc_ref, ow_ref, imp_ref = scratch[2 * n_hg:]
    cols4 = GQA_REP * TQ
    groups = range(N_GROUPS_KV)
    zeros_q = jnp.zeros((HEAD_DIM, cols4), BF16)
    grp = lax.broadcasted_iota(jnp.int32, (128, cols4), 0)
    neg_row = jnp.where(grp == 0, NEG, 0.0).astype(BF16)
    tq_col = lax.broadcasted_iota(jnp.int32, (1, cols4), 1) % TQ

    def chunk_start(c):
        return pl.multiple_of(WINDOW + jnp.minimum(c, n_far_max - 1) * FAR, FAR)

    def scores(h, c, par):
        kch = kaug_ref[pl.ds(chunk_start(c), FAR), :]
        for g in groups:
            sf = jnp.dot(kch, qa_ref[2 * h + g], preferred_element_type=F32)
            sb_ref[4 * h + 2 * g + par] = sf
            sm_ref[4 * h + 2 * g + par] = jnp.max(sf, axis=0, keepdims=True)

    def consume(h, nfar, c, par):
        c0 = chunk_start(c)
        live = c < nfar
        for g in groups:
            hg = 2 * h + g
            sf = sb_ref[4 * h + 2 * g + par]
            mprev = m_ref[hg]
            mcand = jnp.maximum(mprev, sm_ref[4 * h + 2 * g + par])
            mnew = jnp.where(live, mcand, mprev)
            pf = jnp.exp2(sf - jnp.where(live, mcand, -NEG)).astype(BF16)
            acc_ref[hg] = jnp.exp2(mprev - mnew) * acc_ref[hg] + jnp.dot(
                vst_ref[g, :, pl.ds(c0, FAR)], pf, preferred_element_type=F32)
            m_ref[hg] = mnew

    tiles = range(TILES)
    chains = [(h, g) for h in tiles for g in groups]
    t_i = [pl.program_id(1) * TILES + h for h in tiles]
    t_qs = [pl.multiple_of(i * TQ, TQ) for i in t_i]
    t_nfar = [jnp.maximum(i - 1, 0) // 4 for i in t_i]
    def qpad(h, g):
        qgt = jnp.concatenate(
            [qt_ref[(GQA_REP * g + r) * HEAD_DIM:(GQA_REP * g + r + 1) * HEAD_DIM, h * TQ:(h + 1) * TQ]
             for r in range(GQA_REP)], axis=1)
        return jnp.concatenate([qgt, zeros_q] if g == 0 else [zeros_q, qgt], axis=0)

    def compressed_and_window(rows):
        for h, g in chains:
            i = t_i[h]
            gmask = jnp.where((grp < C_OFF // 8) | (grp >= i + BAND // 8), NEG, 0.0).astype(BF16)
            band0 = pl.multiple_of(i * 8, 8)
            sc_ref[2 * h + g][0:rows, :] = jnp.dot(kc_ref[0:rows, :],
                                                   jnp.concatenate([qpad(h, g), gmask], axis=0),
                                                   preferred_element_type=F32)
            sc_ref[2 * h + g][pl.ds(band0, BAND), :] = sc_ref[2 * h + g][pl.ds(band0, BAND), :] + bc_ref[g]
        w_max = {}
        for h, g in chains:
            sw = jnp.dot(kw_ref[pl.ds(t_qs[h], TAIL), :], jnp.concatenate([qpad(h, g), neg_row], axis=0),
                         preferred_element_type=F32) + dw_ref[g]
            sw_ref[2 * h + g][...] = sw
            w_max[h, g] = jnp.max(sw, axis=0, keepdims=True)
        for h, g in chains:
            s = sc_ref[2 * h + g][0:rows, :]
            e = jnp.exp2(s - jnp.max(s, axis=0, keepdims=True))
            has_cmp = t_qs[h] + tq_col >= CMP_LEN - 1
            pinv = jnp.where(has_cmp, 1.0 / jnp.sum(e, axis=0, keepdims=True), 0.0)
            res = jnp.dot(vct_ref[:, 0:rows], e.astype(BF16), preferred_element_type=F32)
            oc_ref[2 * h + g] = res[0:2 * HEAD_DIM, :] * pinv
            imp_ref[2 * h + g] = sum(res[2 * HEAD_DIM:, r * TQ:(r + 1) * TQ] * pinv[:, r * TQ:(r + 1) * TQ]
                                     for r in range(GQA_REP))
        for h, g in chains:
            pw = jnp.exp2(sw_ref[2 * h + g][...] - w_max[h, g]).astype(BF16)
            acc_w = jnp.dot(vwt_ref[g, :, pl.ds(t_qs[h], TAIL)], pw, preferred_element_type=F32)
            ow_ref[2 * h + g] = acc_w[0:HEAD_DIM, :] * (1.0 / acc_w[HEAD_DIM:HEAD_DIM + 1, :])

    step = pl.program_id(1)
    first_step = 0
    for rows in CMP_ROW_LEVELS:
        end_step = ((rows - BAND) // 8 - TILES + 1) // TILES + 1
        in_level = step >= first_step if rows == C_PAD else (step >= first_step) & (step < end_step)
        pl.when(in_level)(functools.partial(compressed_and_window, rows))
        first_step = end_step

    def select_blocks(nsel):
        sid = lax.broadcasted_iota(jnp.int32, (nsel, TQ), 0)
        sid_f = sid.astype(F32)
        lane = lax.broadcasted_iota(jnp.int32, (nsel, TQ), 1)
        score, mbt = {}, {}
        for h, g in chains:
            cur = (t_qs[h] + lane) // SEL_BLOCK
            forced = (sid == 0) | (sid == cur) | (sid == cur - 1)
            score[h, g] = jnp.where(forced, -jnp.inf, jnp.where(sid <= cur, imp_ref[2 * h + g, 0:nsel, :], NEG))
            mbt[h, g] = jnp.where(forced, 0.0, NEG)
        for _ in range(N_SEL - 3):
            for ch in chains:
                cm = jnp.max(score[ch], axis=0, keepdims=True)
                first = jnp.min(jnp.where(score[ch] == cm, sid_f, float(N_SLC)), axis=0, keepdims=True)
                pick = sid_f == first
                mbt[ch] = jnp.where(pick, 0.0, mbt[ch])
                score[ch] = jnp.where(pick, -jnp.inf, score[ch])
        for h, g in chains:
            mbt_tail = jnp.where(sid < t_nfar[h] * (FAR // SEL_BLOCK), NEG, mbt[h, g])
            for dst, m in ((qa_ref, mbt[h, g]), (qat_ref, mbt_tail)):
                m = jnp.concatenate([m, jnp.full((N_SLC - nsel, TQ), NEG, F32)], axis=0) if nsel < N_SLC else m
                dst[2 * h + g, 0:N_SLC, :] = jnp.concatenate([m.astype(BF16)] * GQA_REP, axis=1)

    first_step = 0
    for nsel in SEL_ROW_LEVELS:
        end_step = nsel // (TILES * TQ // SEL_BLOCK)
        in_level = step >= first_step if nsel == N_SLC else (step >= first_step) & (step < end_step)
        pl.when(in_level)(functools.partial(select_blocks, nsel))
        first_step = end_step

    for h, g in chains:
        qa_ref[2 * h + g, N_SLC:, :] = qpad(h, g)
        qa_tail = jnp.concatenate([qat_ref[2 * h + g], qpad(h, g)], axis=0)
        st = jnp.dot(kaug_ref[pl.ds(t_qs[h], TAIL), :], qa_tail, preferred_element_type=F32) + ds_ref[g]
        sc_ref[2 * h + g][...] = st
        m_ref[2 * h + g] = jnp.max(st, axis=0, keepdims=True)
    for h in tiles:
        scores(h, 0, 0)
    for h, g in chains:
        pt = jnp.exp2(sc_ref[2 * h + g][...] - m_ref[2 * h + g]).astype(BF16)
        acc_ref[2 * h + g] = jnp.dot(vst_ref[g, :, pl.ds(t_qs[h], TAIL)], pt, preferred_element_type=F32)

    def far_pair(kk, carry):
        c = 2 * kk
        for h in tiles:
            scores(h, c + 1, 1)
        for h in tiles:
            consume(h, t_nfar[h], c, 0)
        for h in tiles:
            scores(h, c + 2, 0)
        for h in tiles:
            consume(h, t_nfar[h], c + 1, 1)
        return carry

    n_chunks = t_nfar[TILES - 1]
    n_pairs = jnp.maximum(n_chunks - 1, 0) // 2
    lax.fori_loop(0, n_pairs, far_pair, 0)

    @pl.when(n_chunks - 2 * n_pairs == 1)
    def _():
        for h in tiles:
            consume(h, t_nfar[h], n_chunks - 1, 0)

    @pl.when(n_chunks - 2 * n_pairs == 2)
    def _():
        for h in tiles:
            scores(h, n_chunks - 1, 1)
        for h in tiles:
            consume(h, t_nfar[h], n_chunks - 2, 0)
        for h in tiles:
            consume(h, t_nfar[h], n_chunks - 1, 1)

    for h in tiles:
        lanes = slice(h * TQ, (h + 1) * TQ)
        out_rows = []
        for g in groups:
            dsl = slice(g * HEAD_DIM, (g + 1) * HEAD_DIM)
            hg = 2 * h + g
            o_s = acc_ref[hg, 0:HEAD_DIM, :] * (1.0 / acc_ref[hg, HEAD_DIM:HEAD_DIM + 1, :])
            for r in range(GQA_REP):
                cs = slice(r * TQ, (r + 1) * TQ)
                gc = 3 * (GQA_REP * g + r)
                out_rows.append(gt_ref[gc:gc + 1, lanes] * oc_ref[hg, dsl, cs]
                                + gt_ref[gc + 1:gc + 2, lanes] * o_s[:, cs]
                                + gt_ref[gc + 2:gc + 3, lanes] * ow_ref[hg, :, cs])
        o_ref[h * TQ:(h + 1) * TQ, :] = jnp.concatenate(out_rows, axis=0).T


def _attention(t_all, gates_t, kc2, vct, kaug, vst, kw_pad, vwt, ds, dw, bc, B, T):
    tp = T + WINDOW
    tqs = TQ * TILES
    per_b = lambda shape: pl.BlockSpec((None,) + shape, lambda b, i: (b,) + (0,) * len(shape),
                                      pipeline_mode=pl.Buffered(1))
    const = lambda a: pl.BlockSpec(a.shape, lambda b, i: (0,) * a.ndim, pipeline_mode=pl.Buffered(1))
    cols4 = GQA_REP * TQ
    n_hg = TILES * N_GROUPS_KV
    return pl.pallas_call(
        functools.partial(_attn_kernel, n_far_max=T // FAR),
        grid=(B, T // tqs),
        in_specs=[pl.BlockSpec((None, 512, tqs), lambda b, i: (b, 0, i)),
                  pl.BlockSpec((None, GATE_ROWS, tqs), lambda b, i: (b, 0, i)),
                  per_b((C_PAD, 256)),
                  per_b((256, C_PAD)),
                  per_b((tp, 256)),
                  per_b((N_GROUPS_KV, V_ROWS, tp)),
                  per_b((tp, 256)),
                  per_b((N_GROUPS_KV, V_ROWS, tp)),
                  const(ds), const(dw), const(bc)],
        out_specs=pl.BlockSpec((None, tqs, 512), lambda b, i: (b, i, 0)),
        out_shape=jax.ShapeDtypeStruct((B, T, 512), F32),
        scratch_shapes=([pltpu.VMEM((C_PAD, cols4), F32)] * n_hg
                        + [pltpu.VMEM((TAIL, cols4), F32)] * n_hg
                        + [pltpu.VMEM((n_hg, N_SLC + 2 * HEAD_DIM, cols4), BF16),
                           pltpu.VMEM((n_hg, N_SLC, cols4), BF16),
                           pltpu.VMEM((n_hg, 1, cols4), F32),
                           pltpu.VMEM((n_hg, V_ROWS, cols4), F32),
                           pltpu.VMEM((2 * n_hg, FAR, cols4), F32),
                           pltpu.VMEM((2 * n_hg, 1, cols4), F32),
                           pltpu.VMEM((n_hg, 2 * HEAD_DIM, cols4), F32),
                           pltpu.VMEM((n_hg, HEAD_DIM, cols4), F32),
                           pltpu.VMEM((n_hg, N_SLC, TQ), F32)]),
        compiler_params=_cparams(("parallel", "arbitrary"), VMEM_LIMIT),
        name="attn",
    )(t_all, gates_t, kc2, vct, kaug, vst, kw_pad, vwt, ds, dw, bc)


def _rglru_kernel(*refs, L):
    J = L // 8
    n_col = D_RNN // 128
    rx_refs, ry_refs = refs[:n_col], refs[n_col:2 * n_col]
    (cw_ref, cb_ref, wa_ref, ba_ref, wx_ref, bx_ref, sp_ref, eu_ref, ed_ref, o_ref, eub_ref, edb_ref,
     xprev_ref, h_ref, os_ref) = refs[2 * n_col:]
    eub_ref[...] = eu_ref[...].astype(BF16)
    edb_ref[...] = ed_ref[...].astype(BF16)

    @pl.when(pl.program_id(1) == 0)
    def _():
        xprev_ref[...] = jnp.zeros(xprev_ref.shape, F32)
        h_ref[...] = jnp.zeros(h_ref.shape, F32)

    def strided(col_refs):
        return jnp.concatenate(
            [jnp.concatenate([ref[pl.ds(j, 8, stride=J), :] for j in range(J)], axis=0)
             for ref in col_refs], axis=1)

    xp = strided(rx_refs)
    yp = strided(ry_refs)
    sub = lax.broadcasted_iota(jnp.int32, (8, D_RNN), 0)

    def delayed(d):
        heads = []
        for j in range(d):
            src = pltpu.roll(xp[(J + j - d) * 8:(J + j - d + 1) * 8], 1, axis=0)
            heads.append(jnp.where(sub == 0, xprev_ref[8 + j - d:9 + j - d, :], src))
        return jnp.concatenate(heads + [xp[:L - 8 * d]], axis=0)

    xc = (cb_ref[...] + cw_ref[0:1, :] * delayed(3) + cw_ref[1:2, :] * delayed(2)
          + cw_ref[2:3, :] * delayed(1) + cw_ref[3:4, :] * xp)
    xprev_ref[...] = jnp.concatenate([ref[L - 8:L, :] for ref in rx_refs], axis=1)
    xcb = xc.astype(BF16)
    r = jax.nn.sigmoid(jnp.dot(xcb, wa_ref[...], preferred_element_type=F32) + ba_ref[...])
    ig = jax.nn.sigmoid(jnp.dot(xcb, wx_ref[...], preferred_element_type=F32) + bx_ref[...])
    a = jnp.exp(-RG_C * r * sp_ref[...])
    bt = jnp.sqrt(1.0 - a * a) * (ig * xc)

    h = jnp.zeros((8, D_RNN), F32)
    dec = jnp.ones((8, D_RNN), F32)
    hs, decs = [], []
    for j in range(J):
        aj = a[j * 8:(j + 1) * 8]
        h = aj * h + bt[j * 8:(j + 1) * 8]
        dec = aj * dec
        hs.append(h)
        decs.append(dec)
    carry = h_ref[0:1, :]
    carries = []
    for s in range(8):
        carries.append(carry)
        carry = dec[s:s + 1, :] * carry + h[s:s + 1, :]
    h_ref[...] = jnp.broadcast_to(carry, h_ref.shape)
    h_in = jnp.concatenate(carries, axis=0)
    cdf_c = math.sqrt(2.0 / math.pi)
    for j in range(J):
        y = yp[j * 8:(j + 1) * 8]
        cdf = 0.5 * (1.0 + jnp.tanh(cdf_c * (y + 0.044715 * (y * y * y))))
        res = (hs[j] + decs[j] * h_in) * (y * cdf)
        for c in range(n_col):
            os_ref[c, pl.ds(j, 8, stride=J), :] = res[:, c * 128:(c + 1) * 128]
    for c in range(n_col):
        o_ref[:, c * 128:(c + 1) * 128] = os_ref[c]


def _rglru(rx, ry, conv_w, conv_b, wa_bd, b_a, wx_bd, b_x, sp, w_up, w_dn, B, T):
    L = 512
    steps = B * T // L
    nt = T // L
    up_rows, dn_rows = w_up.shape[0] // steps, w_dn.shape[0] // steps
    assert up_rows * steps == w_up.shape[0] and dn_rows * steps == w_dn.shape[0] and dn_rows % 16 == 0
    slab = lambda rows, a: pl.BlockSpec((rows, a.shape[1]), lambda b, t: (b * nt + t, 0))
    blk = pl.BlockSpec((None, L, D_RNN), lambda b, t: (b, t, 0))
    n_col = D_RNN // 128
    cols = [pl.BlockSpec((None, L, 128), functools.partial(lambda b, t, c: (b, t, c), c=c))
            for c in range(n_col)]
    full = lambda a: pl.BlockSpec(a.shape, lambda b, t: (0,) * a.ndim)
    return pl.pallas_call(
        functools.partial(_rglru_kernel, L=L),
        grid=(B, T // L),
        in_specs=cols + cols + [full(conv_w), full(conv_b), full(wa_bd), full(b_a), full(wx_bd),
                                full(b_x), full(sp), slab(up_rows, w_up), slab(dn_rows, w_dn)],
        out_specs=[blk, slab(up_rows, w_up), slab(dn_rows, w_dn)],
        out_shape=[jax.ShapeDtypeStruct((B, T, D_RNN), F32),
                   jax.ShapeDtypeStruct(w_up.shape, BF16),
                   jax.ShapeDtypeStruct(w_dn.shape, BF16)],
        scratch_shapes=[pltpu.VMEM((8, D_RNN), F32), pltpu.VMEM((8, D_RNN), F32),
                        pltpu.VMEM((n_col, L, 128), F32)],
        compiler_params=_cparams(("parallel", "arbitrary"), VMEM_LIMIT),
        name="rglru",
    )(*([rx] * n_col + [ry] * n_col), conv_w, conv_b, wa_bd, b_a, wx_bd, b_x, sp, w_up, w_dn)


def _layer_norm(y, g, b):
    mu = jnp.mean(y, axis=-1, keepdims=True)
    d = y - mu
    var = jnp.mean(d * d, axis=-1, keepdims=True)
    return d * lax.rsqrt(var + LN_EPS) * g + b


def _outproj_kernel(at_ref, rn_ref, x_ref, ga_ref, gr_ref, wo_ref, g1_ref, b1_ref, wr2_ref, br_ref,
                    x1_ref, comb_ref):
    sub = at_ref.shape[0] // OUT_SUBS
    subs = [slice(h * sub, (h + 1) * sub) for h in range(OUT_SUBS)]
    heads, mix, xh, xl, hh_hl, lo = {}, {}, {}, {}, {}, {}
    for r in subs:
        a = at_ref[r, :]
        rn = rn_ref[r, :]
        ha = a * lax.rsqrt(jnp.mean(a * a, axis=-1, keepdims=True) + RMS_EPS) * ga_ref[...]
        hr = rn * lax.rsqrt(jnp.mean(rn * rn, axis=-1, keepdims=True) + RMS_EPS) * gr_ref[...]
        heads[r] = jnp.concatenate([ha, hr], axis=1).astype(BF16)
    for r in subs:
        mix[r] = jnp.dot(heads[r], wo_ref[...], preferred_element_type=F32)
    for r in subs:
        x1 = _layer_norm(ALPHA * x_ref[r, :] + mix[r], g1_ref[...], b1_ref[...])
        x1_ref[r, :] = x1
        xh[r] = x1.astype(BF16)
        xl[r] = (x1 - xh[r].astype(F32)).astype(BF16)
    for r in subs:
        hh_hl[r] = jnp.dot(xh[r], wr2_ref[...], preferred_element_type=F32)
        lo[r] = jnp.dot(xl[r], wr2_ref[:, :128], preferred_element_type=F32)
    for r in subs:
        lg = (hh_hl[r][:, :128] + lo[r] + hh_hl[r][:, 128:]) + br_ref[...]
        lane = lax.broadcasted_iota(jnp.int32, lg.shape, 1)
        lane_f = lane.astype(F32)
        big = 1e9
        isg = lane < N_EXP_GROUPS
        gmax = jnp.max(jnp.where(isg, lg, -jnp.inf), axis=-1, keepdims=True)
        pg_top = 1.0 / jnp.sum(jnp.where(isg, jnp.exp(lg - gmax), 0.0), axis=-1, keepdims=True)
        gi = jnp.min(jnp.where(isg & (lg == gmax), lane_f, big), axis=-1, keepdims=True)
        egrp = ((lane - N_EXP_GROUPS) // EXPERTS_PER_GROUP).astype(F32)
        ise = (lane >= N_EXP_GROUPS) & (lane < N_EXP_GROUPS + N_EXPERTS) & (egrp == gi)
        emax = jnp.max(jnp.where(ise, lg, -jnp.inf), axis=-1, keepdims=True)
        i1 = jnp.min(jnp.where(ise & (lg == emax), lane_f, big), axis=-1, keepdims=True)
        rest = ise & (lane_f != i1)
        m2 = jnp.max(jnp.where(rest, lg, -jnp.inf), axis=-1, keepdims=True)
        i2 = jnp.min(jnp.where(rest & (lg == m2), lane_f, big), axis=-1, keepdims=True)
        e2 = jnp.exp(m2 - emax)
        inv = pg_top / (1.0 + e2)
        comb_ref[r, :] = jnp.where(lane_f == i1, inv, 0.0) + jnp.where(lane_f == i2, inv * e2, 0.0)


def _outproj(attn, rnn, x2, ga, gr, wo, g1, b1, wr2, br, N):
    tm = 512
    row = lambda w: pl.BlockSpec((tm, w), lambda i: (i, 0))
    full = lambda a: pl.BlockSpec(a.shape, lambda i: (0,) * a.ndim)
    return pl.pallas_call(
        _outproj_kernel,
        grid=(N // tm,),
        in_specs=[row(512), row(512), row(D_MODEL), full(ga), full(gr), full(wo), full(g1), full(b1),
                  full(wr2), full(br)],
        out_specs=[row(D_MODEL), row(128)],
        out_shape=[jax.ShapeDtypeStruct((N, D_MODEL), F32), jax.ShapeDtypeStruct((N, 128), F32)],
        compiler_params=_cparams(("parallel",), VMEM_LIMIT),
        name="outproj",
    )(attn, rnn, x2, ga, gr, wo, g1, b1, wr2, br)


def _moe_kernel(x1_ref, comb_ref, p_ref, tri_ref, wup_ref, wdn_ref, wg_ref, wp_ref, g2_ref, b2_ref, o_ref,
                acc_ref, xb_ref, rank_ref, rankt_ref, combt_ref):
    pi = pl.program_id(1)
    n_sub = x1_ref.shape[0] // MOE_SUB
    sub_rows = [slice(h * MOE_SUB, (h + 1) * MOE_SUB) for h in range(n_sub)]
    lane_col = N_EXP_GROUPS

    @pl.when(pi == 0)
    def _():
        x1 = x1_ref[...]
        xb = x1.astype(BF16)
        xb_ref[...] = xb
        gate = jax.nn.sigmoid(jnp.dot(xb, wg_ref[...], preferred_element_type=F32))
        ple = gate * jnp.dot(p_ref[...].astype(BF16), wp_ref[...], preferred_element_type=F32)
        acc_ref[...] = ALPHA * x1 + ple
        for rows in sub_rows:
            comb = comb_ref[rows, :]
            chosen = comb > 0.0
            rank = jnp.dot(tri_ref[...], jnp.where(chosen, 1.0, 0.0).astype(BF16),
                           preferred_element_type=F32)
            rank = jnp.where(chosen, rank, -1.0)
            rank_ref[rows, :] = rank
            rankt_ref[:, rows] = rank.T
            combt_ref[:, rows] = comb.T

    lane = lax.broadcasted_iota(jnp.int32, (MOE_SUB, 128), 1)
    slot_r = lax.broadcasted_iota(jnp.int32, (MOE_SLOTS, MOE_SUB), 0).astype(F32)
    slot_c = lax.broadcasted_iota(jnp.int32, (MOE_SUB, MOE_PAIR_K), 1).astype(F32)
    pad_rows = jnp.zeros((MOE_PAIR_K - 2 * MOE_SLOTS, D_MODEL), BF16)

    def route_pair(base):
        cmask, xc, y, ys, rk_col = {}, {}, {}, {}, {}
        for rows in sub_rows:
            for k in range(2):
                rk_row = rankt_ref[pl.ds(2 * pi + k + lane_col, 1), rows] - base
                cmask[rows, k] = rk_row == slot_r
            sel = jnp.concatenate([jnp.where(cmask[rows, k], 1.0, 0.0).astype(BF16) for k in range(2)], axis=0)
            xcs = jnp.dot(sel, xb_ref[rows, :], preferred_element_type=F32).astype(BF16)
            for k in range(2):
                xc[rows, k] = xcs[k * MOE_SLOTS:(k + 1) * MOE_SLOTS, :]
        u = [jnp.dot(jnp.concatenate([xc[rows, k] for rows in sub_rows], axis=0), wup_ref[k],
                     preferred_element_type=F32) for k in range(2)]
        hsw = []
        for k in range(2):
            ua = u[k][:, :D_FF]
            hsw.append((ua * jax.nn.sigmoid(ua) * u[k][:, D_FF:]).astype(BF16))
        for k in range(2):
            yk = jnp.dot(hsw[k], wdn_ref[k], preferred_element_type=F32)
            for h, rows in enumerate(sub_rows):
                y[rows, k] = yk[h * MOE_SLOTS:(h + 1) * MOE_SLOTS, :]
        for rows in sub_rows:
            for k in range(2):
                e_lane = 2 * pi + k + lane_col
                w_row = combt_ref[pl.ds(e_lane, 1), rows]
                w_slot = jnp.sum(jnp.where(cmask[rows, k], w_row, 0.0), axis=1, keepdims=True)
                ys[rows, k] = (w_slot * y[rows, k]).astype(BF16)
                rk_col[rows, k] = jnp.sum(jnp.where(lane == e_lane, rank_ref[rows, :], 0.0), axis=1,
                                          keepdims=True) - base
        for rows in sub_rows:
            slot_of = jnp.where(slot_c < MOE_SLOTS, rk_col[rows, 0], rk_col[rows, 1] + MOE_SLOTS)
            acc_ref[rows, :] = acc_ref[rows, :] + jnp.dot(
                jnp.where(slot_of == slot_c, 1.0, 0.0).astype(BF16),
                jnp.concatenate([ys[rows, 0], ys[rows, 1], pad_rows], axis=0), preferred_element_type=F32)

    route_pair(0.0)

    top_rank = jnp.max(jnp.maximum(rankt_ref[pl.ds(2 * pi + lane_col, 1), :],
                                   rankt_ref[pl.ds(2 * pi + 1 + lane_col, 1), :]))

    @pl.when(top_rank >= MOE_SLOTS)
    def _():
        def extra(sb, carry):
            route_pair((sb * MOE_SLOTS).astype(F32))
            return carry

        lax.fori_loop(1, top_rank.astype(jnp.int32) // MOE_SLOTS + 1, extra, 0)

    @pl.when(pi == N_EXPERTS // 2 - 1)
    def _():
        o_ref[...] = _layer_norm(acc_ref[...], g2_ref[...], b2_ref[...])


def _moe(x1, comb, p2, wup, wdn, wg, wp, g2, b2, N):
    tm = MOE_TILE
    tri = (jnp.arange(MOE_SUB)[None, :] < jnp.arange(MOE_SUB)[:, None]).astype(BF16)
    row = lambda w: pl.BlockSpec((tm, w), lambda i, e: (i, 0))
    full = lambda a: pl.BlockSpec(a.shape, lambda i, e: (0,) * a.ndim)
    return pl.pallas_call(
        _moe_kernel,
        grid=(N // tm, N_EXPERTS // 2),
        in_specs=[row(D_MODEL), row(128), row(D_PLE), full(tri),
                  pl.BlockSpec((2, D_MODEL, 2 * D_FF), lambda i, e: (e, 0, 0)),
                  pl.BlockSpec((2, D_FF, D_MODEL), lambda i, e: (e, 0, 0)),
                  full(wg), full(wp), full(g2), full(b2)],
        out_specs=row(D_MODEL),
        out_shape=jax.ShapeDtypeStruct((N, D_MODEL), F32),
        scratch_shapes=[pltpu.VMEM((tm, D_MODEL), F32),
                        pltpu.VMEM((tm, D_MODEL), BF16),
                        pltpu.VMEM((tm, 128), F32),
                        pltpu.VMEM((128, tm), F32),
                        pltpu.VMEM((128, tm), F32)],
        compiler_params=_cparams(("parallel", "arbitrary"), VMEM_LIMIT),
        name="moe",
    )(x1, comb, p2, tri, wup, wdn, wg, wp, g2, b2)


def _block_diag(w):
    n, k, j = w.shape
    eye = jnp.eye(n, dtype=w.dtype)
    return (w[:, :, None, :] * eye[:, None, :, None]).reshape(n * k, n * j)


def kernel(x, p, rel_bias, w_in, cmp_pe_k, cmp_pe_v, cmp_w_k, cmp_w_v, conv_w, conv_b, rg_w_a, rg_b_a,
           rg_w_x, rg_b_x, rg_lambda, attn_out_gain, rnn_out_gain, w_out, ln1_g, ln1_b, router_group_w,
           router_group_b, router_expert_w, router_expert_b, expert_w_up, expert_w_down, ple_w,
           ple_gate_w, ln2_g, ln2_b):
    B, T, _ = x.shape
    N = B * T
    assert T % FAR == 0 and T % (TQ * TILES) == 0 and T % WINDOW == 0 and T // SEL_BLOCK <= N_SLC
    assert w_in.shape[0] == 1
    n_cmp = (T - CMP_LEN) // CMP_STRIDE + 1
    n_chunk = T // CMP_STRIDE
    assert C_OFF + n_chunk <= C_PAD
    row1 = lambda v: v.reshape(1, -1)

    w = w_in[0]
    c0 = D_ATTN
    kv = lambda k: w[:, c0 + k * D_KV:c0 + (k + 1) * D_KV]
    g0 = c0 + 6 * D_KV
    wq = w[:, :D_ATTN] * (HEAD_DIM ** -0.5 * LOG2E)
    w_main = jnp.concatenate([kv(2), kv(4), kv(0), kv(1), w[:, g0 + 24:]], axis=1).astype(BF16)
    w_gate = jnp.pad(w[:, g0:g0 + 24], ((0, 0), (0, GATE_ROWS - 24)))
    w_t = jnp.concatenate([wq, kv(3), kv(5), w_gate], axis=1).T.astype(BF16)

    x2 = x.reshape(N, D_MODEL)
    kaug, kw_pad, kc_raw, vc_raw, rx, ry, t_all, vst, vwt, gates_t = _inproj(x2, w_main, w_t, B, T)

    def per_pos(wc):
        wl = wc.reshape(CMP_LEN, HEAD_DIM, HEAD_DIM)
        z = jnp.zeros_like(wl)
        return jnp.concatenate([jnp.concatenate([wl, z], axis=2), jnp.concatenate([z, wl], axis=2)],
                               axis=1).astype(BF16)

    pek = jnp.tile(cmp_pe_k[0], (1, N_GROUPS_KV))
    pev = jnp.tile(cmp_pe_v[0], (1, N_GROUPS_KV))
    grp_onehot = (jnp.arange(C_PAD)[:, None] // 8 == jnp.arange(128)[None, :]).astype(BF16)
    kc2, vct = _compress(kc_raw.reshape(B, T, 128), vc_raw.reshape(B, T, 128), pek, pev,
                         per_pos(cmp_w_k[0]), per_pos(cmp_w_v[0]), grp_onehot, B, T, n_cmp)

    ds, dw, bc = _bias_tiles(rel_bias)

    cidx = jnp.arange(C_PAD) - C_OFF
    s_lo = jnp.arange(N_SLC)[:, None] * SEL_BLOCK
    overlap = ((cidx[None, :] * CMP_STRIDE < s_lo + SEL_BLOCK) & (cidx[None, :] * CMP_STRIDE + CMP_LEN > s_lo)
               & (cidx[None, :] >= 0) & (cidx[None, :] < n_cmp)).astype(BF16)
    vct = jnp.concatenate([vct, jnp.broadcast_to(overlap, (B, N_SLC, C_PAD))], axis=1)
    attn = _attention(t_all, gates_t, kc2, vct, kaug, vst, kw_pad, vwt, ds, dw, bc, B, T)

    sp = jax.nn.softplus(-rg_lambda[0].astype(F32))
    rnn, wup_b, wdn_b = _rglru(rx.reshape(B, T, D_RNN), ry.reshape(B, T, D_RNN), conv_w[0], row1(conv_b[0]),
                               _block_diag(rg_w_a[0]).astype(BF16), row1(rg_b_a[0]),
                               _block_diag(rg_w_x[0]).astype(BF16), row1(rg_b_x[0]), row1(sp),
                               expert_w_up[0].reshape(N_EXPERTS * D_MODEL, 2 * D_FF),
                               expert_w_down[0].reshape(N_EXPERTS * D_FF, D_MODEL), B, T)

    wr = jnp.pad(jnp.concatenate([router_group_w[0], router_expert_w[0]], axis=1), ((0, 0), (0, 108)))
    wrh = wr.astype(BF16)
    wr2 = jnp.concatenate([wrh, (wr - wrh.astype(F32)).astype(BF16)], axis=1)
    br = jnp.pad(jnp.concatenate([router_group_b[0], router_expert_b[0]]), (0, 108)).reshape(1, 128)
    x1, comb = _outproj(attn.reshape(N, 512), rnn.reshape(N, D_RNN), x2, row1(attn_out_gain[0]),
                        row1(rnn_out_gain[0]), w_out[0].astype(BF16), row1(ln1_g[0]), row1(ln1_b[0]),
                        wr2, br, N)

    out = _moe(x1, comb, p[0].reshape(N, D_PLE), wup_b.reshape(N_EXPERTS, D_MODEL, 2 * D_FF),
               wdn_b.reshape(N_EXPERTS, D_FF, D_MODEL), ple_gate_w[0].astype(BF16), ple_w[0].astype(BF16),
               row1(ln2_g[0]), row1(ln2_b[0]), N)
    return out.reshape(B, T, D_MODEL)
```

```python
import functools
import math

import jax
import jax.numpy as jnp
from jax import lax
from jax.experimental import pallas as pl
from jax.experimental.pallas import tpu as pltpu

F32 = jnp.float32
BF16 = jnp.bfloat16
NEG = -1e30
LOG2E = 1.4426950408889634

D_MODEL = 1024
HEAD_DIM = 64
N_HEADS = 8
N_GROUPS_KV = 2
GQA_REP = 4
D_ATTN = 512
D_RNN = 512
D_KV = 128
CMP_LEN = 32
CMP_STRIDE = 16
SEL_BLOCK = 64
N_SEL = 16
WINDOW = 512
N_BUCKETS = 32
MAX_DISTANCE = 128
N_EXPERTS = 16
EXPERTS_PER_GROUP = 4
N_EXP_GROUPS = 4
D_FF = 512
D_PLE = 256
ALPHA = 2.0 ** 0.25
LN_EPS = 1e-5
RMS_EPS = 1e-6
FORCE_BONUS = 1e4
RG_C = 8.0

TQ = 128
TILES = 2
N_SLC = 128
C_PAD = 640
C_OFF = 16
CMP_ROW_LEVELS = (256, 384, 512, C_PAD)
SEL_ROW_LEVELS = (32, 64, 96, N_SLC)
FAR = 512
TAIL = WINDOW + TQ
BAND = 24
GATE_ROWS = 32
V_ROWS = 80
OUT_SUBS = 2
MOE_TILE = 1024
MOE_SUB = 512
MOE_SLOTS = 112
MOE_FF_BLOCK = 256
MOE_PAIR_K = 256
VMEM_LIMIT = 56 * 1024 * 1024


def _cparams(sem, vmem=None):
    return pltpu.CompilerParams(dimension_semantics=sem, vmem_limit_bytes=vmem)


def _inproj_kernel(x_ref, wm_ref, wt_ref, ka_ref, kw_ref, kc_ref, vc_ref, rx_ref, ry_ref, t_ref, vs_ref,
                   vw_ref, gt_ref):
    tm = x_ref.shape[0]
    j = pl.program_id(1)
    ones_rows = jnp.where(lax.broadcasted_iota(jnp.int32, (V_ROWS - HEAD_DIM, tm), 0) == 0, 1.0, 0.0).astype(BF16)
    lane = lax.broadcasted_iota(jnp.int32, (tm, 2 * N_SLC), 1)

    @pl.when(j == 0)
    def _():
        ka_ref[...] = jnp.where(lane == N_SLC - 1, 1.0, 0.0).astype(BF16)
        kw_ref[...] = jnp.where(lane == N_SLC, 1.0, 0.0).astype(BF16)
        for v_ref in (vs_ref, vw_ref):
            for g in range(N_GROUPS_KV):
                v_ref[g, :HEAD_DIM, :] = jnp.zeros((HEAD_DIM, tm), BF16)
                v_ref[g, HEAD_DIM:, :] = ones_rows

    @pl.when(j > 0)
    def _():
        _inproj_tile(x_ref, wm_ref, wt_ref, ka_ref, kw_ref, kc_ref, vc_ref, rx_ref, ry_ref, t_ref, vs_ref,
                     vw_ref, gt_ref, (j - 1) * tm, ones_rows)


def _inproj_tile(x_ref, wm_ref, wt_ref, ka_ref, kw_ref, kc_ref, vc_ref, rx_ref, ry_ref, t_ref, vs_ref, vw_ref,
                 gt_ref, t0, ones_rows):
    tm = x_ref.shape[0]
    xb = x_ref[...].astype(BF16)

    def mm(lo, hi):
        return jnp.dot(xb, wm_ref[:, lo:hi], preferred_element_type=F32)

    k_pair = mm(0, 256)
    blk = (t0 + lax.broadcasted_iota(jnp.int32, (tm, N_SLC), 0)) // SEL_BLOCK
    ka_ref[:, :N_SLC] = jnp.where(blk == lax.broadcasted_iota(jnp.int32, (tm, N_SLC), 1), 1.0, 0.0).astype(BF16)
    ka_ref[:, N_SLC:] = k_pair[:, :128].astype(BF16)
    kw_ref[:, :N_SLC] = k_pair[:, 128:].astype(BF16)
    kw_ref[:, N_SLC:] = jnp.zeros((tm, N_SLC), BF16)
    c_pair = mm(256, 512)
    kc_ref[...] = c_pair[:, :128]
    vc_ref[...] = c_pair[:, 128:]
    rx_ref[...] = mm(512, 1024)
    ry_ref[...] = mm(1024, 1536)
    tr = lax.dot_general(wt_ref[...], xb, (((1,), (1,)), ((), ())), preferred_element_type=F32)
    t_ref[...] = tr[:D_ATTN].astype(BF16)
    for v_ref, base in ((vs_ref, D_ATTN), (vw_ref, D_ATTN + D_KV)):
        for g in range(N_GROUPS_KV):
            v_ref[g, :HEAD_DIM, :] = tr[base + g * HEAD_DIM:base + (g + 1) * HEAD_DIM].astype(BF16)
            v_ref[g, HEAD_DIM:, :] = ones_rows
    gt_ref[...] = jax.nn.sigmoid(tr[D_ATTN + 2 * D_KV:])


def _inproj(x2, w_main, w_t, B, T):
    N = B * T
    tm = WINDOW
    nt = T // tm
    tp = T + WINDOW
    data = lambda j: jnp.maximum(j - 1, 0)
    row = lambda w: pl.BlockSpec((tm, w), lambda b, j: (b * nt + data(j), 0))
    full = lambda a: pl.BlockSpec(a.shape, lambda b, j: (0,) * a.ndim)
    tr = lambda r: pl.BlockSpec((None, r, tm), lambda b, j: (b, 0, data(j)))
    lead = pl.BlockSpec((None, tm, 2 * N_SLC), lambda b, j: (b, j, 0))
    lead_t = pl.BlockSpec((None, N_GROUPS_KV, V_ROWS, tm), lambda b, j: (b, 0, 0, j))
    return pl.pallas_call(
        _inproj_kernel,
        grid=(B, nt + 1),
        in_specs=[row(D_MODEL), full(w_main), full(w_t)],
        out_specs=[lead, lead, row(128), row(128), row(512), row(512), tr(D_ATTN), lead_t, lead_t,
                   tr(GATE_ROWS)],
        out_shape=[jax.ShapeDtypeStruct((B, tp, 2 * N_SLC), BF16),
                   jax.ShapeDtypeStruct((B, tp, 2 * N_SLC), BF16),
                   jax.ShapeDtypeStruct((N, 128), F32),
                   jax.ShapeDtypeStruct((N, 128), F32),
                   jax.ShapeDtypeStruct((N, 512), F32),
                   jax.ShapeDtypeStruct((N, 512), F32),
                   jax.ShapeDtypeStruct((B, D_ATTN, T), BF16),
                   jax.ShapeDtypeStruct((B, N_GROUPS_KV, V_ROWS, tp), BF16),
                   jax.ShapeDtypeStruct((B, N_GROUPS_KV, V_ROWS, tp), BF16),
                   jax.ShapeDtypeStruct((B, GATE_ROWS, T), F32)],
        compiler_params=_cparams(("parallel", "arbitrary"), VMEM_LIMIT),
        name="inproj",
    )(x2, w_main, w_t)


def _compress_kernel(kr_ref, vr_ref, pek_ref, pev_ref, wk_ref, wv_ref, oh_ref, kc_ref, vct_ref, *,
                     n_cmp, n_chunk):
    def compress(raw_ref, pe_ref, w_ref):
        lo = jnp.zeros((n_chunk, 2 * HEAD_DIM), F32)
        hi = jnp.zeros((n_chunk, 2 * HEAD_DIM), F32)
        for j in range(CMP_STRIDE):
            a = raw_ref[pl.ds(j, n_chunk, stride=CMP_STRIDE), :]
            lo = lo + jnp.dot((a + pe_ref[j:j + 1, :]).astype(BF16), w_ref[j],
                              preferred_element_type=F32)
            hi = hi + jnp.dot((a + pe_ref[CMP_STRIDE + j:CMP_STRIDE + j + 1, :]).astype(BF16),
                              w_ref[CMP_STRIDE + j], preferred_element_type=F32)
        hi = pltpu.roll(hi, n_chunk - 1, axis=0)
        rid = lax.broadcasted_iota(jnp.int32, (n_chunk, 2 * HEAD_DIM), 0)
        out = jnp.where(rid < n_cmp, lo + hi, 0.0)
        return jnp.concatenate([jnp.zeros((C_OFF, 2 * HEAD_DIM), F32), out,
                                jnp.zeros((C_PAD - C_OFF - n_chunk, 2 * HEAD_DIM), F32)], axis=0)

    kc_ref[:, 0:2 * HEAD_DIM] = compress(kr_ref, pek_ref, wk_ref).astype(BF16)
    kc_ref[:, 2 * HEAD_DIM:] = oh_ref[...]
    vct_ref[...] = compress(vr_ref, pev_ref, wv_ref).T.astype(BF16)


def _compress(kc_raw, vc_raw, pek, pev, wk, wv, grp_onehot, B, T, n_cmp):
    n_chunk = T // CMP_STRIDE
    blk = pl.BlockSpec((None, T, 2 * HEAD_DIM), lambda b: (b, 0, 0))
    full = lambda a: pl.BlockSpec(a.shape, lambda b: (0,) * a.ndim)
    return pl.pallas_call(
        functools.partial(_compress_kernel, n_cmp=n_cmp, n_chunk=n_chunk),
        grid=(B,),
        in_specs=[blk, blk, full(pek), full(pev), full(wk), full(wv), full(grp_onehot)],
        out_specs=[pl.BlockSpec((None, C_PAD, 256), lambda b: (b, 0, 0)),
                   pl.BlockSpec((None, 2 * HEAD_DIM, C_PAD), lambda b: (b, 0, 0))],
        out_shape=[jax.ShapeDtypeStruct((B, C_PAD, 256), BF16),
                   jax.ShapeDtypeStruct((B, 2 * HEAD_DIM, C_PAD), BF16)],
        compiler_params=_cparams(("parallel",), VMEM_LIMIT),
        name="compress",
    )(kc_raw, vc_raw, pek, pev, wk, wv, grp_onehot)


def _bias_kernel(tab_ref, dt_ref, bt_ref, dc_ref, bc_ref, os_ref, ow_ref, oc_ref):
    h = pl.program_id(0) * GQA_REP + pl.program_id(1)

    def lookup(bk):
        out = jnp.zeros(bk.shape, F32)
        for b in range(N_BUCKETS):
            out = jnp.where(bk == b, tab_ref[h, b], out)
        return (out - tab_ref[h, N_BUCKETS - 1]) * LOG2E

    dt = dt_ref[...]
    v = lookup(bt_ref[...])
    os_ref[...] = jnp.where(dt >= 0, v, NEG)
    ow_ref[...] = jnp.where((dt >= 0) & (dt < WINDOW), v, NEG)
    oc_ref[...] = jnp.where(dc_ref[...] >= 0, lookup(bc_ref[...]), NEG)


def _t5_bucket(dist):
    max_exact = N_BUCKETS // 2
    d = jnp.maximum(dist, 0)
    df = jnp.maximum(d, 1).astype(F32)
    large = max_exact + (jnp.log(df / max_exact) / math.log(MAX_DISTANCE / max_exact)
                         * (N_BUCKETS - max_exact)).astype(jnp.int32)
    large = jnp.minimum(large, N_BUCKETS - 1)
    return jnp.where(d < max_exact, d, large)


def _bias_tiles(rel_bias):
    tq = jnp.arange(TQ, dtype=jnp.int32)
    dt = tq[None, :] + WINDOW - jnp.arange(TAIL, dtype=jnp.int32)[:, None]
    cc = jnp.arange(BAND, dtype=jnp.int32) - C_OFF
    dc = tq[None, :] - (cc[:, None] * CMP_STRIDE + CMP_LEN - 1)
    full = lambda a: pl.BlockSpec(a.shape, lambda g, r: (0,) * a.ndim)
    out = lambda rows: pl.BlockSpec((None, rows, TQ), lambda g, r: (g, 0, r))
    return pl.pallas_call(
        _bias_kernel,
        grid=(N_GROUPS_KV, GQA_REP),
        in_specs=[pl.BlockSpec(memory_space=pltpu.SMEM), full(dt), full(dt), full(dc), full(dc)],
        out_specs=[out(TAIL), out(TAIL), out(BAND)],
        out_shape=[jax.ShapeDtypeStruct((N_GROUPS_KV, TAIL, GQA_REP * TQ), F32),
                   jax.ShapeDtypeStruct((N_GROUPS_KV, TAIL, GQA_REP * TQ), F32),
                   jax.ShapeDtypeStruct((N_GROUPS_KV, BAND, GQA_REP * TQ), F32)],
        compiler_params=_cparams(("arbitrary", "arbitrary")),
        name="biastile",
    )(rel_bias.T, dt, _t5_bucket(dt), dc, _t5_bucket(dc))


def _attn_kernel(qt_ref, gt_ref, kc_ref, vct_ref, kaug_ref, vst_ref, kw_ref, vwt_ref, ds_ref, dw_ref,
                 bc_ref, o_ref, *scratch, n_far_max):
    n_hg = TILES * N_GROUPS_KV
    sc_ref, sw_ref = (scratch[n_hg * k:n_hg * (k + 1)] for k in range(2))
    qa_ref, qat_ref, m_ref, acc_ref, sb_ref, sm_ref, oc_ref, ow_ref, imp_ref = scratch[2 * n_hg:]
    cols4 = GQA_REP * TQ
    groups = range(N_GROUPS_KV)
    zeros_q = jnp.zeros((HEAD_DIM, cols4), BF16)
    grp = lax.broadcasted_iota(jnp.int32, (128, cols4), 0)
    neg_row = jnp.where(grp == 0, NEG, 0.0).astype(BF16)
    tq_col = lax.broadcasted_iota(jnp.int32, (1, cols4), 1) % TQ

    def chunk_start(c):
        return pl.multiple_of(WINDOW + jnp.minimum(c, n_far_max - 1) * FAR, FAR)

    def scores(h, c, par):
        kch = kaug_ref[pl.ds(chunk_start(c), FAR), :]
        for g in groups:
            sf = jnp.dot(kch, qa_ref[2 * h + g], preferred_element_type=F32)
            sb_ref[4 * h + 2 * g + par] = sf
            sm_ref[4 * h + 2 * g + par] = jnp.max(sf, axis=0, keepdims=True)

    def consume(h, nfar, c, par):
        c0 = chunk_start(c)
        live = c < nfar
        for g in groups:
            hg = 2 * h + g
            sf = sb_ref[4 * h + 2 * g + par]
            mprev = m_ref[hg]
            mcand = jnp.maximum(mprev, sm_ref[4 * h + 2 * g + par])
            mnew = jnp.where(live, mcand, mprev)
            pf = jnp.exp2(sf - jnp.where(live, mcand, -NEG)).astype(BF16)
            acc_ref[hg] = jnp.exp2(mprev - mnew) * acc_ref[hg] + jnp.dot(
                vst_ref[g, :, pl.ds(c0, FAR)], pf, preferred_element_type=F32)
            m_ref[hg] = mnew

    tiles = range(TILES)
    chains = [(h, g) for h in tiles for g in groups]
    t_i = [pl.program_id(1) * TILES + h for h in tiles]
    t_qs = [pl.multiple_of(i * TQ, TQ) for i in t_i]
    t_nfar = [jnp.maximum(i - 1, 0) // 4 for i in t_i]
    def qpad(h, g):
        qgt = jnp.concatenate(
            [qt_ref[(GQA_REP * g + r) * HEAD_DIM:(GQA_REP * g + r + 1) * HEAD_DIM, h * TQ:(h + 1) * TQ]
             for r in range(GQA_REP)], axis=1)
        return jnp.concatenate([qgt, zeros_q] if g == 0 else [zeros_q, qgt], axis=0)

    def compressed_and_window(rows):
        for h, g in chains:
            i = t_i[h]
            gmask = jnp.where((grp < C_OFF // 8) | (grp >= i + BAND // 8), NEG, 0.0).astype(BF16)
            band0 = pl.multiple_of(i * 8, 8)
            sc_ref[2 * h + g][0:rows, :] = jnp.dot(kc_ref[0:rows, :],
                                                   jnp.concatenate([qpad(h, g), gmask], axis=0),
                                                   preferred_element_type=F32)
            sc_ref[2 * h + g][pl.ds(band0, BAND), :] = sc_ref[2 * h + g][pl.ds(band0, BAND), :] + bc_ref[g]
        w_max = {}
        for h, g in chains:
            sw = jnp.dot(kw_ref[pl.ds(t_qs[h], TAIL), :], jnp.concatenate([qpad(h, g), neg_row], axis=0),
                         preferred_element_type=F32) + dw_ref[g]
            sw_ref[2 * h + g][...] = sw
            w_max[h, g] = jnp.max(sw, axis=0, keepdims=True)
        for h, g in chains:
            s = sc_ref[2 * h + g][0:rows, :]
            e = jnp.exp2(s - jnp.max(s, axis=0, keepdims=True))
            has_cmp = t_qs[h] + tq_col >= CMP_LEN - 1
            pinv = jnp.where(has_cmp, 1.0 / jnp.sum(e, axis=0, keepdims=True), 0.0)
            res = jnp.dot(vct_ref[:, 0:rows], e.astype(BF16), preferred_element_type=F32)
            oc_ref[2 * h + g] = res[0:2 * HEAD_DIM, :] * pinv
            imp_ref[2 * h + g] = sum(res[2 * HEAD_DIM:, r * TQ:(r + 1) * TQ] * pinv[:, r * TQ:(r + 1) * TQ]
                                     for r in range(GQA_REP))
        for h, g in chains:
            pw = jnp.exp2(sw_ref[2 * h + g][...] - w_max[h, g]).astype(BF16)
            acc_w = jnp.dot(vwt_ref[g, :, pl.ds(t_qs[h], TAIL)], pw, preferred_element_type=F32)
            ow_ref[2 * h + g] = acc_w[0:HEAD_DIM, :] * (1.0 / acc_w[HEAD_DIM:HEAD_DIM + 1, :])

    step = pl.program_id(1)
    first_step = 0
    for rows in CMP_ROW_LEVELS:
        end_step = ((rows - BAND) // 8 - TILES + 1) // TILES + 1
        in_level = step >= first_step if rows == C_PAD else (step >= first_step) & (step < end_step)
        pl.when(in_level)(functools.partial(compressed_and_window, rows))
        first_step = end_step

    def select_blocks(nsel):
        sid = lax.broadcasted_iota(jnp.int32, (nsel, TQ), 0)
        sid_f = sid.astype(F32)
        lane = lax.broadcasted_iota(jnp.int32, (nsel, TQ), 1)
        score, mbt = {}, {}
        for h, g in chains:
            cur = (t_qs[h] + lane) // SEL_BLOCK
            forced = (sid == 0) | (sid == cur) | (sid == cur - 1)
            score[h, g] = jnp.where(forced, -jnp.inf, jnp.where(sid <= cur, imp_ref[2 * h + g, 0:nsel, :], NEG))
            mbt[h, g] = jnp.where(forced, 0.0, NEG)
        for _ in range(N_SEL - 3):
            for ch in chains:
                cm = jnp.max(score[ch], axis=0, keepdims=True)
                first = jnp.min(jnp.where(score[ch] == cm, sid_f, float(N_SLC)), axis=0, keepdims=True)
                pick = sid_f == first
                mbt[ch] = jnp.where(pick, 0.0, mbt[ch])
                score[ch] = jnp.where(pick, -jnp.inf, score[ch])
        for h, g in chains:
            mbt_tail = jnp.where(sid < t_nfar[h] * (FAR // SEL_BLOCK), NEG, mbt[h, g])
            for dst, m in ((qa_ref, mbt[h, g]), (qat_ref, mbt_tail)):
                m = jnp.concatenate([m, jnp.full((N_SLC - nsel, TQ), NEG, F32)], axis=0) if nsel < N_SLC else m
                dst[2 * h + g, 0:N_SLC, :] = jnp.concatenate([m.astype(BF16)] * GQA_REP, axis=1)

    first_step = 0
    for nsel in SEL_ROW_LEVELS:
        end_step = nsel // (TILES * TQ // SEL_BLOCK)
        in_level = step >= first_step if nsel == N_SLC else (step >= first_step) & (step < end_step)
        pl.when(in_level)(functools.partial(select_blocks, nsel))
        first_step = end_step

    for h, g in chains:
        qa_ref[2 * h + g, N_SLC:, :] = qpad(h, g)
        qa_tail = jnp.concatenate([qat_ref[2 * h + g], qpad(h, g)], axis=0)
        st = jnp.dot(kaug_ref[pl.ds(t_qs[h], TAIL), :], qa_tail, preferred_element_type=F32) + ds_ref[g]
        sc_ref[2 * h + g][...] = st
        m_ref[2 * h + g] = jnp.max(st, axis=0, keepdims=True)
    for h in tiles:
        scores(h, 0, 0)
    for h, g in chains:
        pt = jnp.exp2(sc_ref[2 * h + g][...] - m_ref[2 * h + g]).astype(BF16)
        acc_ref[2 * h + g] = jnp.dot(vst_ref[g, :, pl.ds(t_qs[h], TAIL)], pt, preferred_element_type=F32)

    def far_pair(kk, carry):
        c = 2 * kk
        for h in tiles:
            scores(h, c + 1, 1)
        for h in tiles:
            consume(h, t_nfar[h], c, 0)
        for h in tiles:
            scores(h, c + 2, 0)
        for h in tiles:
            consume(h, t_nfar[h], c + 1, 1)
        return carry

    n_chunks = t_nfar[TILES - 1]
    n_pairs = jnp.maximum(n_chunks - 1, 0) // 2
    lax.fori_loop(0, n_pairs, far_pair, 0)

    @pl.when(n_chunks - 2 * n_pairs == 1)
    def _():
        for h in tiles:
            consume(h, t_nfar[h], n_chunks - 1, 0)

    @pl.when(n_chunks - 2 * n_pairs == 2)
    def _():
        for h in tiles:
            scores(h, n_chunks - 1, 1)
        for h in tiles:
            consume(h, t_nfar[h], n_chunks - 2, 0)
        for h in tiles:
            consume(h, t_nfar[h], n_chunks - 1, 1)

    for h in tiles:
        lanes = slice(h * TQ, (h + 1) * TQ)
        out_rows = []
        for g in groups:
            dsl = slice(g * HEAD_DIM, (g + 1) * HEAD_DIM)
            hg = 2 * h + g
            o_s = acc_ref[hg, 0:HEAD_DIM, :] * (1.0 / acc_ref[hg, HEAD_DIM:HEAD_DIM + 1, :])
            for r in range(GQA_REP):
                cs = slice(r * TQ, (r + 1) * TQ)
                gc = 3 * (GQA_REP * g + r)
                out_rows.append(gt_ref[gc:gc + 1, lanes] * oc_ref[hg, dsl, cs]
                                + gt_ref[gc + 1:gc + 2, lanes] * o_s[:, cs]
                                + gt_ref[gc + 2:gc + 3, lanes] * ow_ref[hg, :, cs])
        o_ref[h * TQ:(h + 1) * TQ, :] = jnp.concatenate(out_rows, axis=0).T


def _attention(t_all, gates_t, kc2, vct, kaug, vst, kw_pad, vwt, ds, dw, bc, B, T):
    tp = T + WINDOW
    tqs = TQ * TILES
    per_b = lambda shape: pl.BlockSpec((None,) + shape, lambda b, i: (b,) + (0,) * len(shape),
                                      pipeline_mode=pl.Buffered(1))
    const = lambda a: pl.BlockSpec(a.shape, lambda b, i: (0,) * a.ndim, pipeline_mode=pl.Buffered(1))
    cols4 = GQA_REP * TQ
    n_hg = TILES * N_GROUPS_KV
    return pl.pallas_call(
        functools.partial(_attn_kernel, n_far_max=T // FAR),
        grid=(B, T // tqs),
        in_specs=[pl.BlockSpec((None, 512, tqs), lambda b, i: (b, 0, i)),
                  pl.BlockSpec((None, GATE_ROWS, tqs), lambda b, i: (b, 0, i)),
                  per_b((C_PAD, 256)),
                  per_b((256, C_PAD)),
                  per_b((tp, 256)),
                  per_b((N_GROUPS_KV, V_ROWS, tp)),
                  per_b((tp, 256)),
                  per_b((N_GROUPS_KV, V_ROWS, tp)),
                  const(ds), const(dw), const(bc)],
        out_specs=pl.BlockSpec((None, tqs, 512), lambda b, i: (b, i, 0)),
        out_shape=jax.ShapeDtypeStruct((B, T, 512), F32),
        scratch_shapes=([pltpu.VMEM((C_PAD, cols4), F32)] * n_hg
                        + [pltpu.VMEM((TAIL, cols4), F32)] * n_hg
                        + [pltpu.VMEM((n_hg, N_SLC + 2 * HEAD_DIM, cols4), BF16),
                           pltpu.VMEM((n_hg, N_SLC, cols4), BF16),
                           pltpu.VMEM((n_hg, 1, cols4), F32),
                           pltpu.VMEM((n_hg, V_ROWS, cols4), F32),
                           pltpu.VMEM((2 * n_hg, FAR, cols4), F32),
                           pltpu.VMEM((2 * n_hg, 1, cols4), F32),
                           pltpu.VMEM((n_hg, 2 * HEAD_DIM, cols4), F32),
                           pltpu.VMEM((n_hg, HEAD_DIM, cols4), F32),
                           pltpu.VMEM((n_hg, N_SLC, TQ), F32)]),
        compiler_params=_cparams(("parallel", "arbitrary"), VMEM_LIMIT),
        name="attn",
    )(t_all, gates_t, kc2, vct, kaug, vst, kw_pad, vwt, ds, dw, bc)


def _rglru_kernel(*refs, L):
    J = L // 8
    n_col = D_RNN // 128
    rx_refs, ry_refs = refs[:n_col], refs[n_col:2 * n_col]
    (cw_ref, cb_ref, wa_ref, ba_ref, wx_ref, bx_ref, sp_ref, eu_ref, ed_ref, o_ref, eub_ref, edb_ref,
     xprev_ref, h_ref, os_ref) = refs[2 * n_col:]
    eub_ref[...] = eu_ref[...].astype(BF16)
    edb_ref[...] = ed_ref[...].astype(BF16)

    @pl.when(pl.program_id(1) == 0)
    def _():
        xprev_ref[...] = jnp.zeros(xprev_ref.shape, F32)
        h_ref[...] = jnp.zeros(h_ref.shape, F32)

    def strided(col_refs):
        return jnp.concatenate(
            [jnp.concatenate([ref[pl.ds(j, 8, stride=J), :] for j in range(J)], axis=0)
             for ref in col_refs], axis=1)

    xp = strided(rx_refs)
    yp = strided(ry_refs)
    sub = lax.broadcasted_iota(jnp.int32, (8, D_RNN), 0)

    def delayed(d):
        heads = []
        for j in range(d):
            src = pltpu.roll(xp[(J + j - d) * 8:(J + j - d + 1) * 8], 1, axis=0)
            heads.append(jnp.where(sub == 0, xprev_ref[8 + j - d:9 + j - d, :], src))
        return jnp.concatenate(heads + [xp[:L - 8 * d]], axis=0)

    xc = (cb_ref[...] + cw_ref[0:1, :] * delayed(3) + cw_ref[1:2, :] * delayed(2)
          + cw_ref[2:3, :] * delayed(1) + cw_ref[3:4, :] * xp)
    xprev_ref[...] = jnp.concatenate([ref[L - 8:L, :] for ref in rx_refs], axis=1)
    xcb = xc.astype(BF16)
    r = jax.nn.sigmoid(jnp.dot(xcb, wa_ref[...], preferred_element_type=F32) + ba_ref[...])
    ig = jax.nn.sigmoid(jnp.dot(xcb, wx_ref[...], preferred_element_type=F32) + bx_ref[...])
    a = jnp.exp(-RG_C * r * sp_ref[...])
    bt = jnp.sqrt(1.0 - a * a) * (ig * xc)

    h = jnp.zeros((8, D_RNN), F32)
    dec = jnp.ones((8, D_RNN), F32)
    hs, decs = [], []
    for j in range(J):
        aj = a[j * 8:(j + 1) * 8]
        h = aj * h + bt[j * 8:(j + 1) * 8]
        dec = aj * dec
        hs.append(h)
        decs.append(dec)
    carry = h_ref[0:1, :]
    carries = []
    for s in range(8):
        carries.append(carry)
        carry = dec[s:s + 1, :] * carry + h[s:s + 1, :]
    h_ref[...] = jnp.broadcast_to(carry, h_ref.shape)
    h_in = jnp.concatenate(carries, axis=0)
    cdf_c = math.sqrt(2.0 / math.pi)
    for j in range(J):
        y = yp[j * 8:(j + 1) * 8]
        cdf = 0.5 * (1.0 + jnp.tanh(cdf_c * (y + 0.044715 * (y * y * y))))
        res = (hs[j] + decs[j] * h_in) * (y * cdf)
        for c in range(n_col):
            os_ref[c, pl.ds(j, 8, stride=J), :] = res[:, c * 128:(c + 1) * 128]
    for c in range(n_col):
        o_ref[:, c * 128:(c + 1) * 128] = os_ref[c]


def _rglru(rx, ry, conv_w, conv_b, wa_bd, b_a, wx_bd, b_x, sp, w_up, w_dn, B, T):
    L = 512
    steps = B * T // L
    nt = T // L
    up_rows, dn_rows = w_up.shape[0] // steps, w_dn.shape[0] // steps
    assert up_rows * steps == w_up.shape[0] and dn_rows * steps == w_dn.shape[0] and dn_rows % 16 == 0
    slab = lambda rows, a: pl.BlockSpec((rows, a.shape[1]), lambda b, t: (b * nt + t, 0))
    blk = pl.BlockSpec((None, L, D_RNN), lambda b, t: (b, t, 0))
    n_col = D_RNN // 128
    cols = [pl.BlockSpec((None, L, 128), functools.partial(lambda b, t, c: (b, t, c), c=c))
            for c in range(n_col)]
    full = lambda a: pl.BlockSpec(a.shape, lambda b, t: (0,) * a.ndim)
    return pl.pallas_call(
        functools.partial(_rglru_kernel, L=L),
        grid=(B, T // L),
        in_specs=cols + cols + [full(conv_w), full(conv_b), full(wa_bd), full(b_a), full(wx_bd),
                                full(b_x), full(sp), slab(up_rows, w_up), slab(dn_rows, w_dn)],
        out_specs=[blk, slab(up_rows, w_up), slab(dn_rows, w_dn)],
        out_shape=[jax.ShapeDtypeStruct((B, T, D_RNN), F32),
                   jax.ShapeDtypeStruct(w_up.shape, BF16),
                   jax.ShapeDtypeStruct(w_dn.shape, BF16)],
        scratch_shapes=[pltpu.VMEM((8, D_RNN), F32), pltpu.VMEM((8, D_RNN), F32),
                        pltpu.VMEM((n_col, L, 128), F32)],
        compiler_params=_cparams(("parallel", "arbitrary"), VMEM_LIMIT),
        name="rglru",
    )(*([rx] * n_col + [ry] * n_col), conv_w, conv_b, wa_bd, b_a, wx_bd, b_x, sp, w_up, w_dn)


def _layer_norm(y, g, b):
    mu = jnp.mean(y, axis=-1, keepdims=True)
    d = y - mu
    var = jnp.mean(d * d, axis=-1, keepdims=True)
    return d * lax.rsqrt(var + LN_EPS) * g + b


def _outproj_kernel(at_ref, rn_ref, x_ref, ga_ref, gr_ref, wo_ref, g1_ref, b1_ref, wr2_ref, br_ref,
                    x1_ref, comb_ref):
    sub = at_ref.shape[0] // OUT_SUBS
    subs = [slice(h * sub, (h + 1) * sub) for h in range(OUT_SUBS)]
    heads, mix, xh, xl, hh_hl, lo = {}, {}, {}, {}, {}, {}
    for r in subs:
        a = at_ref[r, :]
        rn = rn_ref[r, :]
        ha = a * lax.rsqrt(jnp.mean(a * a, axis=-1, keepdims=True) + RMS_EPS) * ga_ref[...]
        hr = rn * lax.rsqrt(jnp.mean(rn * rn, axis=-1, keepdims=True) + RMS_EPS) * gr_ref[...]
        heads[r] = jnp.concatenate([ha, hr], axis=1).astype(BF16)
    for r in subs:
        mix[r] = jnp.dot(heads[r], wo_ref[...], preferred_element_type=F32)
    for r in subs:
        x1 = _layer_norm(ALPHA * x_ref[r, :] + mix[r], g1_ref[...], b1_ref[...])
        x1_ref[r, :] = x1
        xh[r] = x1.astype(BF16)
        xl[r] = (x1 - xh[r].astype(F32)).astype(BF16)
    for r in subs:
        hh_hl[r] = jnp.dot(xh[r], wr2_ref[...], preferred_element_type=F32)
        lo[r] = jnp.dot(xl[r], wr2_ref[:, :128], preferred_element_type=F32)
    for r in subs:
        lg = (hh_hl[r][:, :128] + lo[r] + hh_hl[r][:, 128:]) + br_ref[...]
        lane = lax.broadcasted_iota(jnp.int32, lg.shape, 1)
        lane_f = lane.astype(F32)
        big = 1e9
        isg = lane < N_EXP_GROUPS
        gmax = jnp.max(jnp.where(isg, lg, -jnp.inf), axis=-1, keepdims=True)
        pg_top = 1.0 / jnp.sum(jnp.where(isg, jnp.exp(lg - gmax), 0.0), axis=-1, keepdims=True)
        gi = jnp.min(jnp.where(isg & (lg == gmax), lane_f, big), axis=-1, keepdims=True)
        egrp = ((lane - N_EXP_GROUPS) // EXPERTS_PER_GROUP).astype(F32)
        ise = (lane >= N_EXP_GROUPS) & (lane < N_EXP_GROUPS + N_EXPERTS) & (egrp == gi)
        emax = jnp.max(jnp.where(ise, lg, -jnp.inf), axis=-1, keepdims=True)
        i1 = jnp.min(jnp.where(ise & (lg == emax), lane_f, big), axis=-1, keepdims=True)
        rest = ise & (lane_f != i1)
        m2 = jnp.max(jnp.where(rest, lg, -jnp.inf), axis=-1, keepdims=True)
        i2 = jnp.min(jnp.where(rest & (lg == m2), lane_f, big), axis=-1, keepdims=True)
        e2 = jnp.exp(m2 - emax)
        inv = pg_top / (1.0 + e2)
        comb_ref[r, :] = jnp.where(lane_f == i1, inv, 0.0) + jnp.where(lane_f == i2, inv * e2, 0.0)


def _outproj(attn, rnn, x2, ga, gr, wo, g1, b1, wr2, br, N):
    tm = 512
    row = lambda w: pl.BlockSpec((tm, w), lambda i: (i, 0))
    full = lambda a: pl.BlockSpec(a.shape, lambda i: (0,) * a.ndim)
    return pl.pallas_call(
        _outproj_kernel,
        grid=(N // tm,),
        in_specs=[row(512), row(512), row(D_MODEL), full(ga), full(gr), full(wo), full(g1), full(b1),
                  full(wr2), full(br)],
        out_specs=[row(D_MODEL), row(128)],
        out_shape=[jax.ShapeDtypeStruct((N, D_MODEL), F32), jax.ShapeDtypeStruct((N, 128), F32)],
        compiler_params=_cparams(("parallel",), VMEM_LIMIT),
        name="outproj",
    )(attn, rnn, x2, ga, gr, wo, g1, b1, wr2, br)


def _moe_kernel(x1_ref, comb_ref, p_ref, tri_ref, wup_ref, wdn_ref, wg_ref, wp_ref, g2_ref, b2_ref, o_ref,
                acc_ref, xb_ref, rank_ref, rankt_ref, combt_ref):
    pi = pl.program_id(1)
    n_sub = x1_ref.shape[0] // MOE_SUB
    sub_rows = [slice(h * MOE_SUB, (h + 1) * MOE_SUB) for h in range(n_sub)]
    lane_col = N_EXP_GROUPS

    @pl.when(pi == 0)
    def _():
        x1 = x1_ref[...]
        xb = x1.astype(BF16)
        xb_ref[...] = xb
        gate = jax.nn.sigmoid(jnp.dot(xb, wg_ref[...], preferred_element_type=F32))
        ple = gate * jnp.dot(p_ref[...].astype(BF16), wp_ref[...], preferred_element_type=F32)
        acc_ref[...] = ALPHA * x1 + ple
        for rows in sub_rows:
            comb = comb_ref[rows, :]
            chosen = comb > 0.0
            rank = jnp.dot(tri_ref[...], jnp.where(chosen, 1.0, 0.0).astype(BF16),
                           preferred_element_type=F32)
            rank = jnp.where(chosen, rank, -1.0)
            rank_ref[rows, :] = rank
            rankt_ref[:, rows] = rank.T
            combt_ref[:, rows] = comb.T

    lane = lax.broadcasted_iota(jnp.int32, (MOE_SUB, 128), 1)
    slot_r = lax.broadcasted_iota(jnp.int32, (MOE_SLOTS, MOE_SUB), 0).astype(F32)
    slot_c = lax.broadcasted_iota(jnp.int32, (MOE_SUB, MOE_PAIR_K), 1).astype(F32)
    pad_rows = jnp.zeros((MOE_PAIR_K - 2 * MOE_SLOTS, D_MODEL), BF16)

    def route_pair(base):
        cmask, xc, y, ys, rk_col = {}, {}, {}, {}, {}
        for rows in sub_rows:
            for k in range(2):
                rk_row = rankt_ref[pl.ds(2 * pi + k + lane_col, 1), rows] - base
                cmask[rows, k] = rk_row == slot_r
            sel = jnp.concatenate([jnp.where(cmask[rows, k], 1.0, 0.0).astype(BF16) for k in range(2)], axis=0)
            xcs = jnp.dot(sel, xb_ref[rows, :], preferred_element_type=F32).astype(BF16)
            for k in range(2):
                xc[rows, k] = xcs[k * MOE_SLOTS:(k + 1) * MOE_SLOTS, :]
        xs = [jnp.concatenate([xc[rows, k] for rows in sub_rows], axis=0) for k in range(2)]
        hsw = []
        for k in range(2):
            blocks = []
            for c0 in range(0, D_FF, MOE_FF_BLOCK):
                ua = jnp.dot(xs[k], wup_ref[k, :, c0:c0 + MOE_FF_BLOCK], preferred_element_type=F32)
                ub = jnp.dot(xs[k], wup_ref[k, :, D_FF + c0:D_FF + c0 + MOE_FF_BLOCK],
                             preferred_element_type=F32)
                blocks.append((ua * jax.nn.sigmoid(ua) * ub).astype(BF16))
            hsw.append(jnp.concatenate(blocks, axis=1))
        for k in range(2):
            yk = jnp.dot(hsw[k], wdn_ref[k], preferred_element_type=F32)
            for h, rows in enumerate(sub_rows):
                y[rows, k] = yk[h * MOE_SLOTS:(h + 1) * MOE_SLOTS, :]
        for rows in sub_rows:
            for k in range(2):
                e_lane = 2 * pi + k + lane_col
                w_row = combt_ref[pl.ds(e_lane, 1), rows]
                w_slot = jnp.sum(jnp.where(cmask[rows, k], w_row, 0.0), axis=1, keepdims=True)
                ys[rows, k] = (w_slot * y[rows, k]).astype(BF16)
                rk_col[rows, k] = jnp.sum(jnp.where(lane == e_lane, rank_ref[rows, :], 0.0), axis=1,
                                          keepdims=True) - base
        for rows in sub_rows:
            slot_of = jnp.where(slot_c < MOE_SLOTS, rk_col[rows, 0], rk_col[rows, 1] + MOE_SLOTS)
            acc_ref[rows, :] = acc_ref[rows, :] + jnp.dot(
                jnp.where(slot_of == slot_c, 1.0, 0.0).astype(BF16),
                jnp.concatenate([ys[rows, 0], ys[rows, 1], pad_rows], axis=0), preferred_element_type=F32)

    route_pair(0.0)

    top_rank = jnp.max(jnp.maximum(rankt_ref[pl.ds(2 * pi + lane_col, 1), :],
                                   rankt_ref[pl.ds(2 * pi + 1 + lane_col, 1), :]))

    @pl.when(top_rank >= MOE_SLOTS)
    def _():
        def extra(sb, carry):
            route_pair((sb * MOE_SLOTS).astype(F32))
            return carry

        lax.fori_loop(1, top_rank.astype(jnp.int32) // MOE_SLOTS + 1, extra, 0)

    @pl.when(pi == N_EXPERTS // 2 - 1)
    def _():
        o_ref[...] = _layer_norm(acc_ref[...], g2_ref[...], b2_ref[...])


def _moe(x1, comb, p2, wup, wdn, wg, wp, g2, b2, N):
    tm = MOE_TILE
    tri = (jnp.arange(MOE_SUB)[None, :] < jnp.arange(MOE_SUB)[:, None]).astype(BF16)
    row = lambda w: pl.BlockSpec((tm, w), lambda i, e: (i, 0))
    full = lambda a: pl.BlockSpec(a.shape, lambda i, e: (0,) * a.ndim)
    return pl.pallas_call(
        _moe_kernel,
        grid=(N // tm, N_EXPERTS // 2),
        in_specs=[row(D_MODEL), row(128), row(D_PLE), full(tri),
                  pl.BlockSpec((2, D_MODEL, 2 * D_FF), lambda i, e: (e, 0, 0)),
                  pl.BlockSpec((2, D_FF, D_MODEL), lambda i, e: (e, 0, 0)),
                  full(wg), full(wp), full(g2), full(b2)],
        out_specs=row(D_MODEL),
        out_shape=jax.ShapeDtypeStruct((N, D_MODEL), F32),
        scratch_shapes=[pltpu.VMEM((tm, D_MODEL), F32),
                        pltpu.VMEM((tm, D_MODEL), BF16),
                        pltpu.VMEM((tm, 128), F32),
                        pltpu.VMEM((128, tm), F32),
                        pltpu.VMEM((128, tm), F32)],
        compiler_params=_cparams(("parallel", "arbitrary"), VMEM_LIMIT),
        name="moe",
    )(x1, comb, p2, tri, wup, wdn, wg, wp, g2, b2)


def _block_diag(w):
    n, k, j = w.shape
    eye = jnp.eye(n, dtype=w.dtype)
    return (w[:, :, None, :] * eye[:, None, :, None]).reshape(n * k, n * j)


def kernel(x, p, rel_bias, w_in, cmp_pe_k, cmp_pe_v, cmp_w_k, cmp_w_v, conv_w, conv_b, rg_w_a, rg_b_a,
           rg_w_x, rg_b_x, rg_lambda, attn_out_gain, rnn_out_gain, w_out, ln1_g, ln1_b, router_group_w,
           router_group_b, router_expert_w, router_expert_b, expert_w_up, expert_w_down, ple_w,
           ple_gate_w, ln2_g, ln2_b):
    B, T, _ = x.shape
    N = B * T
    assert T % FAR == 0 and T % (TQ * TILES) == 0 and T % WINDOW == 0 and T // SEL_BLOCK <= N_SLC
    assert w_in.shape[0] == 1
    n_cmp = (T - CMP_LEN) // CMP_STRIDE + 1
    n_chunk = T // CMP_STRIDE
    assert C_OFF + n_chunk <= C_PAD
    row1 = lambda v: v.reshape(1, -1)

    w = w_in[0]
    c0 = D_ATTN
    kv = lambda k: w[:, c0 + k * D_KV:c0 + (k + 1) * D_KV]
    g0 = c0 + 6 * D_KV
    wq = w[:, :D_ATTN] * (HEAD_DIM ** -0.5 * LOG2E)
    w_main = jnp.concatenate([kv(2), kv(4), kv(0), kv(1), w[:, g0 + 24:]], axis=1).astype(BF16)
    w_gate = jnp.pad(w[:, g0:g0 + 24], ((0, 0), (0, GATE_ROWS - 24)))
    w_t = jnp.concatenate([wq, kv(3), kv(5), w_gate], axis=1).T.astype(BF16)

    x2 = x.reshape(N, D_MODEL)
    kaug, kw_pad, kc_raw, vc_raw, rx, ry, t_all, vst, vwt, gates_t = _inproj(x2, w_main, w_t, B, T)

    def per_pos(wc):
        wl = wc.reshape(CMP_LEN, HEAD_DIM, HEAD_DIM)
        z = jnp.zeros_like(wl)
        return jnp.concatenate([jnp.concatenate([wl, z], axis=2), jnp.concatenate([z, wl], axis=2)],
                               axis=1).astype(BF16)

    pek = jnp.tile(cmp_pe_k[0], (1, N_GROUPS_KV))
    pev = jnp.tile(cmp_pe_v[0], (1, N_GROUPS_KV))
    grp_onehot = (jnp.arange(C_PAD)[:, None] // 8 == jnp.arange(128)[None, :]).astype(BF16)
    kc2, vct = _compress(kc_raw.reshape(B, T, 128), vc_raw.reshape(B, T, 128), pek, pev,
                         per_pos(cmp_w_k[0]), per_pos(cmp_w_v[0]), grp_onehot, B, T, n_cmp)

    ds, dw, bc = _bias_tiles(rel_bias)

    cidx = jnp.arange(C_PAD) - C_OFF
    s_lo = jnp.arange(N_SLC)[:, None] * SEL_BLOCK
    overlap = ((cidx[None, :] * CMP_STRIDE < s_lo + SEL_BLOCK) & (cidx[None, :] * CMP_STRIDE + CMP_LEN > s_lo)
               & (cidx[None, :] >= 0) & (cidx[None, :] < n_cmp)).astype(BF16)
    vct = jnp.concatenate([vct, jnp.broadcast_to(overlap, (B, N_SLC, C_PAD))], axis=1)
    attn = _attention(t_all, gates_t, kc2, vct, kaug, vst, kw_pad, vwt, ds, dw, bc, B, T)

    sp = jax.nn.softplus(-rg_lambda[0].astype(F32))
    rnn, wup_b, wdn_b = _rglru(rx.reshape(B, T, D_RNN), ry.reshape(B, T, D_RNN), conv_w[0], row1(conv_b[0]),
                               _block_diag(rg_w_a[0]).astype(BF16), row1(rg_b_a[0]),
                               _block_diag(rg_w_x[0]).astype(BF16), row1(rg_b_x[0]), row1(sp),
                               expert_w_up[0].reshape(N_EXPERTS * D_MODEL, 2 * D_FF),
                               expert_w_down[0].reshape(N_EXPERTS * D_FF, D_MODEL), B, T)

    wr = jnp.pad(jnp.concatenate([router_group_w[0], router_expert_w[0]], axis=1), ((0, 0), (0, 108)))
    wrh = wr.astype(BF16)
    wr2 = jnp.concatenate([wrh, (wr - wrh.astype(F32)).astype(BF16)], axis=1)
    br = jnp.pad(jnp.concatenate([router_group_b[0], router_expert_b[0]]), (0, 108)).reshape(1, 128)
    x1, comb = _outproj(attn.reshape(N, 512), rnn.reshape(N, D_RNN), x2, row1(attn_out_gain[0]),
                        row1(rnn_out_gain[0]), w_out[0].astype(BF16), row1(ln1_g[0]), row1(ln1_b[0]),
                        wr2, br, N)

    out = _moe(x1, comb, p[0].reshape(N, D_PLE), wup_b.reshape(N_EXPERTS, D_MODEL, 2 * D_FF),
               wdn_b.reshape(N_EXPERTS, D_FF, D_MODEL), ple_gate_w[0].astype(BF16), ple_w[0].astype(BF16),
               row1(ln2_g[0]), row1(ln2_b[0]), N)
    return out.reshape(B, T, D_MODEL)
```
